```python
import jax, jax.numpy as jnp
from jax import lax
import numpy as np

D_MODEL = 2048
BATCH = 4
SEQ = 8192
DEPTH = 1

MEM_LEN = 256
ROPE_THETA = 500000.0
EPS = 1e-6
Q_BLOCK = 128
NEG = -1e30

NSA_HEADS = 16
NSA_KV_HEADS = 4
NSA_HPG = NSA_HEADS // NSA_KV_HEADS
NSA_DK = 96
NSA_DV = 64
NSA_ROT = NSA_DK // 4
CMP_BLOCK = 32
CMP_STRIDE = 16
SEL_BLOCK = 64
SEL_TOPK = 16
WINDOW = 512

MLA_HEADS = 16
MLA_NOPE = 64
MLA_ROPE = 32
MLA_DV = 64
MLA_Q_RANK = 512
MLA_KV_RANK = 256

XA_HEADS = 4
XA_DIM = 128

D_FF = 4 * D_MODEL

SPLITS = (NSA_HEADS * NSA_DK,
          NSA_KV_HEADS * NSA_DK, NSA_KV_HEADS * NSA_DV,
          NSA_KV_HEADS * NSA_DK, NSA_KV_HEADS * NSA_DV,
          NSA_KV_HEADS * NSA_DK, NSA_KV_HEADS * NSA_DV,
          NSA_HEADS * 3,
          MLA_Q_RANK, MLA_KV_RANK, MLA_ROPE,
          D_MODEL, D_MODEL)
D_IN = (NSA_HEADS * NSA_DK + 3 * NSA_KV_HEADS * (NSA_DK + NSA_DV) + NSA_HEADS * 3
        + MLA_Q_RANK + MLA_KV_RANK + MLA_ROPE + 2 * D_MODEL)

kernel_name = 'hybrid_nsa_mla_gated_block'


def rmsnorm(x, g):
    xf = x.astype(jnp.float32)
    y = xf * lax.rsqrt(jnp.mean(xf * xf, axis=-1, keepdims=True) + EPS)
    return (y * g.astype(jnp.float32)).astype(x.dtype)


def rope_tables(n_pos, dim):
    inv = 1.0 / (ROPE_THETA ** (jnp.arange(0, dim, 2, dtype=jnp.float32) / dim))
    ang = jnp.arange(n_pos, dtype=jnp.float32)[:, None] * inv[None, :]
    return jnp.cos(ang), jnp.sin(ang)


def apply_rope(x, cos, sin):
    half = x.shape[-1] // 2
    shp = (cos.shape[0],) + (1,) * (x.ndim - 3) + (half,)
    c, s = cos.reshape(shp), sin.reshape(shp)
    xf = x.astype(jnp.float32)
    x1, x2 = xf[..., :half], xf[..., half:]
    return jnp.concatenate([x1 * c - x2 * s, x1 * s + x2 * c], axis=-1).astype(x.dtype)


def partial_rope(x, cos, sin):
    return jnp.concatenate([apply_rope(x[..., :NSA_ROT], cos, sin), x[..., NSA_ROT:]], axis=-1)


def masked_softmax(s, mask):
    s = jnp.where(mask, s.astype(jnp.float32), NEG)
    m = jnp.max(s, axis=-1, keepdims=True)
    e = jnp.where(mask, jnp.exp(s - m), 0.0)
    return e / jnp.maximum(jnp.sum(e, axis=-1, keepdims=True), 1e-30)


def compress(kv, pos, w1, w2):
    b, s, g, d = kv.shape
    nc = (s - CMP_BLOCK) // CMP_STRIDE + 1
    idx = (np.arange(nc) * CMP_STRIDE)[:, None] + np.arange(CMP_BLOCK)[None, :]
    blk = kv[:, idx] + pos[None, None, :, None, :].astype(kv.dtype)
    blk = blk.transpose(0, 1, 3, 2, 4).reshape(b, nc, g, CMP_BLOCK * d)
    return jax.nn.gelu(blk @ w1) @ w2


def cmp_to_sel_matrix(nc, nsb):
    cs = np.arange(nc) * CMP_STRIDE
    ce = cs + CMP_BLOCK
    ss = np.arange(nsb) * SEL_BLOCK
    se = ss + SEL_BLOCK
    ov = np.clip(np.minimum(ce[:, None], se[None, :]) - np.maximum(cs[:, None], ss[None, :]), 0, None)
    return jnp.asarray(ov.astype(np.float32) / np.float32(CMP_BLOCK))


def nsa_attention(q, kc, vc, ks, vs, kw, vw, gates,
                  cmp_pos_k, cmp_w1_k, cmp_w2_k, cmp_pos_v, cmp_w1_v, cmp_w2_v):
    b, s = q.shape[:2]
    G, HPG = NSA_KV_HEADS, NSA_HPG
    scale = NSA_DK ** -0.5
    kcc = compress(kc, cmp_pos_k, cmp_w1_k, cmp_w2_k)
    vcc = compress(vc, cmp_pos_v, cmp_w1_v, cmp_w2_v)
    nc = kcc.shape[1]
    cmp_end = jnp.arange(nc) * CMP_STRIDE + CMP_BLOCK - 1
    nsb = s // SEL_BLOCK
    topk = min(SEL_TOPK, nsb)
    m_sel = cmp_to_sel_matrix(nc, nsb)
    ks_blk = ks.reshape(b, nsb, SEL_BLOCK, G, NSA_DK).transpose(0, 3, 1, 2, 4)
    vs_blk = vs.reshape(b, nsb, SEL_BLOCK, G, NSA_DV).transpose(0, 3, 1, 2, 4)
    kw_pad = jnp.pad(kw, ((0, 0), (WINDOW, 0), (0, 0), (0, 0)))
    vw_pad = jnp.pad(vw, ((0, 0), (WINDOW, 0), (0, 0), (0, 0)))
    bi = jnp.arange(b)[:, None, None, None]
    gi = jnp.arange(G)[None, :, None, None]
    blk_ids = jnp.arange(nsb)

    def block(i):
        s0 = i * Q_BLOCK
        t = s0 + jnp.arange(Q_BLOCK)
        qb = lax.dynamic_slice_in_dim(q, s0, Q_BLOCK, axis=1).reshape(b, Q_BLOCK, G, HPG, NSA_DK)
        gb = lax.dynamic_slice_in_dim(gates, s0, Q_BLOCK, axis=1).reshape(b, Q_BLOCK, G, HPG, 3)
        s_c = jnp.einsum('bqghd,bcgd->bghqc', qb, kcc) * scale
        p_c = masked_softmax(s_c, cmp_end[None, :] <= t[:, None])
        o_c = jnp.einsum('bghqc,bcgd->bqghd', p_c.astype(vcc.dtype), vcc)
        imp = jnp.einsum('bghqc,cj->bgqj', p_c, m_sel)
        cur = t // SEL_BLOCK
        valid = blk_ids[None, :] <= cur[:, None]
        forced = (blk_ids[None, :] == 0) | (blk_ids[None, :] == cur[:, None]) | (blk_ids[None, :] == cur[:, None] - 1)
        score = jnp.where(valid, jnp.where(forced, 1e4, imp), -1.0)
        _, sel = lax.top_k(score, topk)
        kg = ks_blk[bi, gi, sel]
        vg = vs_blk[bi, gi, sel]
        kpos = sel[..., None] * SEL_BLOCK + jnp.arange(SEL_BLOCK)
        m_s = (kpos <= t[None, None, :, None, None]).reshape(b, G, 1, Q_BLOCK, topk * SEL_BLOCK)
        s_s = jnp.einsum('bqghd,bgqnrd->bghqnr', qb, kg).reshape(b, G, HPG, Q_BLOCK, topk * SEL_BLOCK) * scale
        p_s = masked_softmax(s_s, m_s).reshape(b, G, HPG, Q_BLOCK, topk, SEL_BLOCK)
        o_s = jnp.einsum('bghqnr,bgqnrd->bqghd', p_s.astype(vg.dtype), vg)
        kwb = lax.dynamic_slice_in_dim(kw_pad, s0, Q_BLOCK + WINDOW, axis=1)
        vwb = lax.dynamic_slice_in_dim(vw_pad, s0, Q_BLOCK + WINDOW, axis=1)
        kp = s0 - WINDOW + jnp.arange(Q_BLOCK + WINDOW)
        diff = t[:, None] - kp[None, :]
        m_w = (kp[None, :] >= 0) & (diff >= 0) & (diff < WINDOW)
        s_w = jnp.einsum('bqghd,bkgd->bghqk', qb, kwb) * scale
        p_w = masked_softmax(s_w, m_w)
        o_w = jnp.einsum('bghqk,bkgd->bqghd', p_w.astype(vwb.dtype), vwb)
        o = gb[..., 0:1] * o_c + gb[..., 1:2] * o_s + gb[..., 2:3] * o_w
        return o.reshape(b, Q_BLOCK, NSA_HEADS * NSA_DV).astype(q.dtype)

    out = lax.map(block, jnp.arange(s // Q_BLOCK))
    return out.transpose(1, 0, 2, 3).reshape(b, s, NSA_HEADS * NSA_DV)


def mla_attention(c_q, c_kv, k_rope, g_q, w_uq, g_kv, w_uk, w_uv, cos, sin):
    b, s = c_q.shape[:2]
    q = (rmsnorm(c_q, g_q) @ w_uq).reshape(b, s, MLA_HEADS, MLA_NOPE + MLA_ROPE)
    q_nope = q[..., :MLA_NOPE]
    q_rope = apply_rope(q[..., MLA_NOPE:], cos, sin)
    ckv = rmsnorm(c_kv, g_kv)
    k_nope = (ckv @ w_uk).reshape(b, s, MLA_HEADS, MLA_NOPE)
    v = (ckv @ w_uv).reshape(b, s, MLA_HEADS, MLA_DV)
    k_r = apply_rope(k_rope, cos, sin)
    scale = (MLA_NOPE + MLA_ROPE) ** -0.5
    kpos = jnp.arange(s)

    def block(i):
        s0 = i * Q_BLOCK
        t = s0 + jnp.arange(Q_BLOCK)
        qn = lax.dynamic_slice_in_dim(q_nope, s0, Q_BLOCK, axis=1)
        qr = lax.dynamic_slice_in_dim(q_rope, s0, Q_BLOCK, axis=1)
        sc = (jnp.einsum('bqhd,bkhd->bhqk', qn, k_nope)
              + jnp.einsum('bqhd,bkd->bhqk', qr, k_r)) * scale
        p = masked_softmax(sc, kpos[None, :] <= t[:, None])
        o = jnp.einsum('bhqk,bkhd->bqhd', p.astype(v.dtype), v)
        return o.reshape(b, Q_BLOCK, MLA_HEADS * MLA_DV).astype(c_q.dtype)

    out = lax.map(block, jnp.arange(s // Q_BLOCK))
    return out.transpose(1, 0, 2, 3).reshape(b, s, MLA_HEADS * MLA_DV)


def memory_cross_attention(xn, memn, wq, wkv, wo):
    b, s = xn.shape[:2]
    m = memn.shape[1]
    q = (xn @ wq).reshape(b, s, XA_HEADS, XA_DIM)
    k, v = jnp.split((memn @ wkv).reshape(m * 0 + b, m, 2, XA_HEADS, XA_DIM), 2, axis=2)
    k, v = k[:, :, 0], v[:, :, 0]
    sc = jnp.einsum('bqhd,bmhd->bhqm', q, k).astype(jnp.float32) * (XA_DIM ** -0.5)
    p = jax.nn.softmax(sc, axis=-1)
    o = jnp.einsum('bhqm,bmhd->bqhd', p.astype(v.dtype), v).reshape(b, s, XA_HEADS * XA_DIM)
    return (o @ wo).astype(xn.dtype)


def setup_inputs(seed: int = 0) -> dict:
    key = jax.random.key(seed)
    ks = jax.random.split(key, 32)
    L = DEPTH

    def nrm(k, shape, scale):
        return jax.random.normal(k, shape, jnp.float32) * scale

    def gain(k, shape):
        return 1.0 + 0.02 * jax.random.normal(k, shape, jnp.float32)

    return {
        "x": nrm(ks[0], (BATCH, SEQ, D_MODEL), 1.0),
        "mem": nrm(ks[1], (BATCH, MEM_LEN, D_MODEL), 1.0),
        "g_mix": gain(ks[2], (L, D_MODEL)),
        "w_in": nrm(ks[3], (L, D_MODEL, D_IN), D_MODEL ** -0.5),
        "cmp_pos_k": nrm(ks[4], (L, CMP_BLOCK, NSA_DK), 0.02),
        "cmp_w1_k": nrm(ks[5], (L, CMP_BLOCK * NSA_DK, NSA_DK), (CMP_BLOCK * NSA_DK) ** -0.5),
        "cmp_w2_k": nrm(ks[6], (L, NSA_DK, NSA_DK), NSA_DK ** -0.5),
        "cmp_pos_v": nrm(ks[7], (L, CMP_BLOCK, NSA_DV), 0.02),
        "cmp_w1_v": nrm(ks[8], (L, CMP_BLOCK * NSA_DV, NSA_DV), (CMP_BLOCK * NSA_DV) ** -0.5),
        "cmp_w2_v": nrm(ks[9], (L, NSA_DV, NSA_DV), NSA_DV ** -0.5),
        "mla_g_q": gain(ks[10], (L, MLA_Q_RANK)),
        "mla_w_uq": nrm(ks[11], (L, MLA_Q_RANK, MLA_HEADS * (MLA_NOPE + MLA_ROPE)), MLA_Q_RANK ** -0.5),
        "mla_g_kv": gain(ks[12], (L, MLA_KV_RANK)),
        "mla_w_uk": nrm(ks[13], (L, MLA_KV_RANK, MLA_HEADS * MLA_NOPE), MLA_KV_RANK ** -0.5),
        "mla_w_uv": nrm(ks[14], (L, MLA_KV_RANK, MLA_HEADS * MLA_DV), MLA_KV_RANK ** -0.5),
        "w_o_nsa": nrm(ks[15], (L, NSA_HEADS * NSA_DV, D_MODEL), (NSA_HEADS * NSA_DV) ** -0.5),
        "w_o_mla": nrm(ks[16], (L, MLA_HEADS * MLA_DV, D_MODEL), (MLA_HEADS * MLA_DV) ** -0.5),
        "w_out": nrm(ks[17], (L, D_MODEL, D_MODEL), D_MODEL ** -0.5),
        "g_xattn": gain(ks[18], (L, D_MODEL)),
        "g_mem": gain(ks[19], (L, D_MODEL)),
        "xa_wq": nrm(ks[20], (L, D_MODEL, XA_HEADS * XA_DIM), D_MODEL ** -0.5),
        "xa_wkv": nrm(ks[21], (L, D_MODEL, 2 * XA_HEADS * XA_DIM), D_MODEL ** -0.5),
        "xa_wo": nrm(ks[22], (L, XA_HEADS * XA_DIM, D_MODEL), (XA_HEADS * XA_DIM) ** -0.5),
        "g_mlp": gain(ks[23], (L, D_MODEL)),
        "w_ff1": nrm(ks[24], (L, D_MODEL, D_FF), D_MODEL ** -0.5),
        "w_ff2": nrm(ks[25], (L, D_FF, D_MODEL), D_FF ** -0.5),
        "g_final": gain(ks[26], (D_MODEL,)),
    }


def reference(x, mem, g_mix, w_in, cmp_pos_k, cmp_w1_k, cmp_w2_k, cmp_pos_v, cmp_w1_v, cmp_w2_v,
              mla_g_q, mla_w_uq, mla_g_kv, mla_w_uk, mla_w_uv, w_o_nsa, w_o_mla, w_out,
              g_xattn, g_mem, xa_wq, xa_wkv, xa_wo, g_mlp, w_ff1, w_ff2, g_final):
    b, s, _ = x.shape
    cos_a, sin_a = rope_tables(s, NSA_ROT)
    cos_b, sin_b = rope_tables(s, MLA_ROPE)
    bounds = [int(v) for v in np.cumsum(SPLITS)[:-1]]
    G = NSA_KV_HEADS
    h = x
    for l in range(DEPTH):
        xn = rmsnorm(h, g_mix[l])
        (q_a, k_c, v_c, k_s, v_s, k_w, v_w, g_nsa, c_q, c_kv, k_r,
         gate_a, gate_b) = jnp.split(xn @ w_in[l], bounds, axis=-1)
        q_a = partial_rope(q_a.reshape(b, s, NSA_HEADS, NSA_DK), cos_a, sin_a)
        k_c = partial_rope(k_c.reshape(b, s, G, NSA_DK), cos_a, sin_a)
        k_s = partial_rope(k_s.reshape(b, s, G, NSA_DK), cos_a, sin_a)
        k_w = partial_rope(k_w.reshape(b, s, G, NSA_DK), cos_a, sin_a)
        v_c = v_c.reshape(b, s, G, NSA_DV)
        v_s = v_s.reshape(b, s, G, NSA_DV)
        v_w = v_w.reshape(b, s, G, NSA_DV)
        g_nsa = jax.nn.sigmoid(g_nsa.reshape(b, s, NSA_HEADS, 3))
        o_a = nsa_attention(q_a, k_c, v_c, k_s, v_s, k_w, v_w, g_nsa,
                            cmp_pos_k[l], cmp_w1_k[l], cmp_w2_k[l],
                            cmp_pos_v[l], cmp_w1_v[l], cmp_w2_v[l])
        o_b = mla_attention(c_q, c_kv, k_r, mla_g_q[l], mla_w_uq[l], mla_g_kv[l],
                            mla_w_uk[l], mla_w_uv[l], cos_b, sin_b)
        mixed = (jax.nn.sigmoid(gate_a) * (o_a @ w_o_nsa[l])
                 + jax.nn.sigmoid(gate_b) * (o_b @ w_o_mla[l]))
        h = h + mixed @ w_out[l]
        h = h + memory_cross_attention(rmsnorm(h, g_xattn[l]), rmsnorm(mem, g_mem[l]),
                                       xa_wq[l], xa_wkv[l], xa_wo[l])
        hn = rmsnorm(h, g_mlp[l])
        h = h + jnp.square(jax.nn.relu(hn @ w_ff1[l])) @ w_ff2[l]
    return rmsnorm(h, g_final)
```

```python
import functools

import numpy as np
import jax
import jax.numpy as jnp
from jax import lax
from jax.experimental import pallas as pl
from jax.experimental.pallas import tpu as pltpu

F32 = jnp.float32
BF16 = jnp.bfloat16

D_MODEL = 2048
MEM_LEN = 256
ROPE_THETA = 500000.0
EPS = 1e-6
NEG = -1e30

NSA_HEADS = 16
NSA_KV_HEADS = 4
NSA_HPG = NSA_HEADS // NSA_KV_HEADS
NSA_DK = 96
NSA_DV = 64
NSA_ROT = NSA_DK // 4
CMP_BLOCK = 32
CMP_STRIDE = 16
SEL_BLOCK = 64
SEL_TOPK = 16
WINDOW = 512

MLA_HEADS = 16
MLA_NOPE = 64
MLA_ROPE = 32
MLA_DV = 64
MLA_Q_RANK = 512
MLA_KV_RANK = 256

XA_HEADS = 4
XA_DIM = 128
D_FF = 4 * D_MODEL

SPLITS = (NSA_HEADS * NSA_DK,
          NSA_KV_HEADS * NSA_DK, NSA_KV_HEADS * NSA_DV,
          NSA_KV_HEADS * NSA_DK, NSA_KV_HEADS * NSA_DV,
          NSA_KV_HEADS * NSA_DK, NSA_KV_HEADS * NSA_DV,
          NSA_HEADS * 3,
          MLA_Q_RANK, MLA_KV_RANK, MLA_ROPE,
          D_MODEL, D_MODEL)

LANES = 128
HALF = LANES // 2
VMEM_LIMIT = 56 * 1024 * 1024

QK_Q0 = 0
QK_KC0 = NSA_HEADS
QK_KS0 = NSA_HEADS + NSA_KV_HEADS
QK_KW0 = NSA_HEADS + 2 * NSA_KV_HEADS
QK_HEADS = NSA_HEADS + 3 * NSA_KV_HEADS


def _cparams(*sem):
    return pltpu.CompilerParams(dimension_semantics=sem, vmem_limit_bytes=VMEM_LIMIT)


def _tiles(seq):
    row = min(512, seq)
    return dict(
        tm=row,
        tq=min(512, seq),
        tk=min(512, seq),
        tq_cmp=min(128, seq),
        tf=512,
    )


def _rms(x, g):
    return x * lax.rsqrt(jnp.mean(x * x, axis=-1, keepdims=True) + EPS) * g


def _rmsnorm_kernel(x_ref, g_ref, o_ref):
    o_ref[...] = _rms(x_ref[...], g_ref[...]).astype(o_ref.dtype)


def rmsnorm_rows(x, g, tm):
    n, d = x.shape
    return pl.pallas_call(
        _rmsnorm_kernel,
        out_shape=jax.ShapeDtypeStruct((n, d), BF16),
        grid=(n // tm,),
        in_specs=[pl.BlockSpec((tm, d), lambda i: (i, 0)),
                  pl.BlockSpec((1, d), lambda i: (0, 0))],
        out_specs=pl.BlockSpec((tm, d), lambda i: (i, 0)),
        compiler_params=_cparams("parallel"),
        name="rmsnorm_rows",
    )(x, g.reshape(1, d))


def _proj_kernel(*refs, has_gain, has_rope, roll_shift, has_add, has_res, act, head_major):
    it = iter(refs)
    a_ref = next(it)
    g_ref = next(it) if has_gain else None
    w_ref = next(it)
    cos_ref = next(it) if has_rope else None
    sin_ref = next(it) if has_rope else None
    add_ref = next(it) if has_add else None
    res_ref = next(it) if has_res else None
    o_ref = next(it)

    a = a_ref[...]
    if has_gain:
        a = _rms(a, g_ref[...]).astype(BF16)
    y = jnp.dot(a, w_ref[...], preferred_element_type=F32)
    if has_rope or has_add or head_major:
        for h in range(y.shape[1] // LANES):
            yh = y[:, h * LANES:(h + 1) * LANES]
            if has_rope:
                yh = yh * cos_ref[...] + pltpu.roll(yh, roll_shift, 1) * sin_ref[...]
            if has_add:
                yh = yh + add_ref[...].astype(F32)
            if head_major:
                o_ref[h] = yh.astype(o_ref.dtype)
            else:
                o_ref[:, h * LANES:(h + 1) * LANES] = yh.astype(o_ref.dtype)
    else:
        if act == "sigmoid":
            y = jax.nn.sigmoid(y)
        if has_res:
            y = res_ref[...] + y
        o_ref[...] = y.astype(o_ref.dtype)


def projection(a, w, *, tm, tn, out_dtype, a_cols=None, gain=None, rope=None, add=None,
               res=None, act=None, head_major=None, name="projection"):
    n = a.shape[0]
    k, nc = w.shape
    a_w, a_blk = a_cols if a_cols is not None else (a.shape[1], 0)
    assert a_w == k and n % tm == 0 and nc % tn == 0
    grid = (n // tm, nc // tn)
    in_specs = [pl.BlockSpec((tm, k), lambda i, j: (i, a_blk))]
    args = [a]
    if gain is not None:
        in_specs.append(pl.BlockSpec((1, k), lambda i, j: (0, 0)))
        args.append(gain.reshape(1, k))
    in_specs.append(pl.BlockSpec((k, tn), lambda i, j: (0, j)))
    args.append(w)
    roll_shift = 0
    if rope is not None:
        cos, sin, roll_shift = rope
        s_tiles = cos.shape[0] // tm
        for t in (cos, sin):
            in_specs.append(pl.BlockSpec((tm, LANES), lambda i, j: (i % s_tiles, 0)))
            args.append(t)
    if add is not None:
        in_specs.append(pl.BlockSpec((tm, LANES), lambda i, j: (i, 0)))
        args.append(add)
    if res is not None:
        in_specs.append(pl.BlockSpec((tm, tn), lambda i, j: (i, j)))
        args.append(res)
    if head_major is not None:
        b, s = head_major
        s_t = s // tm
        hpt = tn // LANES
        out_shape = jax.ShapeDtypeStruct((b, nc // LANES, s, LANES), out_dtype)
        out_spec = pl.BlockSpec((None, hpt, tm, LANES), lambda i, j: (i // s_t, j, i % s_t, 0))
    else:
        out_shape = jax.ShapeDtypeStruct((n, nc), out_dtype)
        out_spec = pl.BlockSpec((tm, tn), lambda i, j: (i, j))
    kern = functools.partial(
        _proj_kernel, has_gain=gain is not None, has_rope=rope is not None, roll_shift=roll_shift,
        has_add=add is not None, has_res=res is not None, act=act, head_major=head_major is not None)
    return pl.pallas_call(
        kern, out_shape=out_shape, grid=grid, in_specs=in_specs, out_specs=out_spec,
        compiler_params=_cparams("parallel", "arbitrary"), name=name,
    )(*args)


def _compress_kernel(r_ref, pa_ref, pb_ref, w1a_ref, w1b_ref, w2_ref, o_ref):
    r = r_ref[...].astype(F32)
    a = jnp.dot((r + pa_ref[...]).astype(BF16), w1a_ref[...], preferred_element_type=F32)
    b = jnp.dot((r + pb_ref[...]).astype(BF16), w1b_ref[...], preferred_element_type=F32)
    nr = a.shape[0]
    hid = a + pltpu.roll(b, nr - 1, 0)
    hid = jax.nn.gelu(hid)
    o_ref[...] = jnp.dot(hid.astype(BF16), w2_ref[...], preferred_element_type=F32).astype(o_ref.dtype)


def compress(r, head0, pos_a, pos_b, w1a, w1b, w2):
    b, _, nr, kk = r.shape
    g = NSA_KV_HEADS
    full = lambda shape: pl.BlockSpec(shape, lambda bi, gi: (0,) * len(shape))
    return pl.pallas_call(
        _compress_kernel,
        out_shape=jax.ShapeDtypeStruct((b, g, nr, LANES), BF16),
        grid=(b, g),
        in_specs=[pl.BlockSpec((None, None, nr, kk), lambda bi, gi: (bi, head0 + gi, 0, 0)),
                  full((1, kk)), full((1, kk)), full((kk, LANES)), full((kk, LANES)),
                  full((LANES, LANES))],
        out_specs=pl.BlockSpec((None, None, nr, LANES), lambda bi, gi: (bi, gi, 0, 0)),
        compiler_params=_cparams("parallel", "parallel"),
        name="nsa_compress",
    )(r, pos_a, pos_b, w1a, w1b, w2)


def _masked_softmax(s, mask):
    s = jnp.where(mask, s, NEG)
    m = jnp.max(s, axis=-1, keepdims=True)
    e = jnp.where(mask, jnp.exp(s - m), 0.0)
    return e / jnp.maximum(jnp.sum(e, axis=-1, keepdims=True), 1e-30)


def _dot_nt(a, b):
    return lax.dot_general(a, b, (((1,), (1,)), ((), ())), preferred_element_type=F32)


def _split_dot(x, w):
    hi = x.astype(BF16)
    lo = (x - hi.astype(F32)).astype(BF16)
    return (jnp.dot(hi, w, preferred_element_type=F32) + jnp.dot(lo, w, preferred_element_type=F32))


def _nsa_cmp_kernel(q_ref, kcc_ref, vcc_ref, msel_ref, glog_ref, e_ref, oc_ref, sel_ref,
                    *, tq, scale, topk):
    s0 = pl.program_id(2) * tq
    rows = NSA_HPG * tq
    q4 = q_ref[...].reshape(rows, LANES)
    s = _dot_nt(q4, kcc_ref[...]) * scale
    nr = s.shape[1]
    t = s0 + lax.rem(lax.broadcasted_iota(jnp.int32, (rows, nr), 0), tq)
    cend = lax.broadcasted_iota(jnp.int32, (rows, nr), 1) * CMP_STRIDE + (CMP_BLOCK - 1)
    p = _masked_softmax(s, cend <= t)
    o = jnp.dot(p.astype(BF16), vcc_ref[...], preferred_element_type=F32)

    psum = p[0:tq]
    for h in range(1, NSA_HPG):
        psum = psum + p[h * tq:(h + 1) * tq]
    imp = _split_dot(psum, msel_ref[...])
    blk = lax.broadcasted_iota(jnp.int32, (tq, LANES), 1)
    cur = (s0 + lax.broadcasted_iota(jnp.int32, (tq, LANES), 0)) // SEL_BLOCK
    valid = blk <= cur
    forced = (blk == 0) | (blk == cur) | (blk == cur - 1)
    score = jnp.where(valid, jnp.where(forced, 1e4, imp), -1.0)
    sc = score.T
    rowid = lax.broadcasted_iota(jnp.int32, (LANES, tq), 0).astype(F32)

    def pick_one(_, carry):
        sc, sel = carry
        cm = jnp.max(sc, axis=0, keepdims=True)
        first = jnp.min(jnp.where(sc == cm, rowid, float(LANES)), axis=0, keepdims=True)
        hit = rowid == first
        return jnp.where(hit, -jnp.inf, sc), jnp.where(hit, 1.0, sel)

    _, sel_t = lax.fori_loop(0, topk, pick_one, (sc, jnp.zeros((LANES, tq), F32)))
    sel_ref[...] = sel_t.T.astype(sel_ref.dtype)

    glog = glog_ref[...]
    lane = lax.broadcasted_iota(jnp.int32, (tq, LANES), 1)
    for pr in range(NSA_HPG // 2):
        gate = jax.nn.sigmoid(_split_dot(glog, e_ref[pr]))
        pair = jnp.where(lane < HALF, o[(2 * pr) * tq:(2 * pr + 1) * tq],
                         o[(2 * pr + 1) * tq:(2 * pr + 2) * tq])
        oc_ref[:, pr * LANES:(pr + 1) * LANES] = (gate * pair).astype(oc_ref.dtype)


def nsa_compressed(qk, kcc, vcc, msel, misc, glog_blk, expand, *, batch, seq, tq):
    g = NSA_KV_HEADS
    nq = seq // tq
    nr = kcc.shape[2]
    topk = min(SEL_TOPK, seq // SEL_BLOCK)
    kern = functools.partial(_nsa_cmp_kernel, tq=tq, scale=NSA_DK ** -0.5, topk=topk)
    return pl.pallas_call(
        kern,
        out_shape=(jax.ShapeDtypeStruct((batch * seq, NSA_HEADS * NSA_DV), F32),
                   jax.ShapeDtypeStruct((batch, g, seq, LANES), BF16)),
        grid=(batch, g, nq),
        in_specs=[
            pl.BlockSpec((None, NSA_HPG, tq, LANES), lambda b, gi, qi: (b, gi, qi, 0)),
            pl.BlockSpec((None, None, nr, LANES), lambda b, gi, qi: (b, gi, 0, 0)),
            pl.BlockSpec((None, None, nr, LANES), lambda b, gi, qi: (b, gi, 0, 0)),
            pl.BlockSpec((nr, LANES), lambda b, gi, qi: (0, 0)),
            pl.BlockSpec((tq, LANES), lambda b, gi, qi: (b * nq + qi, glog_blk)),
            pl.BlockSpec((None, 2, LANES, LANES), lambda b, gi, qi: (0, gi, 0, 0)),
        ],
        out_specs=(pl.BlockSpec((tq, 2 * LANES), lambda b, gi, qi: (b * nq + qi, gi)),
                   pl.BlockSpec((None, None, tq, LANES), lambda b, gi, qi: (b, gi, qi, 0))),
        compiler_params=_cparams("parallel", "parallel", "parallel"),
        name="nsa_compressed_select",
    )(qk, kcc, vcc, msel, misc, expand)


def _flash_kernel(*refs, mode, tq, tk, scale, kv_shared, nback, gated, nk):
    it = iter(refs)
    q_ref = next(it)
    k_ref = next(it)
    v_ref = next(it)
    sel_ref = next(it) if mode == "select" else None
    ex_ref = next(it) if mode == "select" else None
    glog_ref = next(it) if gated else None
    e_ref = next(it) if gated else None
    o_ref = next(it)
    m_sc, l_sc, acc_sc = next(it), next(it), next(it)

    qi = pl.program_id(2)
    kj = pl.program_id(3)
    s0 = qi * tq
    kt = qi * (tq // tk) - nback + kj if mode == "window" else kj
    k0 = kt * tk

    @pl.when(kj == 0)
    def _init():
        m_sc[...] = jnp.full(m_sc.shape, NEG, F32)
        l_sc[...] = jnp.zeros(l_sc.shape, F32)
        acc_sc[...] = jnp.zeros(acc_sc.shape, F32)

    needed = k0 <= s0 + tq - 1
    if mode == "window":
        needed = needed & (kt >= 0)

    @pl.when(needed)
    def _compute():
        kpos = k0 + lax.broadcasted_iota(jnp.int32, (tq, tk), 1)
        tpos = s0 + lax.broadcasted_iota(jnp.int32, (tq, tk), 0)
        if mode == "window":
            d = tpos - kpos
            mask = (d >= 0) & (d < WINDOW)
        else:
            mask = kpos <= tpos
            if mode == "select":
                chosen = jnp.dot(sel_ref[...], ex_ref[...], preferred_element_type=F32)
                mask = mask & (chosen > 0.5)
        v = v_ref[...]
        lane_v = lax.broadcasted_iota(jnp.int32, v.shape, 1)
        zero = jnp.zeros_like(v)
        v_half = (jnp.where(lane_v < HALF, v, zero), jnp.where(lane_v >= HALF, v, zero))
        reps = tk // LANES
        alphas, pvs = [], []
        for hh in range(2):
            q = q_ref[hh]
            k = k_ref[0 if kv_shared else hh]
            s = jnp.where(mask, _dot_nt(q, k) * scale, NEG)
            m_old = m_sc[hh]
            m_new = jnp.maximum(m_old, jnp.max(s, axis=-1, keepdims=True))
            alpha = jnp.exp(m_old - m_new)
            p = jnp.where(mask, jnp.exp(s - jnp.concatenate([m_new] * reps, axis=1)), 0.0)
            l_sc[hh] = alpha * l_sc[hh] + jnp.sum(p, axis=-1, keepdims=True)
            m_sc[hh] = m_new
            alphas.append(alpha)
            pvs.append(jnp.dot(p.astype(BF16), v_half[hh], preferred_element_type=F32))
        lane = lax.broadcasted_iota(jnp.int32, (tq, LANES), 1)
        acc_sc[...] = acc_sc[...] * jnp.where(lane < HALF, alphas[0], alphas[1]) + pvs[0] + pvs[1]

    @pl.when(kj == nk - 1)
    def _finish():
        lane = lax.broadcasted_iota(jnp.int32, (tq, LANES), 1)
        l_wide = jnp.where(lane < HALF, l_sc[0], l_sc[1])
        out = acc_sc[...] / jnp.maximum(l_wide, 1e-30)
        if gated:
            out = jax.nn.sigmoid(_split_dot(glog_ref[...], e_ref[...])) * out
        o_ref[...] = out.astype(o_ref.dtype)


def flash_pairs(q, k, v, *, mode, batch, seq, tq, tk, scale, q_head0, k_head0, v_blk0,
                kv_shared, out_dtype, sel=None, ex=None, gate=None, name="flash"):
    npairs = NSA_HEADS // 2
    nq = seq // tq
    assert tq % tk == 0
    if mode == "window":
        nback = -(-(WINDOW - 1) // tk)
        nk = nback + tq // tk
        k_tile = lambda qi, kj: jnp.maximum(qi * (tq // tk) - nback + kj, 0)
    else:
        nback = 0
        nk = seq // tk
        k_tile = lambda qi, kj: jnp.minimum(kj, (qi * tq + tq - 1) // tk)
    nkt = seq // tk
    if kv_shared:
        k_spec = pl.BlockSpec((None, 1, tk, LANES),
                              lambda b, hp, qi, kj: (b, k_head0 + hp // 2, k_tile(qi, kj), 0))
        v_spec = pl.BlockSpec((tk, LANES),
                              lambda b, hp, qi, kj: (b * nkt + k_tile(qi, kj), v_blk0 + hp // 2))
    else:
        k_spec = pl.BlockSpec((None, 2, tk, LANES),
                              lambda b, hp, qi, kj: (b, k_head0 // 2 + hp, k_tile(qi, kj), 0))
        v_spec = pl.BlockSpec((tk, LANES),
                              lambda b, hp, qi, kj: (b * nkt + k_tile(qi, kj), v_blk0 + hp))
    in_specs = [pl.BlockSpec((None, 2, tq, LANES), lambda b, hp, qi, kj: (b, q_head0 // 2 + hp, qi, 0)),
                k_spec, v_spec]
    args = [q, k, v]
    if mode == "select":
        in_specs += [pl.BlockSpec((None, None, tq, LANES), lambda b, hp, qi, kj: (b, hp // 2, qi, 0)),
                     pl.BlockSpec((None, LANES, tk), lambda b, hp, qi, kj: (k_tile(qi, kj), 0, 0))]
        args += [sel, ex]
    if gate is not None:
        misc, glog_blk, expand, branch = gate
        in_specs += [pl.BlockSpec((tq, LANES), lambda b, hp, qi, kj: (b * nq + qi, glog_blk)),
                     pl.BlockSpec((None, None, LANES, LANES), lambda b, hp, qi, kj: (branch, hp, 0, 0))]
        args += [misc, expand]
    kern = functools.partial(_flash_kernel, mode=mode, tq=tq, tk=tk, scale=scale, kv_shared=kv_shared,
                             nback=nback, gated=gate is not None, nk=nk)
    return pl.pallas_call(
        kern,
        out_shape=jax.ShapeDtypeStruct((batch * seq, npairs * LANES), out_dtype),
        grid=(batch, npairs, nq, nk),
        in_specs=in_specs,
        out_specs=pl.BlockSpec((tq, LANES), lambda b, hp, qi, kj: (b * nq + qi, hp)),
        scratch_shapes=[pltpu.VMEM((2, tq, LANES), F32), pltpu.VMEM((2, tq, LANES), F32),
                        pltpu.VMEM((tq, LANES), F32)],
        compiler_params=_cparams("parallel", "parallel", "parallel", "arbitrary"),
        name=name,
    )(*args)


def _mix_kernel(oc_ref, os_ref, ow_ref, ob_ref, ga_ref, gb_ref, wa_ref, wb_ref, o_ref):
    oa = (oc_ref[...] + os_ref[...] + ow_ref[...]).astype(BF16)
    ya = jnp.dot(oa, wa_ref[...], preferred_element_type=F32)
    yb = jnp.dot(ob_ref[...], wb_ref[...], preferred_element_type=F32)
    o_ref[...] = (ga_ref[...].astype(F32) * ya + gb_ref[...].astype(F32) * yb).astype(o_ref.dtype)


def gated_mix(oc, osel, ow, ob, gates, wa, wb, *, tm, tn):
    n, ka = oc.shape
    d = wa.shape[1]
    nb = d // tn
    row = lambda w: pl.BlockSpec((tm, w), lambda i, j: (i, 0))
    return pl.pallas_call(
        _mix_kernel,
        out_shape=jax.ShapeDtypeStruct((n, d), BF16),
        grid=(n // tm, nb),
        in_specs=[row(ka), row(ka), row(ka), row(ob.shape[1]),
                  pl.BlockSpec((tm, tn), lambda i, j: (i, j)),
                  pl.BlockSpec((tm, tn), lambda i, j: (i, nb + j)),
                  pl.BlockSpec((ka, tn), lambda i, j: (0, j)),
                  pl.BlockSpec((wb.shape[0], tn), lambda i, j: (0, j))],
        out_specs=pl.BlockSpec((tm, tn), lambda i, j: (i, j)),
        compiler_params=_cparams("parallel", "arbitrary"),
        name="gated_mix",
    )(oc, osel, ow, ob, gates, gates, wa, wb)


def _xattn_kernel(h_ref, g_ref, wq_ref, kv_ref, wo_ref, o_ref):
    h = h_ref[...]
    hn = _rms(h, g_ref[...]).astype(BF16)
    q = jnp.dot(hn, wq_ref[...], preferred_element_type=F32).astype(BF16)
    kv = kv_ref[...]
    outs = []
    for hd in range(XA_HEADS):
        qh = q[:, hd * XA_DIM:(hd + 1) * XA_DIM]
        kh = kv[:, hd * XA_DIM:(hd + 1) * XA_DIM]
        vh = kv[:, (XA_HEADS + hd) * XA_DIM:(XA_HEADS + hd + 1) * XA_DIM]
        s = _dot_nt(qh, kh) * (XA_DIM ** -0.5)
        e = jnp.exp(s - jnp.max(s, axis=-1, keepdims=True))
        p = e / jnp.sum(e, axis=-1, keepdims=True)
        outs.append(jnp.dot(p.astype(BF16), vh, preferred_element_type=F32))
    o = jnp.concatenate(outs, axis=1).astype(BF16)
    o_ref[...] = h + jnp.dot(o, wo_ref[...], preferred_element_type=F32)


def memory_xattn(h, g, wq, kv, wo, *, seq, tm):
    n, d = h.shape
    s_t = seq // tm
    dq = wq.shape[1]
    return pl.pallas_call(
        _xattn_kernel,
        out_shape=jax.ShapeDtypeStruct((n, d), F32),
        grid=(n // tm,),
        in_specs=[pl.BlockSpec((tm, d), lambda i: (i, 0)),
                  pl.BlockSpec((1, d), lambda i: (0, 0)),
                  pl.BlockSpec((d, dq), lambda i: (0, 0)),
                  pl.BlockSpec((MEM_LEN, 2 * dq), lambda i: (i // s_t, 0)),
                  pl.BlockSpec((dq, d), lambda i: (0, 0))],
        out_specs=pl.BlockSpec((tm, d), lambda i: (i, 0)),
        compiler_params=_cparams("parallel"),
        name="memory_xattn",
    )(h, g.reshape(1, d), wq, kv, wo)


def _mlp_kernel(h_ref, g_ref, w1_ref, w2_ref, gf_ref, o_ref, hn_sc, acc_sc, *, nf):
    f = pl.program_id(1)

    @pl.when(f == 0)
    def _init():
        hn_sc[...] = _rms(h_ref[...], g_ref[...]).astype(BF16)
        acc_sc[...] = jnp.zeros(acc_sc.shape, F32)

    u = jnp.dot(hn_sc[...], w1_ref[...], preferred_element_type=F32)
    u = jnp.square(jnp.maximum(u, 0.0))
    acc_sc[...] += jnp.dot(u.astype(BF16), w2_ref[...], preferred_element_type=F32)

    @pl.when(f == nf - 1)
    def _finish():
        o_ref[...] = _rms(h_ref[...] + acc_sc[...], gf_ref[...])


def mlp_final(h, g, w1, w2, gf, *, tm, tf):
    n, d = h.shape
    dff = w1.shape[1]
    nf = dff // tf
    return pl.pallas_call(
        functools.partial(_mlp_kernel, nf=nf),
        out_shape=jax.ShapeDtypeStruct((n, d), F32),
        grid=(n // tm, nf),
        in_specs=[pl.BlockSpec((tm, d), lambda i, f: (i, 0)),
                  pl.BlockSpec((1, d), lambda i, f: (0, 0)),
                  pl.BlockSpec((d, tf), lambda i, f: (0, f)),
                  pl.BlockSpec((tf, d), lambda i, f: (f, 0)),
                  pl.BlockSpec((1, d), lambda i, f: (0, 0))],
        out_specs=pl.BlockSpec((tm, d), lambda i, f: (i, 0)),
        scratch_shapes=[pltpu.VMEM((tm, d), BF16), pltpu.VMEM((tm, d), F32)],
        compiler_params=_cparams("parallel", "arbitrary"),
        name="mlp_final",
    )(h, g.reshape(1, d), w1, w2, gf.reshape(1, d))


def _rot_partner(w, half):
    return jnp.concatenate([-w[..., half:], w[..., :half]], axis=-1)


def _pad_heads_rope(w, heads, hd, rot0, rot_dim):
    k = w.shape[0]
    w = w.reshape(k, heads, hd)
    partner = _rot_partner(w[:, :, rot0:rot0 + rot_dim], rot_dim // 2)
    pad = jnp.zeros((k, heads, LANES - hd - rot_dim), w.dtype)
    return jnp.concatenate([w, partner, pad], axis=-1).reshape(k, heads * LANES)


def _rope_lane_tables(seq, rot0, rot_dim, hd):
    inv = 1.0 / (ROPE_THETA ** (jnp.arange(0, rot_dim, 2, dtype=F32) / rot_dim))
    ang = jnp.arange(seq, dtype=F32)[:, None] * inv[None, :]
    cos = jnp.concatenate([jnp.cos(ang), jnp.cos(ang)], axis=1)
    sin = jnp.concatenate([jnp.sin(ang), jnp.sin(ang)], axis=1)
    cos_t = jnp.concatenate([jnp.ones((seq, rot0), F32), cos,
                             jnp.ones((seq, hd - rot0 - rot_dim), F32),
                             jnp.zeros((seq, LANES - hd), F32)], axis=1)
    sin_t = jnp.concatenate([jnp.zeros((seq, rot0), F32), sin,
                             jnp.zeros((seq, LANES - rot0 - rot_dim), F32)], axis=1)
    return cos_t, sin_t


def _dup_halves(w, groups):
    k = w.shape[0]
    w = w.reshape(k, groups, HALF)
    return jnp.concatenate([w, w], axis=-1).reshape(k, groups * LANES)


def _pad_lanes(w, groups, width):
    k = w.shape[0]
    w = w.reshape(k, groups, width)
    return jnp.pad(w, ((0, 0), (0, 0), (0, LANES - width))).reshape(k, groups * LANES)


def _compress_params(pos, w1, w2, d, dup_out):
    half = CMP_BLOCK // 2
    pos_p = jnp.pad(pos, ((0, 0), (0, LANES - d)))
    pos_a = pos_p[:half].reshape(1, half * LANES)
    pos_b = pos_p[half:].reshape(1, half * LANES)
    w1p = jnp.pad(w1.reshape(CMP_BLOCK, d, d), ((0, 0), (0, LANES - d), (0, LANES - d)))
    w1a = w1p[:half].reshape(half * LANES, LANES).astype(BF16)
    w1b = w1p[half:].reshape(half * LANES, LANES).astype(BF16)
    w2p = jnp.pad(w2, ((0, LANES - d), (0, LANES - d)))
    if dup_out:
        w2p = w2p.at[:d, HALF:HALF + d].set(w2)
    return pos_a, pos_b, w1a, w1b, w2p.astype(BF16)


def _cmp_to_sel(nr, nsb):
    cs = np.arange(nr) * CMP_STRIDE
    ce = cs + CMP_BLOCK
    ss = np.arange(LANES) * SEL_BLOCK
    se = ss + SEL_BLOCK
    ov = np.clip(np.minimum(ce[:, None], se[None, :]) - np.maximum(cs[:, None], ss[None, :]), 0, None)
    ov = ov.astype(np.float32) / np.float32(CMP_BLOCK)
    ov[:, nsb:] = 0.0
    ov[nr - 1:, :] = 0.0
    return jnp.asarray(ov, BF16)


def _gate_expand():
    e = np.zeros((3, NSA_HEADS // 2, LANES, LANES), np.float32)
    for br in range(3):
        for hp in range(NSA_HEADS // 2):
            for hh in range(2):
                e[br, hp, 3 * (2 * hp + hh) + br, hh * HALF:(hh + 1) * HALF] = 1.0
    return jnp.asarray(e, BF16)


def _block_expand(seq, tk):
    nkt = seq // tk
    kpos = np.arange(nkt)[:, None] * tk + np.arange(tk)[None, :]
    e = (np.arange(LANES)[None, :, None] == (kpos // SEL_BLOCK)[:, None, :])
    return jnp.asarray(e.astype(np.float32), BF16)


def kernel(x, mem, g_mix, w_in, cmp_pos_k, cmp_w1_k, cmp_w2_k, cmp_pos_v, cmp_w1_v, cmp_w2_v,
           mla_g_q, mla_w_uq, mla_g_kv, mla_w_uk, mla_w_uv, w_o_nsa, w_o_mla, w_out,
           g_xattn, g_mem, xa_wq, xa_wkv, xa_wo, g_mlp, w_ff1, w_ff2, g_final):
    b, s, d = x.shape
    assert d == D_MODEL and s % (CMP_STRIDE * 8) == 0 and s // SEL_BLOCK <= LANES
    assert g_mix.shape[0] == 1
    n = b * s
    T = _tiles(s)
    tm, tq, tk = T["tm"], T["tq"], T["tk"]
    G = NSA_KV_HEADS
    bounds = [int(v) for v in np.cumsum(SPLITS)[:-1]]

    (w_qa, w_kc, w_vc, w_ks, w_vs, w_kw, w_vw, w_gn, w_cq, w_ckv, w_kr,
     w_ga, w_gb) = jnp.split(w_in[0], bounds, axis=1)
    nsa_rope = lambda w, heads: _pad_heads_rope(w, heads, NSA_DK, 0, NSA_ROT)
    w_rope = jnp.concatenate([nsa_rope(w_qa, NSA_HEADS), nsa_rope(w_kc, G), nsa_rope(w_ks, G),
                              nsa_rope(w_kw, G)], axis=1).astype(BF16)
    w_krp = jnp.concatenate([jnp.zeros((d, MLA_NOPE), F32), w_kr, _rot_partner(w_kr, MLA_ROPE // 2)],
                            axis=1).astype(BF16)
    w_vsw = jnp.concatenate([_dup_halves(w_vs, G), _dup_halves(w_vw, G)], axis=1).astype(BF16)
    w_vcp = _pad_lanes(w_vc, G, NSA_DV).astype(BF16)
    w_misc = jnp.concatenate([w_cq, w_ckv, jnp.pad(w_gn, ((0, 0), (0, LANES - w_gn.shape[1])))],
                             axis=1).astype(BF16)
    glog_blk = (MLA_Q_RANK + MLA_KV_RANK) // LANES
    w_gates = jnp.concatenate([w_ga, w_gb], axis=1).astype(BF16)
    cos_a, sin_a = _rope_lane_tables(s, 0, NSA_ROT, NSA_DK)
    cos_b, sin_b = _rope_lane_tables(s, MLA_NOPE, MLA_ROPE, MLA_NOPE + MLA_ROPE)
    shift_a = LANES - NSA_DK
    shift_b = LANES - (NSA_DK - MLA_NOPE)
    w_uq = _pad_heads_rope(mla_w_uq[0], MLA_HEADS, MLA_NOPE + MLA_ROPE, MLA_NOPE, MLA_ROPE).astype(BF16)
    w_uk = _pad_lanes(mla_w_uk[0], MLA_HEADS, MLA_NOPE).astype(BF16)
    w_uv = mla_w_uv[0].astype(BF16)

    x2 = x.reshape(n, d)
    xn = rmsnorm_rows(x2, g_mix[0], tm)
    qk = projection(xn, w_rope, tm=tm, tn=512, out_dtype=BF16, rope=(cos_a, sin_a, shift_a),
                    head_major=(b, s), name="proj_qk_rope")
    krp = projection(xn, w_krp, tm=tm, tn=LANES, out_dtype=BF16, rope=(cos_b, sin_b, shift_b),
                     name="proj_k_rope_mla")
    vsw = projection(xn, w_vsw, tm=tm, tn=512, out_dtype=BF16, name="proj_v")
    vcp = projection(xn, w_vcp, tm=tm, tn=512, out_dtype=BF16, head_major=(b, s), name="proj_vc")
    misc = projection(xn, w_misc, tm=tm, tn=w_misc.shape[1], out_dtype=F32, name="proj_misc")
    gates = projection(xn, w_gates, tm=tm, tn=512, out_dtype=BF16, act="sigmoid", name="proj_gates")

    nr = s // CMP_STRIDE
    pk = _compress_params(cmp_pos_k[0], cmp_w1_k[0], cmp_w2_k[0], NSA_DK, False)
    pv = _compress_params(cmp_pos_v[0], cmp_w1_v[0], cmp_w2_v[0], NSA_DV, True)
    kcc = compress(qk.reshape(b, QK_HEADS, nr, CMP_STRIDE * LANES), QK_KC0, *pk)
    vcc = compress(vcp.reshape(b, G, nr, CMP_STRIDE * LANES), 0, *pv)
    expand = _gate_expand()
    o_c, sel = nsa_compressed(qk, kcc, vcc, _cmp_to_sel(nr, s // SEL_BLOCK), misc, glog_blk, expand,
                              batch=b, seq=s, tq=T["tq_cmp"])
    nsa_scale = NSA_DK ** -0.5
    o_s = flash_pairs(qk, qk, vsw, mode="select", batch=b, seq=s, tq=tq, tk=tk, scale=nsa_scale,
                      q_head0=QK_Q0, k_head0=QK_KS0, v_blk0=0, kv_shared=True, out_dtype=F32,
                      sel=sel, ex=_block_expand(s, tk), gate=(misc, glog_blk, expand, 1),
                      name="nsa_selected")
    o_w = flash_pairs(qk, qk, vsw, mode="window", batch=b, seq=s, tq=tq, tk=tk, scale=nsa_scale,
                      q_head0=QK_Q0, k_head0=QK_KW0, v_blk0=G, kv_shared=True, out_dtype=F32,
                      gate=(misc, glog_blk, expand, 2), name="nsa_window")

    qm = projection(misc, w_uq, tm=tm, tn=512, out_dtype=BF16, a_cols=(MLA_Q_RANK, 0), gain=mla_g_q[0],
                    rope=(cos_b, sin_b, shift_b), head_major=(b, s), name="mla_q")
    km = projection(misc, w_uk, tm=tm, tn=512, out_dtype=BF16,
                    a_cols=(MLA_KV_RANK, MLA_Q_RANK // MLA_KV_RANK), gain=mla_g_kv[0], add=krp,
                    head_major=(b, s), name="mla_k")
    vm = projection(misc, w_uv, tm=tm, tn=512, out_dtype=BF16,
                    a_cols=(MLA_KV_RANK, MLA_Q_RANK // MLA_KV_RANK), gain=mla_g_kv[0], name="mla_v")
    o_b = flash_pairs(qm, km, vm, mode="causal", batch=b, seq=s, tq=tq, tk=tk,
                      scale=(MLA_NOPE + MLA_ROPE) ** -0.5, q_head0=0, k_head0=0, v_blk0=0,
                      kv_shared=False, out_dtype=BF16, name="mla_attention")

    mixed = gated_mix(o_c, o_s, o_w, o_b, gates, w_o_nsa[0].astype(BF16), w_o_mla[0].astype(BF16),
                      tm=tm, tn=512)
    h1 = projection(mixed, w_out[0].astype(BF16), tm=tm, tn=512, out_dtype=F32, res=x2, name="proj_out")

    kv_mem = projection(mem.reshape(b * MEM_LEN, d), xa_wkv[0].astype(BF16), tm=MEM_LEN, tn=512,
                        out_dtype=BF16, gain=g_mem[0], name="xattn_kv")
    h2 = memory_xattn(h1, g_xattn[0], xa_wq[0].astype(BF16), kv_mem, xa_wo[0].astype(BF16), seq=s, tm=tm)

    out = mlp_final(h2, g_mlp[0], w_ff1[0].astype(BF16), w_ff2[0].astype(BF16), g_final, tm=tm, tf=T["tf"])
    return out.reshape(b, s, d)
```

```python
import functools

import numpy as np
import jax
import jax.numpy as jnp
from jax import lax
from jax.experimental import pallas as pl
from jax.experimental.pallas import tpu as pltpu

F32 = jnp.float32
BF16 = jnp.bfloat16

D_MODEL = 2048
MEM_LEN = 256
ROPE_THETA = 500000.0
EPS = 1e-6
NEG = -1e30

NSA_HEADS = 16
NSA_KV_HEADS = 4
NSA_HPG = NSA_HEADS // NSA_KV_HEADS
NSA_DK = 96
NSA_DV = 64
NSA_ROT = NSA_DK // 4
CMP_BLOCK = 32
CMP_STRIDE = 16
SEL_BLOCK = 64
SEL_TOPK = 16
WINDOW = 512

MLA_HEADS = 16
MLA_NOPE = 64
MLA_ROPE = 32
MLA_DV = 64
MLA_Q_RANK = 512
MLA_KV_RANK = 256

XA_HEADS = 4
XA_DIM = 128
D_FF = 4 * D_MODEL

SPLITS = (NSA_HEADS * NSA_DK,
          NSA_KV_HEADS * NSA_DK, NSA_KV_HEADS * NSA_DV,
          NSA_KV_HEADS * NSA_DK, NSA_KV_HEADS * NSA_DV,
          NSA_KV_HEADS * NSA_DK, NSA_KV_HEADS * NSA_DV,
          NSA_HEADS * 3,
          MLA_Q_RANK, MLA_KV_RANK, MLA_ROPE,
          D_MODEL, D_MODEL)

LANES = 128
HALF = LANES // 2
VMEM_LIMIT = 56 * 1024 * 1024

QK_Q0 = 0
QK_KC0 = NSA_HEADS
QK_KS0 = NSA_HEADS + NSA_KV_HEADS
QK_KW0 = NSA_HEADS + 2 * NSA_KV_HEADS
QK_HEADS = NSA_HEADS + 3 * NSA_KV_HEADS


def _cparams(*sem):
    return pltpu.CompilerParams(dimension_semantics=sem, vmem_limit_bytes=VMEM_LIMIT)


def _tiles(seq):
    row = min(512, seq)
    return dict(
        tm=row,
        tq=min(512, seq),
        tk=min(512, seq),
        rc=32,
        tq_cmp=min(128, seq),
        tf=512,
    )


def _rms(x, g):
    return x * lax.rsqrt(jnp.mean(x * x, axis=-1, keepdims=True) + EPS) * g


def _rmsnorm_kernel(x_ref, g_ref, o_ref):
    o_ref[...] = _rms(x_ref[...], g_ref[...]).astype(o_ref.dtype)


def rmsnorm_rows(x, g, tm):
    n, d = x.shape
    return pl.pallas_call(
        _rmsnorm_kernel,
        out_shape=jax.ShapeDtypeStruct((n, d), BF16),
        grid=(n // tm,),
        in_specs=[pl.BlockSpec((tm, d), lambda i: (i, 0)),
                  pl.BlockSpec((1, d), lambda i: (0, 0))],
        out_specs=pl.BlockSpec((tm, d), lambda i: (i, 0)),
        compiler_params=_cparams("parallel"),
        name="rmsnorm_rows",
    )(x, g.reshape(1, d))


def _proj_kernel(*refs, has_gain, has_rope, roll_shift, has_add, has_res, act, head_major):
    it = iter(refs)
    a_ref = next(it)
    g_ref = next(it) if has_gain else None
    w_ref = next(it)
    cos_ref = next(it) if has_rope else None
    sin_ref = next(it) if has_rope else None
    add_ref = next(it) if has_add else None
    res_ref = next(it) if has_res else None
    o_ref = next(it)

    a = a_ref[...]
    if has_gain:
        a = _rms(a, g_ref[...]).astype(BF16)
    y = jnp.dot(a, w_ref[...], preferred_element_type=F32)
    if has_rope or has_add or head_major:
        for h in range(y.shape[1] // LANES):
            yh = y[:, h * LANES:(h + 1) * LANES]
            if has_rope:
                yh = yh * cos_ref[...] + pltpu.roll(yh, roll_shift, 1) * sin_ref[...]
            if has_add:
                yh = yh + add_ref[...].astype(F32)
            if head_major:
                o_ref[h] = yh.astype(o_ref.dtype)
            else:
                o_ref[:, h * LANES:(h + 1) * LANES] = yh.astype(o_ref.dtype)
    else:
        if act == "sigmoid":
            y = jax.nn.sigmoid(y)
        if has_res:
            y = res_ref[...] + y
        o_ref[...] = y.astype(o_ref.dtype)


def projection(a, w, *, tm, tn, out_dtype, a_cols=None, gain=None, rope=None, add=None,
               res=None, act=None, head_major=None, name="projection"):
    n = a.shape[0]
    k, nc = w.shape
    a_w, a_blk = a_cols if a_cols is not None else (a.shape[1], 0)
    assert a_w == k and n % tm == 0 and nc % tn == 0
    grid = (n // tm, nc // tn)
    in_specs = [pl.BlockSpec((tm, k), lambda i, j: (i, a_blk))]
    args = [a]
    if gain is not None:
        in_specs.append(pl.BlockSpec((1, k), lambda i, j: (0, 0)))
        args.append(gain.reshape(1, k))
    in_specs.append(pl.BlockSpec((k, tn), lambda i, j: (0, j)))
    args.append(w)
    roll_shift = 0
    if rope is not None:
        cos, sin, roll_shift = rope
        s_tiles = cos.shape[0] // tm
        for t in (cos, sin):
            in_specs.append(pl.BlockSpec((tm, LANES), lambda i, j: (i % s_tiles, 0)))
            args.append(t)
    if add is not None:
        in_specs.append(pl.BlockSpec((tm, LANES), lambda i, j: (i, 0)))
        args.append(add)
    if res is not None:
        in_specs.append(pl.BlockSpec((tm, tn), lambda i, j: (i, j)))
        args.append(res)
    if head_major is not None:
        b, s = head_major
        s_t = s // tm
        hpt = tn // LANES
        out_shape = jax.ShapeDtypeStruct((b, nc // LANES, s, LANES), out_dtype)
        out_spec = pl.BlockSpec((None, hpt, tm, LANES), lambda i, j: (i // s_t, j, i % s_t, 0))
    else:
        out_shape = jax.ShapeDtypeStruct((n, nc), out_dtype)
        out_spec = pl.BlockSpec((tm, tn), lambda i, j: (i, j))
    kern = functools.partial(
        _proj_kernel, has_gain=gain is not None, has_rope=rope is not None, roll_shift=roll_shift,
        has_add=add is not None, has_res=res is not None, act=act, head_major=head_major is not None)
    return pl.pallas_call(
        kern, out_shape=out_shape, grid=grid, in_specs=in_specs, out_specs=out_spec,
        compiler_params=_cparams("parallel", "arbitrary"), name=name,
    )(*args)


def _compress_kernel(r_ref, pa_ref, pb_ref, w1a_ref, w1b_ref, w2_ref, o_ref):
    r = r_ref[...].astype(F32)
    a = jnp.dot((r + pa_ref[...]).astype(BF16), w1a_ref[...], preferred_element_type=F32)
    b = jnp.dot((r + pb_ref[...]).astype(BF16), w1b_ref[...], preferred_element_type=F32)
    nr = a.shape[0]
    hid = a + pltpu.roll(b, nr - 1, 0)
    hid = jax.nn.gelu(hid)
    o_ref[...] = jnp.dot(hid.astype(BF16), w2_ref[...], preferred_element_type=F32).astype(o_ref.dtype)


def compress(r, head0, pos_a, pos_b, w1a, w1b, w2):
    b, _, nr, kk = r.shape
    g = NSA_KV_HEADS
    full = lambda shape: pl.BlockSpec(shape, lambda bi, gi: (0,) * len(shape))
    return pl.pallas_call(
        _compress_kernel,
        out_shape=jax.ShapeDtypeStruct((b, g, nr, LANES), BF16),
        grid=(b, g),
        in_specs=[pl.BlockSpec((None, None, nr, kk), lambda bi, gi: (bi, head0 + gi, 0, 0)),
                  full((1, kk)), full((1, kk)), full((kk, LANES)), full((kk, LANES)),
                  full((LANES, LANES))],
        out_specs=pl.BlockSpec((None, None, nr, LANES), lambda bi, gi: (bi, gi, 0, 0)),
        compiler_params=_cparams("parallel", "parallel"),
        name="nsa_compress",
    )(r, pos_a, pos_b, w1a, w1b, w2)


def _masked_softmax(s, mask):
    s = jnp.where(mask, s, NEG)
    m = jnp.max(s, axis=-1, keepdims=True)
    e = jnp.where(mask, jnp.exp(s - m), 0.0)
    return e / jnp.maximum(jnp.sum(e, axis=-1, keepdims=True), 1e-30)


def _dot_nt(a, b):
    return lax.dot_general(a, b, (((1,), (1,)), ((), ())), preferred_element_type=F32)


def _split_dot(x, w):
    hi = x.astype(BF16)
    lo = (x - hi.astype(F32)).astype(BF16)
    return (jnp.dot(hi, w, preferred_element_type=F32) + jnp.dot(lo, w, preferred_element_type=F32))


def _nsa_cmp_kernel(q_ref, kcc_ref, vcc_ref, msel_ref, glog_ref, e_ref, oc_ref, sel_ref,
                    *, tq, scale, topk):
    s0 = pl.program_id(2) * tq
    rows = NSA_HPG * tq
    q4 = q_ref[...].reshape(rows, LANES)
    s = _dot_nt(q4, kcc_ref[...]) * scale
    nr = s.shape[1]
    t = s0 + lax.rem(lax.broadcasted_iota(jnp.int32, (rows, nr), 0), tq)
    cend = lax.broadcasted_iota(jnp.int32, (rows, nr), 1) * CMP_STRIDE + (CMP_BLOCK - 1)
    p = _masked_softmax(s, cend <= t)
    o = jnp.dot(p.astype(BF16), vcc_ref[...], preferred_element_type=F32)

    psum = p[0:tq]
    for h in range(1, NSA_HPG):
        psum = psum + p[h * tq:(h + 1) * tq]
    imp = _split_dot(psum, msel_ref[...])
    blk = lax.broadcasted_iota(jnp.int32, (tq, LANES), 1)
    cur = (s0 + lax.broadcasted_iota(jnp.int32, (tq, LANES), 0)) // SEL_BLOCK
    valid = blk <= cur
    forced = (blk == 0) | (blk == cur) | (blk == cur - 1)
    score = jnp.where(valid, jnp.where(forced, 1e4, imp), -1.0)
    sc = score.T
    rowid = lax.broadcasted_iota(jnp.int32, (LANES, tq), 0).astype(F32)

    def pick_one(_, carry):
        sc, sel = carry
        cm = jnp.max(sc, axis=0, keepdims=True)
        first = jnp.min(jnp.where(sc == cm, rowid, float(LANES)), axis=0, keepdims=True)
        hit = rowid == first
        return jnp.where(hit, -jnp.inf, sc), jnp.where(hit, 1.0, sel)

    _, sel_t = lax.fori_loop(0, topk, pick_one, (sc, jnp.zeros((LANES, tq), F32)))
    sel_ref[...] = sel_t.T.astype(sel_ref.dtype)

    glog = glog_ref[...]
    lane = lax.broadcasted_iota(jnp.int32, (tq, LANES), 1)
    for pr in range(NSA_HPG // 2):
        gate = jax.nn.sigmoid(_split_dot(glog, e_ref[pr]))
        pair = jnp.where(lane < HALF, o[(2 * pr) * tq:(2 * pr + 1) * tq],
                         o[(2 * pr + 1) * tq:(2 * pr + 2) * tq])
        oc_ref[:, pr * LANES:(pr + 1) * LANES] = (gate * pair).astype(oc_ref.dtype)


def nsa_compressed(qk, kcc, vcc, msel, misc, glog_blk, expand, *, batch, seq, tq):
    g = NSA_KV_HEADS
    nq = seq // tq
    nr = kcc.shape[2]
    topk = min(SEL_TOPK, seq // SEL_BLOCK)
    kern = functools.partial(_nsa_cmp_kernel, tq=tq, scale=NSA_DK ** -0.5, topk=topk)
    return pl.pallas_call(
        kern,
        out_shape=(jax.ShapeDtypeStruct((batch * seq, NSA_HEADS * NSA_DV), F32),
                   jax.ShapeDtypeStruct((batch, g, seq, LANES), BF16)),
        grid=(batch, g, nq),
        in_specs=[
            pl.BlockSpec((None, NSA_HPG, tq, LANES), lambda b, gi, qi: (b, gi, qi, 0)),
            pl.BlockSpec((None, None, nr, LANES), lambda b, gi, qi: (b, gi, 0, 0)),
            pl.BlockSpec((None, None, nr, LANES), lambda b, gi, qi: (b, gi, 0, 0)),
            pl.BlockSpec((nr, LANES), lambda b, gi, qi: (0, 0)),
            pl.BlockSpec((tq, LANES), lambda b, gi, qi: (b * nq + qi, glog_blk)),
            pl.BlockSpec((None, 2, LANES, LANES), lambda b, gi, qi: (0, gi, 0, 0)),
        ],
        out_specs=(pl.BlockSpec((tq, 2 * LANES), lambda b, gi, qi: (b * nq + qi, gi)),
                   pl.BlockSpec((None, None, tq, LANES), lambda b, gi, qi: (b, gi, qi, 0))),
        compiler_params=_cparams("parallel", "parallel", "parallel"),
        name="nsa_compressed_select",
    )(qk, kcc, vcc, msel, misc, expand)


def _flash_steps(mode, nq, tq, tk):
    r = tq // tk
    qi_l, kt_l, first_l, last_l = [], [], [], []
    for qi in range(nq):
        hi = qi * r + r - 1
        lo = max(0, qi * r - (-(-(WINDOW - 1) // tk))) if mode == "window" else 0
        for kt in range(lo, hi + 1):
            qi_l.append(qi)
            kt_l.append(kt)
            first_l.append(int(kt == lo))
            last_l.append(int(kt == hi))
    return tuple(jnp.asarray(np.asarray(a, np.int32)) for a in (qi_l, kt_l, first_l, last_l))


def _flash_kernel(*refs, mode, tq, tk, rc, c2, kv_shared, gated):
    it = iter(refs)
    qi_ref, kt_ref, first_ref, last_ref = next(it), next(it), next(it), next(it)
    q_ref, k_ref, v_ref = next(it), next(it), next(it)
    sel_ref = next(it) if mode == "select" else None
    ex_ref = next(it) if mode == "select" else None
    glog_ref = next(it) if gated else None
    e_ref = next(it) if gated else None
    o_ref = next(it)
    m_sc, acc_sc, s_sc, p_sc = next(it), next(it), next(it), next(it)
    bias_sc = next(it) if mode == "select" else None

    st = pl.program_id(2)
    s0 = qi_ref[st] * tq
    k0 = kt_ref[st] * tk
    reps = tk // LANES

    @pl.when(first_ref[st] == 1)
    def _init():
        m_sc[...] = jnp.full(m_sc.shape, NEG, F32)
        acc_sc[...] = jnp.zeros(acc_sc.shape, F32)

    if mode == "select":
        chosen = jnp.dot(sel_ref[...], ex_ref[...], preferred_element_type=F32)
        bias_sc[...] = (1.0 - chosen) * NEG

    def tile(masked):
        for hh in range(2):
            s_sc[hh] = _dot_nt(q_ref[hh], k_ref[0 if kv_shared else hh])
        lane_v = lax.broadcasted_iota(jnp.int32, (tk, LANES), 1)
        if masked:
            dmat = (lax.broadcasted_iota(jnp.int32, (rc, tk), 0)
                    - lax.broadcasted_iota(jnp.int32, (rc, tk), 1))
        for hh in range(2):
            v = v_ref[0 if kv_shared else hh]
            v_aug = jnp.where(lane_v == HALF, jnp.ones_like(v), v)

            for c in range(tq // rc):
                r0 = c * rc
                rows = pl.ds(r0, rc)
                s = s_sc[hh, rows, :]
                if mode == "select":
                    s = s + bias_sc[rows, :]
                if masked:
                    d = dmat + (s0 - k0 + r0)
                    msk = (d >= 0) & (d < WINDOW) if mode == "window" else d >= 0
                    s = jnp.where(msk, s, NEG)
                m_old = m_sc[hh, rows, :]
                m_new = jnp.maximum(m_old, jnp.max(s, axis=-1, keepdims=True))
                p = jnp.exp2((s - jnp.concatenate([m_new] * reps, axis=1)) * c2)
                if masked:
                    p = jnp.where(msk, p, 0.0)
                m_sc[hh, rows, :] = m_new
                acc_sc[hh, rows, :] = acc_sc[hh, rows, :] * jnp.exp2((m_old - m_new) * c2)
                p_sc[hh, rows, :] = p.astype(BF16)
            acc_sc[hh] +=jnp.dot(p_sc[hh], v_aug, preferred_element_type=F32)

    if mode == "window":
        tile(True)
    else:
        interior = k0 + tk - 1 <= s0

        @pl.when(interior)
        def _interior():
            tile(False)

        @pl.when(jnp.logical_not(interior))
        def _diagonal():
            tile(True)

    @pl.when(last_ref[st] == 1)
    def _finish():
        lane = lax.broadcasted_iota(jnp.int32, (tq, LANES), 1)
        outs = []
        for hh in range(2):
            a = acc_sc[hh]
            outs.append(a / jnp.maximum(a[:, HALF:HALF + 1], 1e-30))
        out = jnp.where(lane < HALF, outs[0], pltpu.roll(outs[1], HALF, 1))
        if gated:
            out = jax.nn.sigmoid(_split_dot(glog_ref[...], e_ref[...])) * out
        o_ref[...] = out.astype(o_ref.dtype)


def flash_pairs(q, k, v, *, mode, batch, seq, tq, tk, rc, scale, q_head0, k_head0, v_head0,
                kv_shared, out_dtype, sel=None, ex=None, gate=None, name="flash"):
    npairs = NSA_HEADS // 2
    nq = seq // tq
    assert tq % tk == 0 and tq % rc == 0
    tabs = _flash_steps(mode, nq, tq, tk)
    nsteps = tabs[0].shape[0]
    if kv_shared:
        kv_spec = lambda h0: pl.BlockSpec(
            (None, 1, tk, LANES), lambda b, hp, st, qi, kt, fi, la: (b, h0 + hp // 2, kt[st], 0))
    else:
        kv_spec = lambda h0: pl.BlockSpec(
            (None, 2, tk, LANES), lambda b, hp, st, qi, kt, fi, la: (b, h0 // 2 + hp, kt[st], 0))
    in_specs = [pl.BlockSpec((None, 2, tq, LANES),
                             lambda b, hp, st, qi, kt, fi, la: (b, q_head0 // 2 + hp, qi[st], 0)),
                kv_spec(k_head0), kv_spec(v_head0)]
    args = [q, k, v]
    scratch = [pltpu.VMEM((2, tq, LANES), F32), pltpu.VMEM((2, tq, LANES), F32),
               pltpu.VMEM((2, tq, tk), F32), pltpu.VMEM((2, tq, tk), BF16)]
    if mode == "select":
        in_specs += [pl.BlockSpec((None, None, tq, LANES),
                                  lambda b, hp, st, qi, kt, fi, la: (b, hp // 2, qi[st], 0)),
                     pl.BlockSpec((None, LANES, tk), lambda b, hp, st, qi, kt, fi, la: (kt[st], 0, 0))]
        args += [sel, ex]
        scratch.append(pltpu.VMEM((tq, tk), F32))
    if gate is not None:
        misc, glog_blk, expand, branch = gate
        in_specs += [pl.BlockSpec((tq, LANES),
                                  lambda b, hp, st, qi, kt, fi, la: (b * nq + qi[st], glog_blk)),
                     pl.BlockSpec((None, None, LANES, LANES),
                                  lambda b, hp, st, qi, kt, fi, la: (branch, hp, 0, 0))]
        args += [misc, expand]
    kern = functools.partial(_flash_kernel, mode=mode, tq=tq, tk=tk, rc=rc,
                             c2=float(scale * np.log2(np.e)), kv_shared=kv_shared, gated=gate is not None)
    grid_spec = pltpu.PrefetchScalarGridSpec(
        num_scalar_prefetch=4, grid=(batch, npairs, nsteps), in_specs=in_specs,
        out_specs=pl.BlockSpec((tq, LANES), lambda b, hp, st, qi, kt, fi, la: (b * nq + qi[st], hp)),
        scratch_shapes=scratch)
    return pl.pallas_call(
        kern,
        out_shape=jax.ShapeDtypeStruct((batch * seq, npairs * LANES), out_dtype),
        grid_spec=grid_spec,
        compiler_params=_cparams("parallel", "parallel", "arbitrary"),
        name=name,
    )(*tabs, *args)


def _mix_kernel(oc_ref, os_ref, ow_ref, ob_ref, ga_ref, gb_ref, wa_ref, wb_ref, o_ref):
    oa = (oc_ref[...] + os_ref[...] + ow_ref[...]).astype(BF16)
    ya = jnp.dot(oa, wa_ref[...], preferred_element_type=F32)
    yb = jnp.dot(ob_ref[...], wb_ref[...], preferred_element_type=F32)
    o_ref[...] = (ga_ref[...].astype(F32) * ya + gb_ref[...].astype(F32) * yb).astype(o_ref.dtype)


def gated_mix(oc, osel, ow, ob, gates, wa, wb, *, tm, tn):
    n, ka = oc.shape
    d = wa.shape[1]
    nb = d // tn
    row = lambda w: pl.BlockSpec((tm, w), lambda i, j: (i, 0))
    return pl.pallas_call(
        _mix_kernel,
        out_shape=jax.ShapeDtypeStruct((n, d), BF16),
        grid=(n // tm, nb),
        in_specs=[row(ka), row(ka), row(ka), row(ob.shape[1]),
                  pl.BlockSpec((tm, tn), lambda i, j: (i, j)),
                  pl.BlockSpec((tm, tn), lambda i, j: (i, nb + j)),
                  pl.BlockSpec((ka, tn), lambda i, j: (0, j)),
                  pl.BlockSpec((wb.shape[0], tn), lambda i, j: (0, j))],
        out_specs=pl.BlockSpec((tm, tn), lambda i, j: (i, j)),
        compiler_params=_cparams("parallel", "arbitrary"),
        name="gated_mix",
    )(oc, osel, ow, ob, gates, gates, wa, wb)


def _xattn_kernel(h_ref, g_ref, wq_ref, kv_ref, wo_ref, o_ref):
    h = h_ref[...]
    hn = _rms(h, g_ref[...]).astype(BF16)
    q = jnp.dot(hn, wq_ref[...], preferred_element_type=F32).astype(BF16)
    kv = kv_ref[...]
    outs = []
    for hd in range(XA_HEADS):
        qh = q[:, hd * XA_DIM:(hd + 1) * XA_DIM]
        kh = kv[:, hd * XA_DIM:(hd + 1) * XA_DIM]
        vh = kv[:, (XA_HEADS + hd) * XA_DIM:(XA_HEADS + hd + 1) * XA_DIM]
        s = _dot_nt(qh, kh) * (XA_DIM ** -0.5)
        e = jnp.exp(s - jnp.max(s, axis=-1, keepdims=True))
        p = e / jnp.sum(e, axis=-1, keepdims=True)
        outs.append(jnp.dot(p.astype(BF16), vh, preferred_element_type=F32))
    o = jnp.concatenate(outs, axis=1).astype(BF16)
    o_ref[...] = h + jnp.dot(o, wo_ref[...], preferred_element_type=F32)


def memory_xattn(h, g, wq, kv, wo, *, seq, tm):
    n, d = h.shape
    s_t = seq // tm
    dq = wq.shape[1]
    return pl.pallas_call(
        _xattn_kernel,
        out_shape=jax.ShapeDtypeStruct((n, d), F32),
        grid=(n // tm,),
        in_specs=[pl.BlockSpec((tm, d), lambda i: (i, 0)),
                  pl.BlockSpec((1, d), lambda i: (0, 0)),
                  pl.BlockSpec((d, dq), lambda i: (0, 0)),
                  pl.BlockSpec((MEM_LEN, 2 * dq), lambda i: (i // s_t, 0)),
                  pl.BlockSpec((dq, d), lambda i: (0, 0))],
        out_specs=pl.BlockSpec((tm, d), lambda i: (i, 0)),
        compiler_params=_cparams("parallel"),
        name="memory_xattn",
    )(h, g.reshape(1, d), wq, kv, wo)


def _mlp_kernel(h_ref, g_ref, w1_ref, w2_ref, gf_ref, o_ref, hn_sc, acc_sc, *, nf):
    f = pl.program_id(1)

    @pl.when(f == 0)
    def _init():
        hn_sc[...] = _rms(h_ref[...], g_ref[...]).astype(BF16)
        acc_sc[...] = jnp.zeros(acc_sc.shape, F32)

    u = jnp.dot(hn_sc[...], w1_ref[...], preferred_element_type=F32)
    u = jnp.square(jnp.maximum(u, 0.0))
    acc_sc[...] += jnp.dot(u.astype(BF16), w2_ref[...], preferred_element_type=F32)

    @pl.when(f == nf - 1)
    def _finish():
        o_ref[...] = _rms(h_ref[...] + acc_sc[...], gf_ref[...])


def mlp_final(h, g, w1, w2, gf, *, tm, tf):
    n, d = h.shape
    dff = w1.shape[1]
    nf = dff // tf
    return pl.pallas_call(
        functools.partial(_mlp_kernel, nf=nf),
        out_shape=jax.ShapeDtypeStruct((n, d), F32),
        grid=(n // tm, nf),
        in_specs=[pl.BlockSpec((tm, d), lambda i, f: (i, 0)),
                  pl.BlockSpec((1, d), lambda i, f: (0, 0)),
                  pl.BlockSpec((d, tf), lambda i, f: (0, f)),
                  pl.BlockSpec((tf, d), lambda i, f: (f, 0)),
                  pl.BlockSpec((1, d), lambda i, f: (0, 0))],
        out_specs=pl.BlockSpec((tm, d), lambda i, f: (i, 0)),
        scratch_shapes=[pltpu.VMEM((tm, d), BF16), pltpu.VMEM((tm, d), F32)],
        compiler_params=_cparams("parallel", "arbitrary"),
        name="mlp_final",
    )(h, g.reshape(1, d), w1, w2, gf.reshape(1, d))


def _rot_partner(w, half):
    return jnp.concatenate([-w[..., half:], w[..., :half]], axis=-1)


def _pad_heads_rope(w, heads, hd, rot0, rot_dim):
    k = w.shape[0]
    w = w.reshape(k, heads, hd)
    partner = _rot_partner(w[:, :, rot0:rot0 + rot_dim], rot_dim // 2)
    pad = jnp.zeros((k, heads, LANES - hd - rot_dim), w.dtype)
    return jnp.concatenate([w, partner, pad], axis=-1).reshape(k, heads * LANES)


def _rope_lane_tables(seq, rot0, rot_dim, hd):
    inv = 1.0 / (ROPE_THETA ** (jnp.arange(0, rot_dim, 2, dtype=F32) / rot_dim))
    ang = jnp.arange(seq, dtype=F32)[:, None] * inv[None, :]
    cos = jnp.concatenate([jnp.cos(ang), jnp.cos(ang)], axis=1)
    sin = jnp.concatenate([jnp.sin(ang), jnp.sin(ang)], axis=1)
    cos_t = jnp.concatenate([jnp.ones((seq, rot0), F32), cos,
                             jnp.ones((seq, hd - rot0 - rot_dim), F32),
                             jnp.zeros((seq, LANES - hd), F32)], axis=1)
    sin_t = jnp.concatenate([jnp.zeros((seq, rot0), F32), sin,
                             jnp.zeros((seq, LANES - rot0 - rot_dim), F32)], axis=1)
    return cos_t, sin_t


def _pad_lanes(w, groups, width):
    k = w.shape[0]
    w = w.reshape(k, groups, width)
    return jnp.pad(w, ((0, 0), (0, 0), (0, LANES - width))).reshape(k, groups * LANES)


def _compress_params(pos, w1, w2, d, dup_out):
    half = CMP_BLOCK // 2
    pos_p = jnp.pad(pos, ((0, 0), (0, LANES - d)))
    pos_a = pos_p[:half].reshape(1, half * LANES)
    pos_b = pos_p[half:].reshape(1, half * LANES)
    w1p = jnp.pad(w1.reshape(CMP_BLOCK, d, d), ((0, 0), (0, LANES - d), (0, LANES - d)))
    w1a = w1p[:half].reshape(half * LANES, LANES).astype(BF16)
    w1b = w1p[half:].reshape(half * LANES, LANES).astype(BF16)
    w2p = jnp.pad(w2, ((0, LANES - d), (0, LANES - d)))
    if dup_out:
        w2p = w2p.at[:d, HALF:HALF + d].set(w2)
    return pos_a, pos_b, w1a, w1b, w2p.astype(BF16)


def _cmp_to_sel(nr, nsb):
    cs = np.arange(nr) * CMP_STRIDE
    ce = cs + CMP_BLOCK
    ss = np.arange(LANES) * SEL_BLOCK
    se = ss + SEL_BLOCK
    ov = np.clip(np.minimum(ce[:, None], se[None, :]) - np.maximum(cs[:, None], ss[None, :]), 0, None)
    ov = ov.astype(np.float32) / np.float32(CMP_BLOCK)
    ov[:, nsb:] = 0.0
    ov[nr - 1:, :] = 0.0
    return jnp.asarray(ov, BF16)


def _gate_expand():
    e = np.zeros((3, NSA_HEADS // 2, LANES, LANES), np.float32)
    for br in range(3):
        for hp in range(NSA_HEADS // 2):
            for hh in range(2):
                e[br, hp, 3 * (2 * hp + hh) + br, hh * HALF:(hh + 1) * HALF] = 1.0
    return jnp.asarray(e, BF16)


def _block_expand(seq, tk):
    nkt = seq // tk
    kpos = np.arange(nkt)[:, None] * tk + np.arange(tk)[None, :]
    e = (np.arange(LANES)[None, :, None] == (kpos // SEL_BLOCK)[:, None, :])
    return jnp.asarray(e.astype(np.float32), BF16)


def kernel(x, mem, g_mix, w_in, cmp_pos_k, cmp_w1_k, cmp_w2_k, cmp_pos_v, cmp_w1_v, cmp_w2_v,
           mla_g_q, mla_w_uq, mla_g_kv, mla_w_uk, mla_w_uv, w_o_nsa, w_o_mla, w_out,
           g_xattn, g_mem, xa_wq, xa_wkv, xa_wo, g_mlp, w_ff1, w_ff2, g_final):
    b, s, d = x.shape
    assert d == D_MODEL and s % (CMP_STRIDE * 8) == 0 and s // SEL_BLOCK <= LANES
    assert g_mix.shape[0] == 1
    n = b * s
    T = _tiles(s)
    tm, tq, tk, rc = T["tm"], T["tq"], T["tk"], T["rc"]
    G = NSA_KV_HEADS
    bounds = [int(v) for v in np.cumsum(SPLITS)[:-1]]

    (w_qa, w_kc, w_vc, w_ks, w_vs, w_kw, w_vw, w_gn, w_cq, w_ckv, w_kr,
     w_ga, w_gb) = jnp.split(w_in[0], bounds, axis=1)
    nsa_rope = lambda w, heads: _pad_heads_rope(w, heads, NSA_DK, 0, NSA_ROT)
    w_rope = jnp.concatenate([nsa_rope(w_qa, NSA_HEADS), nsa_rope(w_kc, G), nsa_rope(w_ks, G),
                              nsa_rope(w_kw, G)], axis=1).astype(BF16)
    w_krp = jnp.concatenate([jnp.zeros((d, MLA_NOPE), F32), w_kr, _rot_partner(w_kr, MLA_ROPE // 2)],
                            axis=1).astype(BF16)
    w_vsw = jnp.concatenate([_pad_lanes(w_vs, G, NSA_DV), _pad_lanes(w_vw, G, NSA_DV)], axis=1).astype(BF16)
    w_vcp = _pad_lanes(w_vc, G, NSA_DV).astype(BF16)
    w_misc = jnp.concatenate([w_cq, w_ckv, jnp.pad(w_gn, ((0, 0), (0, LANES - w_gn.shape[1])))],
                             axis=1).astype(BF16)
    glog_blk = (MLA_Q_RANK + MLA_KV_RANK) // LANES
    w_gates = jnp.concatenate([w_ga, w_gb], axis=1).astype(BF16)
    cos_a, sin_a = _rope_lane_tables(s, 0, NSA_ROT, NSA_DK)
    cos_b, sin_b = _rope_lane_tables(s, MLA_NOPE, MLA_ROPE, MLA_NOPE + MLA_ROPE)
    shift_a = LANES - NSA_DK
    shift_b = LANES - (NSA_DK - MLA_NOPE)
    w_uq = _pad_heads_rope(mla_w_uq[0], MLA_HEADS, MLA_NOPE + MLA_ROPE, MLA_NOPE, MLA_ROPE).astype(BF16)
    w_uk = _pad_lanes(mla_w_uk[0], MLA_HEADS, MLA_NOPE).astype(BF16)
    w_uv = _pad_lanes(mla_w_uv[0], MLA_HEADS, MLA_DV).astype(BF16)

    x2 = x.reshape(n, d)
    xn = rmsnorm_rows(x2, g_mix[0], tm)
    qk = projection(xn, w_rope, tm=tm, tn=512, out_dtype=BF16, rope=(cos_a, sin_a, shift_a),
                    head_major=(b, s), name="proj_qk_rope")
    krp = projection(xn, w_krp, tm=tm, tn=LANES, out_dtype=BF16, rope=(cos_b, sin_b, shift_b),
                     name="proj_k_rope_mla")
    vsw = projection(xn, w_vsw, tm=tm, tn=512, out_dtype=BF16, head_major=(b, s), name="proj_v")
    vcp = projection(xn, w_vcp, tm=tm, tn=512, out_dtype=BF16, head_major=(b, s), name="proj_vc")
    misc = projection(xn, w_misc, tm=tm, tn=w_misc.shape[1], out_dtype=F32, name="proj_misc")
    gates = projection(xn, w_gates, tm=tm, tn=512, out_dtype=BF16, act="sigmoid", name="proj_gates")

    nr = s // CMP_STRIDE
    pk = _compress_params(cmp_pos_k[0], cmp_w1_k[0], cmp_w2_k[0], NSA_DK, False)
    pv = _compress_params(cmp_pos_v[0], cmp_w1_v[0], cmp_w2_v[0], NSA_DV, True)
    kcc = compress(qk.reshape(b, QK_HEADS, nr, CMP_STRIDE * LANES), QK_KC0, *pk)
    vcc = compress(vcp.reshape(b, G, nr, CMP_STRIDE * LANES), 0, *pv)
    expand = _gate_expand()
    o_c, sel = nsa_compressed(qk, kcc, vcc, _cmp_to_sel(nr, s // SEL_BLOCK), misc, glog_blk, expand,
                              batch=b, seq=s, tq=T["tq_cmp"])
    nsa_scale = NSA_DK ** -0.5
    o_s = flash_pairs(qk, qk, vsw, mode="select", batch=b, seq=s, tq=tq, tk=tk, rc=rc, scale=nsa_scale,
                      q_head0=QK_Q0, k_head0=QK_KS0, v_head0=0, kv_shared=True, out_dtype=F32,
                      sel=sel, ex=_block_expand(s, tk), gate=(misc, glog_blk, expand, 1),
                      name="nsa_selected")
    o_w = flash_pairs(qk, qk, vsw, mode="window", batch=b, seq=s, tq=tq, tk=tk, rc=rc, scale=nsa_scale,
                      q_head0=QK_Q0, k_head0=QK_KW0, v_head0=G, kv_shared=True, out_dtype=F32,
                      gate=(misc, glog_blk, expand, 2), name="nsa_window")

    qm = projection(misc, w_uq, tm=tm, tn=512, out_dtype=BF16, a_cols=(MLA_Q_RANK, 0), gain=mla_g_q[0],
                    rope=(cos_b, sin_b, shift_b), head_major=(b, s), name="mla_q")
    km = projection(misc, w_uk, tm=tm, tn=512, out_dtype=BF16,
                    a_cols=(MLA_KV_RANK, MLA_Q_RANK // MLA_KV_RANK), gain=mla_g_kv[0], add=krp,
                    head_major=(b, s), name="mla_k")
    vm = projection(misc, w_uv, tm=tm, tn=512, out_dtype=BF16,
                    a_cols=(MLA_KV_RANK, MLA_Q_RANK // MLA_KV_RANK), gain=mla_g_kv[0],
                    head_major=(b, s), name="mla_v")
    o_b = flash_pairs(qm, km, vm, mode="causal", batch=b, seq=s, tq=tq, tk=tk, rc=rc,
                      scale=(MLA_NOPE + MLA_ROPE) ** -0.5, q_head0=0, k_head0=0, v_head0=0,
                      kv_shared=False, out_dtype=BF16, name="mla_attention")

    mixed = gated_mix(o_c, o_s, o_w, o_b, gates, w_o_nsa[0].astype(BF16), w_o_mla[0].astype(BF16),
                      tm=tm, tn=512)
    h1 = projection(mixed, w_out[0].astype(BF16), tm=tm, tn=512, out_dtype=F32, res=x2, name="proj_out")

    kv_mem = projection(mem.reshape(b * MEM_LEN, d), xa_wkv[0].astype(BF16), tm=MEM_LEN, tn=512,
                        out_dtype=BF16, gain=g_mem[0], name="xattn_kv")
    h2 = memory_xattn(h1, g_xattn[0], xa_wq[0].astype(BF16), kv_mem, xa_wo[0].astype(BF16), seq=s, tm=tm)

    out = mlp_final(h2, g_mlp[0], w_ff1[0].astype(BF16), w_ff2[0].astype(BF16), g_final, tm=tm, tf=T["tf"])
    return out.reshape(b, s, d)
```

```python
import functools

import numpy as np
import jax
import jax.numpy as jnp
from jax import lax
from jax.experimental import pallas as pl
from jax.experimental.pallas import tpu as pltpu

F32 = jnp.float32
BF16 = jnp.bfloat16

D_MODEL = 2048
MEM_LEN = 256
ROPE_THETA = 500000.0
EPS = 1e-6
NEG = -1e30

NSA_HEADS = 16
NSA_KV_HEADS = 4
NSA_HPG = NSA_HEADS // NSA_KV_HEADS
NSA_DK = 96
NSA_DV = 64
NSA_ROT = NSA_DK // 4
CMP_BLOCK = 32
CMP_STRIDE = 16
SEL_BLOCK = 64
SEL_TOPK = 16
WINDOW = 512

MLA_HEADS = 16
MLA_NOPE = 64
MLA_ROPE = 32
MLA_DV = 64
MLA_Q_RANK = 512
MLA_KV_RANK = 256

XA_HEADS = 4
XA_DIM = 128
D_FF = 4 * D_MODEL

SPLITS = (NSA_HEADS * NSA_DK,
          NSA_KV_HEADS * NSA_DK, NSA_KV_HEADS * NSA_DV,
          NSA_KV_HEADS * NSA_DK, NSA_KV_HEADS * NSA_DV,
          NSA_KV_HEADS * NSA_DK, NSA_KV_HEADS * NSA_DV,
          NSA_HEADS * 3,
          MLA_Q_RANK, MLA_KV_RANK, MLA_ROPE,
          D_MODEL, D_MODEL)

LANES = 128
HALF = LANES // 2
VMEM_LIMIT = 56 * 1024 * 1024

QK_Q0 = 0
QK_KC0 = NSA_HEADS
QK_KS0 = NSA_HEADS + NSA_KV_HEADS
QK_KW0 = NSA_HEADS + 2 * NSA_KV_HEADS
QK_HEADS = NSA_HEADS + 3 * NSA_KV_HEADS
FLASH_HEADS = NSA_HPG


def _cparams(*sem):
    return pltpu.CompilerParams(dimension_semantics=sem, vmem_limit_bytes=VMEM_LIMIT)


def _tiles(seq):
    row = min(512, seq)
    return dict(
        tm=row,
        tq=min(512, seq),
        tk=min(512, seq),
        rc=32,
        tq_cmp=min(128, seq),
        tf=512,
    )


def _rms(x, g):
    return x * lax.rsqrt(jnp.mean(x * x, axis=-1, keepdims=True) + EPS) * g


def _rmsnorm_kernel(x_ref, g_ref, o_ref):
    o_ref[...] = _rms(x_ref[...], g_ref[...]).astype(o_ref.dtype)


def rmsnorm_rows(x, g, tm):
    n, d = x.shape
    return pl.pallas_call(
        _rmsnorm_kernel,
        out_shape=jax.ShapeDtypeStruct((n, d), BF16),
        grid=(n // tm,),
        in_specs=[pl.BlockSpec((tm, d), lambda i: (i, 0)),
                  pl.BlockSpec((1, d), lambda i: (0, 0))],
        out_specs=pl.BlockSpec((tm, d), lambda i: (i, 0)),
        compiler_params=_cparams("parallel"),
        name="rmsnorm_rows",
    )(x, g.reshape(1, d))


def _proj_kernel(*refs, has_gain, has_rope, roll_shift, has_add, has_res, act, head_major):
    it = iter(refs)
    a_ref = next(it)
    g_ref = next(it) if has_gain else None
    w_ref = next(it)
    cos_ref = next(it) if has_rope else None
    sin_ref = next(it) if has_rope else None
    add_ref = next(it) if has_add else None
    res_ref = next(it) if has_res else None
    o_ref = next(it)

    a = a_ref[...]
    if has_gain:
        a = _rms(a, g_ref[...]).astype(BF16)
    y = jnp.dot(a, w_ref[...], preferred_element_type=F32)
    if has_rope or has_add or head_major:
        for h in range(y.shape[1] // LANES):
            yh = y[:, h * LANES:(h + 1) * LANES]
            if has_rope:
                yh = yh * cos_ref[...] + pltpu.roll(yh, roll_shift, 1) * sin_ref[...]
            if has_add:
                yh = yh + add_ref[...].astype(F32)
            if head_major:
                o_ref[h] = yh.astype(o_ref.dtype)
            else:
                o_ref[:, h * LANES:(h + 1) * LANES] = yh.astype(o_ref.dtype)
    else:
        if act == "sigmoid":
            y = jax.nn.sigmoid(y)
        if has_res:
            y = res_ref[...] + y
        o_ref[...] = y.astype(o_ref.dtype)


def projection(a, w, *, tm, tn, out_dtype, a_cols=None, gain=None, rope=None, add=None,
               res=None, act=None, head_major=None, name="projection"):
    n = a.shape[0]
    k, nc = w.shape
    a_w, a_blk = a_cols if a_cols is not None else (a.shape[1], 0)
    assert a_w == k and n % tm == 0 and nc % tn == 0
    grid = (n // tm, nc // tn)
    in_specs = [pl.BlockSpec((tm, k), lambda i, j: (i, a_blk))]
    args = [a]
    if gain is not None:
        in_specs.append(pl.BlockSpec((1, k), lambda i, j: (0, 0)))
        args.append(gain.reshape(1, k))
    in_specs.append(pl.BlockSpec((k, tn), lambda i, j: (0, j)))
    args.append(w)
    roll_shift = 0
    if rope is not None:
        cos, sin, roll_shift = rope
        s_tiles = cos.shape[0] // tm
        for t in (cos, sin):
            in_specs.append(pl.BlockSpec((tm, LANES), lambda i, j: (i % s_tiles, 0)))
            args.append(t)
    if add is not None:
        in_specs.append(pl.BlockSpec((tm, LANES), lambda i, j: (i, 0)))
        args.append(add)
    if res is not None:
        in_specs.append(pl.BlockSpec((tm, tn), lambda i, j: (i, j)))
        args.append(res)
    if head_major is not None:
        b, s = head_major
        s_t = s // tm
        hpt = tn // LANES
        out_shape = jax.ShapeDtypeStruct((b, nc // LANES, s, LANES), out_dtype)
        out_spec = pl.BlockSpec((None, hpt, tm, LANES), lambda i, j: (i // s_t, j, i % s_t, 0))
    else:
        out_shape = jax.ShapeDtypeStruct((n, nc), out_dtype)
        out_spec = pl.BlockSpec((tm, tn), lambda i, j: (i, j))
    kern = functools.partial(
        _proj_kernel, has_gain=gain is not None, has_rope=rope is not None, roll_shift=roll_shift,
        has_add=add is not None, has_res=res is not None, act=act, head_major=head_major is not None)
    return pl.pallas_call(
        kern, out_shape=out_shape, grid=grid, in_specs=in_specs, out_specs=out_spec,
        compiler_params=_cparams("parallel", "arbitrary"), name=name,
    )(*args)


def _compress_kernel(r_ref, pa_ref, pb_ref, w1a_ref, w1b_ref, w2_ref, o_ref):
    r = r_ref[...].astype(F32)
    a = jnp.dot((r + pa_ref[...]).astype(BF16), w1a_ref[...], preferred_element_type=F32)
    b = jnp.dot((r + pb_ref[...]).astype(BF16), w1b_ref[...], preferred_element_type=F32)
    nr = a.shape[0]
    hid = a + pltpu.roll(b, nr - 1, 0)
    hid = jax.nn.gelu(hid)
    o_ref[...] = jnp.dot(hid.astype(BF16), w2_ref[...], preferred_element_type=F32).astype(o_ref.dtype)


def compress(r, head0, pos_a, pos_b, w1a, w1b, w2):
    b, _, nr, kk = r.shape
    g = NSA_KV_HEADS
    full = lambda shape: pl.BlockSpec(shape, lambda bi, gi: (0,) * len(shape))
    return pl.pallas_call(
        _compress_kernel,
        out_shape=jax.ShapeDtypeStruct((b, g, nr, LANES), BF16),
        grid=(b, g),
        in_specs=[pl.BlockSpec((None, None, nr, kk), lambda bi, gi: (bi, head0 + gi, 0, 0)),
                  full((1, kk)), full((1, kk)), full((kk, LANES)), full((kk, LANES)),
                  full((LANES, LANES))],
        out_specs=pl.BlockSpec((None, None, nr, LANES), lambda bi, gi: (bi, gi, 0, 0)),
        compiler_params=_cparams("parallel", "parallel"),
        name="nsa_compress",
    )(r, pos_a, pos_b, w1a, w1b, w2)


def _masked_softmax(s, mask):
    s = jnp.where(mask, s, NEG)
    m = jnp.max(s, axis=-1, keepdims=True)
    e = jnp.where(mask, jnp.exp(s - m), 0.0)
    return e / jnp.maximum(jnp.sum(e, axis=-1, keepdims=True), 1e-30)


def _dot_nt(a, b):
    return lax.dot_general(a, b, (((1,), (1,)), ((), ())), preferred_element_type=F32)


def _split_dot(x, w):
    hi = x.astype(BF16)
    lo = (x - hi.astype(F32)).astype(BF16)
    return (jnp.dot(hi, w, preferred_element_type=F32) + jnp.dot(lo, w, preferred_element_type=F32))


def _nsa_cmp_kernel(q_ref, kcc_ref, vcc_ref, msel_ref, glog_ref, e_ref, oc_ref, sel_ref,
                    *, tq, scale, topk):
    s0 = pl.program_id(2) * tq
    rows = NSA_HPG * tq
    q4 = q_ref[...].reshape(rows, LANES)
    s = _dot_nt(q4, kcc_ref[...]) * scale
    nr = s.shape[1]
    t = s0 + lax.rem(lax.broadcasted_iota(jnp.int32, (rows, nr), 0), tq)
    cend = lax.broadcasted_iota(jnp.int32, (rows, nr), 1) * CMP_STRIDE + (CMP_BLOCK - 1)
    p = _masked_softmax(s, cend <= t)
    o = jnp.dot(p.astype(BF16), vcc_ref[...], preferred_element_type=F32)

    psum = p[0:tq]
    for h in range(1, NSA_HPG):
        psum = psum + p[h * tq:(h + 1) * tq]
    imp = _split_dot(psum, msel_ref[...])
    blk = lax.broadcasted_iota(jnp.int32, (tq, LANES), 1)
    cur = (s0 + lax.broadcasted_iota(jnp.int32, (tq, LANES), 0)) // SEL_BLOCK
    valid = blk <= cur
    forced = (blk == 0) | (blk == cur) | (blk == cur - 1)
    score = jnp.where(valid, jnp.where(forced, 1e4, imp), -1.0)
    sc = score.T
    rowid = lax.broadcasted_iota(jnp.int32, (LANES, tq), 0).astype(F32)

    def pick_one(_, carry):
        sc, sel = carry
        cm = jnp.max(sc, axis=0, keepdims=True)
        first = jnp.min(jnp.where(sc == cm, rowid, float(LANES)), axis=0, keepdims=True)
        hit = rowid == first
        return jnp.where(hit, -jnp.inf, sc), jnp.where(hit, 1.0, sel)

    _, sel_t = lax.fori_loop(0, topk, pick_one, (sc, jnp.zeros((LANES, tq), F32)))
    sel_ref[...] = ((1.0 - sel_t.T) * NEG).astype(sel_ref.dtype)

    glog = glog_ref[...]
    lane = lax.broadcasted_iota(jnp.int32, (tq, LANES), 1)
    for pr in range(NSA_HPG // 2):
        gate = jax.nn.sigmoid(_split_dot(glog, e_ref[pr]))
        pair = jnp.where(lane < HALF, o[(2 * pr) * tq:(2 * pr + 1) * tq],
                         o[(2 * pr + 1) * tq:(2 * pr + 2) * tq])
        oc_ref[:, pr * LANES:(pr + 1) * LANES] = (gate * pair).astype(oc_ref.dtype)


def nsa_compressed(qk, kcc, vcc, msel, misc, glog_blk, expand, *, batch, seq, tq):
    g = NSA_KV_HEADS
    nq = seq // tq
    nr = kcc.shape[2]
    topk = min(SEL_TOPK, seq // SEL_BLOCK)
    kern = functools.partial(_nsa_cmp_kernel, tq=tq, scale=NSA_DK ** -0.5, topk=topk)
    return pl.pallas_call(
        kern,
        out_shape=(jax.ShapeDtypeStruct((batch * seq, NSA_HEADS * NSA_DV), F32),
                   jax.ShapeDtypeStruct((batch, g, seq, LANES), BF16)),
        grid=(batch, g, nq),
        in_specs=[
            pl.BlockSpec((None, NSA_HPG, tq, LANES), lambda b, gi, qi: (b, gi, qi, 0)),
            pl.BlockSpec((None, None, nr, LANES), lambda b, gi, qi: (b, gi, 0, 0)),
            pl.BlockSpec((None, None, nr, LANES), lambda b, gi, qi: (b, gi, 0, 0)),
            pl.BlockSpec((nr, LANES), lambda b, gi, qi: (0, 0)),
            pl.BlockSpec((tq, LANES), lambda b, gi, qi: (b * nq + qi, glog_blk)),
            pl.BlockSpec((None, 2, LANES, LANES), lambda b, gi, qi: (0, gi, 0, 0)),
        ],
        out_specs=(pl.BlockSpec((tq, 2 * LANES), lambda b, gi, qi: (b * nq + qi, gi)),
                   pl.BlockSpec((None, None, tq, LANES), lambda b, gi, qi: (b, gi, qi, 0))),
        compiler_params=_cparams("parallel", "parallel", "parallel"),
        name="nsa_compressed_select",
    )(qk, kcc, vcc, msel, misc, expand)


def _flash_steps(mode, nq, tq, tk):
    r = tq // tk
    qi_l, kt_l, first_l, last_l = [], [], [], []
    for qi in range(nq):
        hi = qi * r + r - 1
        lo = max(0, qi * r - (-(-(WINDOW - 1) // tk))) if mode == "window" else 0
        for kt in range(lo, hi + 1):
            qi_l.append(qi)
            kt_l.append(kt)
            first_l.append(int(kt == lo))
            last_l.append(int(kt == hi))
    return tuple(jnp.asarray(np.asarray(a, np.int32)) for a in (qi_l, kt_l, first_l, last_l))


def _flash_kernel(*refs, mode, tq, tk, rc, c2, kv_shared, gated):
    it = iter(refs)
    qi_ref, kt_ref, first_ref, last_ref = next(it), next(it), next(it), next(it)
    q_ref, k_ref, v_ref = next(it), next(it), next(it)
    selb_ref = next(it) if mode == "select" else None
    oh_ref = next(it) if mode == "select" else None
    glog_ref = next(it) if gated else None
    e_ref = next(it) if gated else None
    o_ref = next(it)
    m_sc, acc_sc, s_sc, p_sc = next(it), next(it), next(it), next(it)
    qa_sc = next(it) if mode == "select" else None

    st = pl.program_id(2)
    s0 = qi_ref[st] * tq
    k0 = kt_ref[st] * tk
    reps = tk // LANES

    @pl.when(first_ref[st] == 1)
    def _init():
        m_sc[...] = jnp.full(m_sc.shape, NEG, F32)
        acc_sc[...] = jnp.zeros(acc_sc.shape, F32)
        if mode == "select":
            for hh in range(FLASH_HEADS):
                qa_sc[hh] = jnp.concatenate([q_ref[hh], selb_ref[...]], axis=1)

    def tile(masked):
        for hh in range(FLASH_HEADS):
            k = k_ref[0 if kv_shared else hh]
            if mode == "select":
                s_sc[hh] = _dot_nt(qa_sc[hh], jnp.concatenate([k, oh_ref[...]], axis=1))
            else:
                s_sc[hh] = _dot_nt(q_ref[hh], k)
        lane_v = lax.broadcasted_iota(jnp.int32, (tk, LANES), 1)
        if masked:
            dmat = (lax.broadcasted_iota(jnp.int32, (rc, tk), 0)
                    - lax.broadcasted_iota(jnp.int32, (rc, tk), 1))
        for hh in range(FLASH_HEADS):
            v = v_ref[0 if kv_shared else hh]
            v_aug = jnp.where(lane_v == HALF, jnp.ones_like(v), v)

            for c in range(tq // rc):
                r0 = c * rc
                rows = pl.ds(r0, rc)
                s = s_sc[hh, rows, :]
                if masked:
                    d = dmat + (s0 - k0 + r0)
                    msk = (d >= 0) & (d < WINDOW) if mode == "window" else d >= 0
                    s = jnp.where(msk, s, NEG)
                m_old = m_sc[hh, rows, :]
                m_new = jnp.maximum(m_old, jnp.max(s, axis=-1, keepdims=True))
                p = jnp.exp2((s - jnp.concatenate([m_new] * reps, axis=1)) * c2)
                if masked:
                    p = jnp.where(msk, p, 0.0)
                m_sc[hh, rows, :] = m_new
                acc_sc[hh, rows, :] = acc_sc[hh, rows, :] * jnp.exp2((m_old - m_new) * c2)
                p_sc[hh, rows, :] = p.astype(BF16)
            acc_sc[hh] +=jnp.dot(p_sc[hh], v_aug, preferred_element_type=F32)

    if mode == "window":
        tile(True)
    else:
        interior = k0 + tk - 1 <= s0

        @pl.when(interior)
        def _interior():
            tile(False)

        @pl.when(jnp.logical_not(interior))
        def _diagonal():
            tile(True)

    @pl.when(last_ref[st] == 1)
    def _finish():
        lane = lax.broadcasted_iota(jnp.int32, (tq, LANES), 1)
        if gated:
            glog = glog_ref[...]
        for pr in range(FLASH_HEADS // 2):
            outs = []
            for hh in (2 * pr, 2 * pr + 1):
                a = acc_sc[hh]
                outs.append(a / jnp.maximum(a[:, HALF:HALF + 1], 1e-30))
            out = jnp.where(lane < HALF, outs[0], pltpu.roll(outs[1], HALF, 1))
            if gated:
                out = jax.nn.sigmoid(_split_dot(glog, e_ref[pr])) * out
            o_ref[:, pr * LANES:(pr + 1) * LANES] = out.astype(o_ref.dtype)


def flash_heads(q, k, v, *, mode, batch, seq, tq, tk, rc, scale, q_head0, k_head0, v_head0,
                kv_shared, out_dtype, selb=None, onehot=None, gate=None, name="flash"):
    nh = FLASH_HEADS
    ngroups = NSA_HEADS // nh
    nq = seq // tq
    assert tq % tk == 0 and tq % rc == 0
    tabs = _flash_steps(mode, nq, tq, tk)
    nsteps = tabs[0].shape[0]
    if kv_shared:
        kv_spec = lambda h0: pl.BlockSpec(
            (None, 1, tk, LANES), lambda b, g, st, qi, kt, fi, la: (b, h0 + g, kt[st], 0))
    else:
        kv_spec = lambda h0: pl.BlockSpec(
            (None, nh, tk, LANES), lambda b, g, st, qi, kt, fi, la: (b, h0 // nh + g, kt[st], 0))
    in_specs = [pl.BlockSpec((None, nh, tq, LANES),
                             lambda b, g, st, qi, kt, fi, la: (b, q_head0 // nh + g, qi[st], 0)),
                kv_spec(k_head0), kv_spec(v_head0)]
    args = [q, k, v]
    scratch = [pltpu.VMEM((nh, tq, LANES), F32), pltpu.VMEM((nh, tq, LANES), F32),
               pltpu.VMEM((nh, tq, tk), F32), pltpu.VMEM((nh, tq, tk), BF16)]
    if mode == "select":
        in_specs += [pl.BlockSpec((None, None, tq, LANES),
                                  lambda b, g, st, qi, kt, fi, la: (b, g, qi[st], 0)),
                     pl.BlockSpec((tk, LANES), lambda b, g, st, qi, kt, fi, la: (kt[st], 0))]
        args += [selb, onehot]
        scratch.append(pltpu.VMEM((nh, tq, 2 * LANES), BF16))
    if gate is not None:
        misc, glog_blk, expand, branch = gate
        in_specs += [pl.BlockSpec((tq, LANES),
                                  lambda b, g, st, qi, kt, fi, la: (b * nq + qi[st], glog_blk)),
                     pl.BlockSpec((None, nh // 2, LANES, LANES),
                                  lambda b, g, st, qi, kt, fi, la: (branch, g, 0, 0))]
        args += [misc, expand]
    kern = functools.partial(_flash_kernel, mode=mode, tq=tq, tk=tk, rc=rc,
                             c2=float(scale * np.log2(np.e)), kv_shared=kv_shared, gated=gate is not None)
    ow = nh * HALF
    grid_spec = pltpu.PrefetchScalarGridSpec(
        num_scalar_prefetch=4, grid=(batch, ngroups, nsteps), in_specs=in_specs,
        out_specs=pl.BlockSpec((tq, ow), lambda b, g, st, qi, kt, fi, la: (b * nq + qi[st], g)),
        scratch_shapes=scratch)
    return pl.pallas_call(
        kern,
        out_shape=jax.ShapeDtypeStruct((batch * seq, ngroups * ow), out_dtype),
        grid_spec=grid_spec,
        compiler_params=_cparams("parallel", "parallel", "arbitrary"),
        name=name,
    )(*tabs, *args)


def _mix_kernel(oc_ref, os_ref, ow_ref, ob_ref, ga_ref, gb_ref, wa_ref, wb_ref, o_ref):
    oa = (oc_ref[...] + os_ref[...] + ow_ref[...]).astype(BF16)
    ya = jnp.dot(oa, wa_ref[...], preferred_element_type=F32)
    yb = jnp.dot(ob_ref[...], wb_ref[...], preferred_element_type=F32)
    o_ref[...] = (ga_ref[...].astype(F32) * ya + gb_ref[...].astype(F32) * yb).astype(o_ref.dtype)


def gated_mix(oc, osel, ow, ob, gates, wa, wb, *, tm, tn):
    n, ka = oc.shape
    d = wa.shape[1]
    nb = d // tn
    row = lambda w: pl.BlockSpec((tm, w), lambda i, j: (i, 0))
    return pl.pallas_call(
        _mix_kernel,
        out_shape=jax.ShapeDtypeStruct((n, d), BF16),
        grid=(n // tm, nb),
        in_specs=[row(ka), row(ka), row(ka), row(ob.shape[1]),
                  pl.BlockSpec((tm, tn), lambda i, j: (i, j)),
                  pl.BlockSpec((tm, tn), lambda i, j: (i, nb + j)),
                  pl.BlockSpec((ka, tn), lambda i, j: (0, j)),
                  pl.BlockSpec((wb.shape[0], tn), lambda i, j: (0, j))],
        out_specs=pl.BlockSpec((tm, tn), lambda i, j: (i, j)),
        compiler_params=_cparams("parallel", "arbitrary"),
        name="gated_mix",
    )(oc, osel, ow, ob, gates, gates, wa, wb)


def _xattn_kernel(h_ref, g_ref, wq_ref, kv_ref, wo_ref, o_ref):
    h = h_ref[...]
    hn = _rms(h, g_ref[...]).astype(BF16)
    q = jnp.dot(hn, wq_ref[...], preferred_element_type=F32).astype(BF16)
    kv = kv_ref[...]
    outs = []
    for hd in range(XA_HEADS):
        qh = q[:, hd * XA_DIM:(hd + 1) * XA_DIM]
        kh = kv[:, hd * XA_DIM:(hd + 1) * XA_DIM]
        vh = kv[:, (XA_HEADS + hd) * XA_DIM:(XA_HEADS + hd + 1) * XA_DIM]
        s = _dot_nt(qh, kh) * (XA_DIM ** -0.5)
        e = jnp.exp(s - jnp.max(s, axis=-1, keepdims=True))
        p = e / jnp.sum(e, axis=-1, keepdims=True)
        outs.append(jnp.dot(p.astype(BF16), vh, preferred_element_type=F32))
    o = jnp.concatenate(outs, axis=1).astype(BF16)
    o_ref[...] = h + jnp.dot(o, wo_ref[...], preferred_element_type=F32)


def memory_xattn(h, g, wq, kv, wo, *, seq, tm):
    n, d = h.shape
    s_t = seq // tm
    dq = wq.shape[1]
    return pl.pallas_call(
        _xattn_kernel,
        out_shape=jax.ShapeDtypeStruct((n, d), F32),
        grid=(n // tm,),
        in_specs=[pl.BlockSpec((tm, d), lambda i: (i, 0)),
                  pl.BlockSpec((1, d), lambda i: (0, 0)),
                  pl.BlockSpec((d, dq), lambda i: (0, 0)),
                  pl.BlockSpec((MEM_LEN, 2 * dq), lambda i: (i // s_t, 0)),
                  pl.BlockSpec((dq, d), lambda i: (0, 0))],
        out_specs=pl.BlockSpec((tm, d), lambda i: (i, 0)),
        compiler_params=_cparams("parallel"),
        name="memory_xattn",
    )(h, g.reshape(1, d), wq, kv, wo)


def _mlp_kernel(h_ref, g_ref, w1_ref, w2_ref, gf_ref, o_ref, hn_sc, acc_sc, *, nf):
    f = pl.program_id(1)

    @pl.when(f == 0)
    def _init():
        hn_sc[...] = _rms(h_ref[...], g_ref[...]).astype(BF16)
        acc_sc[...] = jnp.zeros(acc_sc.shape, F32)

    u = jnp.dot(hn_sc[...], w1_ref[...], preferred_element_type=F32)
    u = jnp.square(jnp.maximum(u, 0.0))
    acc_sc[...] += jnp.dot(u.astype(BF16), w2_ref[...], preferred_element_type=F32)

    @pl.when(f == nf - 1)
    def _finish():
        o_ref[...] = _rms(h_ref[...] + acc_sc[...], gf_ref[...])


def mlp_final(h, g, w1, w2, gf, *, tm, tf):
    n, d = h.shape
    dff = w1.shape[1]
    nf = dff // tf
    return pl.pallas_call(
        functools.partial(_mlp_kernel, nf=nf),
        out_shape=jax.ShapeDtypeStruct((n, d), F32),
        grid=(n // tm, nf),
        in_specs=[pl.BlockSpec((tm, d), lambda i, f: (i, 0)),
                  pl.BlockSpec((1, d), lambda i, f: (0, 0)),
                  pl.BlockSpec((d, tf), lambda i, f: (0, f)),
                  pl.BlockSpec((tf, d), lambda i, f: (f, 0)),
                  pl.BlockSpec((1, d), lambda i, f: (0, 0))],
        out_specs=pl.BlockSpec((tm, d), lambda i, f: (i, 0)),
        scratch_shapes=[pltpu.VMEM((tm, d), BF16), pltpu.VMEM((tm, d), F32)],
        compiler_params=_cparams("parallel", "arbitrary"),
        name="mlp_final",
    )(h, g.reshape(1, d), w1, w2, gf.reshape(1, d))


def _rot_partner(w, half):
    return jnp.concatenate([-w[..., half:], w[..., :half]], axis=-1)


def _pad_heads_rope(w, heads, hd, rot0, rot_dim):
    k = w.shape[0]
    w = w.reshape(k, heads, hd)
    partner = _rot_partner(w[:, :, rot0:rot0 + rot_dim], rot_dim // 2)
    pad = jnp.zeros((k, heads, LANES - hd - rot_dim), w.dtype)
    return jnp.concatenate([w, partner, pad], axis=-1).reshape(k, heads * LANES)


def _rope_lane_tables(seq, rot0, rot_dim, hd):
    inv = 1.0 / (ROPE_THETA ** (jnp.arange(0, rot_dim, 2, dtype=F32) / rot_dim))
    ang = jnp.arange(seq, dtype=F32)[:, None] * inv[None, :]
    cos = jnp.concatenate([jnp.cos(ang), jnp.cos(ang)], axis=1)
    sin = jnp.concatenate([jnp.sin(ang), jnp.sin(ang)], axis=1)
    cos_t = jnp.concatenate([jnp.ones((seq, rot0), F32), cos,
                             jnp.ones((seq, hd - rot0 - rot_dim), F32),
                             jnp.zeros((seq, LANES - hd), F32)], axis=1)
    sin_t = jnp.concatenate([jnp.zeros((seq, rot0), F32), sin,
                             jnp.zeros((seq, LANES - rot0 - rot_dim), F32)], axis=1)
    return cos_t, sin_t


def _pad_lanes(w, groups, width):
    k = w.shape[0]
    w = w.reshape(k, groups, width)
    return jnp.pad(w, ((0, 0), (0, 0), (0, LANES - width))).reshape(k, groups * LANES)


def _compress_params(pos, w1, w2, d, dup_out):
    half = CMP_BLOCK // 2
    pos_p = jnp.pad(pos, ((0, 0), (0, LANES - d)))
    pos_a = pos_p[:half].reshape(1, half * LANES)
    pos_b = pos_p[half:].reshape(1, half * LANES)
    w1p = jnp.pad(w1.reshape(CMP_BLOCK, d, d), ((0, 0), (0, LANES - d), (0, LANES - d)))
    w1a = w1p[:half].reshape(half * LANES, LANES).astype(BF16)
    w1b = w1p[half:].reshape(half * LANES, LANES).astype(BF16)
    w2p = jnp.pad(w2, ((0, LANES - d), (0, LANES - d)))
    if dup_out:
        w2p = w2p.at[:d, HALF:HALF + d].set(w2)
    return pos_a, pos_b, w1a, w1b, w2p.astype(BF16)


def _cmp_to_sel(nr, nsb):
    cs = np.arange(nr) * CMP_STRIDE
    ce = cs + CMP_BLOCK
    ss = np.arange(LANES) * SEL_BLOCK
    se = ss + SEL_BLOCK
    ov = np.clip(np.minimum(ce[:, None], se[None, :]) - np.maximum(cs[:, None], ss[None, :]), 0, None)
    ov = ov.astype(np.float32) / np.float32(CMP_BLOCK)
    ov[:, nsb:] = 0.0
    ov[nr - 1:, :] = 0.0
    return jnp.asarray(ov, BF16)


def _gate_expand():
    e = np.zeros((3, NSA_HEADS // 2, LANES, LANES), np.float32)
    for br in range(3):
        for hp in range(NSA_HEADS // 2):
            for hh in range(2):
                e[br, hp, 3 * (2 * hp + hh) + br, hh * HALF:(hh + 1) * HALF] = 1.0
    return jnp.asarray(e, BF16)


def _key_block_onehot(seq):
    e = (np.arange(seq)[:, None] // SEL_BLOCK) == np.arange(LANES)[None, :]
    return jnp.asarray(e.astype(np.float32), BF16)


def kernel(x, mem, g_mix, w_in, cmp_pos_k, cmp_w1_k, cmp_w2_k, cmp_pos_v, cmp_w1_v, cmp_w2_v,
           mla_g_q, mla_w_uq, mla_g_kv, mla_w_uk, mla_w_uv, w_o_nsa, w_o_mla, w_out,
           g_xattn, g_mem, xa_wq, xa_wkv, xa_wo, g_mlp, w_ff1, w_ff2, g_final):
    b, s, d = x.shape
    assert d == D_MODEL and s % (CMP_STRIDE * 8) == 0 and s // SEL_BLOCK <= LANES
    assert g_mix.shape[0] == 1
    n = b * s
    T = _tiles(s)
    tm, tq, tk, rc = T["tm"], T["tq"], T["tk"], T["rc"]
    G = NSA_KV_HEADS
    bounds = [int(v) for v in np.cumsum(SPLITS)[:-1]]

    (w_qa, w_kc, w_vc, w_ks, w_vs, w_kw, w_vw, w_gn, w_cq, w_ckv, w_kr,
     w_ga, w_gb) = jnp.split(w_in[0], bounds, axis=1)
    nsa_rope = lambda w, heads: _pad_heads_rope(w, heads, NSA_DK, 0, NSA_ROT)
    w_rope = jnp.concatenate([nsa_rope(w_qa, NSA_HEADS), nsa_rope(w_kc, G), nsa_rope(w_ks, G),
                              nsa_rope(w_kw, G)], axis=1).astype(BF16)
    w_krp = jnp.concatenate([jnp.zeros((d, MLA_NOPE), F32), w_kr, _rot_partner(w_kr, MLA_ROPE // 2)],
                            axis=1).astype(BF16)
    w_vsw = jnp.concatenate([_pad_lanes(w_vs, G, NSA_DV), _pad_lanes(w_vw, G, NSA_DV)], axis=1).astype(BF16)
    w_vcp = _pad_lanes(w_vc, G, NSA_DV).astype(BF16)
    w_misc = jnp.concatenate([w_cq, w_ckv, jnp.pad(w_gn, ((0, 0), (0, LANES - w_gn.shape[1])))],
                             axis=1).astype(BF16)
    glog_blk = (MLA_Q_RANK + MLA_KV_RANK) // LANES
    w_gates = jnp.concatenate([w_ga, w_gb], axis=1).astype(BF16)
    cos_a, sin_a = _rope_lane_tables(s, 0, NSA_ROT, NSA_DK)
    cos_b, sin_b = _rope_lane_tables(s, MLA_NOPE, MLA_ROPE, MLA_NOPE + MLA_ROPE)
    shift_a = LANES - NSA_DK
    shift_b = LANES - (NSA_DK - MLA_NOPE)
    w_uq = _pad_heads_rope(mla_w_uq[0], MLA_HEADS, MLA_NOPE + MLA_ROPE, MLA_NOPE, MLA_ROPE).astype(BF16)
    w_uk = _pad_lanes(mla_w_uk[0], MLA_HEADS, MLA_NOPE).astype(BF16)
    w_uv = _pad_lanes(mla_w_uv[0], MLA_HEADS, MLA_DV).astype(BF16)

    x2 = x.reshape(n, d)
    xn = rmsnorm_rows(x2, g_mix[0], tm)
    qk = projection(xn, w_rope, tm=tm, tn=512, out_dtype=BF16, rope=(cos_a, sin_a, shift_a),
                    head_major=(b, s), name="proj_qk_rope")
    krp = projection(xn, w_krp, tm=tm, tn=LANES, out_dtype=BF16, rope=(cos_b, sin_b, shift_b),
                     name="proj_k_rope_mla")
    vsw = projection(xn, w_vsw, tm=tm, tn=512, out_dtype=BF16, head_major=(b, s), name="proj_v")
    vcp = projection(xn, w_vcp, tm=tm, tn=512, out_dtype=BF16, head_major=(b, s), name="proj_vc")
    misc = projection(xn, w_misc, tm=tm, tn=w_misc.shape[1], out_dtype=F32, name="proj_misc")
    gates = projection(xn, w_gates, tm=tm, tn=512, out_dtype=BF16, act="sigmoid", name="proj_gates")

    nr = s // CMP_STRIDE
    pk = _compress_params(cmp_pos_k[0], cmp_w1_k[0], cmp_w2_k[0], NSA_DK, False)
    pv = _compress_params(cmp_pos_v[0], cmp_w1_v[0], cmp_w2_v[0], NSA_DV, True)
    kc_rows = qk[:, QK_KC0:QK_KC0 + G].reshape(b, G, nr, CMP_STRIDE * LANES)
    kcc = compress(kc_rows, 0, *pk)
    vcc = compress(vcp.reshape(b, G, nr, CMP_STRIDE * LANES), 0, *pv)
    expand = _gate_expand()
    o_c, selb = nsa_compressed(qk, kcc, vcc, _cmp_to_sel(nr, s // SEL_BLOCK), misc, glog_blk, expand,
                               batch=b, seq=s, tq=T["tq_cmp"])
    nsa_scale = NSA_DK ** -0.5
    o_s = flash_heads(qk, qk, vsw, mode="select", batch=b, seq=s, tq=tq, tk=tk, rc=rc, scale=nsa_scale,
                      q_head0=QK_Q0, k_head0=QK_KS0, v_head0=0, kv_shared=True, out_dtype=F32,
                      selb=selb, onehot=_key_block_onehot(s), gate=(misc, glog_blk, expand, 1),
                      name="nsa_selected")
    o_w = flash_heads(qk, qk, vsw, mode="window", batch=b, seq=s, tq=tq, tk=tk, rc=rc, scale=nsa_scale,
                      q_head0=QK_Q0, k_head0=QK_KW0, v_head0=G, kv_shared=True, out_dtype=F32,
                      gate=(misc, glog_blk, expand, 2), name="nsa_window")

    qm = projection(misc, w_uq, tm=tm, tn=512, out_dtype=BF16, a_cols=(MLA_Q_RANK, 0), gain=mla_g_q[0],
                    rope=(cos_b, sin_b, shift_b), head_major=(b, s), name="mla_q")
    km = projection(misc, w_uk, tm=tm, tn=512, out_dtype=BF16,
                    a_cols=(MLA_KV_RANK, MLA_Q_RANK // MLA_KV_RANK), gain=mla_g_kv[0], add=krp,
                    head_major=(b, s), name="mla_k")
    vm = projection(misc, w_uv, tm=tm, tn=512, out_dtype=BF16,
                    a_cols=(MLA_KV_RANK, MLA_Q_RANK // MLA_KV_RANK), gain=mla_g_kv[0],
                    head_major=(b, s), name="mla_v")
    o_b = flash_heads(qm, km, vm, mode="causal", batch=b, seq=s, tq=tq, tk=tk, rc=rc,
                      scale=(MLA_NOPE + MLA_ROPE) ** -0.5, q_head0=0, k_head0=0, v_head0=0,
                      kv_shared=False, out_dtype=BF16, name="mla_attention")

    mixed = gated_mix(o_c, o_s, o_w, o_b, gates, w_o_nsa[0].astype(BF16), w_o_mla[0].astype(BF16),
                      tm=tm, tn=512)
    h1 = projection(mixed, w_out[0].astype(BF16), tm=tm, tn=512, out_dtype=F32, res=x2, name="proj_out")

    kv_mem = projection(mem.reshape(b * MEM_LEN, d), xa_wkv[0].astype(BF16), tm=MEM_LEN, tn=512,
                        out_dtype=BF16, gain=g_mem[0], name="xattn_kv")
    h2 = memory_xattn(h1, g_xattn[0], xa_wq[0].astype(BF16), kv_mem, xa_wo[0].astype(BF16), seq=s, tm=tm)

    out = mlp_final(h2, g_mlp[0], w_ff1[0].astype(BF16), w_ff2[0].astype(BF16), g_final, tm=tm, tf=T["tf"])
    return out.reshape(b, s, d)
```

```python
import functools

import numpy as np
import jax
import jax.numpy as jnp
from jax import lax
from jax.experimental import pallas as pl
from jax.experimental.pallas import tpu as pltpu

F32 = jnp.float32
BF16 = jnp.bfloat16

D_MODEL = 2048
MEM_LEN = 256
ROPE_THETA = 500000.0
EPS = 1e-6
NEG = -1e30

NSA_HEADS = 16
NSA_KV_HEADS = 4
NSA_HPG = NSA_HEADS // NSA_KV_HEADS
NSA_DK = 96
NSA_DV = 64
NSA_ROT = NSA_DK // 4
CMP_BLOCK = 32
CMP_STRIDE = 16
SEL_BLOCK = 64
SEL_TOPK = 16
WINDOW = 512

MLA_HEADS = 16
MLA_NOPE = 64
MLA_ROPE = 32
MLA_DV = 64
MLA_Q_RANK = 512
MLA_KV_RANK = 256

XA_HEADS = 4
XA_DIM = 128
D_FF = 4 * D_MODEL

SPLITS = (NSA_HEADS * NSA_DK,
          NSA_KV_HEADS * NSA_DK, NSA_KV_HEADS * NSA_DV,
          NSA_KV_HEADS * NSA_DK, NSA_KV_HEADS * NSA_DV,
          NSA_KV_HEADS * NSA_DK, NSA_KV_HEADS * NSA_DV,
          NSA_HEADS * 3,
          MLA_Q_RANK, MLA_KV_RANK, MLA_ROPE,
          D_MODEL, D_MODEL)

LANES = 128
HALF = LANES // 2
VMEM_LIMIT = 56 * 1024 * 1024

QK_Q0 = 0
QK_KC0 = NSA_HEADS
QK_KS0 = NSA_HEADS + NSA_KV_HEADS
QK_KW0 = NSA_HEADS + 2 * NSA_KV_HEADS
QK_HEADS = NSA_HEADS + 3 * NSA_KV_HEADS
FLASH_HEADS = NSA_HPG


def _cparams(*sem):
    return pltpu.CompilerParams(dimension_semantics=sem, vmem_limit_bytes=VMEM_LIMIT)


def _tiles(seq):
    return dict(
        tm=min(1024, seq),
        tm_x=min(512, seq),
        tq=min(512, seq),
        tk=min(512, seq),
        rc=32,
        tq_cmp=min(256, seq),
        tf=512,
    )


def _rms(x, g):
    return x * lax.rsqrt(jnp.mean(x * x, axis=-1, keepdims=True) + EPS) * g


def _rmsnorm_kernel(x_ref, g_ref, o_ref):
    o_ref[...] = _rms(x_ref[...], g_ref[...]).astype(o_ref.dtype)


def rmsnorm_rows(x, g, tm):
    n, d = x.shape
    return pl.pallas_call(
        _rmsnorm_kernel,
        out_shape=jax.ShapeDtypeStruct((n, d), BF16),
        grid=(n // tm,),
        in_specs=[pl.BlockSpec((tm, d), lambda i: (i, 0)),
                  pl.BlockSpec((1, d), lambda i: (0, 0))],
        out_specs=pl.BlockSpec((tm, d), lambda i: (i, 0)),
        compiler_params=_cparams("parallel"),
        name="rmsnorm_rows",
    )(x, g.reshape(1, d))


def _proj_kernel(*refs, has_gain, has_rope, roll_shift, has_add, has_res, act, head_major):
    it = iter(refs)
    a_ref = next(it)
    g_ref = next(it) if has_gain else None
    w_ref = next(it)
    cos_ref = next(it) if has_rope else None
    sin_ref = next(it) if has_rope else None
    add_ref = next(it) if has_add else None
    res_ref = next(it) if has_res else None
    o_ref = next(it)

    a = a_ref[...]
    if has_gain:
        a = _rms(a, g_ref[...]).astype(BF16)
    y = jnp.dot(a, w_ref[...], preferred_element_type=F32)
    if has_rope or has_add or head_major:
        for h in range(y.shape[1] // LANES):
            yh = y[:, h * LANES:(h + 1) * LANES]
            if has_rope:
                yh = yh * cos_ref[...] + pltpu.roll(yh, roll_shift, 1) * sin_ref[...]
            if has_add:
                yh = yh + add_ref[...].astype(F32)
            if head_major:
                o_ref[h] = yh.astype(o_ref.dtype)
            else:
                o_ref[:, h * LANES:(h + 1) * LANES] = yh.astype(o_ref.dtype)
    else:
        if act == "sigmoid":
            y = jax.nn.sigmoid(y)
        if has_res:
            y = res_ref[...] + y
        o_ref[...] = y.astype(o_ref.dtype)


def projection(a, w, *, tm, tn, out_dtype, a_cols=None, gain=None, rope=None, add=None,
               res=None, act=None, head_major=None, name="projection"):
    n = a.shape[0]
    k, nc = w.shape
    a_w, a_blk = a_cols if a_cols is not None else (a.shape[1], 0)
    assert a_w == k and n % tm == 0 and nc % tn == 0
    grid = (n // tm, nc // tn)
    in_specs = [pl.BlockSpec((tm, k), lambda i, j: (i, a_blk))]
    args = [a]
    if gain is not None:
        in_specs.append(pl.BlockSpec((1, k), lambda i, j: (0, 0)))
        args.append(gain.reshape(1, k))
    in_specs.append(pl.BlockSpec((k, tn), lambda i, j: (0, j)))
    args.append(w)
    roll_shift = 0
    if rope is not None:
        cos, sin, roll_shift = rope
        s_tiles = cos.shape[0] // tm
        for t in (cos, sin):
            in_specs.append(pl.BlockSpec((tm, LANES), lambda i, j: (i % s_tiles, 0)))
            args.append(t)
    if add is not None:
        in_specs.append(pl.BlockSpec((tm, LANES), lambda i, j: (i, 0)))
        args.append(add)
    if res is not None:
        in_specs.append(pl.BlockSpec((tm, tn), lambda i, j: (i, j)))
        args.append(res)
    if head_major is not None:
        b, s = head_major
        s_t = s // tm
        hpt = tn // LANES
        out_shape = jax.ShapeDtypeStruct((b, nc // LANES, s, LANES), out_dtype)
        out_spec = pl.BlockSpec((None, hpt, tm, LANES), lambda i, j: (i // s_t, j, i % s_t, 0))
    else:
        out_shape = jax.ShapeDtypeStruct((n, nc), out_dtype)
        out_spec = pl.BlockSpec((tm, tn), lambda i, j: (i, j))
    kern = functools.partial(
        _proj_kernel, has_gain=gain is not None, has_rope=rope is not None, roll_shift=roll_shift,
        has_add=add is not None, has_res=res is not None, act=act, head_major=head_major is not None)
    return pl.pallas_call(
        kern, out_shape=out_shape, grid=grid, in_specs=in_specs, out_specs=out_spec,
        compiler_params=_cparams("parallel", "arbitrary"), name=name,
    )(*args)


def _compress_kernel(r_ref, pa_ref, pb_ref, w1a_ref, w1b_ref, w2_ref, o_ref):
    r = r_ref[...].astype(F32)
    a = jnp.dot((r + pa_ref[...]).astype(BF16), w1a_ref[...], preferred_element_type=F32)
    b = jnp.dot((r + pb_ref[...]).astype(BF16), w1b_ref[...], preferred_element_type=F32)
    nr = a.shape[0]
    hid = a + pltpu.roll(b, nr - 1, 0)
    hid = jax.nn.gelu(hid)
    o_ref[...] = jnp.dot(hid.astype(BF16), w2_ref[...], preferred_element_type=F32).astype(o_ref.dtype)


def compress(r, head0, pos_a, pos_b, w1a, w1b, w2):
    b, _, nr, kk = r.shape
    g = NSA_KV_HEADS
    full = lambda shape: pl.BlockSpec(shape, lambda bi, gi: (0,) * len(shape))
    return pl.pallas_call(
        _compress_kernel,
        out_shape=jax.ShapeDtypeStruct((b, g, nr, LANES), BF16),
        grid=(b, g),
        in_specs=[pl.BlockSpec((None, None, nr, kk), lambda bi, gi: (bi, head0 + gi, 0, 0)),
                  full((1, kk)), full((1, kk)), full((kk, LANES)), full((kk, LANES)),
                  full((LANES, LANES))],
        out_specs=pl.BlockSpec((None, None, nr, LANES), lambda bi, gi: (bi, gi, 0, 0)),
        compiler_params=_cparams("parallel", "parallel"),
        name="nsa_compress",
    )(r, pos_a, pos_b, w1a, w1b, w2)


def _dot_nt(a, b):
    return lax.dot_general(a, b, (((1,), (1,)), ((), ())), preferred_element_type=F32)


def _split_dot(x, w):
    hi = x.astype(BF16)
    lo = (x - hi.astype(F32)).astype(BF16)
    return (jnp.dot(hi, w, preferred_element_type=F32) + jnp.dot(lo, w, preferred_element_type=F32))


def _nsa_cmp_kernel(q_ref, kcc_ref, vcc_ref, msel_ref, glog_ref, e_ref, oc_ref, sel_ref, s_sc, p_sc,
                    *, tq, rc, c2, topk):
    s0 = pl.program_id(2) * tq
    nr = kcc_ref.shape[0]
    for h in range(NSA_HPG):
        s_sc[h] = _dot_nt(q_ref[h], kcc_ref[...])
    lane_v = lax.broadcasted_iota(jnp.int32, (nr, LANES), 1)
    vcc = vcc_ref[...]
    vm = jnp.concatenate([jnp.where(lane_v == HALF, jnp.ones_like(vcc), vcc), msel_ref[...]], axis=1)
    dmat = (lax.broadcasted_iota(jnp.int32, (rc, nr), 0)
            - lax.broadcasted_iota(jnp.int32, (rc, nr), 1) * CMP_STRIDE)
    outs = []
    imp = jnp.zeros((tq, LANES), F32)
    for h in range(NSA_HPG):
        for c in range(tq // rc):
            rows = pl.ds(c * rc, rc)
            msk = dmat + (s0 + c * rc - (CMP_BLOCK - 1)) >= 0
            s = jnp.where(msk, s_sc[h, rows, :], NEG)
            m = jnp.max(s, axis=-1, keepdims=True)
            p_sc[h, rows, :] = jnp.where(msk, jnp.exp2((s - m) * c2), 0.0).astype(BF16)
        a = jnp.dot(p_sc[h], vm, preferred_element_type=F32)
        denom = jnp.maximum(a[:, HALF:HALF + 1], 1e-30)
        outs.append(a[:, :LANES] / denom)
        imp = imp + a[:, LANES:] / denom

    blk = lax.broadcasted_iota(jnp.int32, (tq, LANES), 1)
    cur = (s0 + lax.broadcasted_iota(jnp.int32, (tq, LANES), 0)) // SEL_BLOCK
    valid = blk <= cur
    forced = (blk == 0) | (blk == cur) | (blk == cur - 1)
    score = jnp.where(valid, jnp.where(forced, 1e4, imp), -1.0)
    sc = score.T
    rowid = lax.broadcasted_iota(jnp.int32, (LANES, tq), 0).astype(F32)

    def pick_one(_, sc):
        cm = jnp.max(sc, axis=0, keepdims=True)
        first = jnp.min(jnp.where(sc == cm, rowid, float(LANES)), axis=0, keepdims=True)
        return jnp.where(rowid == first, -jnp.inf, sc)

    picked = lax.fori_loop(0, topk, pick_one, sc) == -jnp.inf
    sel_ref[...] = jnp.where(picked, 0.0, NEG).T.astype(sel_ref.dtype)

    glog = glog_ref[...]
    lane = lax.broadcasted_iota(jnp.int32, (tq, LANES), 1)
    for pr in range(NSA_HPG // 2):
        gate = jax.nn.sigmoid(_split_dot(glog, e_ref[pr]))
        pair = jnp.where(lane < HALF, outs[2 * pr], pltpu.roll(outs[2 * pr + 1], HALF, 1))
        oc_ref[:, pr * LANES:(pr + 1) * LANES] = (gate * pair).astype(oc_ref.dtype)


def nsa_compressed(qk, kcc, vcc, msel, misc, glog_blk, expand, *, batch, seq, tq, rc):
    g = NSA_KV_HEADS
    nq = seq // tq
    nr = kcc.shape[2]
    topk = min(SEL_TOPK, seq // SEL_BLOCK)
    kern = functools.partial(_nsa_cmp_kernel, tq=tq, rc=rc, c2=float(NSA_DK ** -0.5 * np.log2(np.e)),
                             topk=topk)
    return pl.pallas_call(
        kern,
        out_shape=(jax.ShapeDtypeStruct((batch * seq, NSA_HEADS * NSA_DV), F32),
                   jax.ShapeDtypeStruct((batch, g, seq, LANES), BF16)),
        grid=(batch, g, nq),
        in_specs=[
            pl.BlockSpec((None, NSA_HPG, tq, LANES), lambda b, gi, qi: (b, gi, qi, 0)),
            pl.BlockSpec((None, None, nr, LANES), lambda b, gi, qi: (b, gi, 0, 0)),
            pl.BlockSpec((None, None, nr, LANES), lambda b, gi, qi: (b, gi, 0, 0)),
            pl.BlockSpec((nr, LANES), lambda b, gi, qi: (0, 0)),
            pl.BlockSpec((tq, LANES), lambda b, gi, qi: (b * nq + qi, glog_blk)),
            pl.BlockSpec((None, 2, LANES, LANES), lambda b, gi, qi: (0, gi, 0, 0)),
        ],
        out_specs=(pl.BlockSpec((tq, 2 * LANES), lambda b, gi, qi: (b * nq + qi, gi)),
                   pl.BlockSpec((None, None, tq, LANES), lambda b, gi, qi: (b, gi, qi, 0))),
        scratch_shapes=[pltpu.VMEM((NSA_HPG, tq, nr), F32), pltpu.VMEM((NSA_HPG, tq, nr), BF16)],
        compiler_params=_cparams("parallel", "parallel", "parallel"),
        name="nsa_compressed_select",
    )(qk, kcc, vcc, msel, misc, expand)


def _flash_steps(mode, nq, tq, tk):
    r = tq // tk
    qi_l, kt_l, first_l, last_l = [], [], [], []
    for qi in range(nq):
        hi = qi * r + r - 1
        lo = max(0, qi * r - (-(-(WINDOW - 1) // tk))) if mode == "window" else 0
        for kt in range(lo, hi + 1):
            qi_l.append(qi)
            kt_l.append(kt)
            first_l.append(int(kt == lo))
            last_l.append(int(kt == hi))
    return tuple(jnp.asarray(np.asarray(a, np.int32)) for a in (qi_l, kt_l, first_l, last_l))


def _flash_kernel(*refs, mode, tq, tk, rc, c2, kv_shared, gated):
    it = iter(refs)
    qi_ref, kt_ref, first_ref, last_ref = next(it), next(it), next(it), next(it)
    q_ref, k_ref, v_ref = next(it), next(it), next(it)
    selb_ref = next(it) if mode == "select" else None
    oh_ref = next(it) if mode == "select" else None
    glog_ref = next(it) if gated else None
    e_ref = next(it) if gated else None
    o_ref = next(it)
    m_sc, acc_sc, s_sc, p_sc = next(it), next(it), next(it), next(it)
    qa_sc = next(it) if mode == "select" else None

    st = pl.program_id(2)
    s0 = qi_ref[st] * tq
    k0 = kt_ref[st] * tk
    reps = tk // LANES

    @pl.when(first_ref[st] == 1)
    def _init():
        m_sc[...] = jnp.full(m_sc.shape, NEG, F32)
        acc_sc[...] = jnp.zeros(acc_sc.shape, F32)
        if mode == "select":
            for hh in range(FLASH_HEADS):
                qa_sc[hh] = jnp.concatenate([q_ref[hh], selb_ref[...]], axis=1)

    def tile(masked):
        for hh in range(FLASH_HEADS):
            k = k_ref[0 if kv_shared else hh]
            if mode == "select":
                s_sc[hh] = _dot_nt(qa_sc[hh], jnp.concatenate([k, oh_ref[...]], axis=1))
            else:
                s_sc[hh] = _dot_nt(q_ref[hh], k)
        lane_v = lax.broadcasted_iota(jnp.int32, (tk, LANES), 1)
        if masked:
            dmat = (lax.broadcasted_iota(jnp.int32, (rc, tk), 0)
                    - lax.broadcasted_iota(jnp.int32, (rc, tk), 1))
        for hh in range(FLASH_HEADS):
            v = v_ref[0 if kv_shared else hh]
            v_aug = jnp.where(lane_v == HALF, jnp.ones_like(v), v)

            for c in range(tq // rc):
                r0 = c * rc
                rows = pl.ds(r0, rc)
                s = s_sc[hh, rows, :]
                if masked:
                    d = dmat + (s0 - k0 + r0)
                    msk = (d >= 0) & (d < WINDOW) if mode == "window" else d >= 0
                    s = jnp.where(msk, s, NEG)
                m_old = m_sc[hh, rows, :]
                m_new = jnp.maximum(m_old, jnp.max(s, axis=-1, keepdims=True))
                p = jnp.exp2((s - jnp.concatenate([m_new] * reps, axis=1)) * c2)
                if masked:
                    p = jnp.where(msk, p, 0.0)
                m_sc[hh, rows, :] = m_new
                acc_sc[hh, rows, :] = acc_sc[hh, rows, :] * jnp.exp2((m_old - m_new) * c2)
                p_sc[hh, rows, :] = p.astype(BF16)
            acc_sc[hh] +=jnp.dot(p_sc[hh], v_aug, preferred_element_type=F32)

    if mode == "window":
        tile(True)
    else:
        interior = k0 + tk - 1 <= s0

        @pl.when(interior)
        def _interior():
            tile(False)

        @pl.when(jnp.logical_not(interior))
        def _diagonal():
            tile(True)

    @pl.when(last_ref[st] == 1)
    def _finish():
        lane = lax.broadcasted_iota(jnp.int32, (tq, LANES), 1)
        if gated:
            glog = glog_ref[...]
        for pr in range(FLASH_HEADS // 2):
            outs = []
            for hh in (2 * pr, 2 * pr + 1):
                a = acc_sc[hh]
                outs.append(a / jnp.maximum(a[:, HALF:HALF + 1], 1e-30))
            out = jnp.where(lane < HALF, outs[0], pltpu.roll(outs[1], HALF, 1))
            if gated:
                out = jax.nn.sigmoid(_split_dot(glog, e_ref[pr])) * out
            o_ref[:, pr * LANES:(pr + 1) * LANES] = out.astype(o_ref.dtype)


def flash_heads(q, k, v, *, mode, batch, seq, tq, tk, rc, scale, q_head0, k_head0, v_head0,
                kv_shared, out_dtype, selb=None, onehot=None, gate=None, name="flash"):
    nh = FLASH_HEADS
    ngroups = NSA_HEADS // nh
    nq = seq // tq
    assert tq % tk == 0 and tq % rc == 0
    tabs = _flash_steps(mode, nq, tq, tk)
    nsteps = tabs[0].shape[0]
    if kv_shared:
        kv_spec = lambda h0: pl.BlockSpec(
            (None, 1, tk, LANES), lambda b, g, st, qi, kt, fi, la: (b, h0 + g, kt[st], 0))
    else:
        kv_spec = lambda h0: pl.BlockSpec(
            (None, nh, tk, LANES), lambda b, g, st, qi, kt, fi, la: (b, h0 // nh + g, kt[st], 0))
    in_specs = [pl.BlockSpec((None, nh, tq, LANES),
                             lambda b, g, st, qi, kt, fi, la: (b, q_head0 // nh + g, qi[st], 0)),
                kv_spec(k_head0), kv_spec(v_head0)]
    args = [q, k, v]
    scratch = [pltpu.VMEM((nh, tq, LANES), F32), pltpu.VMEM((nh, tq, LANES), F32),
               pltpu.VMEM((nh, tq, tk), F32), pltpu.VMEM((nh, tq, tk), BF16)]
    if mode == "select":
        in_specs += [pl.BlockSpec((None, None, tq, LANES),
                                  lambda b, g, st, qi, kt, fi, la: (b, g, qi[st], 0)),
                     pl.BlockSpec((tk, LANES), lambda b, g, st, qi, kt, fi, la: (kt[st], 0))]
        args += [selb, onehot]
        scratch.append(pltpu.VMEM((nh, tq, 2 * LANES), BF16))
    if gate is not None:
        misc, glog_blk, expand, branch = gate
        in_specs += [pl.BlockSpec((tq, LANES),
                                  lambda b, g, st, qi, kt, fi, la: (b * nq + qi[st], glog_blk)),
                     pl.BlockSpec((None, nh // 2, LANES, LANES),
                                  lambda b, g, st, qi, kt, fi, la: (branch, g, 0, 0))]
        args += [misc, expand]
    kern = functools.partial(_flash_kernel, mode=mode, tq=tq, tk=tk, rc=rc,
                             c2=float(scale * np.log2(np.e)), kv_shared=kv_shared, gated=gate is not None)
    ow = nh * HALF
    grid_spec = pltpu.PrefetchScalarGridSpec(
        num_scalar_prefetch=4, grid=(batch, ngroups, nsteps), in_specs=in_specs,
        out_specs=pl.BlockSpec((tq, ow), lambda b, g, st, qi, kt, fi, la: (b * nq + qi[st], g)),
        scratch_shapes=scratch)
    return pl.pallas_call(
        kern,
        out_shape=jax.ShapeDtypeStruct((batch * seq, ngroups * ow), out_dtype),
        grid_spec=grid_spec,
        compiler_params=_cparams("parallel", "parallel", "arbitrary"),
        name=name,
    )(*tabs, *args)


def _mix_kernel(oc_ref, os_ref, ow_ref, ob_ref, ga_ref, gb_ref, wa_ref, wb_ref, o_ref):
    oa = (oc_ref[...] + os_ref[...] + ow_ref[...]).astype(BF16)
    ya = jnp.dot(oa, wa_ref[...], preferred_element_type=F32)
    yb = jnp.dot(ob_ref[...], wb_ref[...], preferred_element_type=F32)
    o_ref[...] = (ga_ref[...].astype(F32) * ya + gb_ref[...].astype(F32) * yb).astype(o_ref.dtype)


def gated_mix(oc, osel, ow, ob, gates, wa, wb, *, tm, tn):
    n, ka = oc.shape
    d = wa.shape[1]
    nb = d // tn
    row = lambda w: pl.BlockSpec((tm, w), lambda i, j: (i, 0))
    return pl.pallas_call(
        _mix_kernel,
        out_shape=jax.ShapeDtypeStruct((n, d), BF16),
        grid=(n // tm, nb),
        in_specs=[row(ka), row(ka), row(ka), row(ob.shape[1]),
                  pl.BlockSpec((tm, tn), lambda i, j: (i, j)),
                  pl.BlockSpec((tm, tn), lambda i, j: (i, nb + j)),
                  pl.BlockSpec((ka, tn), lambda i, j: (0, j)),
                  pl.BlockSpec((wb.shape[0], tn), lambda i, j: (0, j))],
        out_specs=pl.BlockSpec((tm, tn), lambda i, j: (i, j)),
        compiler_params=_cparams("parallel", "arbitrary"),
        name="gated_mix",
    )(oc, osel, ow, ob, gates, gates, wa, wb)


def _xattn_kernel(h_ref, g_ref, wq_ref, kv_ref, wo_ref, o_ref):
    h = h_ref[...]
    hn = _rms(h, g_ref[...]).astype(BF16)
    q = jnp.dot(hn, wq_ref[...], preferred_element_type=F32).astype(BF16)
    kv = kv_ref[...]
    outs = []
    for hd in range(XA_HEADS):
        qh = q[:, hd * XA_DIM:(hd + 1) * XA_DIM]
        kh = kv[:, hd * XA_DIM:(hd + 1) * XA_DIM]
        vh = kv[:, (XA_HEADS + hd) * XA_DIM:(XA_HEADS + hd + 1) * XA_DIM]
        s = _dot_nt(qh, kh) * (XA_DIM ** -0.5)
        e = jnp.exp(s - jnp.max(s, axis=-1, keepdims=True))
        p = e / jnp.sum(e, axis=-1, keepdims=True)
        outs.append(jnp.dot(p.astype(BF16), vh, preferred_element_type=F32))
    o = jnp.concatenate(outs, axis=1).astype(BF16)
    o_ref[...] = h + jnp.dot(o, wo_ref[...], preferred_element_type=F32)


def memory_xattn(h, g, wq, kv, wo, *, seq, tm):
    n, d = h.shape
    s_t = seq // tm
    dq = wq.shape[1]
    return pl.pallas_call(
        _xattn_kernel,
        out_shape=jax.ShapeDtypeStruct((n, d), F32),
        grid=(n // tm,),
        in_specs=[pl.BlockSpec((tm, d), lambda i: (i, 0)),
                  pl.BlockSpec((1, d), lambda i: (0, 0)),
                  pl.BlockSpec((d, dq), lambda i: (0, 0)),
                  pl.BlockSpec((MEM_LEN, 2 * dq), lambda i: (i // s_t, 0)),
                  pl.BlockSpec((dq, d), lambda i: (0, 0))],
        out_specs=pl.BlockSpec((tm, d), lambda i: (i, 0)),
        compiler_params=_cparams("parallel"),
        name="memory_xattn",
    )(h, g.reshape(1, d), wq, kv, wo)


def _mlp_kernel(h_ref, g_ref, w1_ref, w2_ref, gf_ref, o_ref, hn_sc, acc_sc, *, nf):
    f = pl.program_id(1)

    @pl.when(f == 0)
    def _init():
        hn_sc[...] = _rms(h_ref[...], g_ref[...]).astype(BF16)
        acc_sc[...] = jnp.zeros(acc_sc.shape, F32)

    u = jnp.dot(hn_sc[...], w1_ref[...], preferred_element_type=F32)
    u = jnp.square(jnp.maximum(u, 0.0))
    acc_sc[...] += jnp.dot(u.astype(BF16), w2_ref[...], preferred_element_type=F32)

    @pl.when(f == nf - 1)
    def _finish():
        o_ref[...] = _rms(h_ref[...] + acc_sc[...], gf_ref[...])


def mlp_final(h, g, w1, w2, gf, *, tm, tf):
    n, d = h.shape
    dff = w1.shape[1]
    nf = dff // tf
    return pl.pallas_call(
        functools.partial(_mlp_kernel, nf=nf),
        out_shape=jax.ShapeDtypeStruct((n, d), F32),
        grid=(n // tm, nf),
        in_specs=[pl.BlockSpec((tm, d), lambda i, f: (i, 0)),
                  pl.BlockSpec((1, d), lambda i, f: (0, 0)),
                  pl.BlockSpec((d, tf), lambda i, f: (0, f)),
                  pl.BlockSpec((tf, d), lambda i, f: (f, 0)),
                  pl.BlockSpec((1, d), lambda i, f: (0, 0))],
        out_specs=pl.BlockSpec((tm, d), lambda i, f: (i, 0)),
        scratch_shapes=[pltpu.VMEM((tm, d), BF16), pltpu.VMEM((tm, d), F32)],
        compiler_params=_cparams("parallel", "arbitrary"),
        name="mlp_final",
    )(h, g.reshape(1, d), w1, w2, gf.reshape(1, d))


def _rot_partner(w, half):
    return jnp.concatenate([-w[..., half:], w[..., :half]], axis=-1)


def _pad_heads_rope(w, heads, hd, rot0, rot_dim):
    k = w.shape[0]
    w = w.reshape(k, heads, hd)
    partner = _rot_partner(w[:, :, rot0:rot0 + rot_dim], rot_dim // 2)
    pad = jnp.zeros((k, heads, LANES - hd - rot_dim), w.dtype)
    return jnp.concatenate([w, partner, pad], axis=-1).reshape(k, heads * LANES)


def _rope_lane_tables(seq, rot0, rot_dim, hd):
    inv = 1.0 / (ROPE_THETA ** (jnp.arange(0, rot_dim, 2, dtype=F32) / rot_dim))
    ang = jnp.arange(seq, dtype=F32)[:, None] * inv[None, :]
    cos = jnp.concatenate([jnp.cos(ang), jnp.cos(ang)], axis=1)
    sin = jnp.concatenate([jnp.sin(ang), jnp.sin(ang)], axis=1)
    cos_t = jnp.concatenate([jnp.ones((seq, rot0), F32), cos,
                             jnp.ones((seq, hd - rot0 - rot_dim), F32),
                             jnp.zeros((seq, LANES - hd), F32)], axis=1)
    sin_t = jnp.concatenate([jnp.zeros((seq, rot0), F32), sin,
                             jnp.zeros((seq, LANES - rot0 - rot_dim), F32)], axis=1)
    return cos_t, sin_t


def _pad_lanes(w, groups, width):
    k = w.shape[0]
    w = w.reshape(k, groups, width)
    return jnp.pad(w, ((0, 0), (0, 0), (0, LANES - width))).reshape(k, groups * LANES)


def _compress_params(pos, w1, w2, d):
    half = CMP_BLOCK // 2
    pos_p = jnp.pad(pos, ((0, 0), (0, LANES - d)))
    pos_a = pos_p[:half].reshape(1, half * LANES)
    pos_b = pos_p[half:].reshape(1, half * LANES)
    w1p = jnp.pad(w1.reshape(CMP_BLOCK, d, d), ((0, 0), (0, LANES - d), (0, LANES - d)))
    w1a = w1p[:half].reshape(half * LANES, LANES).astype(BF16)
    w1b = w1p[half:].reshape(half * LANES, LANES).astype(BF16)
    w2p = jnp.pad(w2, ((0, LANES - d), (0, LANES - d)))
    return pos_a, pos_b, w1a, w1b, w2p.astype(BF16)


def _cmp_to_sel(nr, nsb):
    cs = np.arange(nr) * CMP_STRIDE
    ce = cs + CMP_BLOCK
    ss = np.arange(LANES) * SEL_BLOCK
    se = ss + SEL_BLOCK
    ov = np.clip(np.minimum(ce[:, None], se[None, :]) - np.maximum(cs[:, None], ss[None, :]), 0, None)
    ov = ov.astype(np.float32) / np.float32(CMP_BLOCK)
    ov[:, nsb:] = 0.0
    ov[nr - 1:, :] = 0.0
    return jnp.asarray(ov, BF16)


def _gate_expand():
    e = np.zeros((3, NSA_HEADS // 2, LANES, LANES), np.float32)
    for br in range(3):
        for hp in range(NSA_HEADS // 2):
            for hh in range(2):
                e[br, hp, 3 * (2 * hp + hh) + br, hh * HALF:(hh + 1) * HALF] = 1.0
    return jnp.asarray(e, BF16)


def _key_block_onehot(seq):
    e = (np.arange(seq)[:, None] // SEL_BLOCK) == np.arange(LANES)[None, :]
    return jnp.asarray(e.astype(np.float32), BF16)


def kernel(x, mem, g_mix, w_in, cmp_pos_k, cmp_w1_k, cmp_w2_k, cmp_pos_v, cmp_w1_v, cmp_w2_v,
           mla_g_q, mla_w_uq, mla_g_kv, mla_w_uk, mla_w_uv, w_o_nsa, w_o_mla, w_out,
           g_xattn, g_mem, xa_wq, xa_wkv, xa_wo, g_mlp, w_ff1, w_ff2, g_final):
    b, s, d = x.shape
    assert d == D_MODEL and s % (CMP_STRIDE * 8) == 0 and s // SEL_BLOCK <= LANES
    assert g_mix.shape[0] == 1
    n = b * s
    T = _tiles(s)
    tm, tq, tk, rc = T["tm"], T["tq"], T["tk"], T["rc"]
    G = NSA_KV_HEADS
    bounds = [int(v) for v in np.cumsum(SPLITS)[:-1]]

    (w_qa, w_kc, w_vc, w_ks, w_vs, w_kw, w_vw, w_gn, w_cq, w_ckv, w_kr,
     w_ga, w_gb) = jnp.split(w_in[0], bounds, axis=1)
    nsa_rope = lambda w, heads: _pad_heads_rope(w, heads, NSA_DK, 0, NSA_ROT)
    w_rope = jnp.concatenate([nsa_rope(w_qa, NSA_HEADS), nsa_rope(w_kc, G), nsa_rope(w_ks, G),
                              nsa_rope(w_kw, G)], axis=1).astype(BF16)
    w_krp = jnp.concatenate([jnp.zeros((d, MLA_NOPE), F32), w_kr, _rot_partner(w_kr, MLA_ROPE // 2)],
                            axis=1).astype(BF16)
    w_vsw = jnp.concatenate([_pad_lanes(w_vs, G, NSA_DV), _pad_lanes(w_vw, G, NSA_DV)], axis=1).astype(BF16)
    w_vcp = _pad_lanes(w_vc, G, NSA_DV).astype(BF16)
    w_misc = jnp.concatenate([w_cq, w_ckv, jnp.pad(w_gn, ((0, 0), (0, LANES - w_gn.shape[1])))],
                             axis=1).astype(BF16)
    glog_blk = (MLA_Q_RANK + MLA_KV_RANK) // LANES
    w_gates = jnp.concatenate([w_ga, w_gb], axis=1).astype(BF16)
    cos_a, sin_a = _rope_lane_tables(s, 0, NSA_ROT, NSA_DK)
    cos_b, sin_b = _rope_lane_tables(s, MLA_NOPE, MLA_ROPE, MLA_NOPE + MLA_ROPE)
    shift_a = LANES - NSA_DK
    shift_b = LANES - (NSA_DK - MLA_NOPE)
    w_uq = _pad_heads_rope(mla_w_uq[0], MLA_HEADS, MLA_NOPE + MLA_ROPE, MLA_NOPE, MLA_ROPE).astype(BF16)
    w_uk = _pad_lanes(mla_w_uk[0], MLA_HEADS, MLA_NOPE).astype(BF16)
    w_uv = _pad_lanes(mla_w_uv[0], MLA_HEADS, MLA_DV).astype(BF16)

    x2 = x.reshape(n, d)
    xn = rmsnorm_rows(x2, g_mix[0], tm)
    qk = projection(xn, w_rope, tm=tm, tn=512, out_dtype=BF16, rope=(cos_a, sin_a, shift_a),
                    head_major=(b, s), name="proj_qk_rope")
    krp = projection(xn, w_krp, tm=tm, tn=LANES, out_dtype=BF16, rope=(cos_b, sin_b, shift_b),
                     name="proj_k_rope_mla")
    vsw = projection(xn, w_vsw, tm=tm, tn=512, out_dtype=BF16, head_major=(b, s), name="proj_v")
    vcp = projection(xn, w_vcp, tm=tm, tn=512, out_dtype=BF16, head_major=(b, s), name="proj_vc")
    misc = projection(xn, w_misc, tm=tm, tn=w_misc.shape[1], out_dtype=F32, name="proj_misc")
    gates = projection(xn, w_gates, tm=tm, tn=512, out_dtype=BF16, act="sigmoid", name="proj_gates")

    nr = s // CMP_STRIDE
    pk = _compress_params(cmp_pos_k[0], cmp_w1_k[0], cmp_w2_k[0], NSA_DK)
    pv = _compress_params(cmp_pos_v[0], cmp_w1_v[0], cmp_w2_v[0], NSA_DV)
    kc_rows = qk[:, QK_KC0:QK_KC0 + G].reshape(b, G, nr, CMP_STRIDE * LANES)
    kcc = compress(kc_rows, 0, *pk)
    vcc = compress(vcp.reshape(b, G, nr, CMP_STRIDE * LANES), 0, *pv)
    expand = _gate_expand()
    o_c, selb = nsa_compressed(qk, kcc, vcc, _cmp_to_sel(nr, s // SEL_BLOCK), misc, glog_blk, expand,
                               batch=b, seq=s, tq=T["tq_cmp"], rc=rc)
    nsa_scale = NSA_DK ** -0.5
    o_s = flash_heads(qk, qk, vsw, mode="select", batch=b, seq=s, tq=tq, tk=tk, rc=rc, scale=nsa_scale,
                      q_head0=QK_Q0, k_head0=QK_KS0, v_head0=0, kv_shared=True, out_dtype=F32,
                      selb=selb, onehot=_key_block_onehot(s), gate=(misc, glog_blk, expand, 1),
                      name="nsa_selected")
    o_w = flash_heads(qk, qk, vsw, mode="window", batch=b, seq=s, tq=tq, tk=tk, rc=rc, scale=nsa_scale,
                      q_head0=QK_Q0, k_head0=QK_KW0, v_head0=G, kv_shared=True, out_dtype=F32,
                      gate=(misc, glog_blk, expand, 2), name="nsa_window")

    qm = projection(misc, w_uq, tm=tm, tn=512, out_dtype=BF16, a_cols=(MLA_Q_RANK, 0), gain=mla_g_q[0],
                    rope=(cos_b, sin_b, shift_b), head_major=(b, s), name="mla_q")
    km = projection(misc, w_uk, tm=tm, tn=512, out_dtype=BF16,
                    a_cols=(MLA_KV_RANK, MLA_Q_RANK // MLA_KV_RANK), gain=mla_g_kv[0], add=krp,
                    head_major=(b, s), name="mla_k")
    vm = projection(misc, w_uv, tm=tm, tn=512, out_dtype=BF16,
                    a_cols=(MLA_KV_RANK, MLA_Q_RANK // MLA_KV_RANK), gain=mla_g_kv[0],
                    head_major=(b, s), name="mla_v")
    o_b = flash_heads(qm, km, vm, mode="causal", batch=b, seq=s, tq=tq, tk=tk, rc=rc,
                      scale=(MLA_NOPE + MLA_ROPE) ** -0.5, q_head0=0, k_head0=0, v_head0=0,
                      kv_shared=False, out_dtype=BF16, name="mla_attention")

    mixed = gated_mix(o_c, o_s, o_w, o_b, gates, w_o_nsa[0].astype(BF16), w_o_mla[0].astype(BF16),
                      tm=tm, tn=512)
    h1 = projection(mixed, w_out[0].astype(BF16), tm=tm, tn=512, out_dtype=F32, res=x2, name="proj_out")

    kv_mem = projection(mem.reshape(b * MEM_LEN, d), xa_wkv[0].astype(BF16), tm=MEM_LEN, tn=512,
                        out_dtype=BF16, gain=g_mem[0], name="xattn_kv")
    h2 = memory_xattn(h1, g_xattn[0], xa_wq[0].astype(BF16), kv_mem, xa_wo[0].astype(BF16), seq=s,
                      tm=T["tm_x"])

    out = mlp_final(h2, g_mlp[0], w_ff1[0].astype(BF16), w_ff2[0].astype(BF16), g_final,
                    tm=T["tm_x"], tf=T["tf"])
    return out.reshape(b, s, d)
```

```python
import functools

import numpy as np
import jax
import jax.numpy as jnp
from jax import lax
from jax.experimental import pallas as pl
from jax.experimental.pallas import tpu as pltpu

F32 = jnp.float32
BF16 = jnp.bfloat16

D_MODEL = 2048
MEM_LEN = 256
ROPE_THETA = 500000.0
EPS = 1e-6
NEG = -1e30

NSA_HEADS = 16
NSA_KV_HEADS = 4
NSA_HPG = NSA_HEADS // NSA_KV_HEADS
NSA_DK = 96
NSA_DV = 64
NSA_ROT = NSA_DK // 4
CMP_BLOCK = 32
CMP_STRIDE = 16
SEL_BLOCK = 64
SEL_TOPK = 16
WINDOW = 512

MLA_HEADS = 16
MLA_NOPE = 64
MLA_ROPE = 32
MLA_DV = 64
MLA_Q_RANK = 512
MLA_KV_RANK = 256

XA_HEADS = 4
XA_DIM = 128
D_FF = 4 * D_MODEL

SPLITS = (NSA_HEADS * NSA_DK,
          NSA_KV_HEADS * NSA_DK, NSA_KV_HEADS * NSA_DV,
          NSA_KV_HEADS * NSA_DK, NSA_KV_HEADS * NSA_DV,
          NSA_KV_HEADS * NSA_DK, NSA_KV_HEADS * NSA_DV,
          NSA_HEADS * 3,
          MLA_Q_RANK, MLA_KV_RANK, MLA_ROPE,
          D_MODEL, D_MODEL)

LANES = 128
HALF = LANES // 2
VMEM_LIMIT = 56 * 1024 * 1024

QK_Q0 = 0
QK_KC0 = NSA_HEADS
QK_KS0 = NSA_HEADS + NSA_KV_HEADS
QK_KW0 = NSA_HEADS + 2 * NSA_KV_HEADS
QK_HEADS = NSA_HEADS + 3 * NSA_KV_HEADS
FLASH_HEADS = NSA_HPG


def _cparams(*sem):
    return pltpu.CompilerParams(dimension_semantics=sem, vmem_limit_bytes=VMEM_LIMIT)


def _tiles(seq):
    return dict(
        tm=min(1024, seq),
        tm_x=min(512, seq),
        tq=min(1024, seq),
        tk=min(512, seq),
        rc=32,
        tq_cmp=min(256, seq),
        tf=512,
    )


def _rms(x, g):
    return x * lax.rsqrt(jnp.mean(x * x, axis=-1, keepdims=True) + EPS) * g


def _rmsnorm_kernel(x_ref, g_ref, o_ref):
    o_ref[...] = _rms(x_ref[...], g_ref[...]).astype(o_ref.dtype)


def rmsnorm_rows(x, g, tm):
    n, d = x.shape
    return pl.pallas_call(
        _rmsnorm_kernel,
        out_shape=jax.ShapeDtypeStruct((n, d), BF16),
        grid=(n // tm,),
        in_specs=[pl.BlockSpec((tm, d), lambda i: (i, 0)),
                  pl.BlockSpec((1, d), lambda i: (0, 0))],
        out_specs=pl.BlockSpec((tm, d), lambda i: (i, 0)),
        compiler_params=_cparams("parallel"),
        name="rmsnorm_rows",
    )(x, g.reshape(1, d))


def _proj_kernel(*refs, has_gain, has_rope, roll_shift, out_scale, has_add, has_res, act, head_major):
    it = iter(refs)
    a_ref = next(it)
    g_ref = next(it) if has_gain else None
    w_ref = next(it)
    cos_ref = next(it) if has_rope else None
    sin_ref = next(it) if has_rope else None
    add_ref = next(it) if has_add else None
    res_ref = next(it) if has_res else None
    o_ref = next(it)

    a = a_ref[...]
    if has_gain:
        a = _rms(a, g_ref[...]).astype(BF16)
    y = jnp.dot(a, w_ref[...], preferred_element_type=F32)
    if out_scale is not None:
        factor, n_tiles = out_scale
        y = y * jnp.where(pl.program_id(1) < n_tiles, factor, 1.0)
    if has_rope or has_add or head_major:
        for h in range(y.shape[1] // LANES):
            yh = y[:, h * LANES:(h + 1) * LANES]
            if has_rope:
                yh = yh * cos_ref[...] + pltpu.roll(yh, roll_shift, 1) * sin_ref[...]
            if has_add:
                yh = yh + add_ref[...].astype(F32)
            if head_major:
                o_ref[h] = yh.astype(o_ref.dtype)
            else:
                o_ref[:, h * LANES:(h + 1) * LANES] = yh.astype(o_ref.dtype)
    else:
        if act == "sigmoid":
            y = jax.nn.sigmoid(y)
        if has_res:
            y = res_ref[...] + y
        o_ref[...] = y.astype(o_ref.dtype)


def projection(a, w, *, tm, tn, out_dtype, a_cols=None, gain=None, rope=None, out_scale=None, add=None,
               res=None, act=None, head_major=None, name="projection"):
    n = a.shape[0]
    k, nc = w.shape
    a_w, a_blk = a_cols if a_cols is not None else (a.shape[1], 0)
    assert a_w == k and n % tm == 0 and nc % tn == 0
    grid = (n // tm, nc // tn)
    in_specs = [pl.BlockSpec((tm, k), lambda i, j: (i, a_blk))]
    args = [a]
    if gain is not None:
        in_specs.append(pl.BlockSpec((1, k), lambda i, j: (0, 0)))
        args.append(gain.reshape(1, k))
    in_specs.append(pl.BlockSpec((k, tn), lambda i, j: (0, j)))
    args.append(w)
    roll_shift = 0
    if rope is not None:
        cos, sin, roll_shift = rope
        s_tiles = cos.shape[0] // tm
        for t in (cos, sin):
            in_specs.append(pl.BlockSpec((tm, LANES), lambda i, j: (i % s_tiles, 0)))
            args.append(t)
    if add is not None:
        in_specs.append(pl.BlockSpec((tm, LANES), lambda i, j: (i, 0)))
        args.append(add)
    if res is not None:
        in_specs.append(pl.BlockSpec((tm, tn), lambda i, j: (i, j)))
        args.append(res)
    if head_major is not None:
        b, s = head_major
        s_t = s // tm
        hpt = tn // LANES
        out_shape = jax.ShapeDtypeStruct((b, nc // LANES, s, LANES), out_dtype)
        out_spec = pl.BlockSpec((None, hpt, tm, LANES), lambda i, j: (i // s_t, j, i % s_t, 0))
    else:
        out_shape = jax.ShapeDtypeStruct((n, nc), out_dtype)
        out_spec = pl.BlockSpec((tm, tn), lambda i, j: (i, j))
    kern = functools.partial(
        _proj_kernel, has_gain=gain is not None, has_rope=rope is not None, roll_shift=roll_shift,
        out_scale=out_scale, has_add=add is not None, has_res=res is not None, act=act,
        head_major=head_major is not None)
    return pl.pallas_call(
        kern, out_shape=out_shape, grid=grid, in_specs=in_specs, out_specs=out_spec,
        compiler_params=_cparams("parallel", "arbitrary"), name=name,
    )(*args)


def _compress_kernel(r_ref, pa_ref, pb_ref, w1a_ref, w1b_ref, w2_ref, o_ref):
    r = r_ref[...].astype(F32)
    a = jnp.dot((r + pa_ref[...]).astype(BF16), w1a_ref[...], preferred_element_type=F32)
    b = jnp.dot((r + pb_ref[...]).astype(BF16), w1b_ref[...], preferred_element_type=F32)
    nr = a.shape[0]
    hid = a + pltpu.roll(b, nr - 1, 0)
    hid = jax.nn.gelu(hid)
    o_ref[...] = jnp.dot(hid.astype(BF16), w2_ref[...], preferred_element_type=F32).astype(o_ref.dtype)


def compress(r, head0, pos_a, pos_b, w1a, w1b, w2):
    b, _, nr, kk = r.shape
    g = NSA_KV_HEADS
    full = lambda shape: pl.BlockSpec(shape, lambda bi, gi: (0,) * len(shape))
    return pl.pallas_call(
        _compress_kernel,
        out_shape=jax.ShapeDtypeStruct((b, g, nr, LANES), BF16),
        grid=(b, g),
        in_specs=[pl.BlockSpec((None, None, nr, kk), lambda bi, gi: (bi, head0 + gi, 0, 0)),
                  full((1, kk)), full((1, kk)), full((kk, LANES)), full((kk, LANES)),
                  full((LANES, LANES))],
        out_specs=pl.BlockSpec((None, None, nr, LANES), lambda bi, gi: (bi, gi, 0, 0)),
        compiler_params=_cparams("parallel", "parallel"),
        name="nsa_compress",
    )(r, pos_a, pos_b, w1a, w1b, w2)


def _dot_nt(a, b):
    return lax.dot_general(a, b, (((1,), (1,)), ((), ())), preferred_element_type=F32)


def _split_dot(x, w):
    hi = x.astype(BF16)
    lo = (x - hi.astype(F32)).astype(BF16)
    return (jnp.dot(hi, w, preferred_element_type=F32) + jnp.dot(lo, w, preferred_element_type=F32))


def _nsa_cmp_kernel(q_ref, kcc_ref, vcc_ref, msel_ref, glog_ref, e_ref, oc_ref, sel_ref, s_sc, p_sc,
                    *, tq, rc, topk):
    s0 = pl.program_id(2) * tq
    nr = kcc_ref.shape[0]
    for h in range(NSA_HPG):
        s_sc[h] = _dot_nt(q_ref[h], kcc_ref[...])
    lane_v = lax.broadcasted_iota(jnp.int32, (nr, LANES), 1)
    vcc = vcc_ref[...]
    vm = jnp.concatenate([jnp.where(lane_v == HALF, jnp.ones_like(vcc), vcc), msel_ref[...]], axis=1)
    dmat = (lax.broadcasted_iota(jnp.int32, (rc, nr), 0)
            - lax.broadcasted_iota(jnp.int32, (rc, nr), 1) * CMP_STRIDE)
    outs = []
    imp = jnp.zeros((tq, LANES), F32)
    for h in range(NSA_HPG):
        for c in range(tq // rc):
            rows = pl.ds(c * rc, rc)
            msk = dmat + (s0 + c * rc - (CMP_BLOCK - 1)) >= 0
            s = jnp.where(msk, s_sc[h, rows, :], NEG)
            m = jnp.max(s, axis=-1, keepdims=True)
            p_sc[h, rows, :] = jnp.where(msk, jnp.exp2(s - m), 0.0).astype(BF16)
        a = jnp.dot(p_sc[h], vm, preferred_element_type=F32)
        denom = jnp.maximum(a[:, HALF:HALF + 1], 1e-30)
        outs.append(a[:, :LANES] / denom)
        imp = imp + a[:, LANES:] / denom

    blk = lax.broadcasted_iota(jnp.int32, (tq, LANES), 1)
    cur = (s0 + lax.broadcasted_iota(jnp.int32, (tq, LANES), 0)) // SEL_BLOCK
    valid = blk <= cur
    forced = (blk == 0) | (blk == cur) | (blk == cur - 1)
    score = jnp.where(valid, jnp.where(forced, 1e4, imp), -1.0)
    sc = score.T
    rowid = lax.broadcasted_iota(jnp.int32, (LANES, tq), 0).astype(F32)

    def pick_one(_, sc):
        cm = jnp.max(sc, axis=0, keepdims=True)
        first = jnp.min(jnp.where(sc == cm, rowid, float(LANES)), axis=0, keepdims=True)
        return jnp.where(rowid == first, -jnp.inf, sc)

    picked = lax.fori_loop(0, topk, pick_one, sc) == -jnp.inf
    sel_ref[...] = jnp.where(picked, 0.0, NEG).T.astype(sel_ref.dtype)

    glog = glog_ref[...]
    lane = lax.broadcasted_iota(jnp.int32, (tq, LANES), 1)
    for pr in range(NSA_HPG // 2):
        gate = jax.nn.sigmoid(_split_dot(glog, e_ref[pr]))
        pair = jnp.where(lane < HALF, outs[2 * pr], pltpu.roll(outs[2 * pr + 1], HALF, 1))
        oc_ref[:, pr * LANES:(pr + 1) * LANES] = (gate * pair).astype(oc_ref.dtype)


def nsa_compressed(qk, kcc, vcc, msel, misc, glog_blk, expand, *, batch, seq, tq, rc):
    g = NSA_KV_HEADS
    nq = seq // tq
    nr = kcc.shape[2]
    topk = min(SEL_TOPK, seq // SEL_BLOCK)
    kern = functools.partial(_nsa_cmp_kernel, tq=tq, rc=rc, topk=topk)
    return pl.pallas_call(
        kern,
        out_shape=(jax.ShapeDtypeStruct((batch * seq, NSA_HEADS * NSA_DV), F32),
                   jax.ShapeDtypeStruct((batch, g, seq, LANES), BF16)),
        grid=(batch, g, nq),
        in_specs=[
            pl.BlockSpec((None, NSA_HPG, tq, LANES), lambda b, gi, qi: (b, gi, qi, 0)),
            pl.BlockSpec((None, None, nr, LANES), lambda b, gi, qi: (b, gi, 0, 0)),
            pl.BlockSpec((None, None, nr, LANES), lambda b, gi, qi: (b, gi, 0, 0)),
            pl.BlockSpec((nr, LANES), lambda b, gi, qi: (0, 0)),
            pl.BlockSpec((tq, LANES), lambda b, gi, qi: (b * nq + qi, glog_blk)),
            pl.BlockSpec((None, 2, LANES, LANES), lambda b, gi, qi: (0, gi, 0, 0)),
        ],
        out_specs=(pl.BlockSpec((tq, 2 * LANES), lambda b, gi, qi: (b * nq + qi, gi)),
                   pl.BlockSpec((None, None, tq, LANES), lambda b, gi, qi: (b, gi, qi, 0))),
        scratch_shapes=[pltpu.VMEM((NSA_HPG, tq, nr), F32), pltpu.VMEM((NSA_HPG, tq, nr), BF16)],
        compiler_params=_cparams("parallel", "parallel", "parallel"),
        name="nsa_compressed_select",
    )(qk, kcc, vcc, msel, misc, expand)


def _flash_steps(mode, nq, tq, tk):
    r = tq // tk
    qi_l, kt_l, first_l, last_l = [], [], [], []
    for qi in range(nq):
        hi = qi * r + r - 1
        lo = max(0, qi * r - (-(-(WINDOW - 1) // tk))) if mode == "window" else 0
        for kt in range(lo, hi + 1):
            qi_l.append(qi)
            kt_l.append(kt)
            first_l.append(int(kt == lo))
            last_l.append(int(kt == hi))
    return tuple(jnp.asarray(np.asarray(a, np.int32)) for a in (qi_l, kt_l, first_l, last_l))


def _flash_kernel(*refs, mode, tq, tk, rc, kv_shared, gated):
    it = iter(refs)
    qi_ref, kt_ref, first_ref, last_ref = next(it), next(it), next(it), next(it)
    q_ref, k_ref, v_ref = next(it), next(it), next(it)
    selb_ref = next(it) if mode == "select" else None
    oh_ref = next(it) if mode == "select" else None
    glog_ref = next(it) if gated else None
    e_ref = next(it) if gated else None
    o_ref = next(it)
    m_sc, acc_sc, s_sc, p_sc = next(it), next(it), next(it), next(it)
    qa_sc = next(it) if mode == "select" else None

    st = pl.program_id(2)
    s0 = qi_ref[st] * tq
    k0 = kt_ref[st] * tk
    reps = tk // LANES

    @pl.when(first_ref[st] == 1)
    def _init():
        m_sc[...] = jnp.full(m_sc.shape, NEG, F32)
        acc_sc[...] = jnp.zeros(acc_sc.shape, F32)
        if mode == "select":
            for hh in range(FLASH_HEADS):
                qa_sc[hh] = jnp.concatenate([q_ref[hh], selb_ref[...]], axis=1)

    def tile(masked):
        for hh in range(FLASH_HEADS):
            k = k_ref[0 if kv_shared else hh]
            if mode == "select":
                s_sc[hh] = _dot_nt(qa_sc[hh], jnp.concatenate([k, oh_ref[...]], axis=1))
            else:
                s_sc[hh] = _dot_nt(q_ref[hh], k)
        lane_v = lax.broadcasted_iota(jnp.int32, (tk, LANES), 1)
        if masked:
            dmat = (lax.broadcasted_iota(jnp.int32, (rc, tk), 0)
                    - lax.broadcasted_iota(jnp.int32, (rc, tk), 1))
        for hh in range(FLASH_HEADS):
            v = v_ref[0 if kv_shared else hh]
            v_aug = jnp.where(lane_v == HALF, jnp.ones_like(v), v)

            for c in range(tq // rc):
                r0 = c * rc
                rows = pl.ds(r0, rc)
                s = s_sc[hh, rows, :]
                if masked:
                    d = dmat + (s0 - k0 + r0)
                    msk = (d >= 0) & (d < WINDOW) if mode == "window" else d >= 0
                    s = jnp.where(msk, s, NEG)
                m_old = m_sc[hh, rows, :]
                m_new = jnp.maximum(m_old, jnp.max(s, axis=-1, keepdims=True))
                p = jnp.exp2(s - jnp.concatenate([m_new] * reps, axis=1))
                if masked:
                    p = jnp.where(msk, p, 0.0)
                m_sc[hh, rows, :] = m_new
                acc_sc[hh, rows, :] = acc_sc[hh, rows, :] * jnp.exp2(m_old - m_new)
                p_sc[hh, rows, :] = p.astype(BF16)
            acc_sc[hh] +=jnp.dot(p_sc[hh], v_aug, preferred_element_type=F32)

    if mode == "window":
        tile(True)
    else:
        interior = k0 + tk - 1 <= s0

        @pl.when(interior)
        def _interior():
            tile(False)

        @pl.when(jnp.logical_not(interior))
        def _diagonal():
            tile(True)

    @pl.when(last_ref[st] == 1)
    def _finish():
        lane = lax.broadcasted_iota(jnp.int32, (tq, LANES), 1)
        if gated:
            glog = glog_ref[...]
        for pr in range(FLASH_HEADS // 2):
            outs = []
            for hh in (2 * pr, 2 * pr + 1):
                a = acc_sc[hh]
                outs.append(a / jnp.maximum(a[:, HALF:HALF + 1], 1e-30))
            out = jnp.where(lane < HALF, outs[0], pltpu.roll(outs[1], HALF, 1))
            if gated:
                out = jax.nn.sigmoid(_split_dot(glog, e_ref[pr])) * out
            o_ref[:, pr * LANES:(pr + 1) * LANES] = out.astype(o_ref.dtype)


def flash_heads(q, k, v, *, mode, batch, seq, tq, tk, rc, q_head0, k_head0, v_head0,
                kv_shared, out_dtype, selb=None, onehot=None, gate=None, name="flash"):
    nh = FLASH_HEADS
    ngroups = NSA_HEADS // nh
    nq = seq // tq
    assert tq % tk == 0 and tq % rc == 0
    tabs = _flash_steps(mode, nq, tq, tk)
    nsteps = tabs[0].shape[0]
    if kv_shared:
        kv_spec = lambda h0: pl.BlockSpec(
            (None, 1, tk, LANES), lambda b, g, st, qi, kt, fi, la: (b, h0 + g, kt[st], 0))
    else:
        kv_spec = lambda h0: pl.BlockSpec(
            (None, nh, tk, LANES), lambda b, g, st, qi, kt, fi, la: (b, h0 // nh + g, kt[st], 0))
    in_specs = [pl.BlockSpec((None, nh, tq, LANES),
                             lambda b, g, st, qi, kt, fi, la: (b, q_head0 // nh + g, qi[st], 0)),
                kv_spec(k_head0), kv_spec(v_head0)]
    args = [q, k, v]
    scratch = [pltpu.VMEM((nh, tq, LANES), F32), pltpu.VMEM((nh, tq, LANES), F32),
               pltpu.VMEM((nh, tq, tk), F32), pltpu.VMEM((nh, tq, tk), BF16)]
    if mode == "select":
        in_specs += [pl.BlockSpec((None, None, tq, LANES),
                                  lambda b, g, st, qi, kt, fi, la: (b, g, qi[st], 0)),
                     pl.BlockSpec((tk, LANES), lambda b, g, st, qi, kt, fi, la: (kt[st], 0))]
        args += [selb, onehot]
        scratch.append(pltpu.VMEM((nh, tq, 2 * LANES), BF16))
    if gate is not None:
        misc, glog_blk, expand, branch = gate
        in_specs += [pl.BlockSpec((tq, LANES),
                                  lambda b, g, st, qi, kt, fi, la: (b * nq + qi[st], glog_blk)),
                     pl.BlockSpec((None, nh // 2, LANES, LANES),
                                  lambda b, g, st, qi, kt, fi, la: (branch, g, 0, 0))]
        args += [misc, expand]
    kern = functools.partial(_flash_kernel, mode=mode, tq=tq, tk=tk, rc=rc,
                             kv_shared=kv_shared, gated=gate is not None)
    ow = nh * HALF
    grid_spec = pltpu.PrefetchScalarGridSpec(
        num_scalar_prefetch=4, grid=(batch, ngroups, nsteps), in_specs=in_specs,
        out_specs=pl.BlockSpec((tq, ow), lambda b, g, st, qi, kt, fi, la: (b * nq + qi[st], g)),
        scratch_shapes=scratch)
    return pl.pallas_call(
        kern,
        out_shape=jax.ShapeDtypeStruct((batch * seq, ngroups * ow), out_dtype),
        grid_spec=grid_spec,
        compiler_params=_cparams("parallel", "parallel", "arbitrary"),
        name=name,
    )(*tabs, *args)


def _mix_kernel(oc_ref, os_ref, ow_ref, ob_ref, ga_ref, gb_ref, wa_ref, wb_ref, o_ref):
    oa = (oc_ref[...] + os_ref[...] + ow_ref[...]).astype(BF16)
    ya = jnp.dot(oa, wa_ref[...], preferred_element_type=F32)
    yb = jnp.dot(ob_ref[...], wb_ref[...], preferred_element_type=F32)
    o_ref[...] = (ga_ref[...].astype(F32) * ya + gb_ref[...].astype(F32) * yb).astype(o_ref.dtype)


def gated_mix(oc, osel, ow, ob, gates, wa, wb, *, tm, tn):
    n, ka = oc.shape
    d = wa.shape[1]
    nb = d // tn
    row = lambda w: pl.BlockSpec((tm, w), lambda i, j: (i, 0))
    return pl.pallas_call(
        _mix_kernel,
        out_shape=jax.ShapeDtypeStruct((n, d), BF16),
        grid=(n // tm, nb),
        in_specs=[row(ka), row(ka), row(ka), row(ob.shape[1]),
                  pl.BlockSpec((tm, tn), lambda i, j: (i, j)),
                  pl.BlockSpec((tm, tn), lambda i, j: (i, nb + j)),
                  pl.BlockSpec((ka, tn), lambda i, j: (0, j)),
                  pl.BlockSpec((wb.shape[0], tn), lambda i, j: (0, j))],
        out_specs=pl.BlockSpec((tm, tn), lambda i, j: (i, j)),
        compiler_params=_cparams("parallel", "arbitrary"),
        name="gated_mix",
    )(oc, osel, ow, ob, gates, gates, wa, wb)


def _xattn_kernel(h_ref, g_ref, wq_ref, kv_ref, wo_ref, o_ref):
    h = h_ref[...]
    hn = _rms(h, g_ref[...]).astype(BF16)
    q = jnp.dot(hn, wq_ref[...], preferred_element_type=F32).astype(BF16)
    kv = kv_ref[...]
    outs = []
    for hd in range(XA_HEADS):
        qh = q[:, hd * XA_DIM:(hd + 1) * XA_DIM]
        kh = kv[:, hd * XA_DIM:(hd + 1) * XA_DIM]
        vh = kv[:, (XA_HEADS + hd) * XA_DIM:(XA_HEADS + hd + 1) * XA_DIM]
        s = _dot_nt(qh, kh) * (XA_DIM ** -0.5)
        e = jnp.exp(s - jnp.max(s, axis=-1, keepdims=True))
        p = e / jnp.sum(e, axis=-1, keepdims=True)
        outs.append(jnp.dot(p.astype(BF16), vh, preferred_element_type=F32))
    o = jnp.concatenate(outs, axis=1).astype(BF16)
    o_ref[...] = h + jnp.dot(o, wo_ref[...], preferred_element_type=F32)


def memory_xattn(h, g, wq, kv, wo, *, seq, tm):
    n, d = h.shape
    s_t = seq // tm
    dq = wq.shape[1]
    return pl.pallas_call(
        _xattn_kernel,
        out_shape=jax.ShapeDtypeStruct((n, d), F32),
        grid=(n // tm,),
        in_specs=[pl.BlockSpec((tm, d), lambda i: (i, 0)),
                  pl.BlockSpec((1, d), lambda i: (0, 0)),
                  pl.BlockSpec((d, dq), lambda i: (0, 0)),
                  pl.BlockSpec((MEM_LEN, 2 * dq), lambda i: (i // s_t, 0)),
                  pl.BlockSpec((dq, d), lambda i: (0, 0))],
        out_specs=pl.BlockSpec((tm, d), lambda i: (i, 0)),
        compiler_params=_cparams("parallel"),
        name="memory_xattn",
    )(h, g.reshape(1, d), wq, kv, wo)


def _mlp_kernel(h_ref, g_ref, w1_ref, w2_ref, gf_ref, o_ref, hn_sc, acc_sc, *, nf):
    f = pl.program_id(1)

    @pl.when(f == 0)
    def _init():
        hn_sc[...] = _rms(h_ref[...], g_ref[...]).astype(BF16)
        acc_sc[...] = jnp.zeros(acc_sc.shape, F32)

    u = jnp.dot(hn_sc[...], w1_ref[...], preferred_element_type=F32)
    u = jnp.square(jnp.maximum(u, 0.0))
    acc_sc[...] += jnp.dot(u.astype(BF16), w2_ref[...], preferred_element_type=F32)

    @pl.when(f == nf - 1)
    def _finish():
        o_ref[...] = _rms(h_ref[...] + acc_sc[...], gf_ref[...])


def mlp_final(h, g, w1, w2, gf, *, tm, tf):
    n, d = h.shape
    dff = w1.shape[1]
    nf = dff // tf
    return pl.pallas_call(
        functools.partial(_mlp_kernel, nf=nf),
        out_shape=jax.ShapeDtypeStruct((n, d), F32),
        grid=(n // tm, nf),
        in_specs=[pl.BlockSpec((tm, d), lambda i, f: (i, 0)),
                  pl.BlockSpec((1, d), lambda i, f: (0, 0)),
                  pl.BlockSpec((d, tf), lambda i, f: (0, f)),
                  pl.BlockSpec((tf, d), lambda i, f: (f, 0)),
                  pl.BlockSpec((1, d), lambda i, f: (0, 0))],
        out_specs=pl.BlockSpec((tm, d), lambda i, f: (i, 0)),
        scratch_shapes=[pltpu.VMEM((tm, d), BF16), pltpu.VMEM((tm, d), F32)],
        compiler_params=_cparams("parallel", "arbitrary"),
        name="mlp_final",
    )(h, g.reshape(1, d), w1, w2, gf.reshape(1, d))


def _rot_partner(w, half):
    return jnp.concatenate([-w[..., half:], w[..., :half]], axis=-1)


def _pad_heads_rope(w, heads, hd, rot0, rot_dim):
    k = w.shape[0]
    w = w.reshape(k, heads, hd)
    partner = _rot_partner(w[:, :, rot0:rot0 + rot_dim], rot_dim // 2)
    pad = jnp.zeros((k, heads, LANES - hd - rot_dim), w.dtype)
    return jnp.concatenate([w, partner, pad], axis=-1).reshape(k, heads * LANES)


def _rope_lane_tables(seq, rot0, rot_dim, hd):
    inv = 1.0 / (ROPE_THETA ** (jnp.arange(0, rot_dim, 2, dtype=F32) / rot_dim))
    ang = jnp.arange(seq, dtype=F32)[:, None] * inv[None, :]
    cos = jnp.concatenate([jnp.cos(ang), jnp.cos(ang)], axis=1)
    sin = jnp.concatenate([jnp.sin(ang), jnp.sin(ang)], axis=1)
    cos_t = jnp.concatenate([jnp.ones((seq, rot0), F32), cos,
                             jnp.ones((seq, hd - rot0 - rot_dim), F32),
                             jnp.zeros((seq, LANES - hd), F32)], axis=1)
    sin_t = jnp.concatenate([jnp.zeros((seq, rot0), F32), sin,
                             jnp.zeros((seq, LANES - rot0 - rot_dim), F32)], axis=1)
    return cos_t, sin_t


def _pad_lanes(w, groups, width):
    k = w.shape[0]
    w = w.reshape(k, groups, width)
    return jnp.pad(w, ((0, 0), (0, 0), (0, LANES - width))).reshape(k, groups * LANES)


def _compress_params(pos, w1, w2, d):
    half = CMP_BLOCK // 2
    pos_p = jnp.pad(pos, ((0, 0), (0, LANES - d)))
    pos_a = pos_p[:half].reshape(1, half * LANES)
    pos_b = pos_p[half:].reshape(1, half * LANES)
    w1p = jnp.pad(w1.reshape(CMP_BLOCK, d, d), ((0, 0), (0, LANES - d), (0, LANES - d)))
    w1a = w1p[:half].reshape(half * LANES, LANES).astype(BF16)
    w1b = w1p[half:].reshape(half * LANES, LANES).astype(BF16)
    w2p = jnp.pad(w2, ((0, LANES - d), (0, LANES - d)))
    return pos_a, pos_b, w1a, w1b, w2p.astype(BF16)


def _cmp_to_sel(nr, nsb):
    cs = np.arange(nr) * CMP_STRIDE
    ce = cs + CMP_BLOCK
    ss = np.arange(LANES) * SEL_BLOCK
    se = ss + SEL_BLOCK
    ov = np.clip(np.minimum(ce[:, None], se[None, :]) - np.maximum(cs[:, None], ss[None, :]), 0, None)
    ov = ov.astype(np.float32) / np.float32(CMP_BLOCK)
    ov[:, nsb:] = 0.0
    ov[nr - 1:, :] = 0.0
    return jnp.asarray(ov, BF16)


def _gate_expand():
    e = np.zeros((3, NSA_HEADS // 2, LANES, LANES), np.float32)
    for br in range(3):
        for hp in range(NSA_HEADS // 2):
            for hh in range(2):
                e[br, hp, 3 * (2 * hp + hh) + br, hh * HALF:(hh + 1) * HALF] = 1.0
    return jnp.asarray(e, BF16)


def _key_block_onehot(seq):
    e = (np.arange(seq)[:, None] // SEL_BLOCK) == np.arange(LANES)[None, :]
    return jnp.asarray(e.astype(np.float32), BF16)


def kernel(x, mem, g_mix, w_in, cmp_pos_k, cmp_w1_k, cmp_w2_k, cmp_pos_v, cmp_w1_v, cmp_w2_v,
           mla_g_q, mla_w_uq, mla_g_kv, mla_w_uk, mla_w_uv, w_o_nsa, w_o_mla, w_out,
           g_xattn, g_mem, xa_wq, xa_wkv, xa_wo, g_mlp, w_ff1, w_ff2, g_final):
    b, s, d = x.shape
    assert d == D_MODEL and s % (CMP_STRIDE * 8) == 0 and s // SEL_BLOCK <= LANES
    assert g_mix.shape[0] == 1
    n = b * s
    T = _tiles(s)
    tm, tq, tk, rc = T["tm"], T["tq"], T["tk"], T["rc"]
    G = NSA_KV_HEADS
    bounds = [int(v) for v in np.cumsum(SPLITS)[:-1]]

    (w_qa, w_kc, w_vc, w_ks, w_vs, w_kw, w_vw, w_gn, w_cq, w_ckv, w_kr,
     w_ga, w_gb) = jnp.split(w_in[0], bounds, axis=1)
    nsa_rope = lambda w, heads: _pad_heads_rope(w, heads, NSA_DK, 0, NSA_ROT)
    w_rope = jnp.concatenate([nsa_rope(w_qa, NSA_HEADS), nsa_rope(w_kc, G), nsa_rope(w_ks, G),
                              nsa_rope(w_kw, G)], axis=1).astype(BF16)
    w_krp = jnp.concatenate([jnp.zeros((d, MLA_NOPE), F32), w_kr, _rot_partner(w_kr, MLA_ROPE // 2)],
                            axis=1).astype(BF16)
    w_vsw = jnp.concatenate([_pad_lanes(w_vs, G, NSA_DV), _pad_lanes(w_vw, G, NSA_DV)], axis=1).astype(BF16)
    w_vcp = _pad_lanes(w_vc, G, NSA_DV).astype(BF16)
    w_misc = jnp.concatenate([w_cq, w_ckv, jnp.pad(w_gn, ((0, 0), (0, LANES - w_gn.shape[1])))],
                             axis=1).astype(BF16)
    glog_blk = (MLA_Q_RANK + MLA_KV_RANK) // LANES
    w_gates = jnp.concatenate([w_ga, w_gb], axis=1).astype(BF16)
    cos_a, sin_a = _rope_lane_tables(s, 0, NSA_ROT, NSA_DK)
    cos_b, sin_b = _rope_lane_tables(s, MLA_NOPE, MLA_ROPE, MLA_NOPE + MLA_ROPE)
    shift_a = LANES - NSA_DK
    shift_b = LANES - (NSA_DK - MLA_NOPE)
    w_uq = _pad_heads_rope(mla_w_uq[0], MLA_HEADS, MLA_NOPE + MLA_ROPE, MLA_NOPE, MLA_ROPE).astype(BF16)
    w_uk = _pad_lanes(mla_w_uk[0], MLA_HEADS, MLA_NOPE).astype(BF16)
    w_uv = _pad_lanes(mla_w_uv[0], MLA_HEADS, MLA_DV).astype(BF16)

    x2 = x.reshape(n, d)
    xn = rmsnorm_rows(x2, g_mix[0], tm)
    log2e = float(np.log2(np.e))
    qk = projection(xn, w_rope, tm=tm, tn=512, out_dtype=BF16, rope=(cos_a, sin_a, shift_a),
                    out_scale=(NSA_DK ** -0.5 * log2e, NSA_HEADS * LANES // 512),
                    head_major=(b, s), name="proj_qk_rope")
    krp = projection(xn, w_krp, tm=tm, tn=LANES, out_dtype=BF16, rope=(cos_b, sin_b, shift_b),
                     name="proj_k_rope_mla")
    vsw = projection(xn, w_vsw, tm=tm, tn=512, out_dtype=BF16, head_major=(b, s), name="proj_v")
    vcp = projection(xn, w_vcp, tm=tm, tn=512, out_dtype=BF16, head_major=(b, s), name="proj_vc")
    misc = projection(xn, w_misc, tm=tm, tn=w_misc.shape[1], out_dtype=F32, name="proj_misc")
    gates = projection(xn, w_gates, tm=tm, tn=512, out_dtype=BF16, act="sigmoid", name="proj_gates")

    nr = s // CMP_STRIDE
    pk = _compress_params(cmp_pos_k[0], cmp_w1_k[0], cmp_w2_k[0], NSA_DK)
    pv = _compress_params(cmp_pos_v[0], cmp_w1_v[0], cmp_w2_v[0], NSA_DV)
    kc_rows = qk[:, QK_KC0:QK_KC0 + G].reshape(b, G, nr, CMP_STRIDE * LANES)
    kcc = compress(kc_rows, 0, *pk)
    vcc = compress(vcp.reshape(b, G, nr, CMP_STRIDE * LANES), 0, *pv)
    expand = _gate_expand()
    o_c, selb = nsa_compressed(qk, kcc, vcc, _cmp_to_sel(nr, s // SEL_BLOCK), misc, glog_blk, expand,
                               batch=b, seq=s, tq=T["tq_cmp"], rc=rc)
    o_s = flash_heads(qk, qk, vsw, mode="select", batch=b, seq=s, tq=tq, tk=tk, rc=rc,
                      q_head0=QK_Q0, k_head0=QK_KS0, v_head0=0, kv_shared=True, out_dtype=F32,
                      selb=selb, onehot=_key_block_onehot(s), gate=(misc, glog_blk, expand, 1),
                      name="nsa_selected")
    o_w = flash_heads(qk, qk, vsw, mode="window", batch=b, seq=s, tq=tq, tk=tk, rc=rc,
                      q_head0=QK_Q0, k_head0=QK_KW0, v_head0=G, kv_shared=True, out_dtype=F32,
                      gate=(misc, glog_blk, expand, 2), name="nsa_window")

    qm = projection(misc, w_uq, tm=tm, tn=512, out_dtype=BF16, a_cols=(MLA_Q_RANK, 0), gain=mla_g_q[0],
                    rope=(cos_b, sin_b, shift_b), head_major=(b, s), name="mla_q",
                    out_scale=((MLA_NOPE + MLA_ROPE) ** -0.5 * log2e, MLA_HEADS * LANES // 512))
    km = projection(misc, w_uk, tm=tm, tn=512, out_dtype=BF16,
                    a_cols=(MLA_KV_RANK, MLA_Q_RANK // MLA_KV_RANK), gain=mla_g_kv[0], add=krp,
                    head_major=(b, s), name="mla_k")
    vm = projection(misc, w_uv, tm=tm, tn=512, out_dtype=BF16,
                    a_cols=(MLA_KV_RANK, MLA_Q_RANK // MLA_KV_RANK), gain=mla_g_kv[0],
                    head_major=(b, s), name="mla_v")
    o_b = flash_heads(qm, km, vm, mode="causal", batch=b, seq=s, tq=tq, tk=tk, rc=rc,
                      q_head0=0, k_head0=0, v_head0=0,
                      kv_shared=False, out_dtype=BF16, name="mla_attention")

    mixed = gated_mix(o_c, o_s, o_w, o_b, gates, w_o_nsa[0].astype(BF16), w_o_mla[0].astype(BF16),
                      tm=tm, tn=512)
    h1 = projection(mixed, w_out[0].astype(BF16), tm=tm, tn=512, out_dtype=F32, res=x2, name="proj_out")

    kv_mem = projection(mem.reshape(b * MEM_LEN, d), xa_wkv[0].astype(BF16), tm=MEM_LEN, tn=512,
                        out_dtype=BF16, gain=g_mem[0], name="xattn_kv")
    h2 = memory_xattn(h1, g_xattn[0], xa_wq[0].astype(BF16), kv_mem, xa_wo[0].astype(BF16), seq=s,
                      tm=T["tm_x"])

    out = mlp_final(h2, g_mlp[0], w_ff1[0].astype(BF16), w_ff2[0].astype(BF16), g_final,
                    tm=T["tm_x"], tf=T["tf"])
    return out.reshape(b, s, d)
```

```python
import functools

import numpy as np
import jax
import jax.numpy as jnp
from jax import lax
from jax.experimental import pallas as pl
from jax.experimental.pallas import tpu as pltpu

F32 = jnp.float32
BF16 = jnp.bfloat16

D_MODEL = 2048
MEM_LEN = 256
ROPE_THETA = 500000.0
EPS = 1e-6
NEG = -1e30

NSA_HEADS = 16
NSA_KV_HEADS = 4
NSA_HPG = NSA_HEADS // NSA_KV_HEADS
NSA_DK = 96
NSA_DV = 64
NSA_ROT = NSA_DK // 4
CMP_BLOCK = 32
CMP_STRIDE = 16
SEL_BLOCK = 64
SEL_TOPK = 16
WINDOW = 512

MLA_HEADS = 16
MLA_NOPE = 64
MLA_ROPE = 32
MLA_DV = 64
MLA_Q_RANK = 512
MLA_KV_RANK = 256

XA_HEADS = 4
XA_DIM = 128
D_FF = 4 * D_MODEL

SPLITS = (NSA_HEADS * NSA_DK,
          NSA_KV_HEADS * NSA_DK, NSA_KV_HEADS * NSA_DV,
          NSA_KV_HEADS * NSA_DK, NSA_KV_HEADS * NSA_DV,
          NSA_KV_HEADS * NSA_DK, NSA_KV_HEADS * NSA_DV,
          NSA_HEADS * 3,
          MLA_Q_RANK, MLA_KV_RANK, MLA_ROPE,
          D_MODEL, D_MODEL)

LANES = 128
HALF = LANES // 2
VMEM_LIMIT = 56 * 1024 * 1024

QK_Q0 = 0
QK_KC0 = NSA_HEADS
QK_KS0 = NSA_HEADS + NSA_KV_HEADS
QK_KW0 = NSA_HEADS + 2 * NSA_KV_HEADS
QK_HEADS = NSA_HEADS + 3 * NSA_KV_HEADS
FLASH_HEADS = NSA_HPG


def _cparams(*sem):
    return pltpu.CompilerParams(dimension_semantics=sem, vmem_limit_bytes=VMEM_LIMIT)


def _tiles(seq):
    return dict(
        tm=min(1024, seq),
        tm_x=min(512, seq),
        tm_mlp=min(512, seq),
        tq=min(1024, seq),
        tq_win=min(512, seq),
        tk=min(512, seq),
        rc=32,
        tq_cmp=min(256, seq),
        tf=512,
    )


def _rms(x, g):
    return x * lax.rsqrt(jnp.mean(x * x, axis=-1, keepdims=True) + EPS) * g


def _rmsnorm_kernel(x_ref, g_ref, o_ref):
    o_ref[...] = _rms(x_ref[...], g_ref[...]).astype(o_ref.dtype)


def rmsnorm_rows(x, g, tm):
    n, d = x.shape
    return pl.pallas_call(
        _rmsnorm_kernel,
        out_shape=jax.ShapeDtypeStruct((n, d), BF16),
        grid=(n // tm,),
        in_specs=[pl.BlockSpec((tm, d), lambda i: (i, 0)),
                  pl.BlockSpec((1, d), lambda i: (0, 0))],
        out_specs=pl.BlockSpec((tm, d), lambda i: (i, 0)),
        compiler_params=_cparams("parallel"),
        name="rmsnorm_rows",
    )(x, g.reshape(1, d))


def _proj_kernel(*refs, has_gain, has_rope, roll_shift, out_scale, has_add, has_res, act, head_major):
    it = iter(refs)
    a_ref = next(it)
    g_ref = next(it) if has_gain else None
    w_ref = next(it)
    cos_ref = next(it) if has_rope else None
    sin_ref = next(it) if has_rope else None
    add_ref = next(it) if has_add else None
    res_ref = next(it) if has_res else None
    o_ref = next(it)

    a = a_ref[...]
    if has_gain:
        a = _rms(a, g_ref[...]).astype(BF16)
    y = jnp.dot(a, w_ref[...], preferred_element_type=F32)
    if out_scale is not None:
        factor, n_tiles = out_scale
        y = y * jnp.where(pl.program_id(1) < n_tiles, factor, 1.0)
    if has_rope or has_add or head_major:
        for h in range(y.shape[1] // LANES):
            yh = y[:, h * LANES:(h + 1) * LANES]
            if has_rope:
                yh = yh * cos_ref[...] + pltpu.roll(yh, roll_shift, 1) * sin_ref[...]
            if has_add:
                yh = yh + add_ref[...].astype(F32)
            if head_major:
                o_ref[h] = yh.astype(o_ref.dtype)
            else:
                o_ref[:, h * LANES:(h + 1) * LANES] = yh.astype(o_ref.dtype)
    else:
        if act == "sigmoid":
            y = jax.nn.sigmoid(y)
        if has_res:
            y = res_ref[...] + y
        o_ref[...] = y.astype(o_ref.dtype)


def projection(a, w, *, tm, tn, out_dtype, a_cols=None, gain=None, rope=None, out_scale=None, add=None,
               res=None, act=None, head_major=None, name="projection"):
    n = a.shape[0]
    k, nc = w.shape
    a_w, a_blk = a_cols if a_cols is not None else (a.shape[1], 0)
    assert a_w == k and n % tm == 0 and nc % tn == 0
    grid = (n // tm, nc // tn)
    in_specs = [pl.BlockSpec((tm, k), lambda i, j: (i, a_blk))]
    args = [a]
    if gain is not None:
        in_specs.append(pl.BlockSpec((1, k), lambda i, j: (0, 0)))
        args.append(gain.reshape(1, k))
    in_specs.append(pl.BlockSpec((k, tn), lambda i, j: (0, j)))
    args.append(w)
    roll_shift = 0
    if rope is not None:
        cos, sin, roll_shift = rope
        s_tiles = cos.shape[0] // tm
        for t in (cos, sin):
            in_specs.append(pl.BlockSpec((tm, LANES), lambda i, j: (i % s_tiles, 0)))
            args.append(t)
    if add is not None:
        in_specs.append(pl.BlockSpec((tm, LANES), lambda i, j: (i, 0)))
        args.append(add)
    if res is not None:
        in_specs.append(pl.BlockSpec((tm, tn), lambda i, j: (i, j)))
        args.append(res)
    if head_major is not None:
        b, s = head_major
        s_t = s // tm
        hpt = tn // LANES
        out_shape = jax.ShapeDtypeStruct((b, nc // LANES, s, LANES), out_dtype)
        out_spec = pl.BlockSpec((None, hpt, tm, LANES), lambda i, j: (i // s_t, j, i % s_t, 0))
    else:
        out_shape = jax.ShapeDtypeStruct((n, nc), out_dtype)
        out_spec = pl.BlockSpec((tm, tn), lambda i, j: (i, j))
    kern = functools.partial(
        _proj_kernel, has_gain=gain is not None, has_rope=rope is not None, roll_shift=roll_shift,
        out_scale=out_scale, has_add=add is not None, has_res=res is not None, act=act,
        head_major=head_major is not None)
    return pl.pallas_call(
        kern, out_shape=out_shape, grid=grid, in_specs=in_specs, out_specs=out_spec,
        compiler_params=_cparams("parallel", "arbitrary"), name=name,
    )(*args)


def _compress_kernel(r_ref, pa_ref, pb_ref, w1a_ref, w1b_ref, w2_ref, o_ref):
    r = r_ref[...].astype(F32)
    a = jnp.dot((r + pa_ref[...]).astype(BF16), w1a_ref[...], preferred_element_type=F32)
    b = jnp.dot((r + pb_ref[...]).astype(BF16), w1b_ref[...], preferred_element_type=F32)
    nr = a.shape[0]
    hid = a + pltpu.roll(b, nr - 1, 0)
    hid = jax.nn.gelu(hid)
    o_ref[...] = jnp.dot(hid.astype(BF16), w2_ref[...], preferred_element_type=F32).astype(o_ref.dtype)


def compress(r, head0, pos_a, pos_b, w1a, w1b, w2):
    b, _, nr, kk = r.shape
    g = NSA_KV_HEADS
    full = lambda shape: pl.BlockSpec(shape, lambda bi, gi: (0,) * len(shape))
    return pl.pallas_call(
        _compress_kernel,
        out_shape=jax.ShapeDtypeStruct((b, g, nr, LANES), BF16),
        grid=(b, g),
        in_specs=[pl.BlockSpec((None, None, nr, kk), lambda bi, gi: (bi, head0 + gi, 0, 0)),
                  full((1, kk)), full((1, kk)), full((kk, LANES)), full((kk, LANES)),
                  full((LANES, LANES))],
        out_specs=pl.BlockSpec((None, None, nr, LANES), lambda bi, gi: (bi, gi, 0, 0)),
        compiler_params=_cparams("parallel", "parallel"),
        name="nsa_compress",
    )(r, pos_a, pos_b, w1a, w1b, w2)


def _dot_nt(a, b):
    return lax.dot_general(a, b, (((1,), (1,)), ((), ())), preferred_element_type=F32)


def _split_dot(x, w):
    hi = x.astype(BF16)
    lo = (x - hi.astype(F32)).astype(BF16)
    return (jnp.dot(hi, w, preferred_element_type=F32) + jnp.dot(lo, w, preferred_element_type=F32))


def _nsa_cmp_kernel(q_ref, kcc_ref, vcc_ref, msel_ref, glog_ref, e_ref, oc_ref, sel_ref, s_sc, p_sc,
                    *, tq, rc, topk):
    s0 = pl.program_id(2) * tq
    nr = kcc_ref.shape[0]
    for h in range(NSA_HPG):
        s_sc[h] = _dot_nt(q_ref[h], kcc_ref[...])
    lane_v = lax.broadcasted_iota(jnp.int32, (nr, LANES), 1)
    vcc = vcc_ref[...]
    vm = jnp.concatenate([jnp.where(lane_v == HALF, jnp.ones_like(vcc), vcc), msel_ref[...]], axis=1)
    dmat = (lax.broadcasted_iota(jnp.int32, (rc, nr), 0)
            - lax.broadcasted_iota(jnp.int32, (rc, nr), 1) * CMP_STRIDE)
    outs = []
    imp = jnp.zeros((tq, LANES), F32)
    for h in range(NSA_HPG):
        for c in range(tq // rc):
            rows = pl.ds(c * rc, rc)
            msk = dmat + (s0 + c * rc - (CMP_BLOCK - 1)) >= 0
            s = jnp.where(msk, s_sc[h, rows, :], NEG)
            m = jnp.max(s, axis=-1, keepdims=True)
            p_sc[h, rows, :] = jnp.where(msk, jnp.exp2(s - m), 0.0).astype(BF16)
        a = jnp.dot(p_sc[h], vm, preferred_element_type=F32)
        denom = jnp.maximum(a[:, HALF:HALF + 1], 1e-30)
        outs.append(a[:, :LANES] / denom)
        imp = imp + a[:, LANES:] / denom

    blk = lax.broadcasted_iota(jnp.int32, (tq, LANES), 1)
    cur = (s0 + lax.broadcasted_iota(jnp.int32, (tq, LANES), 0)) // SEL_BLOCK
    valid = blk <= cur
    forced = (blk == 0) | (blk == cur) | (blk == cur - 1)
    score = jnp.where(valid, jnp.where(forced, 1e4, imp), -1.0)
    sc = score.T
    rowid = lax.broadcasted_iota(jnp.int32, (LANES, tq), 0).astype(F32)

    def pick_one(_, sc):
        cm = jnp.max(sc, axis=0, keepdims=True)
        first = jnp.min(jnp.where(sc == cm, rowid, float(LANES)), axis=0, keepdims=True)
        return jnp.where(rowid == first, -jnp.inf, sc)

    picked = lax.fori_loop(0, topk, pick_one, sc) == -jnp.inf
    sel_ref[...] = jnp.where(picked, 0.0, NEG).T.astype(sel_ref.dtype)

    glog = glog_ref[...]
    lane = lax.broadcasted_iota(jnp.int32, (tq, LANES), 1)
    for pr in range(NSA_HPG // 2):
        gate = jax.nn.sigmoid(_split_dot(glog, e_ref[pr]))
        pair = jnp.where(lane < HALF, outs[2 * pr], pltpu.roll(outs[2 * pr + 1], HALF, 1))
        oc_ref[:, pr * LANES:(pr + 1) * LANES] = (gate * pair).astype(oc_ref.dtype)


def nsa_compressed(qk, kcc, vcc, msel, misc, glog_blk, expand, *, batch, seq, tq, rc):
    g = NSA_KV_HEADS
    nq = seq // tq
    nr = kcc.shape[2]
    topk = min(SEL_TOPK, seq // SEL_BLOCK)
    kern = functools.partial(_nsa_cmp_kernel, tq=tq, rc=rc, topk=topk)
    return pl.pallas_call(
        kern,
        out_shape=(jax.ShapeDtypeStruct((batch * seq, NSA_HEADS * NSA_DV), F32),
                   jax.ShapeDtypeStruct((batch, g, seq, LANES), BF16)),
        grid=(batch, g, nq),
        in_specs=[
            pl.BlockSpec((None, NSA_HPG, tq, LANES), lambda b, gi, qi: (b, gi, qi, 0)),
            pl.BlockSpec((None, None, nr, LANES), lambda b, gi, qi: (b, gi, 0, 0)),
            pl.BlockSpec((None, None, nr, LANES), lambda b, gi, qi: (b, gi, 0, 0)),
            pl.BlockSpec((nr, LANES), lambda b, gi, qi: (0, 0)),
            pl.BlockSpec((tq, LANES), lambda b, gi, qi: (b * nq + qi, glog_blk)),
            pl.BlockSpec((None, 2, LANES, LANES), lambda b, gi, qi: (0, gi, 0, 0)),
        ],
        out_specs=(pl.BlockSpec((tq, 2 * LANES), lambda b, gi, qi: (b * nq + qi, gi)),
                   pl.BlockSpec((None, None, tq, LANES), lambda b, gi, qi: (b, gi, qi, 0))),
        scratch_shapes=[pltpu.VMEM((NSA_HPG, tq, nr), F32), pltpu.VMEM((NSA_HPG, tq, nr), BF16)],
        compiler_params=_cparams("parallel", "parallel", "parallel"),
        name="nsa_compressed_select",
    )(qk, kcc, vcc, msel, misc, expand)


def _flash_steps(mode, nq, tq, tk):
    r = tq // tk
    qi_l, kt_l, first_l, last_l = [], [], [], []
    for qi in range(nq):
        hi = qi * r + r - 1
        lo = max(0, qi * r - (-(-(WINDOW - 1) // tk))) if mode == "window" else 0
        for kt in range(lo, hi + 1):
            qi_l.append(qi)
            kt_l.append(kt)
            first_l.append(int(kt == lo))
            last_l.append(int(kt == hi))
    return tuple(jnp.asarray(np.asarray(a, np.int32)) for a in (qi_l, kt_l, first_l, last_l))


def _flash_kernel(*refs, mode, tq, tk, rc, kv_shared, gated):
    it = iter(refs)
    qi_ref, kt_ref, first_ref, last_ref = next(it), next(it), next(it), next(it)
    q_ref, k_ref, v_ref = next(it), next(it), next(it)
    selb_ref = next(it) if mode == "select" else None
    oh_ref = next(it) if mode == "select" else None
    glog_ref = next(it) if gated else None
    e_ref = next(it) if gated else None
    o_ref = next(it)
    m_sc, acc_sc, s_sc, p_sc = next(it), next(it), next(it), next(it)
    qa_sc = next(it) if mode == "select" else None

    st = pl.program_id(2)
    s0 = qi_ref[st] * tq
    k0 = kt_ref[st] * tk

    @pl.when(first_ref[st] == 1)
    def _init():
        m_sc[...] = jnp.full(m_sc.shape, NEG, F32)
        acc_sc[...] = jnp.zeros(acc_sc.shape, F32)
        if mode == "select":
            for hh in range(FLASH_HEADS):
                qa_sc[hh] = jnp.concatenate([q_ref[hh], selb_ref[...]], axis=1)

    def chunk_cols(off, r0):
        if off is None:
            return 0, tk, True
        r1 = r0 + rc
        hi = min(tk, r1 - off)
        lo = max(0, r0 - off - WINDOW + 1) if mode == "window" else 0
        lo, hi = lo // LANES * LANES, -(-hi // LANES) * LANES
        clear = tk - 1 <= r0 - off and (mode != "window" or r1 - 1 - off < WINDOW)
        return lo, hi, clear

    def tile(off):
        for hh in range(FLASH_HEADS):
            k = k_ref[0 if kv_shared else hh]
            if mode == "select":
                s_sc[hh] = _dot_nt(qa_sc[hh], jnp.concatenate([k, oh_ref[...]], axis=1))
            else:
                s_sc[hh] = _dot_nt(q_ref[hh], k)
        lane_v = lax.broadcasted_iota(jnp.int32, (tk, LANES), 1)
        if off is not None:
            dmat = (lax.broadcasted_iota(jnp.int32, (rc, tk), 0)
                    - lax.broadcasted_iota(jnp.int32, (rc, tk), 1))
        for hh in range(FLASH_HEADS):
            v = v_ref[0 if kv_shared else hh]
            v_aug = jnp.where(lane_v == HALF, jnp.ones_like(v), v)

            for c in range(tq // rc):
                r0 = c * rc
                rows = pl.ds(r0, rc)
                lo, hi, clear = chunk_cols(off, r0)
                if lo > 0:
                    p_sc[hh, rows, 0:lo] = jnp.zeros((rc, lo), BF16)
                if hi < tk:
                    p_sc[hh, rows, max(hi, 0):tk] = jnp.zeros((rc, tk - max(hi, 0)), BF16)
                if hi <= lo:
                    continue
                s = s_sc[hh, rows, lo:hi]
                if not clear:
                    d = dmat[:, lo:hi] + (r0 - off)
                    msk = (d >= 0) & (d < WINDOW) if mode == "window" else d >= 0
                    s = jnp.where(msk, s, NEG)
                m_old = m_sc[hh, rows, :]
                m_new = jnp.maximum(m_old, jnp.max(s, axis=-1, keepdims=True))
                p = jnp.exp2(s - jnp.concatenate([m_new] * ((hi - lo) // LANES), axis=1))
                if not clear:
                    p = jnp.where(msk, p, 0.0)
                m_sc[hh, rows, :] = m_new
                acc_sc[hh, rows, :] = acc_sc[hh, rows, :] * jnp.exp2(m_old - m_new)
                p_sc[hh, rows, lo:hi] = p.astype(BF16)
            acc_sc[hh] += jnp.dot(p_sc[hh], v_aug, preferred_element_type=F32)

    if mode == "window":
        offsets = [(j - (-(-(WINDOW - 1) // tk))) * tk for j in range(-(-(WINDOW - 1) // tk) + tq // tk)]
    else:
        offsets = [j * tk for j in range(tq // tk)]

        @pl.when(k0 + tk - 1 <= s0)
        def _interior():
            tile(None)

    for off in offsets:
        @pl.when(k0 - s0 == off)
        def _partial(off=off):
            tile(off)

    @pl.when(last_ref[st] == 1)
    def _finish():
        lane = lax.broadcasted_iota(jnp.int32, (tq, LANES), 1)
        if gated:
            glog = glog_ref[...]
        for pr in range(FLASH_HEADS // 2):
            outs = []
            for hh in (2 * pr, 2 * pr + 1):
                a = acc_sc[hh]
                outs.append(a / jnp.maximum(a[:, HALF:HALF + 1], 1e-30))
            out = jnp.where(lane < HALF, outs[0], pltpu.roll(outs[1], HALF, 1))
            if gated:
                out = jax.nn.sigmoid(_split_dot(glog, e_ref[pr])) * out
            o_ref[:, pr * LANES:(pr + 1) * LANES] = out.astype(o_ref.dtype)


def flash_heads(q, k, v, *, mode, batch, seq, tq, tk, rc, q_head0, k_head0, v_head0,
                kv_shared, out_dtype, selb=None, onehot=None, gate=None, name="flash"):
    nh = FLASH_HEADS
    ngroups = NSA_HEADS // nh
    nq = seq // tq
    assert tq % tk == 0 and tq % rc == 0
    tabs = _flash_steps(mode, nq, tq, tk)
    nsteps = tabs[0].shape[0]
    if kv_shared:
        kv_spec = lambda h0: pl.BlockSpec(
            (None, 1, tk, LANES), lambda b, g, st, qi, kt, fi, la: (b, h0 + g, kt[st], 0))
    else:
        kv_spec = lambda h0: pl.BlockSpec(
            (None, nh, tk, LANES), lambda b, g, st, qi, kt, fi, la: (b, h0 // nh + g, kt[st], 0))
    in_specs = [pl.BlockSpec((None, nh, tq, LANES),
                             lambda b, g, st, qi, kt, fi, la: (b, q_head0 // nh + g, qi[st], 0)),
                kv_spec(k_head0), kv_spec(v_head0)]
    args = [q, k, v]
    scratch = [pltpu.VMEM((nh, tq, LANES), F32), pltpu.VMEM((nh, tq, LANES), F32),
               pltpu.VMEM((nh, tq, tk), F32), pltpu.VMEM((nh, tq, tk), BF16)]
    if mode == "select":
        in_specs += [pl.BlockSpec((None, None, tq, LANES),
                                  lambda b, g, st, qi, kt, fi, la: (b, g, qi[st], 0)),
                     pl.BlockSpec((tk, LANES), lambda b, g, st, qi, kt, fi, la: (kt[st], 0))]
        args += [selb, onehot]
        scratch.append(pltpu.VMEM((nh, tq, 2 * LANES), BF16))
    if gate is not None:
        misc, glog_blk, expand, branch = gate
        in_specs += [pl.BlockSpec((tq, LANES),
                                  lambda b, g, st, qi, kt, fi, la: (b * nq + qi[st], glog_blk)),
                     pl.BlockSpec((None, nh // 2, LANES, LANES),
                                  lambda b, g, st, qi, kt, fi, la: (branch, g, 0, 0))]
        args += [misc, expand]
    kern = functools.partial(_flash_kernel, mode=mode, tq=tq, tk=tk, rc=rc,
                             kv_shared=kv_shared, gated=gate is not None)
    ow = nh * HALF
    grid_spec = pltpu.PrefetchScalarGridSpec(
        num_scalar_prefetch=4, grid=(batch, ngroups, nsteps), in_specs=in_specs,
        out_specs=pl.BlockSpec((tq, ow), lambda b, g, st, qi, kt, fi, la: (b * nq + qi[st], g)),
        scratch_shapes=scratch)
    return pl.pallas_call(
        kern,
        out_shape=jax.ShapeDtypeStruct((batch * seq, ngroups * ow), out_dtype),
        grid_spec=grid_spec,
        compiler_params=_cparams("parallel", "parallel", "arbitrary"),
        name=name,
    )(*tabs, *args)


def _mix_kernel(oc_ref, os_ref, ow_ref, ob_ref, ga_ref, gb_ref, wa_ref, wb_ref, o_ref):
    oa = (oc_ref[...] + os_ref[...] + ow_ref[...]).astype(BF16)
    ya = jnp.dot(oa, wa_ref[...], preferred_element_type=F32)
    yb = jnp.dot(ob_ref[...], wb_ref[...], preferred_element_type=F32)
    o_ref[...] = (ga_ref[...].astype(F32) * ya + gb_ref[...].astype(F32) * yb).astype(o_ref.dtype)


def gated_mix(oc, osel, ow, ob, gates, wa, wb, *, tm, tn):
    n, ka = oc.shape
    d = wa.shape[1]
    nb = d // tn
    row = lambda w: pl.BlockSpec((tm, w), lambda i, j: (i, 0))
    return pl.pallas_call(
        _mix_kernel,
        out_shape=jax.ShapeDtypeStruct((n, d), BF16),
        grid=(n // tm, nb),
        in_specs=[row(ka), row(ka), row(ka), row(ob.shape[1]),
                  pl.BlockSpec((tm, tn), lambda i, j: (i, j)),
                  pl.BlockSpec((tm, tn), lambda i, j: (i, nb + j)),
                  pl.BlockSpec((ka, tn), lambda i, j: (0, j)),
                  pl.BlockSpec((wb.shape[0], tn), lambda i, j: (0, j))],
        out_specs=pl.BlockSpec((tm, tn), lambda i, j: (i, j)),
        compiler_params=_cparams("parallel", "arbitrary"),
        name="gated_mix",
    )(oc, osel, ow, ob, gates, gates, wa, wb)


def _xattn_kernel(h_ref, g_ref, wq_ref, kv_ref, wo_ref, o_ref):
    h = h_ref[...]
    hn = _rms(h, g_ref[...]).astype(BF16)
    q = jnp.dot(hn, wq_ref[...], preferred_element_type=F32).astype(BF16)
    kv = kv_ref[...]
    outs = []
    for hd in range(XA_HEADS):
        qh = q[:, hd * XA_DIM:(hd + 1) * XA_DIM]
        kh = kv[:, hd * XA_DIM:(hd + 1) * XA_DIM]
        vh = kv[:, (XA_HEADS + hd) * XA_DIM:(XA_HEADS + hd + 1) * XA_DIM]
        s = _dot_nt(qh, kh) * (XA_DIM ** -0.5)
        e = jnp.exp(s - jnp.max(s, axis=-1, keepdims=True))
        p = e / jnp.sum(e, axis=-1, keepdims=True)
        outs.append(jnp.dot(p.astype(BF16), vh, preferred_element_type=F32))
    o = jnp.concatenate(outs, axis=1).astype(BF16)
    o_ref[...] = h + jnp.dot(o, wo_ref[...], preferred_element_type=F32)


def memory_xattn(h, g, wq, kv, wo, *, seq, tm):
    n, d = h.shape
    s_t = seq // tm
    dq = wq.shape[1]
    return pl.pallas_call(
        _xattn_kernel,
        out_shape=jax.ShapeDtypeStruct((n, d), F32),
        grid=(n // tm,),
        in_specs=[pl.BlockSpec((tm, d), lambda i: (i, 0)),
                  pl.BlockSpec((1, d), lambda i: (0, 0)),
                  pl.BlockSpec((d, dq), lambda i: (0, 0)),
                  pl.BlockSpec((MEM_LEN, 2 * dq), lambda i: (i // s_t, 0)),
                  pl.BlockSpec((dq, d), lambda i: (0, 0))],
        out_specs=pl.BlockSpec((tm, d), lambda i: (i, 0)),
        compiler_params=_cparams("parallel"),
        name="memory_xattn",
    )(h, g.reshape(1, d), wq, kv, wo)


def _mlp_kernel(h_ref, g_ref, w1_ref, w2_ref, gf_ref, o_ref, hn_sc, acc_sc, *, nf):
    f = pl.program_id(1)

    @pl.when(f == 0)
    def _init():
        hn_sc[...] = _rms(h_ref[...], g_ref[...]).astype(BF16)
        acc_sc[...] = jnp.zeros(acc_sc.shape, F32)

    u = jnp.dot(hn_sc[...], w1_ref[...], preferred_element_type=F32)
    u = jnp.square(jnp.maximum(u, 0.0))
    acc_sc[...] += jnp.dot(u.astype(BF16), w2_ref[...], preferred_element_type=F32)

    @pl.when(f == nf - 1)
    def _finish():
        o_ref[...] = _rms(h_ref[...] + acc_sc[...], gf_ref[...])


def mlp_final(h, g, w1, w2, gf, *, tm, tf):
    n, d = h.shape
    dff = w1.shape[1]
    nf = dff // tf
    return pl.pallas_call(
        functools.partial(_mlp_kernel, nf=nf),
        out_shape=jax.ShapeDtypeStruct((n, d), F32),
        grid=(n // tm, nf),
        in_specs=[pl.BlockSpec((tm, d), lambda i, f: (i, 0)),
                  pl.BlockSpec((1, d), lambda i, f: (0, 0)),
                  pl.BlockSpec((d, tf), lambda i, f: (0, f)),
                  pl.BlockSpec((tf, d), lambda i, f: (f, 0)),
                  pl.BlockSpec((1, d), lambda i, f: (0, 0))],
        out_specs=pl.BlockSpec((tm, d), lambda i, f: (i, 0)),
        scratch_shapes=[pltpu.VMEM((tm, d), BF16), pltpu.VMEM((tm, d), F32)],
        compiler_params=_cparams("parallel", "arbitrary"),
        name="mlp_final",
    )(h, g.reshape(1, d), w1, w2, gf.reshape(1, d))


def _rot_partner(w, half):
    return jnp.concatenate([-w[..., half:], w[..., :half]], axis=-1)


def _pad_heads_rope(w, heads, hd, rot0, rot_dim):
    k = w.shape[0]
    w = w.reshape(k, heads, hd)
    partner = _rot_partner(w[:, :, rot0:rot0 + rot_dim], rot_dim // 2)
    pad = jnp.zeros((k, heads, LANES - hd - rot_dim), w.dtype)
    return jnp.concatenate([w, partner, pad], axis=-1).reshape(k, heads * LANES)


def _rope_lane_tables(seq, rot0, rot_dim, hd):
    inv = 1.0 / (ROPE_THETA ** (jnp.arange(0, rot_dim, 2, dtype=F32) / rot_dim))
    ang = jnp.arange(seq, dtype=F32)[:, None] * inv[None, :]
    cos = jnp.concatenate([jnp.cos(ang), jnp.cos(ang)], axis=1)
    sin = jnp.concatenate([jnp.sin(ang), jnp.sin(ang)], axis=1)
    cos_t = jnp.concatenate([jnp.ones((seq, rot0), F32), cos,
                             jnp.ones((seq, hd - rot0 - rot_dim), F32),
                             jnp.zeros((seq, LANES - hd), F32)], axis=1)
    sin_t = jnp.concatenate([jnp.zeros((seq, rot0), F32), sin,
                             jnp.zeros((seq, LANES - rot0 - rot_dim), F32)], axis=1)
    return cos_t, sin_t


def _pad_lanes(w, groups, width):
    k = w.shape[0]
    w = w.reshape(k, groups, width)
    return jnp.pad(w, ((0, 0), (0, 0), (0, LANES - width))).reshape(k, groups * LANES)


def _compress_params(pos, w1, w2, d):
    half = CMP_BLOCK // 2
    pos_p = jnp.pad(pos, ((0, 0), (0, LANES - d)))
    pos_a = pos_p[:half].reshape(1, half * LANES)
    pos_b = pos_p[half:].reshape(1, half * LANES)
    w1p = jnp.pad(w1.reshape(CMP_BLOCK, d, d), ((0, 0), (0, LANES - d), (0, LANES - d)))
    w1a = w1p[:half].reshape(half * LANES, LANES).astype(BF16)
    w1b = w1p[half:].reshape(half * LANES, LANES).astype(BF16)
    w2p = jnp.pad(w2, ((0, LANES - d), (0, LANES - d)))
    return pos_a, pos_b, w1a, w1b, w2p.astype(BF16)


def _cmp_to_sel(nr, nsb):
    cs = np.arange(nr) * CMP_STRIDE
    ce = cs + CMP_BLOCK
    ss = np.arange(LANES) * SEL_BLOCK
    se = ss + SEL_BLOCK
    ov = np.clip(np.minimum(ce[:, None], se[None, :]) - np.maximum(cs[:, None], ss[None, :]), 0, None)
    ov = ov.astype(np.float32) / np.float32(CMP_BLOCK)
    ov[:, nsb:] = 0.0
    ov[nr - 1:, :] = 0.0
    return jnp.asarray(ov, BF16)


def _gate_expand():
    e = np.zeros((3, NSA_HEADS // 2, LANES, LANES), np.float32)
    for br in range(3):
        for hp in range(NSA_HEADS // 2):
            for hh in range(2):
                e[br, hp, 3 * (2 * hp + hh) + br, hh * HALF:(hh + 1) * HALF] = 1.0
    return jnp.asarray(e, BF16)


def _key_block_onehot(seq):
    e = (np.arange(seq)[:, None] // SEL_BLOCK) == np.arange(LANES)[None, :]
    return jnp.asarray(e.astype(np.float32), BF16)


def kernel(x, mem, g_mix, w_in, cmp_pos_k, cmp_w1_k, cmp_w2_k, cmp_pos_v, cmp_w1_v, cmp_w2_v,
           mla_g_q, mla_w_uq, mla_g_kv, mla_w_uk, mla_w_uv, w_o_nsa, w_o_mla, w_out,
           g_xattn, g_mem, xa_wq, xa_wkv, xa_wo, g_mlp, w_ff1, w_ff2, g_final):
    b, s, d = x.shape
    assert d == D_MODEL and s % (CMP_STRIDE * 8) == 0 and s // SEL_BLOCK <= LANES
    assert g_mix.shape[0] == 1
    n = b * s
    T = _tiles(s)
    tm, tq, tk, rc = T["tm"], T["tq"], T["tk"], T["rc"]
    G = NSA_KV_HEADS
    bounds = [int(v) for v in np.cumsum(SPLITS)[:-1]]

    (w_qa, w_kc, w_vc, w_ks, w_vs, w_kw, w_vw, w_gn, w_cq, w_ckv, w_kr,
     w_ga, w_gb) = jnp.split(w_in[0], bounds, axis=1)
    nsa_rope = lambda w, heads: _pad_heads_rope(w, heads, NSA_DK, 0, NSA_ROT)
    w_rope = jnp.concatenate([nsa_rope(w_qa, NSA_HEADS), nsa_rope(w_kc, G), nsa_rope(w_ks, G),
                              nsa_rope(w_kw, G)], axis=1).astype(BF16)
    w_krp = jnp.concatenate([jnp.zeros((d, MLA_NOPE), F32), w_kr, _rot_partner(w_kr, MLA_ROPE // 2)],
                            axis=1).astype(BF16)
    w_vsw = jnp.concatenate([_pad_lanes(w_vs, G, NSA_DV), _pad_lanes(w_vw, G, NSA_DV)], axis=1).astype(BF16)
    w_vcp = _pad_lanes(w_vc, G, NSA_DV).astype(BF16)
    w_misc = jnp.concatenate([w_cq, w_ckv, jnp.pad(w_gn, ((0, 0), (0, LANES - w_gn.shape[1])))],
                             axis=1).astype(BF16)
    glog_blk = (MLA_Q_RANK + MLA_KV_RANK) // LANES
    w_gates = jnp.concatenate([w_ga, w_gb], axis=1).astype(BF16)
    cos_a, sin_a = _rope_lane_tables(s, 0, NSA_ROT, NSA_DK)
    cos_b, sin_b = _rope_lane_tables(s, MLA_NOPE, MLA_ROPE, MLA_NOPE + MLA_ROPE)
    shift_a = LANES - NSA_DK
    shift_b = LANES - (NSA_DK - MLA_NOPE)
    w_uq = _pad_heads_rope(mla_w_uq[0], MLA_HEADS, MLA_NOPE + MLA_ROPE, MLA_NOPE, MLA_ROPE).astype(BF16)
    w_uk = _pad_lanes(mla_w_uk[0], MLA_HEADS, MLA_NOPE).astype(BF16)
    w_uv = _pad_lanes(mla_w_uv[0], MLA_HEADS, MLA_DV).astype(BF16)

    x2 = x.reshape(n, d)
    xn = rmsnorm_rows(x2, g_mix[0], tm)
    log2e = float(np.log2(np.e))
    qk = projection(xn, w_rope, tm=tm, tn=512, out_dtype=BF16, rope=(cos_a, sin_a, shift_a),
                    out_scale=(NSA_DK ** -0.5 * log2e, NSA_HEADS * LANES // 512),
                    head_major=(b, s), name="proj_qk_rope")
    krp = projection(xn, w_krp, tm=tm, tn=LANES, out_dtype=BF16, rope=(cos_b, sin_b, shift_b),
                     name="proj_k_rope_mla")
    vsw = projection(xn, w_vsw, tm=tm, tn=512, out_dtype=BF16, head_major=(b, s), name="proj_v")
    vcp = projection(xn, w_vcp, tm=tm, tn=512, out_dtype=BF16, head_major=(b, s), name="proj_vc")
    misc = projection(xn, w_misc, tm=tm, tn=w_misc.shape[1], out_dtype=F32, name="proj_misc")
    gates = projection(xn, w_gates, tm=tm, tn=512, out_dtype=BF16, act="sigmoid", name="proj_gates")

    nr = s // CMP_STRIDE
    pk = _compress_params(cmp_pos_k[0], cmp_w1_k[0], cmp_w2_k[0], NSA_DK)
    pv = _compress_params(cmp_pos_v[0], cmp_w1_v[0], cmp_w2_v[0], NSA_DV)
    kc_rows = qk[:, QK_KC0:QK_KC0 + G].reshape(b, G, nr, CMP_STRIDE * LANES)
    kcc = compress(kc_rows, 0, *pk)
    vcc = compress(vcp.reshape(b, G, nr, CMP_STRIDE * LANES), 0, *pv)
    expand = _gate_expand()
    o_c, selb = nsa_compressed(qk, kcc, vcc, _cmp_to_sel(nr, s // SEL_BLOCK), misc, glog_blk, expand,
                               batch=b, seq=s, tq=T["tq_cmp"], rc=rc)
    o_s = flash_heads(qk, qk, vsw, mode="select", batch=b, seq=s, tq=tq, tk=tk, rc=rc,
                      q_head0=QK_Q0, k_head0=QK_KS0, v_head0=0, kv_shared=True, out_dtype=F32,
                      selb=selb, onehot=_key_block_onehot(s), gate=(misc, glog_blk, expand, 1),
                      name="nsa_selected")
    o_w = flash_heads(qk, qk, vsw, mode="window", batch=b, seq=s, tq=T["tq_win"], tk=tk, rc=rc,
                      q_head0=QK_Q0, k_head0=QK_KW0, v_head0=G, kv_shared=True, out_dtype=F32,
                      gate=(misc, glog_blk, expand, 2), name="nsa_window")

    qm = projection(misc, w_uq, tm=tm, tn=512, out_dtype=BF16, a_cols=(MLA_Q_RANK, 0), gain=mla_g_q[0],
                    rope=(cos_b, sin_b, shift_b), head_major=(b, s), name="mla_q",
                    out_scale=((MLA_NOPE + MLA_ROPE) ** -0.5 * log2e, MLA_HEADS * LANES // 512))
    km = projection(misc, w_uk, tm=tm, tn=512, out_dtype=BF16,
                    a_cols=(MLA_KV_RANK, MLA_Q_RANK // MLA_KV_RANK), gain=mla_g_kv[0], add=krp,
                    head_major=(b, s), name="mla_k")
    vm = projection(misc, w_uv, tm=tm, tn=512, out_dtype=BF16,
                    a_cols=(MLA_KV_RANK, MLA_Q_RANK // MLA_KV_RANK), gain=mla_g_kv[0],
                    head_major=(b, s), name="mla_v")
    o_b = flash_heads(qm, km, vm, mode="causal", batch=b, seq=s, tq=tq, tk=tk, rc=rc,
                      q_head0=0, k_head0=0, v_head0=0,
                      kv_shared=False, out_dtype=BF16, name="mla_attention")

    mixed = gated_mix(o_c, o_s, o_w, o_b, gates, w_o_nsa[0].astype(BF16), w_o_mla[0].astype(BF16),
                      tm=tm, tn=512)
    h1 = projection(mixed, w_out[0].astype(BF16), tm=tm, tn=512, out_dtype=F32, res=x2, name="proj_out")

    kv_mem = projection(mem.reshape(b * MEM_LEN, d), xa_wkv[0].astype(BF16), tm=MEM_LEN, tn=512,
                        out_dtype=BF16, gain=g_mem[0], name="xattn_kv")
    h2 = memory_xattn(h1, g_xattn[0], xa_wq[0].astype(BF16), kv_mem, xa_wo[0].astype(BF16), seq=s,
                      tm=T["tm_x"])

    out = mlp_final(h2, g_mlp[0], w_ff1[0].astype(BF16), w_ff2[0].astype(BF16), g_final,
                    tm=T["tm_mlp"], tf=T["tf"])
    return out.reshape(b, s, d)
```

```python
import functools

import numpy as np
import jax
import jax.numpy as jnp
from jax import lax
from jax.experimental import pallas as pl
from jax.experimental.pallas import tpu as pltpu

F32 = jnp.float32
BF16 = jnp.bfloat16

D_MODEL = 2048
MEM_LEN = 256
ROPE_THETA = 500000.0
EPS = 1e-6
NEG = -1e30

NSA_HEADS = 16
NSA_KV_HEADS = 4
NSA_HPG = NSA_HEADS // NSA_KV_HEADS
NSA_DK = 96
NSA_DV = 64
NSA_ROT = NSA_DK // 4
CMP_BLOCK = 32
CMP_STRIDE = 16
SEL_BLOCK = 64
SEL_TOPK = 16
WINDOW = 512

MLA_HEADS = 16
MLA_NOPE = 64
MLA_ROPE = 32
MLA_DV = 64
MLA_Q_RANK = 512
MLA_KV_RANK = 256

XA_HEADS = 4
XA_DIM = 128
D_FF = 4 * D_MODEL

SPLITS = (NSA_HEADS * NSA_DK,
          NSA_KV_HEADS * NSA_DK, NSA_KV_HEADS * NSA_DV,
          NSA_KV_HEADS * NSA_DK, NSA_KV_HEADS * NSA_DV,
          NSA_KV_HEADS * NSA_DK, NSA_KV_HEADS * NSA_DV,
          NSA_HEADS * 3,
          MLA_Q_RANK, MLA_KV_RANK, MLA_ROPE,
          D_MODEL, D_MODEL)

LANES = 128
HALF = LANES // 2
VMEM_LIMIT = 56 * 1024 * 1024

QK_Q0 = 0
QK_KC0 = NSA_HEADS
QK_KS0 = NSA_HEADS + NSA_KV_HEADS
QK_KW0 = NSA_HEADS + 2 * NSA_KV_HEADS
QK_HEADS = NSA_HEADS + 3 * NSA_KV_HEADS
FLASH_HEADS = NSA_HPG


def _cparams(*sem):
    return pltpu.CompilerParams(dimension_semantics=sem, vmem_limit_bytes=VMEM_LIMIT)


def _tiles(seq):
    return dict(
        tm=min(1024, seq),
        tm_x=min(512, seq),
        tm_mlp=min(512, seq),
        tq=min(1024, seq),
        tq_win=min(512, seq),
        tk=min(512, seq),
        rc=32,
        tq_cmp=min(256, seq),
        tf=512,
    )


def _rms(x, g):
    return x * lax.rsqrt(jnp.mean(x * x, axis=-1, keepdims=True) + EPS) * g


def _rmsnorm_kernel(x_ref, g_ref, o_ref):
    o_ref[...] = _rms(x_ref[...], g_ref[...]).astype(o_ref.dtype)


def rmsnorm_rows(x, g, tm):
    n, d = x.shape
    return pl.pallas_call(
        _rmsnorm_kernel,
        out_shape=jax.ShapeDtypeStruct((n, d), BF16),
        grid=(n // tm,),
        in_specs=[pl.BlockSpec((tm, d), lambda i: (i, 0)),
                  pl.BlockSpec((1, d), lambda i: (0, 0))],
        out_specs=pl.BlockSpec((tm, d), lambda i: (i, 0)),
        compiler_params=_cparams("parallel"),
        name="rmsnorm_rows",
    )(x, g.reshape(1, d))


def _proj_kernel(*refs, has_gain, has_rope, roll_shift, out_scale, has_add, has_res, act, head_major):
    it = iter(refs)
    a_ref = next(it)
    g_ref = next(it) if has_gain else None
    w_ref = next(it)
    cos_ref = next(it) if has_rope else None
    sin_ref = next(it) if has_rope else None
    add_ref = next(it) if has_add else None
    res_ref = next(it) if has_res else None
    o_ref = next(it)

    a = a_ref[...]
    if has_gain:
        a = _rms(a, g_ref[...]).astype(BF16)
    y = jnp.dot(a, w_ref[...], preferred_element_type=F32)
    if out_scale is not None:
        factor, n_tiles = out_scale
        y = y * jnp.where(pl.program_id(1) < n_tiles, factor, 1.0)
    if has_rope or has_add or head_major:
        for h in range(y.shape[1] // LANES):
            yh = y[:, h * LANES:(h + 1) * LANES]
            if has_rope:
                yh = yh * cos_ref[...] + pltpu.roll(yh, roll_shift, 1) * sin_ref[...]
            if has_add:
                yh = yh + add_ref[...].astype(F32)
            if head_major:
                o_ref[h] = yh.astype(o_ref.dtype)
            else:
                o_ref[:, h * LANES:(h + 1) * LANES] = yh.astype(o_ref.dtype)
    else:
        if act == "sigmoid":
            y = jax.nn.sigmoid(y)
        if has_res:
            y = res_ref[...] + y
        o_ref[...] = y.astype(o_ref.dtype)


def projection(a, w, *, tm, tn, out_dtype, a_cols=None, gain=None, rope=None, out_scale=None, add=None,
               res=None, act=None, head_major=None, name="projection"):
    n = a.shape[0]
    k, nc = w.shape
    a_w, a_blk = a_cols if a_cols is not None else (a.shape[1], 0)
    assert a_w == k and n % tm == 0 and nc % tn == 0
    grid = (n // tm, nc // tn)
    in_specs = [pl.BlockSpec((tm, k), lambda i, j: (i, a_blk))]
    args = [a]
    if gain is not None:
        in_specs.append(pl.BlockSpec((1, k), lambda i, j: (0, 0)))
        args.append(gain.reshape(1, k))
    in_specs.append(pl.BlockSpec((k, tn), lambda i, j: (0, j)))
    args.append(w)
    roll_shift = 0
    if rope is not None:
        cos, sin, roll_shift = rope
        s_tiles = cos.shape[0] // tm
        for t in (cos, sin):
            in_specs.append(pl.BlockSpec((tm, LANES), lambda i, j: (i % s_tiles, 0)))
            args.append(t)
    if add is not None:
        in_specs.append(pl.BlockSpec((tm, LANES), lambda i, j: (i, 0)))
        args.append(add)
    if res is not None:
        in_specs.append(pl.BlockSpec((tm, tn), lambda i, j: (i, j)))
        args.append(res)
    if head_major is not None:
        b, s = head_major
        s_t = s // tm
        hpt = tn // LANES
        out_shape = jax.ShapeDtypeStruct((b, nc // LANES, s, LANES), out_dtype)
        out_spec = pl.BlockSpec((None, hpt, tm, LANES), lambda i, j: (i // s_t, j, i % s_t, 0))
    else:
        out_shape = jax.ShapeDtypeStruct((n, nc), out_dtype)
        out_spec = pl.BlockSpec((tm, tn), lambda i, j: (i, j))
    kern = functools.partial(
        _proj_kernel, has_gain=gain is not None, has_rope=rope is not None, roll_shift=roll_shift,
        out_scale=out_scale, has_add=add is not None, has_res=res is not None, act=act,
        head_major=head_major is not None)
    return pl.pallas_call(
        kern, out_shape=out_shape, grid=grid, in_specs=in_specs, out_specs=out_spec,
        compiler_params=_cparams("parallel", "arbitrary"), name=name,
    )(*args)


def _compress_kernel(r_ref, pa_ref, pb_ref, w1a_ref, w1b_ref, w2_ref, o_ref):
    r = r_ref[...].astype(F32)
    a = jnp.dot((r + pa_ref[...]).astype(BF16), w1a_ref[...], preferred_element_type=F32)
    b = jnp.dot((r + pb_ref[...]).astype(BF16), w1b_ref[...], preferred_element_type=F32)
    nr = a.shape[0]
    hid = a + pltpu.roll(b, nr - 1, 0)
    hid = jax.nn.gelu(hid)
    o_ref[...] = jnp.dot(hid.astype(BF16), w2_ref[...], preferred_element_type=F32).astype(o_ref.dtype)


def compress(r, head0, pos_a, pos_b, w1a, w1b, w2):
    b, _, nr, kk = r.shape
    g = NSA_KV_HEADS
    full = lambda shape: pl.BlockSpec(shape, lambda bi, gi: (0,) * len(shape))
    return pl.pallas_call(
        _compress_kernel,
        out_shape=jax.ShapeDtypeStruct((b, g, nr, LANES), BF16),
        grid=(b, g),
        in_specs=[pl.BlockSpec((None, None, nr, kk), lambda bi, gi: (bi, head0 + gi, 0, 0)),
                  full((1, kk)), full((1, kk)), full((kk, LANES)), full((kk, LANES)),
                  full((LANES, LANES))],
        out_specs=pl.BlockSpec((None, None, nr, LANES), lambda bi, gi: (bi, gi, 0, 0)),
        compiler_params=_cparams("parallel", "parallel"),
        name="nsa_compress",
    )(r, pos_a, pos_b, w1a, w1b, w2)


def _dot_nt(a, b):
    return lax.dot_general(a, b, (((1,), (1,)), ((), ())), preferred_element_type=F32)


def _split_dot(x, w):
    hi = x.astype(BF16)
    lo = (x - hi.astype(F32)).astype(BF16)
    return (jnp.dot(hi, w, preferred_element_type=F32) + jnp.dot(lo, w, preferred_element_type=F32))


def _nsa_cmp_kernel(q_ref, kcc_ref, vcc_ref, msel_ref, glog_ref, e_ref, oc_ref, sel_ref, s_sc, p_sc,
                    *, tq, rc, topk):
    s0 = pl.program_id(2) * tq
    nr = kcc_ref.shape[0]
    for h in range(NSA_HPG):
        s_sc[h] = _dot_nt(q_ref[h], kcc_ref[...])
    lane_v = lax.broadcasted_iota(jnp.int32, (nr, LANES), 1)
    vcc = vcc_ref[...]
    vm = jnp.concatenate([jnp.where(lane_v == HALF, jnp.ones_like(vcc), vcc), msel_ref[...]], axis=1)
    dmat = (lax.broadcasted_iota(jnp.int32, (rc, nr), 0)
            - lax.broadcasted_iota(jnp.int32, (rc, nr), 1) * CMP_STRIDE)
    outs = []
    imp = jnp.zeros((tq, LANES), F32)
    for h in range(NSA_HPG):
        visible = lambda c: dmat + (s0 + c * rc - (CMP_BLOCK - 1)) >= 0
        row_max = []
        for c in range(tq // rc):
            s = jnp.where(visible(c), s_sc[h, pl.ds(c * rc, rc), :], NEG)
            row_max.append(jnp.max(s, axis=-1, keepdims=True))
        for c in range(tq // rc):
            rows = pl.ds(c * rc, rc)
            p = jnp.exp2(s_sc[h, rows, :] - row_max[c])
            p_sc[h, rows, :] = jnp.where(visible(c), p, 0.0).astype(BF16)
        a = jnp.dot(p_sc[h], vm, preferred_element_type=F32)
        denom = jnp.maximum(a[:, HALF:HALF + 1], 1e-30)
        outs.append(a[:, :LANES] / denom)
        imp = imp + a[:, LANES:] / denom

    blk = lax.broadcasted_iota(jnp.int32, (tq, LANES), 1)
    cur = (s0 + lax.broadcasted_iota(jnp.int32, (tq, LANES), 0)) // SEL_BLOCK
    valid = blk <= cur
    forced = (blk == 0) | (blk == cur) | (blk == cur - 1)
    score = jnp.where(valid, jnp.where(forced, 1e4, imp), -1.0)
    sc = score.T
    rowid = lax.broadcasted_iota(jnp.int32, (LANES, tq), 0).astype(F32)

    def pick_one(_, sc):
        cm = jnp.max(sc, axis=0, keepdims=True)
        first = jnp.min(jnp.where(sc == cm, rowid, float(LANES)), axis=0, keepdims=True)
        return jnp.where(rowid == first, -jnp.inf, sc)

    picked = lax.fori_loop(0, topk, pick_one, sc) == -jnp.inf
    sel_ref[...] = jnp.where(picked, 0.0, NEG).T.astype(sel_ref.dtype)

    glog = glog_ref[...]
    lane = lax.broadcasted_iota(jnp.int32, (tq, LANES), 1)
    for pr in range(NSA_HPG // 2):
        gate = jax.nn.sigmoid(_split_dot(glog, e_ref[pr]))
        pair = jnp.where(lane < HALF, outs[2 * pr], pltpu.roll(outs[2 * pr + 1], HALF, 1))
        oc_ref[:, pr * LANES:(pr + 1) * LANES] = (gate * pair).astype(oc_ref.dtype)


def nsa_compressed(qk, kcc, vcc, msel, misc, glog_blk, expand, *, batch, seq, tq, rc):
    g = NSA_KV_HEADS
    nq = seq // tq
    nr = kcc.shape[2]
    topk = min(SEL_TOPK, seq // SEL_BLOCK)
    kern = functools.partial(_nsa_cmp_kernel, tq=tq, rc=rc, topk=topk)
    return pl.pallas_call(
        kern,
        out_shape=(jax.ShapeDtypeStruct((batch * seq, NSA_HEADS * NSA_DV), F32),
                   jax.ShapeDtypeStruct((batch, g, seq, LANES), BF16)),
        grid=(batch, g, nq),
        in_specs=[
            pl.BlockSpec((None, NSA_HPG, tq, LANES), lambda b, gi, qi: (b, gi, qi, 0)),
            pl.BlockSpec((None, None, nr, LANES), lambda b, gi, qi: (b, gi, 0, 0)),
            pl.BlockSpec((None, None, nr, LANES), lambda b, gi, qi: (b, gi, 0, 0)),
            pl.BlockSpec((nr, LANES), lambda b, gi, qi: (0, 0)),
            pl.BlockSpec((tq, LANES), lambda b, gi, qi: (b * nq + qi, glog_blk)),
            pl.BlockSpec((None, 2, LANES, LANES), lambda b, gi, qi: (0, gi, 0, 0)),
        ],
        out_specs=(pl.BlockSpec((tq, 2 * LANES), lambda b, gi, qi: (b * nq + qi, gi)),
                   pl.BlockSpec((None, None, tq, LANES), lambda b, gi, qi: (b, gi, qi, 0))),
        scratch_shapes=[pltpu.VMEM((NSA_HPG, tq, nr), F32), pltpu.VMEM((NSA_HPG, tq, nr), BF16)],
        compiler_params=_cparams("parallel", "parallel", "parallel"),
        name="nsa_compressed_select",
    )(qk, kcc, vcc, msel, misc, expand)


def _flash_steps(mode, nq, tq, tk):
    r = tq // tk
    qi_l, kt_l, first_l, last_l = [], [], [], []
    for qi in range(nq):
        hi = qi * r + r - 1
        lo = max(0, qi * r - (-(-(WINDOW - 1) // tk))) if mode == "window" else 0
        for kt in range(lo, hi + 1):
            qi_l.append(qi)
            kt_l.append(kt)
            first_l.append(int(kt == lo))
            last_l.append(int(kt == hi))
    return tuple(jnp.asarray(np.asarray(a, np.int32)) for a in (qi_l, kt_l, first_l, last_l))


def _flash_kernel(*refs, mode, tq, tk, rc, kv_shared, gated):
    it = iter(refs)
    qi_ref, kt_ref, first_ref, last_ref = next(it), next(it), next(it), next(it)
    q_ref, k_ref, v_ref = next(it), next(it), next(it)
    selb_ref = next(it) if mode == "select" else None
    oh_ref = next(it) if mode == "select" else None
    glog_ref = next(it) if gated else None
    e_ref = next(it) if gated else None
    o_ref = next(it)
    m_sc, acc_sc, s_sc, p_sc = next(it), next(it), next(it), next(it)
    qa_sc = next(it) if mode == "select" else None

    st = pl.program_id(2)
    s0 = qi_ref[st] * tq
    k0 = kt_ref[st] * tk

    @pl.when(first_ref[st] == 1)
    def _init():
        m_sc[...] = jnp.full(m_sc.shape, NEG, F32)
        acc_sc[...] = jnp.zeros(acc_sc.shape, F32)
        if mode == "select":
            for hh in range(FLASH_HEADS):
                qa_sc[hh] = jnp.concatenate([q_ref[hh], selb_ref[...]], axis=1)

    def chunk_cols(off, r0):
        if off is None:
            return 0, tk, True
        r1 = r0 + rc
        hi = min(tk, r1 - off)
        lo = max(0, r0 - off - WINDOW + 1) if mode == "window" else 0
        lo, hi = lo // LANES * LANES, -(-hi // LANES) * LANES
        clear = tk - 1 <= r0 - off and (mode != "window" or r1 - 1 - off < WINDOW)
        return lo, hi, clear

    def tile(off):
        for hh in range(FLASH_HEADS):
            k = k_ref[0 if kv_shared else hh]
            if mode == "select":
                s_sc[hh] = _dot_nt(qa_sc[hh], jnp.concatenate([k, oh_ref[...]], axis=1))
            else:
                s_sc[hh] = _dot_nt(q_ref[hh], k)
        lane_v = lax.broadcasted_iota(jnp.int32, (tk, LANES), 1)
        if off is not None:
            dmat = (lax.broadcasted_iota(jnp.int32, (rc, tk), 0)
                    - lax.broadcasted_iota(jnp.int32, (rc, tk), 1))
        for hh in range(FLASH_HEADS):
            v = v_ref[0 if kv_shared else hh]
            v_aug = jnp.where(lane_v == HALF, jnp.ones_like(v), v)

            for c in range(tq // rc):
                r0 = c * rc
                rows = pl.ds(r0, rc)
                lo, hi, clear = chunk_cols(off, r0)
                if lo > 0:
                    p_sc[hh, rows, 0:lo] = jnp.zeros((rc, lo), BF16)
                if hi < tk:
                    p_sc[hh, rows, max(hi, 0):tk] = jnp.zeros((rc, tk - max(hi, 0)), BF16)
                if hi <= lo:
                    continue
                s = s_sc[hh, rows, lo:hi]
                if not clear:
                    d = dmat[:, lo:hi] + (r0 - off)
                    msk = (d >= 0) & (d < WINDOW) if mode == "window" else d >= 0
                    s = jnp.where(msk, s, NEG)
                m_old = m_sc[hh, rows, :]
                m_new = jnp.maximum(m_old, jnp.max(s, axis=-1, keepdims=True))
                m_sc[hh, rows, :] = m_new
                acc_sc[hh, rows, :] = acc_sc[hh, rows, :] * jnp.exp2(m_old - m_new)
                p = jnp.exp2(s_sc[hh, rows, lo:hi] - jnp.concatenate([m_new] * ((hi - lo) // LANES), axis=1))
                if not clear:
                    p = jnp.where(msk, p, 0.0)
                p_sc[hh, rows, lo:hi] = p.astype(BF16)
            acc_sc[hh] += jnp.dot(p_sc[hh], v_aug, preferred_element_type=F32)

    if mode == "window":
        offsets = [(j - (-(-(WINDOW - 1) // tk))) * tk for j in range(-(-(WINDOW - 1) // tk) + tq // tk)]
    else:
        offsets = [j * tk for j in range(tq // tk)]

        @pl.when(k0 + tk - 1 <= s0)
        def _interior():
            tile(None)

    for off in offsets:
        @pl.when(k0 - s0 == off)
        def _partial(off=off):
            tile(off)

    @pl.when(last_ref[st] == 1)
    def _finish():
        lane = lax.broadcasted_iota(jnp.int32, (tq, LANES), 1)
        if gated:
            glog = glog_ref[...]
        for pr in range(FLASH_HEADS // 2):
            outs = []
            for hh in (2 * pr, 2 * pr + 1):
                a = acc_sc[hh]
                outs.append(a / jnp.maximum(a[:, HALF:HALF + 1], 1e-30))
            out = jnp.where(lane < HALF, outs[0], pltpu.roll(outs[1], HALF, 1))
            if gated:
                out = jax.nn.sigmoid(_split_dot(glog, e_ref[pr])) * out
            o_ref[:, pr * LANES:(pr + 1) * LANES] = out.astype(o_ref.dtype)


def flash_heads(q, k, v, *, mode, batch, seq, tq, tk, rc, q_head0, k_head0, v_head0,
                kv_shared, out_dtype, selb=None, onehot=None, gate=None, name="flash"):
    nh = FLASH_HEADS
    ngroups = NSA_HEADS // nh
    nq = seq // tq
    assert tq % tk == 0 and tq % rc == 0
    tabs = _flash_steps(mode, nq, tq, tk)
    nsteps = tabs[0].shape[0]
    if kv_shared:
        kv_spec = lambda h0: pl.BlockSpec(
            (None, 1, tk, LANES), lambda b, g, st, qi, kt, fi, la: (b, h0 + g, kt[st], 0))
    else:
        kv_spec = lambda h0: pl.BlockSpec(
            (None, nh, tk, LANES), lambda b, g, st, qi, kt, fi, la: (b, h0 // nh + g, kt[st], 0))
    in_specs = [pl.BlockSpec((None, nh, tq, LANES),
                             lambda b, g, st, qi, kt, fi, la: (b, q_head0 // nh + g, qi[st], 0)),
                kv_spec(k_head0), kv_spec(v_head0)]
    args = [q, k, v]
    scratch = [pltpu.VMEM((nh, tq, LANES), F32), pltpu.VMEM((nh, tq, LANES), F32),
               pltpu.VMEM((nh, tq, tk), F32), pltpu.VMEM((nh, tq, tk), BF16)]
    if mode == "select":
        in_specs += [pl.BlockSpec((None, None, tq, LANES),
                                  lambda b, g, st, qi, kt, fi, la: (b, g, qi[st], 0)),
                     pl.BlockSpec((tk, LANES), lambda b, g, st, qi, kt, fi, la: (kt[st], 0))]
        args += [selb, onehot]
        scratch.append(pltpu.VMEM((nh, tq, 2 * LANES), BF16))
    if gate is not None:
        misc, glog_blk, expand, branch = gate
        in_specs += [pl.BlockSpec((tq, LANES),
                                  lambda b, g, st, qi, kt, fi, la: (b * nq + qi[st], glog_blk)),
                     pl.BlockSpec((None, nh // 2, LANES, LANES),
                                  lambda b, g, st, qi, kt, fi, la: (branch, g, 0, 0))]
        args += [misc, expand]
    kern = functools.partial(_flash_kernel, mode=mode, tq=tq, tk=tk, rc=rc,
                             kv_shared=kv_shared, gated=gate is not None)
    ow = nh * HALF
    grid_spec = pltpu.PrefetchScalarGridSpec(
        num_scalar_prefetch=4, grid=(batch, ngroups, nsteps), in_specs=in_specs,
        out_specs=pl.BlockSpec((tq, ow), lambda b, g, st, qi, kt, fi, la: (b * nq + qi[st], g)),
        scratch_shapes=scratch)
    return pl.pallas_call(
        kern,
        out_shape=jax.ShapeDtypeStruct((batch * seq, ngroups * ow), out_dtype),
        grid_spec=grid_spec,
        compiler_params=_cparams("parallel", "parallel", "arbitrary"),
        name=name,
    )(*tabs, *args)


def _mix_kernel(oc_ref, os_ref, ow_ref, ob_ref, ga_ref, gb_ref, wa_ref, wb_ref, o_ref):
    oa = (oc_ref[...] + os_ref[...] + ow_ref[...]).astype(BF16)
    ya = jnp.dot(oa, wa_ref[...], preferred_element_type=F32)
    yb = jnp.dot(ob_ref[...], wb_ref[...], preferred_element_type=F32)
    o_ref[...] = (ga_ref[...].astype(F32) * ya + gb_ref[...].astype(F32) * yb).astype(o_ref.dtype)


def gated_mix(oc, osel, ow, ob, gates, wa, wb, *, tm, tn):
    n, ka = oc.shape
    d = wa.shape[1]
    nb = d // tn
    row = lambda w: pl.BlockSpec((tm, w), lambda i, j: (i, 0))
    return pl.pallas_call(
        _mix_kernel,
        out_shape=jax.ShapeDtypeStruct((n, d), BF16),
        grid=(n // tm, nb),
        in_specs=[row(ka), row(ka), row(ka), row(ob.shape[1]),
                  pl.BlockSpec((tm, tn), lambda i, j: (i, j)),
                  pl.BlockSpec((tm, tn), lambda i, j: (i, nb + j)),
                  pl.BlockSpec((ka, tn), lambda i, j: (0, j)),
                  pl.BlockSpec((wb.shape[0], tn), lambda i, j: (0, j))],
        out_specs=pl.BlockSpec((tm, tn), lambda i, j: (i, j)),
        compiler_params=_cparams("parallel", "arbitrary"),
        name="gated_mix",
    )(oc, osel, ow, ob, gates, gates, wa, wb)


def _xattn_kernel(h_ref, g_ref, wq_ref, kv_ref, wo_ref, o_ref):
    h = h_ref[...]
    hn = _rms(h, g_ref[...]).astype(BF16)
    q = jnp.dot(hn, wq_ref[...], preferred_element_type=F32).astype(BF16)
    kv = kv_ref[...]
    outs = []
    for hd in range(XA_HEADS):
        qh = q[:, hd * XA_DIM:(hd + 1) * XA_DIM]
        kh = kv[:, hd * XA_DIM:(hd + 1) * XA_DIM]
        vh = kv[:, (XA_HEADS + hd) * XA_DIM:(XA_HEADS + hd + 1) * XA_DIM]
        s = _dot_nt(qh, kh) * (XA_DIM ** -0.5)
        e = jnp.exp(s - jnp.max(s, axis=-1, keepdims=True))
        p = e / jnp.sum(e, axis=-1, keepdims=True)
        outs.append(jnp.dot(p.astype(BF16), vh, preferred_element_type=F32))
    o = jnp.concatenate(outs, axis=1).astype(BF16)
    o_ref[...] = h + jnp.dot(o, wo_ref[...], preferred_element_type=F32)


def memory_xattn(h, g, wq, kv, wo, *, seq, tm):
    n, d = h.shape
    s_t = seq // tm
    dq = wq.shape[1]
    return pl.pallas_call(
        _xattn_kernel,
        out_shape=jax.ShapeDtypeStruct((n, d), F32),
        grid=(n // tm,),
        in_specs=[pl.BlockSpec((tm, d), lambda i: (i, 0)),
                  pl.BlockSpec((1, d), lambda i: (0, 0)),
                  pl.BlockSpec((d, dq), lambda i: (0, 0)),
                  pl.BlockSpec((MEM_LEN, 2 * dq), lambda i: (i // s_t, 0)),
                  pl.BlockSpec((dq, d), lambda i: (0, 0))],
        out_specs=pl.BlockSpec((tm, d), lambda i: (i, 0)),
        compiler_params=_cparams("parallel"),
        name="memory_xattn",
    )(h, g.reshape(1, d), wq, kv, wo)


def _mlp_kernel(h_ref, g_ref, w1_ref, w2_ref, gf_ref, o_ref, hn_sc, acc_sc, *, nf):
    f = pl.program_id(1)

    @pl.when(f == 0)
    def _init():
        hn_sc[...] = _rms(h_ref[...], g_ref[...]).astype(BF16)
        acc_sc[...] = jnp.zeros(acc_sc.shape, F32)

    u = jnp.dot(hn_sc[...], w1_ref[...], preferred_element_type=F32)
    u = jnp.square(jnp.maximum(u, 0.0))
    acc_sc[...] += jnp.dot(u.astype(BF16), w2_ref[...], preferred_element_type=F32)

    @pl.when(f == nf - 1)
    def _finish():
        o_ref[...] = _rms(h_ref[...] + acc_sc[...], gf_ref[...])


def mlp_final(h, g, w1, w2, gf, *, tm, tf):
    n, d = h.shape
    dff = w1.shape[1]
    nf = dff // tf
    return pl.pallas_call(
        functools.partial(_mlp_kernel, nf=nf),
        out_shape=jax.ShapeDtypeStruct((n, d), F32),
        grid=(n // tm, nf),
        in_specs=[pl.BlockSpec((tm, d), lambda i, f: (i, 0)),
                  pl.BlockSpec((1, d), lambda i, f: (0, 0)),
                  pl.BlockSpec((d, tf), lambda i, f: (0, f)),
                  pl.BlockSpec((tf, d), lambda i, f: (f, 0)),
                  pl.BlockSpec((1, d), lambda i, f: (0, 0))],
        out_specs=pl.BlockSpec((tm, d), lambda i, f: (i, 0)),
        scratch_shapes=[pltpu.VMEM((tm, d), BF16), pltpu.VMEM((tm, d), F32)],
        compiler_params=_cparams("parallel", "arbitrary"),
        name="mlp_final",
    )(h, g.reshape(1, d), w1, w2, gf.reshape(1, d))


def _rot_partner(w, half):
    return jnp.concatenate([-w[..., half:], w[..., :half]], axis=-1)


def _pad_heads_rope(w, heads, hd, rot0, rot_dim):
    k = w.shape[0]
    w = w.reshape(k, heads, hd)
    partner = _rot_partner(w[:, :, rot0:rot0 + rot_dim], rot_dim // 2)
    pad = jnp.zeros((k, heads, LANES - hd - rot_dim), w.dtype)
    return jnp.concatenate([w, partner, pad], axis=-1).reshape(k, heads * LANES)


def _rope_lane_tables(seq, rot0, rot_dim, hd):
    inv = 1.0 / (ROPE_THETA ** (jnp.arange(0, rot_dim, 2, dtype=F32) / rot_dim))
    ang = jnp.arange(seq, dtype=F32)[:, None] * inv[None, :]
    cos = jnp.concatenate([jnp.cos(ang), jnp.cos(ang)], axis=1)
    sin = jnp.concatenate([jnp.sin(ang), jnp.sin(ang)], axis=1)
    cos_t = jnp.concatenate([jnp.ones((seq, rot0), F32), cos,
                             jnp.ones((seq, hd - rot0 - rot_dim), F32),
                             jnp.zeros((seq, LANES - hd), F32)], axis=1)
    sin_t = jnp.concatenate([jnp.zeros((seq, rot0), F32), sin,
                             jnp.zeros((seq, LANES - rot0 - rot_dim), F32)], axis=1)
    return cos_t, sin_t


def _pad_lanes(w, groups, width):
    k = w.shape[0]
    w = w.reshape(k, groups, width)
    return jnp.pad(w, ((0, 0), (0, 0), (0, LANES - width))).reshape(k, groups * LANES)


def _compress_params(pos, w1, w2, d):
    half = CMP_BLOCK // 2
    pos_p = jnp.pad(pos, ((0, 0), (0, LANES - d)))
    pos_a = pos_p[:half].reshape(1, half * LANES)
    pos_b = pos_p[half:].reshape(1, half * LANES)
    w1p = jnp.pad(w1.reshape(CMP_BLOCK, d, d), ((0, 0), (0, LANES - d), (0, LANES - d)))
    w1a = w1p[:half].reshape(half * LANES, LANES).astype(BF16)
    w1b = w1p[half:].reshape(half * LANES, LANES).astype(BF16)
    w2p = jnp.pad(w2, ((0, LANES - d), (0, LANES - d)))
    return pos_a, pos_b, w1a, w1b, w2p.astype(BF16)


def _cmp_to_sel(nr, nsb):
    cs = np.arange(nr) * CMP_STRIDE
    ce = cs + CMP_BLOCK
    ss = np.arange(LANES) * SEL_BLOCK
    se = ss + SEL_BLOCK
    ov = np.clip(np.minimum(ce[:, None], se[None, :]) - np.maximum(cs[:, None], ss[None, :]), 0, None)
    ov = ov.astype(np.float32) / np.float32(CMP_BLOCK)
    ov[:, nsb:] = 0.0
    ov[nr - 1:, :] = 0.0
    return jnp.asarray(ov, BF16)


def _gate_expand():
    e = np.zeros((3, NSA_HEADS // 2, LANES, LANES), np.float32)
    for br in range(3):
        for hp in range(NSA_HEADS // 2):
            for hh in range(2):
                e[br, hp, 3 * (2 * hp + hh) + br, hh * HALF:(hh + 1) * HALF] = 1.0
    return jnp.asarray(e, BF16)


def _key_block_onehot(seq):
    e = (np.arange(seq)[:, None] // SEL_BLOCK) == np.arange(LANES)[None, :]
    return jnp.asarray(e.astype(np.float32), BF16)


def kernel(x, mem, g_mix, w_in, cmp_pos_k, cmp_w1_k, cmp_w2_k, cmp_pos_v, cmp_w1_v, cmp_w2_v,
           mla_g_q, mla_w_uq, mla_g_kv, mla_w_uk, mla_w_uv, w_o_nsa, w_o_mla, w_out,
           g_xattn, g_mem, xa_wq, xa_wkv, xa_wo, g_mlp, w_ff1, w_ff2, g_final):
    b, s, d = x.shape
    assert d == D_MODEL and s % (CMP_STRIDE * 8) == 0 and s // SEL_BLOCK <= LANES
    assert g_mix.shape[0] == 1
    n = b * s
    T = _tiles(s)
    tm, tq, tk, rc = T["tm"], T["tq"], T["tk"], T["rc"]
    G = NSA_KV_HEADS
    bounds = [int(v) for v in np.cumsum(SPLITS)[:-1]]

    (w_qa, w_kc, w_vc, w_ks, w_vs, w_kw, w_vw, w_gn, w_cq, w_ckv, w_kr,
     w_ga, w_gb) = jnp.split(w_in[0], bounds, axis=1)
    nsa_rope = lambda w, heads: _pad_heads_rope(w, heads, NSA_DK, 0, NSA_ROT)
    w_rope = jnp.concatenate([nsa_rope(w_qa, NSA_HEADS), nsa_rope(w_kc, G), nsa_rope(w_ks, G),
                              nsa_rope(w_kw, G)], axis=1).astype(BF16)
    w_krp = jnp.concatenate([jnp.zeros((d, MLA_NOPE), F32), w_kr, _rot_partner(w_kr, MLA_ROPE // 2)],
                            axis=1).astype(BF16)
    w_vsw = jnp.concatenate([_pad_lanes(w_vs, G, NSA_DV), _pad_lanes(w_vw, G, NSA_DV)], axis=1).astype(BF16)
    w_vcp = _pad_lanes(w_vc, G, NSA_DV).astype(BF16)
    w_misc = jnp.concatenate([w_cq, w_ckv, jnp.pad(w_gn, ((0, 0), (0, LANES - w_gn.shape[1])))],
                             axis=1).astype(BF16)
    glog_blk = (MLA_Q_RANK + MLA_KV_RANK) // LANES
    w_gates = jnp.concatenate([w_ga, w_gb], axis=1).astype(BF16)
    cos_a, sin_a = _rope_lane_tables(s, 0, NSA_ROT, NSA_DK)
    cos_b, sin_b = _rope_lane_tables(s, MLA_NOPE, MLA_ROPE, MLA_NOPE + MLA_ROPE)
    shift_a = LANES - NSA_DK
    shift_b = LANES - (NSA_DK - MLA_NOPE)
    w_uq = _pad_heads_rope(mla_w_uq[0], MLA_HEADS, MLA_NOPE + MLA_ROPE, MLA_NOPE, MLA_ROPE).astype(BF16)
    w_uk = _pad_lanes(mla_w_uk[0], MLA_HEADS, MLA_NOPE).astype(BF16)
    w_uv = _pad_lanes(mla_w_uv[0], MLA_HEADS, MLA_DV).astype(BF16)

    x2 = x.reshape(n, d)
    xn = rmsnorm_rows(x2, g_mix[0], tm)
    log2e = float(np.log2(np.e))
    qk = projection(xn, w_rope, tm=tm, tn=512, out_dtype=BF16, rope=(cos_a, sin_a, shift_a),
                    out_scale=(NSA_DK ** -0.5 * log2e, NSA_HEADS * LANES // 512),
                    head_major=(b, s), name="proj_qk_rope")
    krp = projection(xn, w_krp, tm=tm, tn=LANES, out_dtype=BF16, rope=(cos_b, sin_b, shift_b),
                     name="proj_k_rope_mla")
    vsw = projection(xn, w_vsw, tm=tm, tn=512, out_dtype=BF16, head_major=(b, s), name="proj_v")
    vcp = projection(xn, w_vcp, tm=tm, tn=512, out_dtype=BF16, head_major=(b, s), name="proj_vc")
    misc = projection(xn, w_misc, tm=tm, tn=w_misc.shape[1], out_dtype=F32, name="proj_misc")
    gates = projection(xn, w_gates, tm=tm, tn=512, out_dtype=BF16, act="sigmoid", name="proj_gates")

    nr = s // CMP_STRIDE
    pk = _compress_params(cmp_pos_k[0], cmp_w1_k[0], cmp_w2_k[0], NSA_DK)
    pv = _compress_params(cmp_pos_v[0], cmp_w1_v[0], cmp_w2_v[0], NSA_DV)
    kc_rows = qk[:, QK_KC0:QK_KC0 + G].reshape(b, G, nr, CMP_STRIDE * LANES)
    kcc = compress(kc_rows, 0, *pk)
    vcc = compress(vcp.reshape(b, G, nr, CMP_STRIDE * LANES), 0, *pv)
    expand = _gate_expand()
    o_c, selb = nsa_compressed(qk, kcc, vcc, _cmp_to_sel(nr, s // SEL_BLOCK), misc, glog_blk, expand,
                               batch=b, seq=s, tq=T["tq_cmp"], rc=rc)
    o_s = flash_heads(qk, qk, vsw, mode="select", batch=b, seq=s, tq=tq, tk=tk, rc=rc,
                      q_head0=QK_Q0, k_head0=QK_KS0, v_head0=0, kv_shared=True, out_dtype=F32,
                      selb=selb, onehot=_key_block_onehot(s), gate=(misc, glog_blk, expand, 1),
                      name="nsa_selected")
    o_w = flash_heads(qk, qk, vsw, mode="window", batch=b, seq=s, tq=T["tq_win"], tk=tk, rc=rc,
                      q_head0=QK_Q0, k_head0=QK_KW0, v_head0=G, kv_shared=True, out_dtype=F32,
                      gate=(misc, glog_blk, expand, 2), name="nsa_window")

    qm = projection(misc, w_uq, tm=tm, tn=512, out_dtype=BF16, a_cols=(MLA_Q_RANK, 0), gain=mla_g_q[0],
                    rope=(cos_b, sin_b, shift_b), head_major=(b, s), name="mla_q",
                    out_scale=((MLA_NOPE + MLA_ROPE) ** -0.5 * log2e, MLA_HEADS * LANES // 512))
    km = projection(misc, w_uk, tm=tm, tn=512, out_dtype=BF16,
                    a_cols=(MLA_KV_RANK, MLA_Q_RANK // MLA_KV_RANK), gain=mla_g_kv[0], add=krp,
                    head_major=(b, s), name="mla_k")
    vm = projection(misc, w_uv, tm=tm, tn=512, out_dtype=BF16,
                    a_cols=(MLA_KV_RANK, MLA_Q_RANK // MLA_KV_RANK), gain=mla_g_kv[0],
                    head_major=(b, s), name="mla_v")
    o_b = flash_heads(qm, km, vm, mode="causal", batch=b, seq=s, tq=tq, tk=tk, rc=rc,
                      q_head0=0, k_head0=0, v_head0=0,
                      kv_shared=False, out_dtype=BF16, name="mla_attention")

    mixed = gated_mix(o_c, o_s, o_w, o_b, gates, w_o_nsa[0].astype(BF16), w_o_mla[0].astype(BF16),
                      tm=tm, tn=512)
    h1 = projection(mixed, w_out[0].astype(BF16), tm=tm, tn=512, out_dtype=F32, res=x2, name="proj_out")

    kv_mem = projection(mem.reshape(b * MEM_LEN, d), xa_wkv[0].astype(BF16), tm=MEM_LEN, tn=512,
                        out_dtype=BF16, gain=g_mem[0], name="xattn_kv")
    h2 = memory_xattn(h1, g_xattn[0], xa_wq[0].astype(BF16), kv_mem, xa_wo[0].astype(BF16), seq=s,
                      tm=T["tm_x"])

    out = mlp_final(h2, g_mlp[0], w_ff1[0].astype(BF16), w_ff2[0].astype(BF16), g_final,
                    tm=T["tm_mlp"], tf=T["tf"])
    return out.reshape(b, s, d)
```

```python
import functools

import numpy as np
import jax
import jax.numpy as jnp
from jax import lax
from jax.experimental import pallas as pl
from jax.experimental.pallas import tpu as pltpu

F32 = jnp.float32
BF16 = jnp.bfloat16

D_MODEL = 2048
MEM_LEN = 256
ROPE_THETA = 500000.0
EPS = 1e-6
NEG = -1e30

NSA_HEADS = 16
NSA_KV_HEADS = 4
NSA_HPG = NSA_HEADS // NSA_KV_HEADS
NSA_DK = 96
NSA_DV = 64
NSA_ROT = NSA_DK // 4
CMP_BLOCK = 32
CMP_STRIDE = 16
SEL_BLOCK = 64
SEL_TOPK = 16
N_FORCED = 3
WINDOW = 512

MLA_HEADS = 16
MLA_NOPE = 64
MLA_ROPE = 32
MLA_DV = 64
MLA_Q_RANK = 512
MLA_KV_RANK = 256

XA_HEADS = 4
XA_DIM = 128
D_FF = 4 * D_MODEL

SPLITS = (NSA_HEADS * NSA_DK,
          NSA_KV_HEADS * NSA_DK, NSA_KV_HEADS * NSA_DV,
          NSA_KV_HEADS * NSA_DK, NSA_KV_HEADS * NSA_DV,
          NSA_KV_HEADS * NSA_DK, NSA_KV_HEADS * NSA_DV,
          NSA_HEADS * 3,
          MLA_Q_RANK, MLA_KV_RANK, MLA_ROPE,
          D_MODEL, D_MODEL)

LANES = 128
HALF = LANES // 2
VMEM_LIMIT = 56 * 1024 * 1024

QK_Q0 = 0
QK_KC0 = NSA_HEADS
QK_KS0 = NSA_HEADS + NSA_KV_HEADS
QK_KW0 = NSA_HEADS + 2 * NSA_KV_HEADS
QK_HEADS = NSA_HEADS + 3 * NSA_KV_HEADS
FLASH_HEADS = NSA_HPG


def _cparams(*sem):
    return pltpu.CompilerParams(dimension_semantics=sem, vmem_limit_bytes=VMEM_LIMIT)


def _tiles(seq):
    return dict(
        tm=min(1024, seq),
        tm_big=min(2048, seq),
        tm_x=min(512, seq),
        tm_mlp=min(512, seq),
        tq=min(1024, seq),
        tq_win=min(512, seq),
        tk=min(512, seq),
        rc=32,
        tq_cmp=min(256, seq),
        tf=512,
    )


def _rms(x, g):
    return x * lax.rsqrt(jnp.mean(x * x, axis=-1, keepdims=True) + EPS) * g


def _rmsnorm_kernel(x_ref, g_ref, o_ref):
    o_ref[...] = _rms(x_ref[...], g_ref[...]).astype(o_ref.dtype)


def rmsnorm_rows(x, g, tm):
    n, d = x.shape
    return pl.pallas_call(
        _rmsnorm_kernel,
        out_shape=jax.ShapeDtypeStruct((n, d), BF16),
        grid=(n // tm,),
        in_specs=[pl.BlockSpec((tm, d), lambda i: (i, 0)),
                  pl.BlockSpec((1, d), lambda i: (0, 0))],
        out_specs=pl.BlockSpec((tm, d), lambda i: (i, 0)),
        compiler_params=_cparams("parallel"),
        name="rmsnorm_rows",
    )(x, g.reshape(1, d))


def _proj_kernel(*refs, has_gain, has_rope, roll_shift, out_scale, has_add, has_res, act, head_major):
    it = iter(refs)
    a_ref = next(it)
    g_ref = next(it) if has_gain else None
    w_ref = next(it)
    cos_ref = next(it) if has_rope else None
    sin_ref = next(it) if has_rope else None
    add_ref = next(it) if has_add else None
    res_ref = next(it) if has_res else None
    o_ref = next(it)

    a = a_ref[...]
    if has_gain:
        a = _rms(a, g_ref[...]).astype(BF16)
    y = jnp.dot(a, w_ref[...], preferred_element_type=F32)
    if out_scale is not None:
        factor, n_tiles = out_scale
        y = y * jnp.where(pl.program_id(1) < n_tiles, factor, 1.0)
    if has_rope or has_add or head_major:
        for h in range(y.shape[1] // LANES):
            yh = y[:, h * LANES:(h + 1) * LANES]
            if has_rope:
                yh = yh * cos_ref[...] + pltpu.roll(yh, roll_shift, 1) * sin_ref[...]
            if has_add:
                yh = yh + add_ref[...].astype(F32)
            if head_major:
                o_ref[h] = yh.astype(o_ref.dtype)
            else:
                o_ref[:, h * LANES:(h + 1) * LANES] = yh.astype(o_ref.dtype)
    else:
        if act == "sigmoid":
            y = jax.nn.sigmoid(y)
        if has_res:
            y = res_ref[...] + y
        o_ref[...] = y.astype(o_ref.dtype)


def projection(a, w, *, tm, tn, out_dtype, a_cols=None, gain=None, rope=None, out_scale=None, add=None,
               res=None, act=None, head_major=None, name="projection"):
    n = a.shape[0]
    k, nc = w.shape
    a_w, a_blk = a_cols if a_cols is not None else (a.shape[1], 0)
    assert a_w == k and n % tm == 0 and nc % tn == 0
    grid = (n // tm, nc // tn)
    in_specs = [pl.BlockSpec((tm, k), lambda i, j: (i, a_blk))]
    args = [a]
    if gain is not None:
        in_specs.append(pl.BlockSpec((1, k), lambda i, j: (0, 0)))
        args.append(gain.reshape(1, k))
    in_specs.append(pl.BlockSpec((k, tn), lambda i, j: (0, j)))
    args.append(w)
    roll_shift = 0
    if rope is not None:
        cos, sin, roll_shift = rope
        s_tiles = cos.shape[0] // tm
        for t in (cos, sin):
            in_specs.append(pl.BlockSpec((tm, LANES), lambda i, j: (i % s_tiles, 0)))
            args.append(t)
    if add is not None:
        in_specs.append(pl.BlockSpec((tm, LANES), lambda i, j: (i, 0)))
        args.append(add)
    if res is not None:
        in_specs.append(pl.BlockSpec((tm, tn), lambda i, j: (i, j)))
        args.append(res)
    if head_major is not None:
        b, s = head_major
        s_t = s // tm
        hpt = tn // LANES
        out_shape = jax.ShapeDtypeStruct((b, nc // LANES, s, LANES), out_dtype)
        out_spec = pl.BlockSpec((None, hpt, tm, LANES), lambda i, j: (i // s_t, j, i % s_t, 0))
    else:
        out_shape = jax.ShapeDtypeStruct((n, nc), out_dtype)
        out_spec = pl.BlockSpec((tm, tn), lambda i, j: (i, j))
    kern = functools.partial(
        _proj_kernel, has_gain=gain is not None, has_rope=rope is not None, roll_shift=roll_shift,
        out_scale=out_scale, has_add=add is not None, has_res=res is not None, act=act,
        head_major=head_major is not None)
    return pl.pallas_call(
        kern, out_shape=out_shape, grid=grid, in_specs=in_specs, out_specs=out_spec,
        compiler_params=_cparams("parallel", "arbitrary"), name=name,
    )(*args)


def _compress_kernel(r_ref, pa_ref, pb_ref, w1a_ref, w1b_ref, w2_ref, o_ref):
    r = r_ref[...].astype(F32)
    a = jnp.dot((r + pa_ref[...]).astype(BF16), w1a_ref[...], preferred_element_type=F32)
    b = jnp.dot((r + pb_ref[...]).astype(BF16), w1b_ref[...], preferred_element_type=F32)
    nr = a.shape[0]
    hid = a + pltpu.roll(b, nr - 1, 0)
    hid = jax.nn.gelu(hid)
    o_ref[...] = jnp.dot(hid.astype(BF16), w2_ref[...], preferred_element_type=F32).astype(o_ref.dtype)


def compress(r, head0, pos_a, pos_b, w1a, w1b, w2):
    b, _, nr, kk = r.shape
    g = NSA_KV_HEADS
    full = lambda shape: pl.BlockSpec(shape, lambda bi, gi: (0,) * len(shape))
    return pl.pallas_call(
        _compress_kernel,
        out_shape=jax.ShapeDtypeStruct((b, g, nr, LANES), BF16),
        grid=(b, g),
        in_specs=[pl.BlockSpec((None, None, nr, kk), lambda bi, gi: (bi, head0 + gi, 0, 0)),
                  full((1, kk)), full((1, kk)), full((kk, LANES)), full((kk, LANES)),
                  full((LANES, LANES))],
        out_specs=pl.BlockSpec((None, None, nr, LANES), lambda bi, gi: (bi, gi, 0, 0)),
        compiler_params=_cparams("parallel", "parallel"),
        name="nsa_compress",
    )(r, pos_a, pos_b, w1a, w1b, w2)


def _dot_nt(a, b):
    return lax.dot_general(a, b, (((1,), (1,)), ((), ())), preferred_element_type=F32)


def _split_dot(x, w):
    hi = x.astype(BF16)
    lo = (x - hi.astype(F32)).astype(BF16)
    return (jnp.dot(hi, w, preferred_element_type=F32) + jnp.dot(lo, w, preferred_element_type=F32))


def _nsa_cmp_kernel(q_ref, kcc_ref, vcc_ref, msel_ref, glog_ref, e_ref, oc_ref, sel_ref, s_sc, p_sc, imp_sc,
                    *, tq, rc, topk):
    s0 = pl.program_id(2) * tq
    nr = kcc_ref.shape[0]

    def attend(ncol):
        kcc = kcc_ref[0:ncol, :]
        for h in range(NSA_HPG):
            s_sc[h, :, 0:ncol] = _dot_nt(q_ref[h], kcc)
        lane_v = lax.broadcasted_iota(jnp.int32, (ncol, LANES), 1)
        vcc = vcc_ref[0:ncol, :]
        vm = jnp.concatenate([jnp.where(lane_v == HALF, jnp.ones_like(vcc), vcc), msel_ref[0:ncol, :]],
                             axis=1)
        dmat = (lax.broadcasted_iota(jnp.int32, (rc, ncol), 0)
                - lax.broadcasted_iota(jnp.int32, (rc, ncol), 1) * CMP_STRIDE)
        visible = lambda c: dmat + (s0 + c * rc - (CMP_BLOCK - 1)) >= 0
        outs = []
        imp = jnp.zeros((tq, LANES), F32)
        for h in range(NSA_HPG):
            row_max = []
            for c in range(tq // rc):
                s = jnp.where(visible(c), s_sc[h, pl.ds(c * rc, rc), 0:ncol], NEG)
                row_max.append(jnp.max(s, axis=-1, keepdims=True))
            for c in range(tq // rc):
                rows = pl.ds(c * rc, rc)
                p = jnp.exp2(s_sc[h, rows, 0:ncol] - row_max[c])
                p_sc[h, rows, 0:ncol] = jnp.where(visible(c), p, 0.0).astype(BF16)
            a = jnp.dot(p_sc[h, :, 0:ncol], vm, preferred_element_type=F32)
            denom = jnp.maximum(a[:, HALF:HALF + 1], 1e-30)
            outs.append(a[:, :LANES] / denom)
            imp = imp + a[:, LANES:] / denom
        imp_sc[...] = imp
        glog = glog_ref[...]
        lane = lax.broadcasted_iota(jnp.int32, (tq, LANES), 1)
        for pr in range(NSA_HPG // 2):
            gate = jax.nn.sigmoid(_split_dot(glog, e_ref[pr]))
            pair = jnp.where(lane < HALF, outs[2 * pr], pltpu.roll(outs[2 * pr + 1], HALF, 1))
            oc_ref[:, pr * LANES:(pr + 1) * LANES] = (gate * pair).astype(oc_ref.dtype)

    ntiles = (s0 + (tq - CMP_BLOCK)) // (CMP_STRIDE * LANES) + 1
    for k in range(1, nr // LANES + 1):
        @pl.when(ntiles == k)
        def _width(k=k):
            attend(k * LANES)

    blk = lax.broadcasted_iota(jnp.int32, (tq, LANES), 1)
    cur = (s0 + lax.broadcasted_iota(jnp.int32, (tq, LANES), 0)) // SEL_BLOCK
    valid = blk <= cur
    forced = (blk == 0) | (blk == cur) | (blk == cur - 1)
    score = jnp.where(valid, jnp.where(forced, -jnp.inf, imp_sc[...]), -1.0)
    sc = score.T
    rowid = lax.broadcasted_iota(jnp.int32, (LANES, tq), 0).astype(F32)

    def pick_one(_, sc):
        cm = jnp.max(sc, axis=0, keepdims=True)
        first = jnp.min(jnp.where(sc == cm, rowid, float(LANES)), axis=0, keepdims=True)
        return jnp.where(rowid == first, -jnp.inf, sc)

    picked = lax.fori_loop(0, topk - N_FORCED, pick_one, sc) == -jnp.inf
    sel_ref[...] = jnp.where(picked, 0.0, NEG).T.astype(sel_ref.dtype)


def nsa_compressed(qk, kcc, vcc, msel, misc, glog_blk, expand, *, batch, seq, tq, rc):
    g = NSA_KV_HEADS
    nq = seq // tq
    nr = kcc.shape[2]
    topk = min(SEL_TOPK, seq // SEL_BLOCK)
    assert topk > N_FORCED and nr % LANES == 0
    kern = functools.partial(_nsa_cmp_kernel, tq=tq, rc=rc, topk=topk)
    return pl.pallas_call(
        kern,
        out_shape=(jax.ShapeDtypeStruct((batch * seq, NSA_HEADS * NSA_DV), BF16),
                   jax.ShapeDtypeStruct((batch, g, seq, LANES), BF16)),
        grid=(batch, g, nq),
        in_specs=[
            pl.BlockSpec((None, NSA_HPG, tq, LANES), lambda b, gi, qi: (b, gi, qi, 0)),
            pl.BlockSpec((None, None, nr, LANES), lambda b, gi, qi: (b, gi, 0, 0)),
            pl.BlockSpec((None, None, nr, LANES), lambda b, gi, qi: (b, gi, 0, 0)),
            pl.BlockSpec((nr, LANES), lambda b, gi, qi: (0, 0)),
            pl.BlockSpec((tq, LANES), lambda b, gi, qi: (b * nq + qi, glog_blk)),
            pl.BlockSpec((None, 2, LANES, LANES), lambda b, gi, qi: (0, gi, 0, 0)),
        ],
        out_specs=(pl.BlockSpec((tq, 2 * LANES), lambda b, gi, qi: (b * nq + qi, gi)),
                   pl.BlockSpec((None, None, tq, LANES), lambda b, gi, qi: (b, gi, qi, 0))),
        scratch_shapes=[pltpu.VMEM((NSA_HPG, tq, nr), F32), pltpu.VMEM((NSA_HPG, tq, nr), BF16),
                        pltpu.VMEM((tq, LANES), F32)],
        compiler_params=_cparams("parallel", "parallel", "parallel"),
        name="nsa_compressed_select",
    )(qk, kcc, vcc, msel, misc, expand)


def _flash_steps(mode, nq, tq, tk):
    r = tq // tk
    qi_l, kt_l, first_l, last_l = [], [], [], []
    for qi in range(nq):
        hi = qi * r + r - 1
        lo = max(0, qi * r - (-(-(WINDOW - 1) // tk))) if mode == "window" else 0
        for kt in range(lo, hi + 1):
            qi_l.append(qi)
            kt_l.append(kt)
            first_l.append(int(kt == lo))
            last_l.append(int(kt == hi))
    return tuple(jnp.asarray(np.asarray(a, np.int32)) for a in (qi_l, kt_l, first_l, last_l))


def _flash_kernel(*refs, mode, tq, tk, rc, kv_shared, gated):
    it = iter(refs)
    qi_ref, kt_ref, first_ref, last_ref = next(it), next(it), next(it), next(it)
    q_ref, k_ref, v_ref = next(it), next(it), next(it)
    selb_ref = next(it) if mode == "select" else None
    oh_ref = next(it) if mode == "select" else None
    glog_ref = next(it) if gated else None
    e_ref = next(it) if gated else None
    o_ref = next(it)
    m_sc, acc_sc, s_sc, p_sc = next(it), next(it), next(it), next(it)
    qa_sc = next(it) if mode == "select" else None

    st = pl.program_id(2)
    s0 = qi_ref[st] * tq
    k0 = kt_ref[st] * tk

    @pl.when(first_ref[st] == 1)
    def _init():
        m_sc[...] = jnp.full(m_sc.shape, NEG, F32)
        acc_sc[...] = jnp.zeros(acc_sc.shape, F32)
        if mode == "select":
            for hh in range(FLASH_HEADS):
                qa_sc[hh] = jnp.concatenate([q_ref[hh], selb_ref[...]], axis=1)

    def chunk_cols(off, r0):
        if off is None:
            return 0, tk, True
        r1 = r0 + rc
        hi = min(tk, r1 - off)
        lo = max(0, r0 - off - WINDOW + 1) if mode == "window" else 0
        lo, hi = lo // LANES * LANES, -(-hi // LANES) * LANES
        clear = tk - 1 <= r0 - off and (mode != "window" or r1 - 1 - off < WINDOW)
        return lo, hi, clear

    def tile(off):
        for hh in range(FLASH_HEADS):
            k = k_ref[0 if kv_shared else hh]
            if mode == "select":
                s_sc[hh] = _dot_nt(qa_sc[hh], jnp.concatenate([k, oh_ref[...]], axis=1))
            else:
                s_sc[hh] = _dot_nt(q_ref[hh], k)
        lane_v = lax.broadcasted_iota(jnp.int32, (tk, LANES), 1)
        if off is not None:
            dmat = (lax.broadcasted_iota(jnp.int32, (rc, tk), 0)
                    - lax.broadcasted_iota(jnp.int32, (rc, tk), 1))
        for hh in range(FLASH_HEADS):
            v = v_ref[0 if kv_shared else hh]
            v_aug = jnp.where(lane_v == HALF, jnp.ones_like(v), v)

            for c in range(tq // rc):
                r0 = c * rc
                rows = pl.ds(r0, rc)
                lo, hi, clear = chunk_cols(off, r0)
                if lo > 0:
                    p_sc[hh, rows, 0:lo] = jnp.zeros((rc, lo), BF16)
                if hi < tk:
                    p_sc[hh, rows, max(hi, 0):tk] = jnp.zeros((rc, tk - max(hi, 0)), BF16)
                if hi <= lo:
                    continue
                s = s_sc[hh, rows, lo:hi]
                if not clear:
                    d = dmat[:, lo:hi] + (r0 - off)
                    msk = (d >= 0) & (d < WINDOW) if mode == "window" else d >= 0
                    s = jnp.where(msk, s, NEG)
                m_old = m_sc[hh, rows, :]
                m_new = jnp.maximum(m_old, jnp.max(s, axis=-1, keepdims=True))
                m_sc[hh, rows, :] = m_new
                acc_sc[hh, rows, :] = acc_sc[hh, rows, :] * jnp.exp2(m_old - m_new)
                p = jnp.exp2(s_sc[hh, rows, lo:hi] - jnp.concatenate([m_new] * ((hi - lo) // LANES), axis=1))
                if not clear:
                    p = jnp.where(msk, p, 0.0)
                p_sc[hh, rows, lo:hi] = p.astype(BF16)
            acc_sc[hh] += jnp.dot(p_sc[hh], v_aug, preferred_element_type=F32)

    if mode == "window":
        offsets = [(j - (-(-(WINDOW - 1) // tk))) * tk for j in range(-(-(WINDOW - 1) // tk) + tq // tk)]
    else:
        offsets = [j * tk for j in range(tq // tk)]

        @pl.when(k0 + tk - 1 <= s0)
        def _interior():
            tile(None)

    for off in offsets:
        @pl.when(k0 - s0 == off)
        def _partial(off=off):
            tile(off)

    @pl.when(last_ref[st] == 1)
    def _finish():
        lane = lax.broadcasted_iota(jnp.int32, (tq, LANES), 1)
        if gated:
            glog = glog_ref[...]
        for pr in range(FLASH_HEADS // 2):
            outs = []
            for hh in (2 * pr, 2 * pr + 1):
                a = acc_sc[hh]
                outs.append(a / jnp.maximum(a[:, HALF:HALF + 1], 1e-30))
            out = jnp.where(lane < HALF, outs[0], pltpu.roll(outs[1], HALF, 1))
            if gated:
                out = jax.nn.sigmoid(_split_dot(glog, e_ref[pr])) * out
            o_ref[:, pr * LANES:(pr + 1) * LANES] = out.astype(o_ref.dtype)


def flash_heads(q, k, v, *, mode, batch, seq, tq, tk, rc, q_head0, k_head0, v_head0,
                kv_shared, out_dtype, selb=None, onehot=None, gate=None, name="flash"):
    nh = FLASH_HEADS
    ngroups = NSA_HEADS // nh
    nq = seq // tq
    assert tq % tk == 0 and tq % rc == 0
    tabs = _flash_steps(mode, nq, tq, tk)
    nsteps = tabs[0].shape[0]
    if kv_shared:
        kv_spec = lambda h0: pl.BlockSpec(
            (None, 1, tk, LANES), lambda b, g, st, qi, kt, fi, la: (b, h0 + g, kt[st], 0))
    else:
        kv_spec = lambda h0: pl.BlockSpec(
            (None, nh, tk, LANES), lambda b, g, st, qi, kt, fi, la: (b, h0 // nh + g, kt[st], 0))
    in_specs = [pl.BlockSpec((None, nh, tq, LANES),
                             lambda b, g, st, qi, kt, fi, la: (b, q_head0 // nh + g, qi[st], 0)),
                kv_spec(k_head0), kv_spec(v_head0)]
    args = [q, k, v]
    scratch = [pltpu.VMEM((nh, tq, LANES), F32), pltpu.VMEM((nh, tq, LANES), F32),
               pltpu.VMEM((nh, tq, tk), F32), pltpu.VMEM((nh, tq, tk), BF16)]
    if mode == "select":
        in_specs += [pl.BlockSpec((None, None, tq, LANES),
                                  lambda b, g, st, qi, kt, fi, la: (b, g, qi[st], 0)),
                     pl.BlockSpec((tk, LANES), lambda b, g, st, qi, kt, fi, la: (kt[st], 0))]
        args += [selb, onehot]
        scratch.append(pltpu.VMEM((nh, tq, 2 * LANES), BF16))
    if gate is not None:
        misc, glog_blk, expand, branch = gate
        in_specs += [pl.BlockSpec((tq, LANES),
                                  lambda b, g, st, qi, kt, fi, la: (b * nq + qi[st], glog_blk)),
                     pl.BlockSpec((None, nh // 2, LANES, LANES),
                                  lambda b, g, st, qi, kt, fi, la: (branch, g, 0, 0))]
        args += [misc, expand]
    kern = functools.partial(_flash_kernel, mode=mode, tq=tq, tk=tk, rc=rc,
                             kv_shared=kv_shared, gated=gate is not None)
    ow = nh * HALF
    grid_spec = pltpu.PrefetchScalarGridSpec(
        num_scalar_prefetch=4, grid=(batch, ngroups, nsteps), in_specs=in_specs,
        out_specs=pl.BlockSpec((tq, ow), lambda b, g, st, qi, kt, fi, la: (b * nq + qi[st], g)),
        scratch_shapes=scratch)
    return pl.pallas_call(
        kern,
        out_shape=jax.ShapeDtypeStruct((batch * seq, ngroups * ow), out_dtype),
        grid_spec=grid_spec,
        compiler_params=_cparams("parallel", "parallel", "arbitrary"),
        name=name,
    )(*tabs, *args)


def _mix_kernel(oc_ref, os_ref, ow_ref, ob_ref, ga_ref, gb_ref, wa_ref, wb_ref, o_ref):
    oa = (oc_ref[...].astype(F32) + os_ref[...].astype(F32) + ow_ref[...].astype(F32)).astype(BF16)
    ya = jnp.dot(oa, wa_ref[...], preferred_element_type=F32)
    yb = jnp.dot(ob_ref[...], wb_ref[...], preferred_element_type=F32)
    o_ref[...] = (ga_ref[...].astype(F32) * ya + gb_ref[...].astype(F32) * yb).astype(o_ref.dtype)


def gated_mix(oc, osel, ow, ob, gates, wa, wb, *, tm, tn):
    n, ka = oc.shape
    d = wa.shape[1]
    nb = d // tn
    row = lambda w: pl.BlockSpec((tm, w), lambda i, j: (i, 0))
    return pl.pallas_call(
        _mix_kernel,
        out_shape=jax.ShapeDtypeStruct((n, d), BF16),
        grid=(n // tm, nb),
        in_specs=[row(ka), row(ka), row(ka), row(ob.shape[1]),
                  pl.BlockSpec((tm, tn), lambda i, j: (i, j)),
                  pl.BlockSpec((tm, tn), lambda i, j: (i, nb + j)),
                  pl.BlockSpec((ka, tn), lambda i, j: (0, j)),
                  pl.BlockSpec((wb.shape[0], tn), lambda i, j: (0, j))],
        out_specs=pl.BlockSpec((tm, tn), lambda i, j: (i, j)),
        compiler_params=_cparams("parallel", "arbitrary"),
        name="gated_mix",
    )(oc, osel, ow, ob, gates, gates, wa, wb)


def _xattn_kernel(h_ref, g_ref, wq_ref, kv_ref, wo_ref, o_ref):
    h = h_ref[...]
    hn = _rms(h, g_ref[...]).astype(BF16)
    q = jnp.dot(hn, wq_ref[...], preferred_element_type=F32).astype(BF16)
    kv = kv_ref[...]
    outs = []
    for hd in range(XA_HEADS):
        qh = q[:, hd * XA_DIM:(hd + 1) * XA_DIM]
        kh = kv[:, hd * XA_DIM:(hd + 1) * XA_DIM]
        vh = kv[:, (XA_HEADS + hd) * XA_DIM:(XA_HEADS + hd + 1) * XA_DIM]
        s = _dot_nt(qh, kh) * (XA_DIM ** -0.5)
        e = jnp.exp(s - jnp.max(s, axis=-1, keepdims=True))
        p = e / jnp.sum(e, axis=-1, keepdims=True)
        outs.append(jnp.dot(p.astype(BF16), vh, preferred_element_type=F32))
    o = jnp.concatenate(outs, axis=1).astype(BF16)
    o_ref[...] = h + jnp.dot(o, wo_ref[...], preferred_element_type=F32)


def memory_xattn(h, g, wq, kv, wo, *, seq, tm):
    n, d = h.shape
    s_t = seq // tm
    dq = wq.shape[1]
    return pl.pallas_call(
        _xattn_kernel,
        out_shape=jax.ShapeDtypeStruct((n, d), F32),
        grid=(n // tm,),
        in_specs=[pl.BlockSpec((tm, d), lambda i: (i, 0)),
                  pl.BlockSpec((1, d), lambda i: (0, 0)),
                  pl.BlockSpec((d, dq), lambda i: (0, 0)),
                  pl.BlockSpec((MEM_LEN, 2 * dq), lambda i: (i // s_t, 0)),
                  pl.BlockSpec((dq, d), lambda i: (0, 0))],
        out_specs=pl.BlockSpec((tm, d), lambda i: (i, 0)),
        compiler_params=_cparams("parallel"),
        name="memory_xattn",
    )(h, g.reshape(1, d), wq, kv, wo)


def _mlp_kernel(h_ref, g_ref, w1_ref, w2_ref, gf_ref, o_ref, hn_sc, acc_sc, *, nf):
    f = pl.program_id(1)

    @pl.when(f == 0)
    def _init():
        hn_sc[...] = _rms(h_ref[...], g_ref[...]).astype(BF16)
        acc_sc[...] = jnp.zeros(acc_sc.shape, F32)

    u = jnp.dot(hn_sc[...], w1_ref[...], preferred_element_type=F32)
    u = jnp.square(jnp.maximum(u, 0.0))
    acc_sc[...] += jnp.dot(u.astype(BF16), w2_ref[...], preferred_element_type=F32)

    @pl.when(f == nf - 1)
    def _finish():
        o_ref[...] = _rms(h_ref[...] + acc_sc[...], gf_ref[...])


def mlp_final(h, g, w1, w2, gf, *, tm, tf):
    n, d = h.shape
    dff = w1.shape[1]
    nf = dff // tf
    return pl.pallas_call(
        functools.partial(_mlp_kernel, nf=nf),
        out_shape=jax.ShapeDtypeStruct((n, d), F32),
        grid=(n // tm, nf),
        in_specs=[pl.BlockSpec((tm, d), lambda i, f: (i, 0)),
                  pl.BlockSpec((1, d), lambda i, f: (0, 0)),
                  pl.BlockSpec((d, tf), lambda i, f: (0, f)),
                  pl.BlockSpec((tf, d), lambda i, f: (f, 0)),
                  pl.BlockSpec((1, d), lambda i, f: (0, 0))],
        out_specs=pl.BlockSpec((tm, d), lambda i, f: (i, 0)),
        scratch_shapes=[pltpu.VMEM((tm, d), BF16), pltpu.VMEM((tm, d), F32)],
        compiler_params=_cparams("parallel", "arbitrary"),
        name="mlp_final",
    )(h, g.reshape(1, d), w1, w2, gf.reshape(1, d))


def _rot_partner(w, half):
    return jnp.concatenate([-w[..., half:], w[..., :half]], axis=-1)


def _pad_heads_rope(w, heads, hd, rot0, rot_dim):
    k = w.shape[0]
    w = w.reshape(k, heads, hd)
    partner = _rot_partner(w[:, :, rot0:rot0 + rot_dim], rot_dim // 2)
    pad = jnp.zeros((k, heads, LANES - hd - rot_dim), w.dtype)
    return jnp.concatenate([w, partner, pad], axis=-1).reshape(k, heads * LANES)


def _rope_lane_tables(seq, rot0, rot_dim, hd):
    inv = 1.0 / (ROPE_THETA ** (jnp.arange(0, rot_dim, 2, dtype=F32) / rot_dim))
    ang = jnp.arange(seq, dtype=F32)[:, None] * inv[None, :]
    cos = jnp.concatenate([jnp.cos(ang), jnp.cos(ang)], axis=1)
    sin = jnp.concatenate([jnp.sin(ang), jnp.sin(ang)], axis=1)
    cos_t = jnp.concatenate([jnp.ones((seq, rot0), F32), cos,
                             jnp.ones((seq, hd - rot0 - rot_dim), F32),
                             jnp.zeros((seq, LANES - hd), F32)], axis=1)
    sin_t = jnp.concatenate([jnp.zeros((seq, rot0), F32), sin,
                             jnp.zeros((seq, LANES - rot0 - rot_dim), F32)], axis=1)
    return cos_t, sin_t


def _pad_lanes(w, groups, width):
    k = w.shape[0]
    w = w.reshape(k, groups, width)
    return jnp.pad(w, ((0, 0), (0, 0), (0, LANES - width))).reshape(k, groups * LANES)


def _compress_params(pos, w1, w2, d):
    half = CMP_BLOCK // 2
    pos_p = jnp.pad(pos, ((0, 0), (0, LANES - d)))
    pos_a = pos_p[:half].reshape(1, half * LANES)
    pos_b = pos_p[half:].reshape(1, half * LANES)
    w1p = jnp.pad(w1.reshape(CMP_BLOCK, d, d), ((0, 0), (0, LANES - d), (0, LANES - d)))
    w1a = w1p[:half].reshape(half * LANES, LANES).astype(BF16)
    w1b = w1p[half:].reshape(half * LANES, LANES).astype(BF16)
    w2p = jnp.pad(w2, ((0, LANES - d), (0, LANES - d)))
    return pos_a, pos_b, w1a, w1b, w2p.astype(BF16)


def _cmp_to_sel(nr, nsb):
    cs = np.arange(nr) * CMP_STRIDE
    ce = cs + CMP_BLOCK
    ss = np.arange(LANES) * SEL_BLOCK
    se = ss + SEL_BLOCK
    ov = np.clip(np.minimum(ce[:, None], se[None, :]) - np.maximum(cs[:, None], ss[None, :]), 0, None)
    ov = ov.astype(np.float32) / np.float32(CMP_BLOCK)
    ov[:, nsb:] = 0.0
    ov[nr - 1:, :] = 0.0
    return jnp.asarray(ov, BF16)


def _gate_expand():
    e = np.zeros((3, NSA_HEADS // 2, LANES, LANES), np.float32)
    for br in range(3):
        for hp in range(NSA_HEADS // 2):
            for hh in range(2):
                e[br, hp, 3 * (2 * hp + hh) + br, hh * HALF:(hh + 1) * HALF] = 1.0
    return jnp.asarray(e, BF16)


def _key_block_onehot(seq):
    e = (np.arange(seq)[:, None] // SEL_BLOCK) == np.arange(LANES)[None, :]
    return jnp.asarray(e.astype(np.float32), BF16)


def kernel(x, mem, g_mix, w_in, cmp_pos_k, cmp_w1_k, cmp_w2_k, cmp_pos_v, cmp_w1_v, cmp_w2_v,
           mla_g_q, mla_w_uq, mla_g_kv, mla_w_uk, mla_w_uv, w_o_nsa, w_o_mla, w_out,
           g_xattn, g_mem, xa_wq, xa_wkv, xa_wo, g_mlp, w_ff1, w_ff2, g_final):
    b, s, d = x.shape
    assert d == D_MODEL and s % (CMP_STRIDE * 8) == 0 and s // SEL_BLOCK <= LANES
    assert g_mix.shape[0] == 1
    n = b * s
    T = _tiles(s)
    tm, tq, tk, rc = T["tm"], T["tq"], T["tk"], T["rc"]
    G = NSA_KV_HEADS
    bounds = [int(v) for v in np.cumsum(SPLITS)[:-1]]

    (w_qa, w_kc, w_vc, w_ks, w_vs, w_kw, w_vw, w_gn, w_cq, w_ckv, w_kr,
     w_ga, w_gb) = jnp.split(w_in[0], bounds, axis=1)
    nsa_rope = lambda w, heads: _pad_heads_rope(w, heads, NSA_DK, 0, NSA_ROT)
    w_rope = jnp.concatenate([nsa_rope(w_qa, NSA_HEADS), nsa_rope(w_kc, G), nsa_rope(w_ks, G),
                              nsa_rope(w_kw, G)], axis=1).astype(BF16)
    w_krp = jnp.concatenate([jnp.zeros((d, MLA_NOPE), F32), w_kr, _rot_partner(w_kr, MLA_ROPE // 2)],
                            axis=1).astype(BF16)
    w_vsw = jnp.concatenate([_pad_lanes(w_vs, G, NSA_DV), _pad_lanes(w_vw, G, NSA_DV)], axis=1).astype(BF16)
    w_vcp = _pad_lanes(w_vc, G, NSA_DV).astype(BF16)
    w_misc = jnp.concatenate([w_cq, w_ckv, jnp.pad(w_gn, ((0, 0), (0, LANES - w_gn.shape[1])))],
                             axis=1).astype(BF16)
    glog_blk = (MLA_Q_RANK + MLA_KV_RANK) // LANES
    w_gates = jnp.concatenate([w_ga, w_gb], axis=1).astype(BF16)
    cos_a, sin_a = _rope_lane_tables(s, 0, NSA_ROT, NSA_DK)
    cos_b, sin_b = _rope_lane_tables(s, MLA_NOPE, MLA_ROPE, MLA_NOPE + MLA_ROPE)
    shift_a = LANES - NSA_DK
    shift_b = LANES - (NSA_DK - MLA_NOPE)
    w_uq = _pad_heads_rope(mla_w_uq[0], MLA_HEADS, MLA_NOPE + MLA_ROPE, MLA_NOPE, MLA_ROPE).astype(BF16)
    w_uk = _pad_lanes(mla_w_uk[0], MLA_HEADS, MLA_NOPE).astype(BF16)
    w_uv = _pad_lanes(mla_w_uv[0], MLA_HEADS, MLA_DV).astype(BF16)

    x2 = x.reshape(n, d)
    xn = rmsnorm_rows(x2, g_mix[0], tm)
    log2e = float(np.log2(np.e))
    tmb = T["tm_big"]
    qk = projection(xn, w_rope, tm=tmb, tn=512, out_dtype=BF16, rope=(cos_a, sin_a, shift_a),
                    out_scale=(NSA_DK ** -0.5 * log2e, NSA_HEADS * LANES // 512),
                    head_major=(b, s), name="proj_qk_rope")
    krp = projection(xn, w_krp, tm=tm, tn=LANES, out_dtype=BF16, rope=(cos_b, sin_b, shift_b),
                     name="proj_k_rope_mla")
    vsw = projection(xn, w_vsw, tm=tmb, tn=512, out_dtype=BF16, head_major=(b, s), name="proj_v")
    vcp = projection(xn, w_vcp, tm=tmb, tn=512, out_dtype=BF16, head_major=(b, s), name="proj_vc")
    misc = projection(xn, w_misc, tm=tm, tn=w_misc.shape[1], out_dtype=F32, name="proj_misc")
    gates = projection(xn, w_gates, tm=tmb, tn=512, out_dtype=BF16, act="sigmoid", name="proj_gates")

    nr = s // CMP_STRIDE
    pk = _compress_params(cmp_pos_k[0], cmp_w1_k[0], cmp_w2_k[0], NSA_DK)
    pv = _compress_params(cmp_pos_v[0], cmp_w1_v[0], cmp_w2_v[0], NSA_DV)
    kc_rows = qk[:, QK_KC0:QK_KC0 + G].reshape(b, G, nr, CMP_STRIDE * LANES)
    kcc = compress(kc_rows, 0, *pk)
    vcc = compress(vcp.reshape(b, G, nr, CMP_STRIDE * LANES), 0, *pv)
    expand = _gate_expand()
    o_c, selb = nsa_compressed(qk, kcc, vcc, _cmp_to_sel(nr, s // SEL_BLOCK), misc, glog_blk, expand,
                               batch=b, seq=s, tq=T["tq_cmp"], rc=rc)
    o_s = flash_heads(qk, qk, vsw, mode="select", batch=b, seq=s, tq=tq, tk=tk, rc=rc,
                      q_head0=QK_Q0, k_head0=QK_KS0, v_head0=0, kv_shared=True, out_dtype=BF16,
                      selb=selb, onehot=_key_block_onehot(s), gate=(misc, glog_blk, expand, 1),
                      name="nsa_selected")
    o_w = flash_heads(qk, qk, vsw, mode="window", batch=b, seq=s, tq=T["tq_win"], tk=tk, rc=rc,
                      q_head0=QK_Q0, k_head0=QK_KW0, v_head0=G, kv_shared=True, out_dtype=BF16,
                      gate=(misc, glog_blk, expand, 2), name="nsa_window")

    qm = projection(misc, w_uq, tm=tm, tn=512, out_dtype=BF16, a_cols=(MLA_Q_RANK, 0), gain=mla_g_q[0],
                    rope=(cos_b, sin_b, shift_b), head_major=(b, s), name="mla_q",
                    out_scale=((MLA_NOPE + MLA_ROPE) ** -0.5 * log2e, MLA_HEADS * LANES // 512))
    km = projection(misc, w_uk, tm=tm, tn=512, out_dtype=BF16,
                    a_cols=(MLA_KV_RANK, MLA_Q_RANK // MLA_KV_RANK), gain=mla_g_kv[0], add=krp,
                    head_major=(b, s), name="mla_k")
    vm = projection(misc, w_uv, tm=tm, tn=512, out_dtype=BF16,
                    a_cols=(MLA_KV_RANK, MLA_Q_RANK // MLA_KV_RANK), gain=mla_g_kv[0],
                    head_major=(b, s), name="mla_v")
    o_b = flash_heads(qm, km, vm, mode="causal", batch=b, seq=s, tq=tq, tk=tk, rc=rc,
                      q_head0=0, k_head0=0, v_head0=0,
                      kv_shared=False, out_dtype=BF16, name="mla_attention")

    mixed = gated_mix(o_c, o_s, o_w, o_b, gates, w_o_nsa[0].astype(BF16), w_o_mla[0].astype(BF16),
                      tm=tm, tn=512)
    h1 = projection(mixed, w_out[0].astype(BF16), tm=tmb, tn=512, out_dtype=F32, res=x2, name="proj_out")

    kv_mem = projection(mem.reshape(b * MEM_LEN, d), xa_wkv[0].astype(BF16), tm=MEM_LEN, tn=512,
                        out_dtype=BF16, gain=g_mem[0], name="xattn_kv")
    h2 = memory_xattn(h1, g_xattn[0], xa_wq[0].astype(BF16), kv_mem, xa_wo[0].astype(BF16), seq=s,
                      tm=T["tm_x"])

    out = mlp_final(h2, g_mlp[0], w_ff1[0].astype(BF16), w_ff2[0].astype(BF16), g_final,
                    tm=T["tm_mlp"], tf=T["tf"])
    return out.reshape(b, s, d)
```

```python
import functools

import numpy as np
import jax
import jax.numpy as jnp
from jax import lax
from jax.experimental import pallas as pl
from jax.experimental.pallas import tpu as pltpu

F32 = jnp.float32
BF16 = jnp.bfloat16

D_MODEL = 2048
MEM_LEN = 256
ROPE_THETA = 500000.0
EPS = 1e-6
NEG = -1e30

NSA_HEADS = 16
NSA_KV_HEADS = 4
NSA_HPG = NSA_HEADS // NSA_KV_HEADS
NSA_DK = 96
NSA_DV = 64
NSA_ROT = NSA_DK // 4
CMP_BLOCK = 32
CMP_STRIDE = 16
SEL_BLOCK = 64
SEL_TOPK = 16
N_FORCED = 3
WINDOW = 512

MLA_HEADS = 16
MLA_NOPE = 64
MLA_ROPE = 32
MLA_DV = 64
MLA_Q_RANK = 512
MLA_KV_RANK = 256

XA_HEADS = 4
XA_DIM = 128
D_FF = 4 * D_MODEL

SPLITS = (NSA_HEADS * NSA_DK,
          NSA_KV_HEADS * NSA_DK, NSA_KV_HEADS * NSA_DV,
          NSA_KV_HEADS * NSA_DK, NSA_KV_HEADS * NSA_DV,
          NSA_KV_HEADS * NSA_DK, NSA_KV_HEADS * NSA_DV,
          NSA_HEADS * 3,
          MLA_Q_RANK, MLA_KV_RANK, MLA_ROPE,
          D_MODEL, D_MODEL)

LANES = 128
HALF = LANES // 2
VMEM_LIMIT = 56 * 1024 * 1024

QK_Q0 = 0
QK_KC0 = NSA_HEADS
QK_KS0 = NSA_HEADS + NSA_KV_HEADS
QK_KW0 = NSA_HEADS + 2 * NSA_KV_HEADS
QK_HEADS = NSA_HEADS + 3 * NSA_KV_HEADS
FLASH_HEADS = NSA_HPG


def _cparams(*sem):
    return pltpu.CompilerParams(dimension_semantics=sem, vmem_limit_bytes=VMEM_LIMIT)


def _tiles(seq):
    return dict(
        tm=min(1024, seq),
        tm_big=min(2048, seq),
        tm_x=min(512, seq),
        tm_mlp=min(512, seq),
        tq=min(1024, seq),
        tq_win=min(512, seq),
        tk=min(512, seq),
        rc=32,
        tq_cmp=min(256, seq),
        tf=512,
    )


def _rms(x, g):
    return x * lax.rsqrt(jnp.mean(x * x, axis=-1, keepdims=True) + EPS) * g


def _rmsnorm_kernel(x_ref, g_ref, o_ref):
    o_ref[...] = _rms(x_ref[...], g_ref[...]).astype(o_ref.dtype)


def rmsnorm_rows(x, g, tm):
    n, d = x.shape
    return pl.pallas_call(
        _rmsnorm_kernel,
        out_shape=jax.ShapeDtypeStruct((n, d), BF16),
        grid=(n // tm,),
        in_specs=[pl.BlockSpec((tm, d), lambda i: (i, 0)),
                  pl.BlockSpec((1, d), lambda i: (0, 0))],
        out_specs=pl.BlockSpec((tm, d), lambda i: (i, 0)),
        compiler_params=_cparams("parallel"),
        name="rmsnorm_rows",
    )(x, g.reshape(1, d))


def _proj_kernel(*refs, has_gain, has_rope, roll_shift, out_scale, has_add, has_res, act, head_major):
    it = iter(refs)
    a_ref = next(it)
    g_ref = next(it) if has_gain else None
    w_ref = next(it)
    cos_ref = next(it) if has_rope else None
    sin_ref = next(it) if has_rope else None
    add_ref = next(it) if has_add else None
    res_ref = next(it) if has_res else None
    o_ref = next(it)

    a = a_ref[...]
    if has_gain:
        a = _rms(a, g_ref[...]).astype(BF16)
    y = jnp.dot(a, w_ref[...], preferred_element_type=F32)
    if out_scale is not None:
        factor, n_tiles = out_scale
        y = y * jnp.where(pl.program_id(1) < n_tiles, factor, 1.0)
    if has_rope or has_add or head_major:
        for h in range(y.shape[1] // LANES):
            yh = y[:, h * LANES:(h + 1) * LANES]
            if has_rope:
                yh = yh * cos_ref[...] + pltpu.roll(yh, roll_shift, 1) * sin_ref[...]
            if has_add:
                yh = yh + add_ref[...].astype(F32)
            if head_major:
                o_ref[h] = yh.astype(o_ref.dtype)
            else:
                o_ref[:, h * LANES:(h + 1) * LANES] = yh.astype(o_ref.dtype)
    else:
        if act == "sigmoid":
            y = jax.nn.sigmoid(y)
        if has_res:
            y = res_ref[...] + y
        o_ref[...] = y.astype(o_ref.dtype)


def projection(a, w, *, tm, tn, out_dtype, a_cols=None, gain=None, rope=None, out_scale=None, add=None,
               res=None, act=None, head_major=None, name="projection"):
    n = a.shape[0]
    k, nc = w.shape
    a_w, a_blk = a_cols if a_cols is not None else (a.shape[1], 0)
    assert a_w == k and n % tm == 0 and nc % tn == 0
    grid = (n // tm, nc // tn)
    in_specs = [pl.BlockSpec((tm, k), lambda i, j: (i, a_blk))]
    args = [a]
    if gain is not None:
        in_specs.append(pl.BlockSpec((1, k), lambda i, j: (0, 0)))
        args.append(gain.reshape(1, k))
    in_specs.append(pl.BlockSpec((k, tn), lambda i, j: (0, j)))
    args.append(w)
    roll_shift = 0
    if rope is not None:
        cos, sin, roll_shift = rope
        s_tiles = cos.shape[0] // tm
        for t in (cos, sin):
            in_specs.append(pl.BlockSpec((tm, LANES), lambda i, j: (i % s_tiles, 0)))
            args.append(t)
    if add is not None:
        in_specs.append(pl.BlockSpec((tm, LANES), lambda i, j: (i, 0)))
        args.append(add)
    if res is not None:
        in_specs.append(pl.BlockSpec((tm, tn), lambda i, j: (i, j)))
        args.append(res)
    if head_major is not None:
        b, s = head_major
        s_t = s // tm
        hpt = tn // LANES
        out_shape = jax.ShapeDtypeStruct((b, nc // LANES, s, LANES), out_dtype)
        out_spec = pl.BlockSpec((None, hpt, tm, LANES), lambda i, j: (i // s_t, j, i % s_t, 0))
    else:
        out_shape = jax.ShapeDtypeStruct((n, nc), out_dtype)
        out_spec = pl.BlockSpec((tm, tn), lambda i, j: (i, j))
    kern = functools.partial(
        _proj_kernel, has_gain=gain is not None, has_rope=rope is not None, roll_shift=roll_shift,
        out_scale=out_scale, has_add=add is not None, has_res=res is not None, act=act,
        head_major=head_major is not None)
    return pl.pallas_call(
        kern, out_shape=out_shape, grid=grid, in_specs=in_specs, out_specs=out_spec,
        compiler_params=_cparams("parallel", "arbitrary"), name=name,
    )(*args)


def _compress_kernel(r_ref, pa_ref, pb_ref, w1a_ref, w1b_ref, w2_ref, o_ref):
    r = r_ref[...].astype(F32)
    a = jnp.dot((r + pa_ref[...]).astype(BF16), w1a_ref[...], preferred_element_type=F32)
    b = jnp.dot((r + pb_ref[...]).astype(BF16), w1b_ref[...], preferred_element_type=F32)
    nr = a.shape[0]
    hid = a + pltpu.roll(b, nr - 1, 0)
    hid = jax.nn.gelu(hid)
    o_ref[...] = jnp.dot(hid.astype(BF16), w2_ref[...], preferred_element_type=F32).astype(o_ref.dtype)


def compress(r, head0, pos_a, pos_b, w1a, w1b, w2):
    b, _, nr, kk = r.shape
    g = NSA_KV_HEADS
    full = lambda shape: pl.BlockSpec(shape, lambda bi, gi: (0,) * len(shape))
    return pl.pallas_call(
        _compress_kernel,
        out_shape=jax.ShapeDtypeStruct((b, g, nr, LANES), BF16),
        grid=(b, g),
        in_specs=[pl.BlockSpec((None, None, nr, kk), lambda bi, gi: (bi, head0 + gi, 0, 0)),
                  full((1, kk)), full((1, kk)), full((kk, LANES)), full((kk, LANES)),
                  full((LANES, LANES))],
        out_specs=pl.BlockSpec((None, None, nr, LANES), lambda bi, gi: (bi, gi, 0, 0)),
        compiler_params=_cparams("parallel", "parallel"),
        name="nsa_compress",
    )(r, pos_a, pos_b, w1a, w1b, w2)


def _dot_nt(a, b):
    return lax.dot_general(a, b, (((1,), (1,)), ((), ())), preferred_element_type=F32)


def _split_dot(x, w):
    hi = x.astype(BF16)
    lo = (x - hi.astype(F32)).astype(BF16)
    return (jnp.dot(hi, w, preferred_element_type=F32) + jnp.dot(lo, w, preferred_element_type=F32))


def _nsa_cmp_kernel(q_ref, kcc_ref, vcc_ref, msel_ref, glog_ref, e_ref, oc_ref, sel_ref, s_sc, p_sc, imp_sc,
                    *, tq, rc, topk):
    s0 = pl.program_id(2) * tq
    nr = kcc_ref.shape[0]

    def attend(ncol):
        kcc = kcc_ref[0:ncol, :]
        for h in range(NSA_HPG):
            s_sc[h, :, 0:ncol] = _dot_nt(q_ref[h], kcc)
        lane_v = lax.broadcasted_iota(jnp.int32, (ncol, LANES), 1)
        vcc = vcc_ref[0:ncol, :]
        vm = jnp.concatenate([jnp.where(lane_v == HALF, jnp.ones_like(vcc), vcc), msel_ref[0:ncol, :]],
                             axis=1)
        dmat = (lax.broadcasted_iota(jnp.int32, (rc, ncol), 0)
                - lax.broadcasted_iota(jnp.int32, (rc, ncol), 1) * CMP_STRIDE)
        visible = lambda c: dmat + (s0 + c * rc - (CMP_BLOCK - 1)) >= 0
        outs = []
        imp = jnp.zeros((tq, LANES), F32)
        for h in range(NSA_HPG):
            row_max = []
            for c in range(tq // rc):
                s = jnp.where(visible(c), s_sc[h, pl.ds(c * rc, rc), 0:ncol], NEG)
                row_max.append(jnp.max(s, axis=-1, keepdims=True))
            for c in range(tq // rc):
                rows = pl.ds(c * rc, rc)
                p = jnp.exp2(s_sc[h, rows, 0:ncol] - row_max[c])
                p_sc[h, rows, 0:ncol] = jnp.where(visible(c), p, 0.0).astype(BF16)
            a = jnp.dot(p_sc[h, :, 0:ncol], vm, preferred_element_type=F32)
            denom = jnp.maximum(a[:, HALF:HALF + 1], 1e-30)
            outs.append(a[:, :LANES] / denom)
            imp = imp + a[:, LANES:] / denom
        imp_sc[...] = imp
        glog = glog_ref[...]
        lane = lax.broadcasted_iota(jnp.int32, (tq, LANES), 1)
        for pr in range(NSA_HPG // 2):
            gate = jax.nn.sigmoid(_split_dot(glog, e_ref[pr]))
            pair = jnp.where(lane < HALF, outs[2 * pr], pltpu.roll(outs[2 * pr + 1], HALF, 1))
            oc_ref[:, pr * LANES:(pr + 1) * LANES] = (gate * pair).astype(oc_ref.dtype)

    ntiles = (s0 + (tq - CMP_BLOCK)) // (CMP_STRIDE * LANES) + 1
    for k in range(1, nr // LANES + 1):
        @pl.when(ntiles == k)
        def _width(k=k):
            attend(k * LANES)

    blk = lax.broadcasted_iota(jnp.int32, (tq, LANES), 1)
    cur = (s0 + lax.broadcasted_iota(jnp.int32, (tq, LANES), 0)) // SEL_BLOCK
    valid = blk <= cur
    forced = (blk == 0) | (blk == cur) | (blk == cur - 1)
    score = jnp.where(valid, jnp.where(forced, -jnp.inf, imp_sc[...]), -1.0)
    sc = score.T
    rowid = lax.broadcasted_iota(jnp.int32, (LANES, tq), 0).astype(F32)

    def pick_one(_, sc):
        cm = jnp.max(sc, axis=0, keepdims=True)
        first = jnp.min(jnp.where(sc == cm, rowid, float(LANES)), axis=0, keepdims=True)
        return jnp.where(rowid == first, -jnp.inf, sc)

    picked = lax.fori_loop(0, topk - N_FORCED, pick_one, sc) == -jnp.inf
    sel_ref[...] = jnp.where(picked, 0.0, NEG).T.astype(sel_ref.dtype)


def nsa_compressed(qk, kcc, vcc, msel, misc, glog_blk, expand, *, batch, seq, tq, rc):
    g = NSA_KV_HEADS
    nq = seq // tq
    nr = kcc.shape[2]
    topk = min(SEL_TOPK, seq // SEL_BLOCK)
    assert topk > N_FORCED and nr % LANES == 0
    kern = functools.partial(_nsa_cmp_kernel, tq=tq, rc=rc, topk=topk)
    return pl.pallas_call(
        kern,
        out_shape=(jax.ShapeDtypeStruct((batch * seq, NSA_HEADS * NSA_DV), BF16),
                   jax.ShapeDtypeStruct((batch, g, seq, LANES), BF16)),
        grid=(batch, g, nq),
        in_specs=[
            pl.BlockSpec((None, NSA_HPG, tq, LANES), lambda b, gi, qi: (b, gi, qi, 0)),
            pl.BlockSpec((None, None, nr, LANES), lambda b, gi, qi: (b, gi, 0, 0)),
            pl.BlockSpec((None, None, nr, LANES), lambda b, gi, qi: (b, gi, 0, 0)),
            pl.BlockSpec((nr, LANES), lambda b, gi, qi: (0, 0)),
            pl.BlockSpec((tq, LANES), lambda b, gi, qi: (b * nq + qi, glog_blk)),
            pl.BlockSpec((None, 2, LANES, LANES), lambda b, gi, qi: (0, gi, 0, 0)),
        ],
        out_specs=(pl.BlockSpec((tq, 2 * LANES), lambda b, gi, qi: (b * nq + qi, gi)),
                   pl.BlockSpec((None, None, tq, LANES), lambda b, gi, qi: (b, gi, qi, 0))),
        scratch_shapes=[pltpu.VMEM((NSA_HPG, tq, nr), F32), pltpu.VMEM((NSA_HPG, tq, nr), BF16),
                        pltpu.VMEM((tq, LANES), F32)],
        compiler_params=_cparams("parallel", "parallel", "parallel"),
        name="nsa_compressed_select",
    )(qk, kcc, vcc, msel, misc, expand)


def _flash_steps(mode, nq, tq, tk):
    r = tq // tk
    qi_l, kt_l, first_l, last_l = [], [], [], []
    for qi in range(nq):
        hi = qi * r + r - 1
        lo = max(0, qi * r - (-(-(WINDOW - 1) // tk))) if mode == "window" else 0
        for kt in range(lo, hi + 1):
            qi_l.append(qi)
            kt_l.append(kt)
            first_l.append(int(kt == lo))
            last_l.append(int(kt == hi))
    return tuple(jnp.asarray(np.asarray(a, np.int32)) for a in (qi_l, kt_l, first_l, last_l))


def _flash_kernel(*refs, mode, tq, tk, rc, kv_shared, gated):
    it = iter(refs)
    qi_ref, kt_ref, first_ref, last_ref = next(it), next(it), next(it), next(it)
    q_ref, k_ref, v_ref = next(it), next(it), next(it)
    selb_ref = next(it) if mode == "select" else None
    oh_ref = next(it) if mode == "select" else None
    glog_ref = next(it) if gated else None
    e_ref = next(it) if gated else None
    o_ref = next(it)
    m_sc, acc_sc, s_sc, p_sc = next(it), next(it), next(it), next(it)
    qa_sc = next(it) if mode == "select" else None

    st = pl.program_id(2)
    s0 = qi_ref[st] * tq
    k0 = kt_ref[st] * tk

    @pl.when(first_ref[st] == 1)
    def _init():
        m_sc[...] = jnp.full(m_sc.shape, NEG, F32)
        acc_sc[...] = jnp.zeros(acc_sc.shape, F32)
        if mode == "select":
            for hh in range(FLASH_HEADS):
                qa_sc[hh] = jnp.concatenate([q_ref[hh], selb_ref[...]], axis=1)

    def chunk_cols(off, r0):
        if off is None:
            return 0, tk, True
        r1 = r0 + rc
        hi = min(tk, r1 - off)
        lo = max(0, r0 - off - WINDOW + 1) if mode == "window" else 0
        lo, hi = lo // LANES * LANES, -(-hi // LANES) * LANES
        clear = tk - 1 <= r0 - off and (mode != "window" or r1 - 1 - off < WINDOW)
        return lo, hi, clear

    def tile(off):
        nh = FLASH_HEADS
        if kv_shared:
            if mode == "select":
                qs = qa_sc[...].reshape(nh * tq, 2 * LANES)
                ks = jnp.concatenate([k_ref[0], oh_ref[...]], axis=1)
            else:
                qs, ks = q_ref[...].reshape(nh * tq, LANES), k_ref[0]
            s_sc[...] = _dot_nt(qs, ks).reshape(nh, tq, tk)
        else:
            for hh in range(nh):
                s_sc[hh] = _dot_nt(q_ref[hh], k_ref[hh])
        lane_v = lax.broadcasted_iota(jnp.int32, (tk, LANES), 1)
        if off is not None:
            dmat = (lax.broadcasted_iota(jnp.int32, (rc, tk), 0)
                    - lax.broadcasted_iota(jnp.int32, (rc, tk), 1))
        for hh in range(FLASH_HEADS):
            v = v_ref[0 if kv_shared else hh]
            v_aug = jnp.where(lane_v == HALF, jnp.ones_like(v), v)

            for c in range(tq // rc):
                r0 = c * rc
                rows = pl.ds(r0, rc)
                lo, hi, clear = chunk_cols(off, r0)
                if lo > 0:
                    p_sc[hh, rows, 0:lo] = jnp.zeros((rc, lo), BF16)
                if hi < tk:
                    p_sc[hh, rows, max(hi, 0):tk] = jnp.zeros((rc, tk - max(hi, 0)), BF16)
                if hi <= lo:
                    continue
                s = s_sc[hh, rows, lo:hi]
                if not clear:
                    d = dmat[:, lo:hi] + (r0 - off)
                    msk = (d >= 0) & (d < WINDOW) if mode == "window" else d >= 0
                    s = jnp.where(msk, s, NEG)
                m_old = m_sc[hh, rows, :]
                m_new = jnp.maximum(m_old, jnp.max(s, axis=-1, keepdims=True))
                m_sc[hh, rows, :] = m_new
                acc_sc[hh, rows, :] = acc_sc[hh, rows, :] * jnp.exp2(m_old - m_new)
                p = jnp.exp2(s_sc[hh, rows, lo:hi] - jnp.concatenate([m_new] * ((hi - lo) // LANES), axis=1))
                if not clear:
                    p = jnp.where(msk, p, 0.0)
                p_sc[hh, rows, lo:hi] = p.astype(BF16)
            if not kv_shared:
                acc_sc[hh] += jnp.dot(p_sc[hh], v_aug, preferred_element_type=F32)
        if kv_shared:
            pv = jnp.dot(p_sc[...].reshape(nh * tq, tk), v_aug, preferred_element_type=F32)
            acc_sc[...] += pv.reshape(nh, tq, LANES)

    if mode == "window":
        offsets = [(j - (-(-(WINDOW - 1) // tk))) * tk for j in range(-(-(WINDOW - 1) // tk) + tq // tk)]
    else:
        offsets = [j * tk for j in range(tq // tk)]

        @pl.when(k0 + tk - 1 <= s0)
        def _interior():
            tile(None)

    for off in offsets:
        @pl.when(k0 - s0 == off)
        def _partial(off=off):
            tile(off)

    @pl.when(last_ref[st] == 1)
    def _finish():
        lane = lax.broadcasted_iota(jnp.int32, (tq, LANES), 1)
        if gated:
            glog = glog_ref[...]
        for pr in range(FLASH_HEADS // 2):
            outs = []
            for hh in (2 * pr, 2 * pr + 1):
                a = acc_sc[hh]
                outs.append(a / jnp.maximum(a[:, HALF:HALF + 1], 1e-30))
            out = jnp.where(lane < HALF, outs[0], pltpu.roll(outs[1], HALF, 1))
            if gated:
                out = jax.nn.sigmoid(_split_dot(glog, e_ref[pr])) * out
            o_ref[:, pr * LANES:(pr + 1) * LANES] = out.astype(o_ref.dtype)


def flash_heads(q, k, v, *, mode, batch, seq, tq, tk, rc, q_head0, k_head0, v_head0,
                kv_shared, out_dtype, selb=None, onehot=None, gate=None, name="flash"):
    nh = FLASH_HEADS
    ngroups = NSA_HEADS // nh
    nq = seq // tq
    assert tq % tk == 0 and tq % rc == 0
    tabs = _flash_steps(mode, nq, tq, tk)
    nsteps = tabs[0].shape[0]
    if kv_shared:
        kv_spec = lambda h0: pl.BlockSpec(
            (None, 1, tk, LANES), lambda b, g, st, qi, kt, fi, la: (b, h0 + g, kt[st], 0))
    else:
        kv_spec = lambda h0: pl.BlockSpec(
            (None, nh, tk, LANES), lambda b, g, st, qi, kt, fi, la: (b, h0 // nh + g, kt[st], 0))
    in_specs = [pl.BlockSpec((None, nh, tq, LANES),
                             lambda b, g, st, qi, kt, fi, la: (b, q_head0 // nh + g, qi[st], 0)),
                kv_spec(k_head0), kv_spec(v_head0)]
    args = [q, k, v]
    scratch = [pltpu.VMEM((nh, tq, LANES), F32), pltpu.VMEM((nh, tq, LANES), F32),
               pltpu.VMEM((nh, tq, tk), F32), pltpu.VMEM((nh, tq, tk), BF16)]
    if mode == "select":
        in_specs += [pl.BlockSpec((None, None, tq, LANES),
                                  lambda b, g, st, qi, kt, fi, la: (b, g, qi[st], 0)),
                     pl.BlockSpec((tk, LANES), lambda b, g, st, qi, kt, fi, la: (kt[st], 0))]
        args += [selb, onehot]
        scratch.append(pltpu.VMEM((nh, tq, 2 * LANES), BF16))
    if gate is not None:
        misc, glog_blk, expand, branch = gate
        in_specs += [pl.BlockSpec((tq, LANES),
                                  lambda b, g, st, qi, kt, fi, la: (b * nq + qi[st], glog_blk)),
                     pl.BlockSpec((None, nh // 2, LANES, LANES),
                                  lambda b, g, st, qi, kt, fi, la: (branch, g, 0, 0))]
        args += [misc, expand]
    kern = functools.partial(_flash_kernel, mode=mode, tq=tq, tk=tk, rc=rc,
                             kv_shared=kv_shared, gated=gate is not None)
    ow = nh * HALF
    grid_spec = pltpu.PrefetchScalarGridSpec(
        num_scalar_prefetch=4, grid=(batch, ngroups, nsteps), in_specs=in_specs,
        out_specs=pl.BlockSpec((tq, ow), lambda b, g, st, qi, kt, fi, la: (b * nq + qi[st], g)),
        scratch_shapes=scratch)
    return pl.pallas_call(
        kern,
        out_shape=jax.ShapeDtypeStruct((batch * seq, ngroups * ow), out_dtype),
        grid_spec=grid_spec,
        compiler_params=_cparams("parallel", "parallel", "arbitrary"),
        name=name,
    )(*tabs, *args)


def _mix_kernel(oc_ref, os_ref, ow_ref, ob_ref, ga_ref, gb_ref, wa_ref, wb_ref, o_ref):
    oa = (oc_ref[...].astype(F32) + os_ref[...].astype(F32) + ow_ref[...].astype(F32)).astype(BF16)
    ya = jnp.dot(oa, wa_ref[...], preferred_element_type=F32)
    yb = jnp.dot(ob_ref[...], wb_ref[...], preferred_element_type=F32)
    o_ref[...] = (ga_ref[...].astype(F32) * ya + gb_ref[...].astype(F32) * yb).astype(o_ref.dtype)


def gated_mix(oc, osel, ow, ob, gates, wa, wb, *, tm, tn):
    n, ka = oc.shape
    d = wa.shape[1]
    nb = d // tn
    row = lambda w: pl.BlockSpec((tm, w), lambda i, j: (i, 0))
    return pl.pallas_call(
        _mix_kernel,
        out_shape=jax.ShapeDtypeStruct((n, d), BF16),
        grid=(n // tm, nb),
        in_specs=[row(ka), row(ka), row(ka), row(ob.shape[1]),
                  pl.BlockSpec((tm, tn), lambda i, j: (i, j)),
                  pl.BlockSpec((tm, tn), lambda i, j: (i, nb + j)),
                  pl.BlockSpec((ka, tn), lambda i, j: (0, j)),
                  pl.BlockSpec((wb.shape[0], tn), lambda i, j: (0, j))],
        out_specs=pl.BlockSpec((tm, tn), lambda i, j: (i, j)),
        compiler_params=_cparams("parallel", "arbitrary"),
        name="gated_mix",
    )(oc, osel, ow, ob, gates, gates, wa, wb)


def _xattn_kernel(h_ref, g_ref, wq_ref, kv_ref, wo_ref, o_ref):
    h = h_ref[...]
    hn = _rms(h, g_ref[...]).astype(BF16)
    q = jnp.dot(hn, wq_ref[...], preferred_element_type=F32).astype(BF16)
    kv = kv_ref[...]
    outs = []
    for hd in range(XA_HEADS):
        qh = q[:, hd * XA_DIM:(hd + 1) * XA_DIM]
        kh = kv[:, hd * XA_DIM:(hd + 1) * XA_DIM]
        vh = kv[:, (XA_HEADS + hd) * XA_DIM:(XA_HEADS + hd + 1) * XA_DIM]
        s = _dot_nt(qh, kh) * (XA_DIM ** -0.5)
        e = jnp.exp(s - jnp.max(s, axis=-1, keepdims=True))
        p = e / jnp.sum(e, axis=-1, keepdims=True)
        outs.append(jnp.dot(p.astype(BF16), vh, preferred_element_type=F32))
    o = jnp.concatenate(outs, axis=1).astype(BF16)
    o_ref[...] = h + jnp.dot(o, wo_ref[...], preferred_element_type=F32)


def memory_xattn(h, g, wq, kv, wo, *, seq, tm):
    n, d = h.shape
    s_t = seq // tm
    dq = wq.shape[1]
    return pl.pallas_call(
        _xattn_kernel,
        out_shape=jax.ShapeDtypeStruct((n, d), F32),
        grid=(n // tm,),
        in_specs=[pl.BlockSpec((tm, d), lambda i: (i, 0)),
                  pl.BlockSpec((1, d), lambda i: (0, 0)),
                  pl.BlockSpec((d, dq), lambda i: (0, 0)),
                  pl.BlockSpec((MEM_LEN, 2 * dq), lambda i: (i // s_t, 0)),
                  pl.BlockSpec((dq, d), lambda i: (0, 0))],
        out_specs=pl.BlockSpec((tm, d), lambda i: (i, 0)),
        compiler_params=_cparams("parallel"),
        name="memory_xattn",
    )(h, g.reshape(1, d), wq, kv, wo)


def _mlp_kernel(h_ref, g_ref, w1_ref, w2_ref, gf_ref, o_ref, hn_sc, acc_sc, *, nf):
    f = pl.program_id(1)

    @pl.when(f == 0)
    def _init():
        hn_sc[...] = _rms(h_ref[...], g_ref[...]).astype(BF16)
        acc_sc[...] = jnp.zeros(acc_sc.shape, F32)

    u = jnp.dot(hn_sc[...], w1_ref[...], preferred_element_type=F32)
    u = jnp.square(jnp.maximum(u, 0.0))
    acc_sc[...] += jnp.dot(u.astype(BF16), w2_ref[...], preferred_element_type=F32)

    @pl.when(f == nf - 1)
    def _finish():
        o_ref[...] = _rms(h_ref[...] + acc_sc[...], gf_ref[...])


def mlp_final(h, g, w1, w2, gf, *, tm, tf):
    n, d = h.shape
    dff = w1.shape[1]
    nf = dff // tf
    return pl.pallas_call(
        functools.partial(_mlp_kernel, nf=nf),
        out_shape=jax.ShapeDtypeStruct((n, d), F32),
        grid=(n // tm, nf),
        in_specs=[pl.BlockSpec((tm, d), lambda i, f: (i, 0)),
                  pl.BlockSpec((1, d), lambda i, f: (0, 0)),
                  pl.BlockSpec((d, tf), lambda i, f: (0, f)),
                  pl.BlockSpec((tf, d), lambda i, f: (f, 0)),
                  pl.BlockSpec((1, d), lambda i, f: (0, 0))],
        out_specs=pl.BlockSpec((tm, d), lambda i, f: (i, 0)),
        scratch_shapes=[pltpu.VMEM((tm, d), BF16), pltpu.VMEM((tm, d), F32)],
        compiler_params=_cparams("parallel", "arbitrary"),
        name="mlp_final",
    )(h, g.reshape(1, d), w1, w2, gf.reshape(1, d))


def _rot_partner(w, half):
    return jnp.concatenate([-w[..., half:], w[..., :half]], axis=-1)


def _pad_heads_rope(w, heads, hd, rot0, rot_dim):
    k = w.shape[0]
    w = w.reshape(k, heads, hd)
    partner = _rot_partner(w[:, :, rot0:rot0 + rot_dim], rot_dim // 2)
    pad = jnp.zeros((k, heads, LANES - hd - rot_dim), w.dtype)
    return jnp.concatenate([w, partner, pad], axis=-1).reshape(k, heads * LANES)


def _rope_lane_tables(seq, rot0, rot_dim, hd):
    inv = 1.0 / (ROPE_THETA ** (jnp.arange(0, rot_dim, 2, dtype=F32) / rot_dim))
    ang = jnp.arange(seq, dtype=F32)[:, None] * inv[None, :]
    cos = jnp.concatenate([jnp.cos(ang), jnp.cos(ang)], axis=1)
    sin = jnp.concatenate([jnp.sin(ang), jnp.sin(ang)], axis=1)
    cos_t = jnp.concatenate([jnp.ones((seq, rot0), F32), cos,
                             jnp.ones((seq, hd - rot0 - rot_dim), F32),
                             jnp.zeros((seq, LANES - hd), F32)], axis=1)
    sin_t = jnp.concatenate([jnp.zeros((seq, rot0), F32), sin,
                             jnp.zeros((seq, LANES - rot0 - rot_dim), F32)], axis=1)
    return cos_t, sin_t


def _pad_lanes(w, groups, width):
    k = w.shape[0]
    w = w.reshape(k, groups, width)
    return jnp.pad(w, ((0, 0), (0, 0), (0, LANES - width))).reshape(k, groups * LANES)


def _compress_params(pos, w1, w2, d):
    half = CMP_BLOCK // 2
    pos_p = jnp.pad(pos, ((0, 0), (0, LANES - d)))
    pos_a = pos_p[:half].reshape(1, half * LANES)
    pos_b = pos_p[half:].reshape(1, half * LANES)
    w1p = jnp.pad(w1.reshape(CMP_BLOCK, d, d), ((0, 0), (0, LANES - d), (0, LANES - d)))
    w1a = w1p[:half].reshape(half * LANES, LANES).astype(BF16)
    w1b = w1p[half:].reshape(half * LANES, LANES).astype(BF16)
    w2p = jnp.pad(w2, ((0, LANES - d), (0, LANES - d)))
    return pos_a, pos_b, w1a, w1b, w2p.astype(BF16)


def _cmp_to_sel(nr, nsb):
    cs = np.arange(nr) * CMP_STRIDE
    ce = cs + CMP_BLOCK
    ss = np.arange(LANES) * SEL_BLOCK
    se = ss + SEL_BLOCK
    ov = np.clip(np.minimum(ce[:, None], se[None, :]) - np.maximum(cs[:, None], ss[None, :]), 0, None)
    ov = ov.astype(np.float32) / np.float32(CMP_BLOCK)
    ov[:, nsb:] = 0.0
    ov[nr - 1:, :] = 0.0
    return jnp.asarray(ov, BF16)


def _gate_expand():
    e = np.zeros((3, NSA_HEADS // 2, LANES, LANES), np.float32)
    for br in range(3):
        for hp in range(NSA_HEADS // 2):
            for hh in range(2):
                e[br, hp, 3 * (2 * hp + hh) + br, hh * HALF:(hh + 1) * HALF] = 1.0
    return jnp.asarray(e, BF16)


def _key_block_onehot(seq):
    e = (np.arange(seq)[:, None] // SEL_BLOCK) == np.arange(LANES)[None, :]
    return jnp.asarray(e.astype(np.float32), BF16)


def kernel(x, mem, g_mix, w_in, cmp_pos_k, cmp_w1_k, cmp_w2_k, cmp_pos_v, cmp_w1_v, cmp_w2_v,
           mla_g_q, mla_w_uq, mla_g_kv, mla_w_uk, mla_w_uv, w_o_nsa, w_o_mla, w_out,
           g_xattn, g_mem, xa_wq, xa_wkv, xa_wo, g_mlp, w_ff1, w_ff2, g_final):
    b, s, d = x.shape
    assert d == D_MODEL and s % (CMP_STRIDE * 8) == 0 and s // SEL_BLOCK <= LANES
    assert g_mix.shape[0] == 1
    n = b * s
    T = _tiles(s)
    tm, tq, tk, rc = T["tm"], T["tq"], T["tk"], T["rc"]
    G = NSA_KV_HEADS
    bounds = [int(v) for v in np.cumsum(SPLITS)[:-1]]

    (w_qa, w_kc, w_vc, w_ks, w_vs, w_kw, w_vw, w_gn, w_cq, w_ckv, w_kr,
     w_ga, w_gb) = jnp.split(w_in[0], bounds, axis=1)
    nsa_rope = lambda w, heads: _pad_heads_rope(w, heads, NSA_DK, 0, NSA_ROT)
    w_rope = jnp.concatenate([nsa_rope(w_qa, NSA_HEADS), nsa_rope(w_kc, G), nsa_rope(w_ks, G),
                              nsa_rope(w_kw, G)], axis=1).astype(BF16)
    w_krp = jnp.concatenate([jnp.zeros((d, MLA_NOPE), F32), w_kr, _rot_partner(w_kr, MLA_ROPE // 2)],
                            axis=1).astype(BF16)
    w_vsw = jnp.concatenate([_pad_lanes(w_vs, G, NSA_DV), _pad_lanes(w_vw, G, NSA_DV)], axis=1).astype(BF16)
    w_vcp = _pad_lanes(w_vc, G, NSA_DV).astype(BF16)
    w_misc = jnp.concatenate([w_cq, w_ckv, jnp.pad(w_gn, ((0, 0), (0, LANES - w_gn.shape[1])))],
                             axis=1).astype(BF16)
    glog_blk = (MLA_Q_RANK + MLA_KV_RANK) // LANES
    w_gates = jnp.concatenate([w_ga, w_gb], axis=1).astype(BF16)
    cos_a, sin_a = _rope_lane_tables(s, 0, NSA_ROT, NSA_DK)
    cos_b, sin_b = _rope_lane_tables(s, MLA_NOPE, MLA_ROPE, MLA_NOPE + MLA_ROPE)
    shift_a = LANES - NSA_DK
    shift_b = LANES - (NSA_DK - MLA_NOPE)
    w_uq = _pad_heads_rope(mla_w_uq[0], MLA_HEADS, MLA_NOPE + MLA_ROPE, MLA_NOPE, MLA_ROPE).astype(BF16)
    w_uk = _pad_lanes(mla_w_uk[0], MLA_HEADS, MLA_NOPE).astype(BF16)
    w_uv = _pad_lanes(mla_w_uv[0], MLA_HEADS, MLA_DV).astype(BF16)

    x2 = x.reshape(n, d)
    xn = rmsnorm_rows(x2, g_mix[0], tm)
    log2e = float(np.log2(np.e))
    tmb = T["tm_big"]
    qk = projection(xn, w_rope, tm=tmb, tn=512, out_dtype=BF16, rope=(cos_a, sin_a, shift_a),
                    out_scale=(NSA_DK ** -0.5 * log2e, NSA_HEADS * LANES // 512),
                    head_major=(b, s), name="proj_qk_rope")
    krp = projection(xn, w_krp, tm=tm, tn=LANES, out_dtype=BF16, rope=(cos_b, sin_b, shift_b),
                     name="proj_k_rope_mla")
    vsw = projection(xn, w_vsw, tm=tmb, tn=512, out_dtype=BF16, head_major=(b, s), name="proj_v")
    vcp = projection(xn, w_vcp, tm=tmb, tn=512, out_dtype=BF16, head_major=(b, s), name="proj_vc")
    misc = projection(xn, w_misc, tm=tm, tn=w_misc.shape[1], out_dtype=F32, name="proj_misc")
    gates = projection(xn, w_gates, tm=tmb, tn=512, out_dtype=BF16, act="sigmoid", name="proj_gates")

    nr = s // CMP_STRIDE
    pk = _compress_params(cmp_pos_k[0], cmp_w1_k[0], cmp_w2_k[0], NSA_DK)
    pv = _compress_params(cmp_pos_v[0], cmp_w1_v[0], cmp_w2_v[0], NSA_DV)
    kc_rows = qk[:, QK_KC0:QK_KC0 + G].reshape(b, G, nr, CMP_STRIDE * LANES)
    kcc = compress(kc_rows, 0, *pk)
    vcc = compress(vcp.reshape(b, G, nr, CMP_STRIDE * LANES), 0, *pv)
    expand = _gate_expand()
    o_c, selb = nsa_compressed(qk, kcc, vcc, _cmp_to_sel(nr, s // SEL_BLOCK), misc, glog_blk, expand,
                               batch=b, seq=s, tq=T["tq_cmp"], rc=rc)
    o_s = flash_heads(qk, qk, vsw, mode="select", batch=b, seq=s, tq=tq, tk=tk, rc=rc,
                      q_head0=QK_Q0, k_head0=QK_KS0, v_head0=0, kv_shared=True, out_dtype=BF16,
                      selb=selb, onehot=_key_block_onehot(s), gate=(misc, glog_blk, expand, 1),
                      name="nsa_selected")
    o_w = flash_heads(qk, qk, vsw, mode="window", batch=b, seq=s, tq=T["tq_win"], tk=tk, rc=rc,
                      q_head0=QK_Q0, k_head0=QK_KW0, v_head0=G, kv_shared=True, out_dtype=BF16,
                      gate=(misc, glog_blk, expand, 2), name="nsa_window")

    qm = projection(misc, w_uq, tm=tm, tn=512, out_dtype=BF16, a_cols=(MLA_Q_RANK, 0), gain=mla_g_q[0],
                    rope=(cos_b, sin_b, shift_b), head_major=(b, s), name="mla_q",
                    out_scale=((MLA_NOPE + MLA_ROPE) ** -0.5 * log2e, MLA_HEADS * LANES // 512))
    km = projection(misc, w_uk, tm=tm, tn=512, out_dtype=BF16,
                    a_cols=(MLA_KV_RANK, MLA_Q_RANK // MLA_KV_RANK), gain=mla_g_kv[0], add=krp,
                    head_major=(b, s), name="mla_k")
    vm = projection(misc, w_uv, tm=tm, tn=512, out_dtype=BF16,
                    a_cols=(MLA_KV_RANK, MLA_Q_RANK // MLA_KV_RANK), gain=mla_g_kv[0],
                    head_major=(b, s), name="mla_v")
    o_b = flash_heads(qm, km, vm, mode="causal", batch=b, seq=s, tq=tq, tk=tk, rc=rc,
                      q_head0=0, k_head0=0, v_head0=0,
                      kv_shared=False, out_dtype=BF16, name="mla_attention")

    mixed = gated_mix(o_c, o_s, o_w, o_b, gates, w_o_nsa[0].astype(BF16), w_o_mla[0].astype(BF16),
                      tm=tm, tn=512)
    h1 = projection(mixed, w_out[0].astype(BF16), tm=tmb, tn=512, out_dtype=F32, res=x2, name="proj_out")

    kv_mem = projection(mem.reshape(b * MEM_LEN, d), xa_wkv[0].astype(BF16), tm=MEM_LEN, tn=512,
                        out_dtype=BF16, gain=g_mem[0], name="xattn_kv")
    h2 = memory_xattn(h1, g_xattn[0], xa_wq[0].astype(BF16), kv_mem, xa_wo[0].astype(BF16), seq=s,
                      tm=T["tm_x"])

    out = mlp_final(h2, g_mlp[0], w_ff1[0].astype(BF16), w_ff2[0].astype(BF16), g_final,
                    tm=T["tm_mlp"], tf=T["tf"])
    return out.reshape(b, s, d)
```

```python
import functools

import numpy as np
import jax
import jax.numpy as jnp
from jax import lax
from jax.experimental import pallas as pl
from jax.experimental.pallas import tpu as pltpu

F32 = jnp.float32
BF16 = jnp.bfloat16

D_MODEL = 2048
MEM_LEN = 256
ROPE_THETA = 500000.0
EPS = 1e-6
NEG = -1e30

NSA_HEADS = 16
NSA_KV_HEADS = 4
NSA_HPG = NSA_HEADS // NSA_KV_HEADS
NSA_DK = 96
NSA_DV = 64
NSA_ROT = NSA_DK // 4
CMP_BLOCK = 32
CMP_STRIDE = 16
SEL_BLOCK = 64
SEL_TOPK = 16
N_FORCED = 3
WINDOW = 512

MLA_HEADS = 16
MLA_NOPE = 64
MLA_ROPE = 32
MLA_DV = 64
MLA_Q_RANK = 512
MLA_KV_RANK = 256

XA_HEADS = 4
XA_DIM = 128
D_FF = 4 * D_MODEL

SPLITS = (NSA_HEADS * NSA_DK,
          NSA_KV_HEADS * NSA_DK, NSA_KV_HEADS * NSA_DV,
          NSA_KV_HEADS * NSA_DK, NSA_KV_HEADS * NSA_DV,
          NSA_KV_HEADS * NSA_DK, NSA_KV_HEADS * NSA_DV,
          NSA_HEADS * 3,
          MLA_Q_RANK, MLA_KV_RANK, MLA_ROPE,
          D_MODEL, D_MODEL)

LANES = 128
HALF = LANES // 2
VMEM_LIMIT = 56 * 1024 * 1024

QK_Q0 = 0
QK_KC0 = NSA_HEADS
QK_KS0 = NSA_HEADS + NSA_KV_HEADS
QK_KW0 = NSA_HEADS + 2 * NSA_KV_HEADS
QK_HEADS = NSA_HEADS + 3 * NSA_KV_HEADS
FLASH_HEADS = NSA_HPG


def _cparams(*sem):
    return pltpu.CompilerParams(dimension_semantics=sem, vmem_limit_bytes=VMEM_LIMIT)


def _tiles(seq):
    return dict(
        tm=min(1024, seq),
        tm_big=min(2048, seq),
        tm_x=min(512, seq),
        tm_mlp=min(512, seq),
        tq=min(1024, seq),
        tq_win=min(512, seq),
        tk=min(512, seq),
        rc=32,
        tq_cmp=min(256, seq),
        tf=512,
    )


def _rms(x, g):
    return x * lax.rsqrt(jnp.mean(x * x, axis=-1, keepdims=True) + EPS) * g


def _rmsnorm_kernel(x_ref, g_ref, o_ref):
    o_ref[...] = _rms(x_ref[...], g_ref[...]).astype(o_ref.dtype)


def rmsnorm_rows(x, g, tm):
    n, d = x.shape
    return pl.pallas_call(
        _rmsnorm_kernel,
        out_shape=jax.ShapeDtypeStruct((n, d), BF16),
        grid=(n // tm,),
        in_specs=[pl.BlockSpec((tm, d), lambda i: (i, 0)),
                  pl.BlockSpec((1, d), lambda i: (0, 0))],
        out_specs=pl.BlockSpec((tm, d), lambda i: (i, 0)),
        compiler_params=_cparams("parallel"),
        name="rmsnorm_rows",
    )(x, g.reshape(1, d))


def _proj_kernel(*refs, has_gain, has_rope, roll_shift, out_scale, has_add, has_res, act, head_major):
    it = iter(refs)
    a_ref = next(it)
    g_ref = next(it) if has_gain else None
    w_ref = next(it)
    cos_ref = next(it) if has_rope else None
    sin_ref = next(it) if has_rope else None
    add_ref = next(it) if has_add else None
    res_ref = next(it) if has_res else None
    o_ref = next(it)

    a = a_ref[...]
    if has_gain:
        a = _rms(a, g_ref[...]).astype(BF16)
    y = jnp.dot(a, w_ref[...], preferred_element_type=F32)
    if out_scale is not None:
        factor, n_tiles = out_scale
        y = y * jnp.where(pl.program_id(1) < n_tiles, factor, 1.0)
    if has_rope or has_add or head_major:
        for h in range(y.shape[1] // LANES):
            yh = y[:, h * LANES:(h + 1) * LANES]
            if has_rope:
                yh = yh * cos_ref[...] + pltpu.roll(yh, roll_shift, 1) * sin_ref[...]
            if has_add:
                yh = yh + add_ref[...].astype(F32)
            if head_major:
                o_ref[h] = yh.astype(o_ref.dtype)
            else:
                o_ref[:, h * LANES:(h + 1) * LANES] = yh.astype(o_ref.dtype)
    else:
        if act == "sigmoid":
            y = jax.nn.sigmoid(y)
        if has_res:
            y = res_ref[...] + y
        o_ref[...] = y.astype(o_ref.dtype)


def projection(a, w, *, tm, tn, out_dtype, a_cols=None, gain=None, rope=None, out_scale=None, add=None,
               res=None, act=None, head_major=None, name="projection"):
    n = a.shape[0]
    k, nc = w.shape
    a_w, a_blk = a_cols if a_cols is not None else (a.shape[1], 0)
    assert a_w == k and n % tm == 0 and nc % tn == 0
    grid = (n // tm, nc // tn)
    in_specs = [pl.BlockSpec((tm, k), lambda i, j: (i, a_blk))]
    args = [a]
    if gain is not None:
        in_specs.append(pl.BlockSpec((1, k), lambda i, j: (0, 0)))
        args.append(gain.reshape(1, k))
    in_specs.append(pl.BlockSpec((k, tn), lambda i, j: (0, j)))
    args.append(w)
    roll_shift = 0
    if rope is not None:
        cos, sin, roll_shift = rope
        s_tiles = cos.shape[0] // tm
        for t in (cos, sin):
            in_specs.append(pl.BlockSpec((tm, LANES), lambda i, j: (i % s_tiles, 0)))
            args.append(t)
    if add is not None:
        in_specs.append(pl.BlockSpec((tm, LANES), lambda i, j: (i, 0)))
        args.append(add)
    if res is not None:
        in_specs.append(pl.BlockSpec((tm, tn), lambda i, j: (i, j)))
        args.append(res)
    if head_major is not None:
        b, s = head_major
        s_t = s // tm
        hpt = tn // LANES
        out_shape = jax.ShapeDtypeStruct((b, nc // LANES, s, LANES), out_dtype)
        out_spec = pl.BlockSpec((None, hpt, tm, LANES), lambda i, j: (i // s_t, j, i % s_t, 0))
    else:
        out_shape = jax.ShapeDtypeStruct((n, nc), out_dtype)
        out_spec = pl.BlockSpec((tm, tn), lambda i, j: (i, j))
    kern = functools.partial(
        _proj_kernel, has_gain=gain is not None, has_rope=rope is not None, roll_shift=roll_shift,
        out_scale=out_scale, has_add=add is not None, has_res=res is not None, act=act,
        head_major=head_major is not None)
    return pl.pallas_call(
        kern, out_shape=out_shape, grid=grid, in_specs=in_specs, out_specs=out_spec,
        compiler_params=_cparams("parallel", "arbitrary"), name=name,
    )(*args)


def _compress_kernel(r_ref, pa_ref, pb_ref, w1a_ref, w1b_ref, w2_ref, o_ref):
    r = r_ref[...].astype(F32)
    a = jnp.dot((r + pa_ref[...]).astype(BF16), w1a_ref[...], preferred_element_type=F32)
    b = jnp.dot((r + pb_ref[...]).astype(BF16), w1b_ref[...], preferred_element_type=F32)
    nr = a.shape[0]
    hid = a + pltpu.roll(b, nr - 1, 0)
    hid = jax.nn.gelu(hid)
    o_ref[...] = jnp.dot(hid.astype(BF16), w2_ref[...], preferred_element_type=F32).astype(o_ref.dtype)


def compress(r, head0, pos_a, pos_b, w1a, w1b, w2):
    b, _, nr, kk = r.shape
    g = NSA_KV_HEADS
    full = lambda shape: pl.BlockSpec(shape, lambda bi, gi: (0,) * len(shape))
    return pl.pallas_call(
        _compress_kernel,
        out_shape=jax.ShapeDtypeStruct((b, g, nr, LANES), BF16),
        grid=(b, g),
        in_specs=[pl.BlockSpec((None, None, nr, kk), lambda bi, gi: (bi, head0 + gi, 0, 0)),
                  full((1, kk)), full((1, kk)), full((kk, LANES)), full((kk, LANES)),
                  full((LANES, LANES))],
        out_specs=pl.BlockSpec((None, None, nr, LANES), lambda bi, gi: (bi, gi, 0, 0)),
        compiler_params=_cparams("parallel", "parallel"),
        name="nsa_compress",
    )(r, pos_a, pos_b, w1a, w1b, w2)


def _dot_nt(a, b):
    return lax.dot_general(a, b, (((1,), (1,)), ((), ())), preferred_element_type=F32)


def _split_dot(x, w):
    hi = x.astype(BF16)
    lo = (x - hi.astype(F32)).astype(BF16)
    return (jnp.dot(hi, w, preferred_element_type=F32) + jnp.dot(lo, w, preferred_element_type=F32))


def _nsa_cmp_kernel(q_ref, kcc_ref, vcc_ref, msel_ref, glog_ref, e_ref, oc_ref, sel_ref, s_sc, p_sc, imp_sc,
                    *, tq, rc, topk):
    s0 = pl.program_id(2) * tq
    nr = kcc_ref.shape[0]

    def attend(ncol):
        kcc = kcc_ref[0:ncol, :]
        for h in range(NSA_HPG):
            s_sc[h, :, 0:ncol] = _dot_nt(q_ref[h], kcc)
        lane_v = lax.broadcasted_iota(jnp.int32, (ncol, LANES), 1)
        vcc = vcc_ref[0:ncol, :]
        vm = jnp.concatenate([jnp.where(lane_v == HALF, jnp.ones_like(vcc), vcc), msel_ref[0:ncol, :]],
                             axis=1)
        dmat = (lax.broadcasted_iota(jnp.int32, (rc, ncol), 0)
                - lax.broadcasted_iota(jnp.int32, (rc, ncol), 1) * CMP_STRIDE)
        visible = lambda c: dmat + (s0 + c * rc - (CMP_BLOCK - 1)) >= 0
        outs = []
        imp = jnp.zeros((tq, LANES), F32)
        for h in range(NSA_HPG):
            row_max = []
            for c in range(tq // rc):
                s = jnp.where(visible(c), s_sc[h, pl.ds(c * rc, rc), 0:ncol], NEG)
                row_max.append(jnp.max(s, axis=-1, keepdims=True))
            for c in range(tq // rc):
                rows = pl.ds(c * rc, rc)
                p = jnp.exp2(s_sc[h, rows, 0:ncol] - row_max[c])
                p_sc[h, rows, 0:ncol] = jnp.where(visible(c), p, 0.0).astype(BF16)
            a = jnp.dot(p_sc[h, :, 0:ncol], vm, preferred_element_type=F32)
            denom = jnp.maximum(a[:, HALF:HALF + 1], 1e-30)
            outs.append(a[:, :LANES] / denom)
            imp = imp + a[:, LANES:] / denom
        imp_sc[...] = imp
        glog = glog_ref[...]
        lane = lax.broadcasted_iota(jnp.int32, (tq, LANES), 1)
        for pr in range(NSA_HPG // 2):
            gate = jax.nn.sigmoid(_split_dot(glog, e_ref[pr]))
            pair = jnp.where(lane < HALF, outs[2 * pr], pltpu.roll(outs[2 * pr + 1], HALF, 1))
            oc_ref[:, pr * LANES:(pr + 1) * LANES] = (gate * pair).astype(oc_ref.dtype)

    ntiles = (s0 + (tq - CMP_BLOCK)) // (CMP_STRIDE * LANES) + 1
    for k in range(1, nr // LANES + 1):
        @pl.when(ntiles == k)
        def _width(k=k):
            attend(k * LANES)

    blk = lax.broadcasted_iota(jnp.int32, (tq, LANES), 1)
    cur = (s0 + lax.broadcasted_iota(jnp.int32, (tq, LANES), 0)) // SEL_BLOCK
    valid = blk <= cur
    forced = (blk == 0) | (blk == cur) | (blk == cur - 1)
    score = jnp.where(valid, jnp.where(forced, -jnp.inf, imp_sc[...]), -1.0)
    sc = score.T
    rowid = lax.broadcasted_iota(jnp.int32, (LANES, tq), 0).astype(F32)

    def pick_one(_, sc):
        cm = jnp.max(sc, axis=0, keepdims=True)
        first = jnp.min(jnp.where(sc == cm, rowid, float(LANES)), axis=0, keepdims=True)
        return jnp.where(rowid == first, -jnp.inf, sc)

    picked = lax.fori_loop(0, topk - N_FORCED, pick_one, sc) == -jnp.inf
    sel_ref[...] = jnp.where(picked, 0.0, NEG).T.astype(sel_ref.dtype)


def nsa_compressed(qk, kcc, vcc, msel, misc, glog_blk, expand, *, batch, seq, tq, rc):
    g = NSA_KV_HEADS
    nq = seq // tq
    nr = kcc.shape[2]
    topk = min(SEL_TOPK, seq // SEL_BLOCK)
    assert topk > N_FORCED and nr % LANES == 0
    kern = functools.partial(_nsa_cmp_kernel, tq=tq, rc=rc, topk=topk)
    return pl.pallas_call(
        kern,
        out_shape=(jax.ShapeDtypeStruct((batch * seq, NSA_HEADS * NSA_DV), BF16),
                   jax.ShapeDtypeStruct((batch, g, seq, LANES), BF16)),
        grid=(batch, g, nq),
        in_specs=[
            pl.BlockSpec((None, NSA_HPG, tq, LANES), lambda b, gi, qi: (b, gi, qi, 0)),
            pl.BlockSpec((None, None, nr, LANES), lambda b, gi, qi: (b, gi, 0, 0)),
            pl.BlockSpec((None, None, nr, LANES), lambda b, gi, qi: (b, gi, 0, 0)),
            pl.BlockSpec((nr, LANES), lambda b, gi, qi: (0, 0)),
            pl.BlockSpec((tq, LANES), lambda b, gi, qi: (b * nq + qi, glog_blk)),
            pl.BlockSpec((None, 2, LANES, LANES), lambda b, gi, qi: (0, gi, 0, 0)),
        ],
        out_specs=(pl.BlockSpec((tq, 2 * LANES), lambda b, gi, qi: (b * nq + qi, gi)),
                   pl.BlockSpec((None, None, tq, LANES), lambda b, gi, qi: (b, gi, qi, 0))),
        scratch_shapes=[pltpu.VMEM((NSA_HPG, tq, nr), F32), pltpu.VMEM((NSA_HPG, tq, nr), BF16),
                        pltpu.VMEM((tq, LANES), F32)],
        compiler_params=_cparams("parallel", "parallel", "parallel"),
        name="nsa_compressed_select",
    )(qk, kcc, vcc, msel, misc, expand)


def _flash_steps(mode, nq, tq, tk):
    r = tq // tk
    qi_l, kt_l, first_l, last_l = [], [], [], []
    for qi in range(nq):
        hi = qi * r + r - 1
        lo = max(0, qi * r - (-(-(WINDOW - 1) // tk))) if mode == "window" else 0
        for kt in range(lo, hi + 1):
            qi_l.append(qi)
            kt_l.append(kt)
            first_l.append(int(kt == lo))
            last_l.append(int(kt == hi))
    return tuple(jnp.asarray(np.asarray(a, np.int32)) for a in (qi_l, kt_l, first_l, last_l))


def _flash_kernel(*refs, mode, tq, tk, rc, kv_shared, gated):
    it = iter(refs)
    qi_ref, kt_ref, first_ref, last_ref = next(it), next(it), next(it), next(it)
    q_ref, k_ref, v_ref = next(it), next(it), next(it)
    selb_ref = next(it) if mode == "select" else None
    oh_ref = next(it) if mode == "select" else None
    glog_ref = next(it) if gated else None
    e_ref = next(it) if gated else None
    o_ref = next(it)
    m_sc, acc_sc, s_sc, p_sc = next(it), next(it), next(it), next(it)
    qa_sc = next(it) if mode == "select" else None

    st = pl.program_id(2)
    s0 = qi_ref[st] * tq
    k0 = kt_ref[st] * tk

    @pl.when(first_ref[st] == 1)
    def _init():
        m_sc[...] = jnp.full(m_sc.shape, NEG, F32)
        acc_sc[...] = jnp.zeros(acc_sc.shape, F32)
        if mode == "select":
            for hh in range(FLASH_HEADS):
                qa_sc[hh] = jnp.concatenate([q_ref[hh], selb_ref[...]], axis=1)

    def chunk_cols(off, r0):
        if off is None:
            return 0, tk, True
        r1 = r0 + rc
        hi = min(tk, r1 - off)
        lo = max(0, r0 - off - WINDOW + 1) if mode == "window" else 0
        lo, hi = lo // LANES * LANES, -(-hi // LANES) * LANES
        clear = tk - 1 <= r0 - off and (mode != "window" or r1 - 1 - off < WINDOW)
        return lo, hi, clear

    def tile(off):
        nh = FLASH_HEADS
        r_lo = max(0, off) if off is not None else 0
        nrow = tq - r_lo
        if kv_shared:
            if mode == "select":
                qs = qa_sc[:, r_lo:, :].reshape(nh * nrow, 2 * LANES)
                ks = jnp.concatenate([k_ref[0], oh_ref[...]], axis=1)
            else:
                qs, ks = q_ref[:, r_lo:, :].reshape(nh * nrow, LANES), k_ref[0]
            s_sc[:, r_lo:, :] = _dot_nt(qs, ks).reshape(nh, nrow, tk)
        else:
            for hh in range(nh):
                s_sc[hh, r_lo:, :] = _dot_nt(q_ref[hh, r_lo:, :], k_ref[hh])
        lane_v = lax.broadcasted_iota(jnp.int32, (tk, LANES), 1)
        if off is not None:
            dmat = (lax.broadcasted_iota(jnp.int32, (rc, tk), 0)
                    - lax.broadcasted_iota(jnp.int32, (rc, tk), 1))
        for hh in range(FLASH_HEADS):
            v = v_ref[0 if kv_shared else hh]
            v_aug = jnp.where(lane_v == HALF, jnp.ones_like(v), v)

            for c in range(r_lo // rc, tq // rc):
                r0 = c * rc
                rows = pl.ds(r0, rc)
                lo, hi, clear = chunk_cols(off, r0)
                if lo > 0:
                    p_sc[hh, rows, 0:lo] = jnp.zeros((rc, lo), BF16)
                if hi < tk:
                    p_sc[hh, rows, max(hi, 0):tk] = jnp.zeros((rc, tk - max(hi, 0)), BF16)
                if hi <= lo:
                    continue
                s = s_sc[hh, rows, lo:hi]
                if not clear:
                    d = dmat[:, lo:hi] + (r0 - off)
                    msk = (d >= 0) & (d < WINDOW) if mode == "window" else d >= 0
                    s = jnp.where(msk, s, NEG)
                m_old = m_sc[hh, rows, :]
                m_new = jnp.maximum(m_old, jnp.max(s, axis=-1, keepdims=True))
                m_sc[hh, rows, :] = m_new
                acc_sc[hh, rows, :] = acc_sc[hh, rows, :] * jnp.exp2(m_old - m_new)
                p = jnp.exp2(s_sc[hh, rows, lo:hi] - jnp.concatenate([m_new] * ((hi - lo) // LANES), axis=1))
                if not clear:
                    p = jnp.where(msk, p, 0.0)
                p_sc[hh, rows, lo:hi] = p.astype(BF16)
            if not kv_shared:
                acc_sc[hh, r_lo:, :] += jnp.dot(p_sc[hh, r_lo:, :], v_aug, preferred_element_type=F32)
        if kv_shared:
            pv = jnp.dot(p_sc[:, r_lo:, :].reshape(nh * nrow, tk), v_aug, preferred_element_type=F32)
            acc_sc[:, r_lo:, :] += pv.reshape(nh, nrow, LANES)

    if mode == "window":
        offsets = [(j - (-(-(WINDOW - 1) // tk))) * tk for j in range(-(-(WINDOW - 1) // tk) + tq // tk)]
    else:
        offsets = [j * tk for j in range(tq // tk)]

        @pl.when(k0 + tk - 1 <= s0)
        def _interior():
            tile(None)

    for off in offsets:
        @pl.when(k0 - s0 == off)
        def _partial(off=off):
            tile(off)

    @pl.when(last_ref[st] == 1)
    def _finish():
        lane = lax.broadcasted_iota(jnp.int32, (tq, LANES), 1)
        if gated:
            glog = glog_ref[...]
        for pr in range(FLASH_HEADS // 2):
            outs = []
            for hh in (2 * pr, 2 * pr + 1):
                a = acc_sc[hh]
                outs.append(a / jnp.maximum(a[:, HALF:HALF + 1], 1e-30))
            out = jnp.where(lane < HALF, outs[0], pltpu.roll(outs[1], HALF, 1))
            if gated:
                out = jax.nn.sigmoid(_split_dot(glog, e_ref[pr])) * out
            o_ref[:, pr * LANES:(pr + 1) * LANES] = out.astype(o_ref.dtype)


def flash_heads(q, k, v, *, mode, batch, seq, tq, tk, rc, q_head0, k_head0, v_head0,
                kv_shared, out_dtype, selb=None, onehot=None, gate=None, name="flash"):
    nh = FLASH_HEADS
    ngroups = NSA_HEADS // nh
    nq = seq // tq
    assert tq % tk == 0 and tq % rc == 0
    tabs = _flash_steps(mode, nq, tq, tk)
    nsteps = tabs[0].shape[0]
    if kv_shared:
        kv_spec = lambda h0: pl.BlockSpec(
            (None, 1, tk, LANES), lambda b, g, st, qi, kt, fi, la: (b, h0 + g, kt[st], 0))
    else:
        kv_spec = lambda h0: pl.BlockSpec(
            (None, nh, tk, LANES), lambda b, g, st, qi, kt, fi, la: (b, h0 // nh + g, kt[st], 0))
    in_specs = [pl.BlockSpec((None, nh, tq, LANES),
                             lambda b, g, st, qi, kt, fi, la: (b, q_head0 // nh + g, qi[st], 0)),
                kv_spec(k_head0), kv_spec(v_head0)]
    args = [q, k, v]
    scratch = [pltpu.VMEM((nh, tq, LANES), F32), pltpu.VMEM((nh, tq, LANES), F32),
               pltpu.VMEM((nh, tq, tk), F32), pltpu.VMEM((nh, tq, tk), BF16)]
    if mode == "select":
        in_specs += [pl.BlockSpec((None, None, tq, LANES),
                                  lambda b, g, st, qi, kt, fi, la: (b, g, qi[st], 0)),
                     pl.BlockSpec((tk, LANES), lambda b, g, st, qi, kt, fi, la: (kt[st], 0))]
        args += [selb, onehot]
        scratch.append(pltpu.VMEM((nh, tq, 2 * LANES), BF16))
    if gate is not None:
        misc, glog_blk, expand, branch = gate
        in_specs += [pl.BlockSpec((tq, LANES),
                                  lambda b, g, st, qi, kt, fi, la: (b * nq + qi[st], glog_blk)),
                     pl.BlockSpec((None, nh // 2, LANES, LANES),
                                  lambda b, g, st, qi, kt, fi, la: (branch, g, 0, 0))]
        args += [misc, expand]
    kern = functools.partial(_flash_kernel, mode=mode, tq=tq, tk=tk, rc=rc,
                             kv_shared=kv_shared, gated=gate is not None)
    ow = nh * HALF
    grid_spec = pltpu.PrefetchScalarGridSpec(
        num_scalar_prefetch=4, grid=(batch, ngroups, nsteps), in_specs=in_specs,
        out_specs=pl.BlockSpec((tq, ow), lambda b, g, st, qi, kt, fi, la: (b * nq + qi[st], g)),
        scratch_shapes=scratch)
    return pl.pallas_call(
        kern,
        out_shape=jax.ShapeDtypeStruct((batch * seq, ngroups * ow), out_dtype),
        grid_spec=grid_spec,
        compiler_params=_cparams("parallel", "parallel", "arbitrary"),
        name=name,
    )(*tabs, *args)


def _mix_kernel(oc_ref, os_ref, ow_ref, ob_ref, ga_ref, gb_ref, wa_ref, wb_ref, o_ref):
    oa = (oc_ref[...].astype(F32) + os_ref[...].astype(F32) + ow_ref[...].astype(F32)).astype(BF16)
    ya = jnp.dot(oa, wa_ref[...], preferred_element_type=F32)
    yb = jnp.dot(ob_ref[...], wb_ref[...], preferred_element_type=F32)
    o_ref[...] = (ga_ref[...].astype(F32) * ya + gb_ref[...].astype(F32) * yb).astype(o_ref.dtype)


def gated_mix(oc, osel, ow, ob, gates, wa, wb, *, tm, tn):
    n, ka = oc.shape
    d = wa.shape[1]
    nb = d // tn
    row = lambda w: pl.BlockSpec((tm, w), lambda i, j: (i, 0))
    return pl.pallas_call(
        _mix_kernel,
        out_shape=jax.ShapeDtypeStruct((n, d), BF16),
        grid=(n // tm, nb),
        in_specs=[row(ka), row(ka), row(ka), row(ob.shape[1]),
                  pl.BlockSpec((tm, tn), lambda i, j: (i, j)),
                  pl.BlockSpec((tm, tn), lambda i, j: (i, nb + j)),
                  pl.BlockSpec((ka, tn), lambda i, j: (0, j)),
                  pl.BlockSpec((wb.shape[0], tn), lambda i, j: (0, j))],
        out_specs=pl.BlockSpec((tm, tn), lambda i, j: (i, j)),
        compiler_params=_cparams("parallel", "arbitrary"),
        name="gated_mix",
    )(oc, osel, ow, ob, gates, gates, wa, wb)


def _xattn_kernel(h_ref, g_ref, wq_ref, kv_ref, wo_ref, o_ref):
    h = h_ref[...]
    hn = _rms(h, g_ref[...]).astype(BF16)
    q = jnp.dot(hn, wq_ref[...], preferred_element_type=F32).astype(BF16)
    kv = kv_ref[...]
    outs = []
    for hd in range(XA_HEADS):
        qh = q[:, hd * XA_DIM:(hd + 1) * XA_DIM]
        kh = kv[:, hd * XA_DIM:(hd + 1) * XA_DIM]
        vh = kv[:, (XA_HEADS + hd) * XA_DIM:(XA_HEADS + hd + 1) * XA_DIM]
        s = _dot_nt(qh, kh) * (XA_DIM ** -0.5)
        e = jnp.exp(s - jnp.max(s, axis=-1, keepdims=True))
        p = e / jnp.sum(e, axis=-1, keepdims=True)
        outs.append(jnp.dot(p.astype(BF16), vh, preferred_element_type=F32))
    o = jnp.concatenate(outs, axis=1).astype(BF16)
    o_ref[...] = h + jnp.dot(o, wo_ref[...], preferred_element_type=F32)


def memory_xattn(h, g, wq, kv, wo, *, seq, tm):
    n, d = h.shape
    s_t = seq // tm
    dq = wq.shape[1]
    return pl.pallas_call(
        _xattn_kernel,
        out_shape=jax.ShapeDtypeStruct((n, d), F32),
        grid=(n // tm,),
        in_specs=[pl.BlockSpec((tm, d), lambda i: (i, 0)),
                  pl.BlockSpec((1, d), lambda i: (0, 0)),
                  pl.BlockSpec((d, dq), lambda i: (0, 0)),
                  pl.BlockSpec((MEM_LEN, 2 * dq), lambda i: (i // s_t, 0)),
                  pl.BlockSpec((dq, d), lambda i: (0, 0))],
        out_specs=pl.BlockSpec((tm, d), lambda i: (i, 0)),
        compiler_params=_cparams("parallel"),
        name="memory_xattn",
    )(h, g.reshape(1, d), wq, kv, wo)


def _mlp_kernel(h_ref, g_ref, w1_ref, w2_ref, gf_ref, o_ref, hn_sc, acc_sc, *, nf):
    f = pl.program_id(1)

    @pl.when(f == 0)
    def _init():
        hn_sc[...] = _rms(h_ref[...], g_ref[...]).astype(BF16)
        acc_sc[...] = jnp.zeros(acc_sc.shape, F32)

    u = jnp.dot(hn_sc[...], w1_ref[...], preferred_element_type=F32)
    u = jnp.square(jnp.maximum(u, 0.0))
    acc_sc[...] += jnp.dot(u.astype(BF16), w2_ref[...], preferred_element_type=F32)

    @pl.when(f == nf - 1)
    def _finish():
        o_ref[...] = _rms(h_ref[...] + acc_sc[...], gf_ref[...])


def mlp_final(h, g, w1, w2, gf, *, tm, tf):
    n, d = h.shape
    dff = w1.shape[1]
    nf = dff // tf
    return pl.pallas_call(
        functools.partial(_mlp_kernel, nf=nf),
        out_shape=jax.ShapeDtypeStruct((n, d), F32),
        grid=(n // tm, nf),
        in_specs=[pl.BlockSpec((tm, d), lambda i, f: (i, 0)),
                  pl.BlockSpec((1, d), lambda i, f: (0, 0)),
                  pl.BlockSpec((d, tf), lambda i, f: (0, f)),
                  pl.BlockSpec((tf, d), lambda i, f: (f, 0)),
                  pl.BlockSpec((1, d), lambda i, f: (0, 0))],
        out_specs=pl.BlockSpec((tm, d), lambda i, f: (i, 0)),
        scratch_shapes=[pltpu.VMEM((tm, d), BF16), pltpu.VMEM((tm, d), F32)],
        compiler_params=_cparams("parallel", "arbitrary"),
        name="mlp_final",
    )(h, g.reshape(1, d), w1, w2, gf.reshape(1, d))


def _rot_partner(w, half):
    return jnp.concatenate([-w[..., half:], w[..., :half]], axis=-1)


def _pad_heads_rope(w, heads, hd, rot0, rot_dim):
    k = w.shape[0]
    w = w.reshape(k, heads, hd)
    partner = _rot_partner(w[:, :, rot0:rot0 + rot_dim], rot_dim // 2)
    pad = jnp.zeros((k, heads, LANES - hd - rot_dim), w.dtype)
    return jnp.concatenate([w, partner, pad], axis=-1).reshape(k, heads * LANES)


def _rope_lane_tables(seq, rot0, rot_dim, hd):
    inv = 1.0 / (ROPE_THETA ** (jnp.arange(0, rot_dim, 2, dtype=F32) / rot_dim))
    ang = jnp.arange(seq, dtype=F32)[:, None] * inv[None, :]
    cos = jnp.concatenate([jnp.cos(ang), jnp.cos(ang)], axis=1)
    sin = jnp.concatenate([jnp.sin(ang), jnp.sin(ang)], axis=1)
    cos_t = jnp.concatenate([jnp.ones((seq, rot0), F32), cos,
                             jnp.ones((seq, hd - rot0 - rot_dim), F32),
                             jnp.zeros((seq, LANES - hd), F32)], axis=1)
    sin_t = jnp.concatenate([jnp.zeros((seq, rot0), F32), sin,
                             jnp.zeros((seq, LANES - rot0 - rot_dim), F32)], axis=1)
    return cos_t, sin_t


def _pad_lanes(w, groups, width):
    k = w.shape[0]
    w = w.reshape(k, groups, width)
    return jnp.pad(w, ((0, 0), (0, 0), (0, LANES - width))).reshape(k, groups * LANES)


def _compress_params(pos, w1, w2, d):
    half = CMP_BLOCK // 2
    pos_p = jnp.pad(pos, ((0, 0), (0, LANES - d)))
    pos_a = pos_p[:half].reshape(1, half * LANES)
    pos_b = pos_p[half:].reshape(1, half * LANES)
    w1p = jnp.pad(w1.reshape(CMP_BLOCK, d, d), ((0, 0), (0, LANES - d), (0, LANES - d)))
    w1a = w1p[:half].reshape(half * LANES, LANES).astype(BF16)
    w1b = w1p[half:].reshape(half * LANES, LANES).astype(BF16)
    w2p = jnp.pad(w2, ((0, LANES - d), (0, LANES - d)))
    return pos_a, pos_b, w1a, w1b, w2p.astype(BF16)


def _cmp_to_sel(nr, nsb):
    cs = np.arange(nr) * CMP_STRIDE
    ce = cs + CMP_BLOCK
    ss = np.arange(LANES) * SEL_BLOCK
    se = ss + SEL_BLOCK
    ov = np.clip(np.minimum(ce[:, None], se[None, :]) - np.maximum(cs[:, None], ss[None, :]), 0, None)
    ov = ov.astype(np.float32) / np.float32(CMP_BLOCK)
    ov[:, nsb:] = 0.0
    ov[nr - 1:, :] = 0.0
    return jnp.asarray(ov, BF16)


def _gate_expand():
    e = np.zeros((3, NSA_HEADS // 2, LANES, LANES), np.float32)
    for br in range(3):
        for hp in range(NSA_HEADS // 2):
            for hh in range(2):
                e[br, hp, 3 * (2 * hp + hh) + br, hh * HALF:(hh + 1) * HALF] = 1.0
    return jnp.asarray(e, BF16)


def _key_block_onehot(seq):
    e = (np.arange(seq)[:, None] // SEL_BLOCK) == np.arange(LANES)[None, :]
    return jnp.asarray(e.astype(np.float32), BF16)


def kernel(x, mem, g_mix, w_in, cmp_pos_k, cmp_w1_k, cmp_w2_k, cmp_pos_v, cmp_w1_v, cmp_w2_v,
           mla_g_q, mla_w_uq, mla_g_kv, mla_w_uk, mla_w_uv, w_o_nsa, w_o_mla, w_out,
           g_xattn, g_mem, xa_wq, xa_wkv, xa_wo, g_mlp, w_ff1, w_ff2, g_final):
    b, s, d = x.shape
    assert d == D_MODEL and s % (CMP_STRIDE * 8) == 0 and s // SEL_BLOCK <= LANES
    assert g_mix.shape[0] == 1
    n = b * s
    T = _tiles(s)
    tm, tq, tk, rc = T["tm"], T["tq"], T["tk"], T["rc"]
    G = NSA_KV_HEADS
    bounds = [int(v) for v in np.cumsum(SPLITS)[:-1]]

    (w_qa, w_kc, w_vc, w_ks, w_vs, w_kw, w_vw, w_gn, w_cq, w_ckv, w_kr,
     w_ga, w_gb) = jnp.split(w_in[0], bounds, axis=1)
    nsa_rope = lambda w, heads: _pad_heads_rope(w, heads, NSA_DK, 0, NSA_ROT)
    w_rope = jnp.concatenate([nsa_rope(w_qa, NSA_HEADS), nsa_rope(w_kc, G), nsa_rope(w_ks, G),
                              nsa_rope(w_kw, G)], axis=1).astype(BF16)
    w_krp = jnp.concatenate([jnp.zeros((d, MLA_NOPE), F32), w_kr, _rot_partner(w_kr, MLA_ROPE // 2)],
                            axis=1).astype(BF16)
    w_vsw = jnp.concatenate([_pad_lanes(w_vs, G, NSA_DV), _pad_lanes(w_vw, G, NSA_DV)], axis=1).astype(BF16)
    w_vcp = _pad_lanes(w_vc, G, NSA_DV).astype(BF16)
    w_misc = jnp.concatenate([w_cq, w_ckv, jnp.pad(w_gn, ((0, 0), (0, LANES - w_gn.shape[1])))],
                             axis=1).astype(BF16)
    glog_blk = (MLA_Q_RANK + MLA_KV_RANK) // LANES
    w_gates = jnp.concatenate([w_ga, w_gb], axis=1).astype(BF16)
    cos_a, sin_a = _rope_lane_tables(s, 0, NSA_ROT, NSA_DK)
    cos_b, sin_b = _rope_lane_tables(s, MLA_NOPE, MLA_ROPE, MLA_NOPE + MLA_ROPE)
    shift_a = LANES - NSA_DK
    shift_b = LANES - (NSA_DK - MLA_NOPE)
    w_uq = _pad_heads_rope(mla_w_uq[0], MLA_HEADS, MLA_NOPE + MLA_ROPE, MLA_NOPE, MLA_ROPE).astype(BF16)
    w_uk = _pad_lanes(mla_w_uk[0], MLA_HEADS, MLA_NOPE).astype(BF16)
    w_uv = _pad_lanes(mla_w_uv[0], MLA_HEADS, MLA_DV).astype(BF16)

    x2 = x.reshape(n, d)
    xn = rmsnorm_rows(x2, g_mix[0], tm)
    log2e = float(np.log2(np.e))
    tmb = T["tm_big"]
    qk = projection(xn, w_rope, tm=tmb, tn=512, out_dtype=BF16, rope=(cos_a, sin_a, shift_a),
                    out_scale=(NSA_DK ** -0.5 * log2e, NSA_HEADS * LANES // 512),
                    head_major=(b, s), name="proj_qk_rope")
    krp = projection(xn, w_krp, tm=tm, tn=LANES, out_dtype=BF16, rope=(cos_b, sin_b, shift_b),
                     name="proj_k_rope_mla")
    vsw = projection(xn, w_vsw, tm=tmb, tn=512, out_dtype=BF16, head_major=(b, s), name="proj_v")
    vcp = projection(xn, w_vcp, tm=tmb, tn=512, out_dtype=BF16, head_major=(b, s), name="proj_vc")
    misc = projection(xn, w_misc, tm=tm, tn=w_misc.shape[1], out_dtype=F32, name="proj_misc")
    gates = projection(xn, w_gates, tm=tmb, tn=512, out_dtype=BF16, act="sigmoid", name="proj_gates")

    nr = s // CMP_STRIDE
    pk = _compress_params(cmp_pos_k[0], cmp_w1_k[0], cmp_w2_k[0], NSA_DK)
    pv = _compress_params(cmp_pos_v[0], cmp_w1_v[0], cmp_w2_v[0], NSA_DV)
    kc_rows = qk[:, QK_KC0:QK_KC0 + G].reshape(b, G, nr, CMP_STRIDE * LANES)
    kcc = compress(kc_rows, 0, *pk)
    vcc = compress(vcp.reshape(b, G, nr, CMP_STRIDE * LANES), 0, *pv)
    expand = _gate_expand()
    o_c, selb = nsa_compressed(qk, kcc, vcc, _cmp_to_sel(nr, s // SEL_BLOCK), misc, glog_blk, expand,
                               batch=b, seq=s, tq=T["tq_cmp"], rc=rc)
    o_s = flash_heads(qk, qk, vsw, mode="select", batch=b, seq=s, tq=tq, tk=tk, rc=rc,
                      q_head0=QK_Q0, k_head0=QK_KS0, v_head0=0, kv_shared=True, out_dtype=BF16,
                      selb=selb, onehot=_key_block_onehot(s), gate=(misc, glog_blk, expand, 1),
                      name="nsa_selected")
    o_w = flash_heads(qk, qk, vsw, mode="window", batch=b, seq=s, tq=T["tq_win"], tk=tk, rc=rc,
                      q_head0=QK_Q0, k_head0=QK_KW0, v_head0=G, kv_shared=True, out_dtype=BF16,
                      gate=(misc, glog_blk, expand, 2), name="nsa_window")

    qm = projection(misc, w_uq, tm=tm, tn=512, out_dtype=BF16, a_cols=(MLA_Q_RANK, 0), gain=mla_g_q[0],
                    rope=(cos_b, sin_b, shift_b), head_major=(b, s), name="mla_q",
                    out_scale=((MLA_NOPE + MLA_ROPE) ** -0.5 * log2e, MLA_HEADS * LANES // 512))
    km = projection(misc, w_uk, tm=tm, tn=512, out_dtype=BF16,
                    a_cols=(MLA_KV_RANK, MLA_Q_RANK // MLA_KV_RANK), gain=mla_g_kv[0], add=krp,
                    head_major=(b, s), name="mla_k")
    vm = projection(misc, w_uv, tm=tm, tn=512, out_dtype=BF16,
                    a_cols=(MLA_KV_RANK, MLA_Q_RANK // MLA_KV_RANK), gain=mla_g_kv[0],
                    head_major=(b, s), name="mla_v")
    o_b = flash_heads(qm, km, vm, mode="causal", batch=b, seq=s, tq=tq, tk=tk, rc=rc,
                      q_head0=0, k_head0=0, v_head0=0,
                      kv_shared=False, out_dtype=BF16, name="mla_attention")

    mixed = gated_mix(o_c, o_s, o_w, o_b, gates, w_o_nsa[0].astype(BF16), w_o_mla[0].astype(BF16),
                      tm=tm, tn=512)
    h1 = projection(mixed, w_out[0].astype(BF16), tm=tmb, tn=512, out_dtype=F32, res=x2, name="proj_out")

    kv_mem = projection(mem.reshape(b * MEM_LEN, d), xa_wkv[0].astype(BF16), tm=MEM_LEN, tn=512,
                        out_dtype=BF16, gain=g_mem[0], name="xattn_kv")
    h2 = memory_xattn(h1, g_xattn[0], xa_wq[0].astype(BF16), kv_mem, xa_wo[0].astype(BF16), seq=s,
                      tm=T["tm_x"])

    out = mlp_final(h2, g_mlp[0], w_ff1[0].astype(BF16), w_ff2[0].astype(BF16), g_final,
                    tm=T["tm_mlp"], tf=T["tf"])
    return out.reshape(b, s, d)
```

```python
import functools

import numpy as np
import jax
import jax.numpy as jnp
from jax import lax
from jax.experimental import pallas as pl
from jax.experimental.pallas import tpu as pltpu

F32 = jnp.float32
BF16 = jnp.bfloat16

D_MODEL = 2048
MEM_LEN = 256
ROPE_THETA = 500000.0
EPS = 1e-6
NEG = -1e30

NSA_HEADS = 16
NSA_KV_HEADS = 4
NSA_HPG = NSA_HEADS // NSA_KV_HEADS
NSA_DK = 96
NSA_DV = 64
NSA_ROT = NSA_DK // 4
CMP_BLOCK = 32
CMP_STRIDE = 16
SEL_BLOCK = 64
SEL_TOPK = 16
N_FORCED = 3
WINDOW = 512

MLA_HEADS = 16
MLA_NOPE = 64
MLA_ROPE = 32
MLA_DV = 64
MLA_Q_RANK = 512
MLA_KV_RANK = 256

XA_HEADS = 4
XA_DIM = 128
D_FF = 4 * D_MODEL

SPLITS = (NSA_HEADS * NSA_DK,
          NSA_KV_HEADS * NSA_DK, NSA_KV_HEADS * NSA_DV,
          NSA_KV_HEADS * NSA_DK, NSA_KV_HEADS * NSA_DV,
          NSA_KV_HEADS * NSA_DK, NSA_KV_HEADS * NSA_DV,
          NSA_HEADS * 3,
          MLA_Q_RANK, MLA_KV_RANK, MLA_ROPE,
          D_MODEL, D_MODEL)

LANES = 128
HALF = LANES // 2
VMEM_LIMIT = 56 * 1024 * 1024

QK_Q0 = 0
QK_KC0 = NSA_HEADS
QK_KS0 = NSA_HEADS + NSA_KV_HEADS
QK_KW0 = NSA_HEADS + 2 * NSA_KV_HEADS
QK_HEADS = NSA_HEADS + 3 * NSA_KV_HEADS
FLASH_HEADS = NSA_HPG


def _cparams(*sem):
    return pltpu.CompilerParams(dimension_semantics=sem, vmem_limit_bytes=VMEM_LIMIT)


def _tiles(seq):
    return dict(
        tm=min(1024, seq),
        tm_big=min(2048, seq),
        tm_x=min(512, seq),
        tm_mlp=min(512, seq),
        tq=min(2048, seq),
        tq_win=min(512, seq),
        tk=min(512, seq),
        rc=32,
        tq_cmp=min(256, seq),
        tf=512,
    )


def _rms(x, g):
    return x * lax.rsqrt(jnp.mean(x * x, axis=-1, keepdims=True) + EPS) * g


def _rmsnorm_kernel(x_ref, g_ref, o_ref):
    o_ref[...] = _rms(x_ref[...], g_ref[...]).astype(o_ref.dtype)


def rmsnorm_rows(x, g, tm):
    n, d = x.shape
    return pl.pallas_call(
        _rmsnorm_kernel,
        out_shape=jax.ShapeDtypeStruct((n, d), BF16),
        grid=(n // tm,),
        in_specs=[pl.BlockSpec((tm, d), lambda i: (i, 0)),
                  pl.BlockSpec((1, d), lambda i: (0, 0))],
        out_specs=pl.BlockSpec((tm, d), lambda i: (i, 0)),
        compiler_params=_cparams("parallel"),
        name="rmsnorm_rows",
    )(x, g.reshape(1, d))


def _proj_kernel(*refs, has_gain, has_rope, roll_shift, out_scale, has_add, has_res, act, head_major):
    it = iter(refs)
    a_ref = next(it)
    g_ref = next(it) if has_gain else None
    w_ref = next(it)
    cos_ref = next(it) if has_rope else None
    sin_ref = next(it) if has_rope else None
    add_ref = next(it) if has_add else None
    res_ref = next(it) if has_res else None
    o_ref = next(it)

    a = a_ref[...]
    if has_gain:
        a = _rms(a, g_ref[...]).astype(BF16)
    y = jnp.dot(a, w_ref[...], preferred_element_type=F32)
    if out_scale is not None:
        factor, n_tiles = out_scale
        y = y * jnp.where(pl.program_id(1) < n_tiles, factor, 1.0)
    if has_rope or has_add or head_major:
        for h in range(y.shape[1] // LANES):
            yh = y[:, h * LANES:(h + 1) * LANES]
            if has_rope:
                yh = yh * cos_ref[...] + pltpu.roll(yh, roll_shift, 1) * sin_ref[...]
            if has_add:
                yh = yh + add_ref[...].astype(F32)
            if head_major:
                o_ref[h] = yh.astype(o_ref.dtype)
            else:
                o_ref[:, h * LANES:(h + 1) * LANES] = yh.astype(o_ref.dtype)
    else:
        if act == "sigmoid":
            y = jax.nn.sigmoid(y)
        if has_res:
            y = res_ref[...] + y
        o_ref[...] = y.astype(o_ref.dtype)


def projection(a, w, *, tm, tn, out_dtype, a_cols=None, gain=None, rope=None, out_scale=None, add=None,
               res=None, act=None, head_major=None, name="projection"):
    n = a.shape[0]
    k, nc = w.shape
    a_w, a_blk = a_cols if a_cols is not None else (a.shape[1], 0)
    assert a_w == k and n % tm == 0 and nc % tn == 0
    grid = (n // tm, nc // tn)
    in_specs = [pl.BlockSpec((tm, k), lambda i, j: (i, a_blk))]
    args = [a]
    if gain is not None:
        in_specs.append(pl.BlockSpec((1, k), lambda i, j: (0, 0)))
        args.append(gain.reshape(1, k))
    in_specs.append(pl.BlockSpec((k, tn), lambda i, j: (0, j)))
    args.append(w)
    roll_shift = 0
    if rope is not None:
        cos, sin, roll_shift = rope
        s_tiles = cos.shape[0] // tm
        for t in (cos, sin):
            in_specs.append(pl.BlockSpec((tm, LANES), lambda i, j: (i % s_tiles, 0)))
            args.append(t)
    if add is not None:
        in_specs.append(pl.BlockSpec((tm, LANES), lambda i, j: (i, 0)))
        args.append(add)
    if res is not None:
        in_specs.append(pl.BlockSpec((tm, tn), lambda i, j: (i, j)))
        args.append(res)
    if head_major is not None:
        b, s = head_major
        s_t = s // tm
        hpt = tn // LANES
        out_shape = jax.ShapeDtypeStruct((b, nc // LANES, s, LANES), out_dtype)
        out_spec = pl.BlockSpec((None, hpt, tm, LANES), lambda i, j: (i // s_t, j, i % s_t, 0))
    else:
        out_shape = jax.ShapeDtypeStruct((n, nc), out_dtype)
        out_spec = pl.BlockSpec((tm, tn), lambda i, j: (i, j))
    kern = functools.partial(
        _proj_kernel, has_gain=gain is not None, has_rope=rope is not None, roll_shift=roll_shift,
        out_scale=out_scale, has_add=add is not None, has_res=res is not None, act=act,
        head_major=head_major is not None)
    return pl.pallas_call(
        kern, out_shape=out_shape, grid=grid, in_specs=in_specs, out_specs=out_spec,
        compiler_params=_cparams("parallel", "arbitrary"), name=name,
    )(*args)


def _compress_kernel(r_ref, pa_ref, pb_ref, w1a_ref, w1b_ref, w2_ref, o_ref):
    r = r_ref[...].astype(F32)
    a = jnp.dot((r + pa_ref[...]).astype(BF16), w1a_ref[...], preferred_element_type=F32)
    b = jnp.dot((r + pb_ref[...]).astype(BF16), w1b_ref[...], preferred_element_type=F32)
    nr = a.shape[0]
    hid = a + pltpu.roll(b, nr - 1, 0)
    hid = jax.nn.gelu(hid)
    o_ref[...] = jnp.dot(hid.astype(BF16), w2_ref[...], preferred_element_type=F32).astype(o_ref.dtype)


def compress(r, head0, pos_a, pos_b, w1a, w1b, w2):
    b, _, nr, kk = r.shape
    g = NSA_KV_HEADS
    full = lambda shape: pl.BlockSpec(shape, lambda bi, gi: (0,) * len(shape))
    return pl.pallas_call(
        _compress_kernel,
        out_shape=jax.ShapeDtypeStruct((b, g, nr, LANES), BF16),
        grid=(b, g),
        in_specs=[pl.BlockSpec((None, None, nr, kk), lambda bi, gi: (bi, head0 + gi, 0, 0)),
                  full((1, kk)), full((1, kk)), full((kk, LANES)), full((kk, LANES)),
                  full((LANES, LANES))],
        out_specs=pl.BlockSpec((None, None, nr, LANES), lambda bi, gi: (bi, gi, 0, 0)),
        compiler_params=_cparams("parallel", "parallel"),
        name="nsa_compress",
    )(r, pos_a, pos_b, w1a, w1b, w2)


def _dot_nt(a, b):
    return lax.dot_general(a, b, (((1,), (1,)), ((), ())), preferred_element_type=F32)


def _split_dot(x, w):
    hi = x.astype(BF16)
    lo = (x - hi.astype(F32)).astype(BF16)
    return (jnp.dot(hi, w, preferred_element_type=F32) + jnp.dot(lo, w, preferred_element_type=F32))


def _nsa_cmp_kernel(q_ref, kcc_ref, vcc_ref, msel_ref, glog_ref, e_ref, oc_ref, sel_ref, s_sc, p_sc, imp_sc,
                    *, tq, rc, topk):
    s0 = pl.program_id(2) * tq
    nr = kcc_ref.shape[0]

    def attend(ncol):
        kcc = kcc_ref[0:ncol, :]
        for h in range(NSA_HPG):
            s_sc[h, :, 0:ncol] = _dot_nt(q_ref[h], kcc)
        lane_v = lax.broadcasted_iota(jnp.int32, (ncol, LANES), 1)
        vcc = vcc_ref[0:ncol, :]
        vm = jnp.concatenate([jnp.where(lane_v == HALF, jnp.ones_like(vcc), vcc), msel_ref[0:ncol, :]],
                             axis=1)
        dmat = (lax.broadcasted_iota(jnp.int32, (rc, ncol), 0)
                - lax.broadcasted_iota(jnp.int32, (rc, ncol), 1) * CMP_STRIDE)
        visible = lambda c: dmat + (s0 + c * rc - (CMP_BLOCK - 1)) >= 0
        outs = []
        imp = jnp.zeros((tq, LANES), F32)
        for h in range(NSA_HPG):
            row_max = []
            for c in range(tq // rc):
                s = jnp.where(visible(c), s_sc[h, pl.ds(c * rc, rc), 0:ncol], NEG)
                row_max.append(jnp.max(s, axis=-1, keepdims=True))
            for c in range(tq // rc):
                rows = pl.ds(c * rc, rc)
                p = jnp.exp2(s_sc[h, rows, 0:ncol] - row_max[c])
                p_sc[h, rows, 0:ncol] = jnp.where(visible(c), p, 0.0).astype(BF16)
            a = jnp.dot(p_sc[h, :, 0:ncol], vm, preferred_element_type=F32)
            denom = jnp.maximum(a[:, HALF:HALF + 1], 1e-30)
            outs.append(a[:, :LANES] / denom)
            imp = imp + a[:, LANES:] / denom
        imp_sc[...] = imp
        glog = glog_ref[...]
        lane = lax.broadcasted_iota(jnp.int32, (tq, LANES), 1)
        for pr in range(NSA_HPG // 2):
            gate = jax.nn.sigmoid(_split_dot(glog, e_ref[pr]))
            pair = jnp.where(lane < HALF, outs[2 * pr], pltpu.roll(outs[2 * pr + 1], HALF, 1))
            oc_ref[:, pr * LANES:(pr + 1) * LANES] = (gate * pair).astype(oc_ref.dtype)

    ntiles = (s0 + (tq - CMP_BLOCK)) // (CMP_STRIDE * LANES) + 1
    for k in range(1, nr // LANES + 1):
        @pl.when(ntiles == k)
        def _width(k=k):
            attend(k * LANES)

    blk = lax.broadcasted_iota(jnp.int32, (tq, LANES), 1)
    cur = (s0 + lax.broadcasted_iota(jnp.int32, (tq, LANES), 0)) // SEL_BLOCK
    valid = blk <= cur
    forced = (blk == 0) | (blk == cur) | (blk == cur - 1)
    score = jnp.where(valid, jnp.where(forced, -jnp.inf, imp_sc[...]), -1.0)
    sc = score.T
    rowid = lax.broadcasted_iota(jnp.int32, (LANES, tq), 0).astype(F32)

    def pick_one(_, sc):
        cm = jnp.max(sc, axis=0, keepdims=True)
        first = jnp.min(jnp.where(sc == cm, rowid, float(LANES)), axis=0, keepdims=True)
        return jnp.where(rowid == first, -jnp.inf, sc)

    picked = lax.fori_loop(0, topk - N_FORCED, pick_one, sc) == -jnp.inf
    sel_ref[...] = jnp.where(picked, 0.0, NEG).T.astype(sel_ref.dtype)


def nsa_compressed(qk, kcc, vcc, msel, misc, glog_blk, expand, *, batch, seq, tq, rc):
    g = NSA_KV_HEADS
    nq = seq // tq
    nr = kcc.shape[2]
    topk = min(SEL_TOPK, seq // SEL_BLOCK)
    assert topk > N_FORCED and nr % LANES == 0
    kern = functools.partial(_nsa_cmp_kernel, tq=tq, rc=rc, topk=topk)
    return pl.pallas_call(
        kern,
        out_shape=(jax.ShapeDtypeStruct((batch * seq, NSA_HEADS * NSA_DV), BF16),
                   jax.ShapeDtypeStruct((batch, g, seq, LANES), BF16)),
        grid=(batch, g, nq),
        in_specs=[
            pl.BlockSpec((None, NSA_HPG, tq, LANES), lambda b, gi, qi: (b, gi, qi, 0)),
            pl.BlockSpec((None, None, nr, LANES), lambda b, gi, qi: (b, gi, 0, 0)),
            pl.BlockSpec((None, None, nr, LANES), lambda b, gi, qi: (b, gi, 0, 0)),
            pl.BlockSpec((nr, LANES), lambda b, gi, qi: (0, 0)),
            pl.BlockSpec((tq, LANES), lambda b, gi, qi: (b * nq + qi, glog_blk)),
            pl.BlockSpec((None, 2, LANES, LANES), lambda b, gi, qi: (0, gi, 0, 0)),
        ],
        out_specs=(pl.BlockSpec((tq, 2 * LANES), lambda b, gi, qi: (b * nq + qi, gi)),
                   pl.BlockSpec((None, None, tq, LANES), lambda b, gi, qi: (b, gi, qi, 0))),
        scratch_shapes=[pltpu.VMEM((NSA_HPG, tq, nr), F32), pltpu.VMEM((NSA_HPG, tq, nr), BF16),
                        pltpu.VMEM((tq, LANES), F32)],
        compiler_params=_cparams("parallel", "parallel", "parallel"),
        name="nsa_compressed_select",
    )(qk, kcc, vcc, msel, misc, expand)


def _flash_steps(mode, nq, tq, tk):
    r = tq // tk
    qi_l, kt_l, first_l, last_l = [], [], [], []
    for qi in range(nq):
        hi = qi * r + r - 1
        lo = max(0, qi * r - (-(-(WINDOW - 1) // tk))) if mode == "window" else 0
        for kt in range(lo, hi + 1):
            qi_l.append(qi)
            kt_l.append(kt)
            first_l.append(int(kt == lo))
            last_l.append(int(kt == hi))
    return tuple(jnp.asarray(np.asarray(a, np.int32)) for a in (qi_l, kt_l, first_l, last_l))


def _flash_kernel(*refs, mode, tq, tk, rc, kv_shared, gated):
    it = iter(refs)
    qi_ref, kt_ref, first_ref, last_ref = next(it), next(it), next(it), next(it)
    q_ref, k_ref, v_ref = next(it), next(it), next(it)
    selb_ref = next(it) if mode == "select" else None
    oh_ref = next(it) if mode == "select" else None
    glog_ref = next(it) if gated else None
    e_ref = next(it) if gated else None
    o_ref = next(it)
    m_sc, acc_sc, s_sc, p_sc = next(it), next(it), next(it), next(it)
    qa_sc = next(it) if mode == "select" else None

    st = pl.program_id(2)
    s0 = qi_ref[st] * tq
    k0 = kt_ref[st] * tk

    @pl.when(first_ref[st] == 1)
    def _init():
        m_sc[...] = jnp.full(m_sc.shape, NEG, F32)
        acc_sc[...] = jnp.zeros(acc_sc.shape, F32)
        if mode == "select":
            for hh in range(FLASH_HEADS):
                qa_sc[hh] = jnp.concatenate([q_ref[hh], selb_ref[...]], axis=1)

    def chunk_cols(off, r0):
        if off is None:
            return 0, tk, True
        r1 = r0 + rc
        hi = min(tk, r1 - off)
        lo = max(0, r0 - off - WINDOW + 1) if mode == "window" else 0
        lo, hi = lo // LANES * LANES, -(-hi // LANES) * LANES
        clear = tk - 1 <= r0 - off and (mode != "window" or r1 - 1 - off < WINDOW)
        return lo, hi, clear

    def tile(off):
        nh = FLASH_HEADS
        r_lo = max(0, off) if off is not None else 0
        nrow = tq - r_lo
        if kv_shared:
            if mode == "select":
                qs = qa_sc[:, r_lo:, :].reshape(nh * nrow, 2 * LANES)
                ks = jnp.concatenate([k_ref[0], oh_ref[...]], axis=1)
            else:
                qs, ks = q_ref[:, r_lo:, :].reshape(nh * nrow, LANES), k_ref[0]
            s_sc[:, r_lo:, :] = _dot_nt(qs, ks).reshape(nh, nrow, tk)
        else:
            for hh in range(nh):
                s_sc[hh, r_lo:, :] = _dot_nt(q_ref[hh, r_lo:, :], k_ref[hh])
        lane_v = lax.broadcasted_iota(jnp.int32, (tk, LANES), 1)
        if off is not None:
            dmat = (lax.broadcasted_iota(jnp.int32, (rc, tk), 0)
                    - lax.broadcasted_iota(jnp.int32, (rc, tk), 1))
        for hh in range(FLASH_HEADS):
            v = v_ref[0 if kv_shared else hh]
            v_aug = jnp.where(lane_v == HALF, jnp.ones_like(v), v)

            for c in range(r_lo // rc, tq // rc):
                r0 = c * rc
                rows = pl.ds(r0, rc)
                lo, hi, clear = chunk_cols(off, r0)
                if lo > 0:
                    p_sc[hh, rows, 0:lo] = jnp.zeros((rc, lo), BF16)
                if hi < tk:
                    p_sc[hh, rows, max(hi, 0):tk] = jnp.zeros((rc, tk - max(hi, 0)), BF16)
                if hi <= lo:
                    continue
                s = s_sc[hh, rows, lo:hi]
                if not clear:
                    d = dmat[:, lo:hi] + (r0 - off)
                    msk = (d >= 0) & (d < WINDOW) if mode == "window" else d >= 0
                    s = jnp.where(msk, s, NEG)
                m_old = m_sc[hh, rows, :]
                m_new = jnp.maximum(m_old, jnp.max(s, axis=-1, keepdims=True))
                m_sc[hh, rows, :] = m_new
                acc_sc[hh, rows, :] = acc_sc[hh, rows, :] * jnp.exp2(m_old - m_new)
                p = jnp.exp2(s_sc[hh, rows, lo:hi] - jnp.concatenate([m_new] * ((hi - lo) // LANES), axis=1))
                if not clear:
                    p = jnp.where(msk, p, 0.0)
                p_sc[hh, rows, lo:hi] = p.astype(BF16)
            if not kv_shared:
                acc_sc[hh, r_lo:, :] += jnp.dot(p_sc[hh, r_lo:, :], v_aug, preferred_element_type=F32)
        if kv_shared:
            pv = jnp.dot(p_sc[:, r_lo:, :].reshape(nh * nrow, tk), v_aug, preferred_element_type=F32)
            acc_sc[:, r_lo:, :] += pv.reshape(nh, nrow, LANES)

    if mode == "window":
        offsets = [(j - (-(-(WINDOW - 1) // tk))) * tk for j in range(-(-(WINDOW - 1) // tk) + tq // tk)]
    else:
        offsets = [j * tk for j in range(tq // tk)]

        @pl.when(k0 + tk - 1 <= s0)
        def _interior():
            tile(None)

    for off in offsets:
        @pl.when(k0 - s0 == off)
        def _partial(off=off):
            tile(off)

    @pl.when(last_ref[st] == 1)
    def _finish():
        lane = lax.broadcasted_iota(jnp.int32, (tq, LANES), 1)
        if gated:
            glog = glog_ref[...]
        for pr in range(FLASH_HEADS // 2):
            outs = []
            for hh in (2 * pr, 2 * pr + 1):
                a = acc_sc[hh]
                outs.append(a / jnp.maximum(a[:, HALF:HALF + 1], 1e-30))
            out = jnp.where(lane < HALF, outs[0], pltpu.roll(outs[1], HALF, 1))
            if gated:
                out = jax.nn.sigmoid(_split_dot(glog, e_ref[pr])) * out
            o_ref[:, pr * LANES:(pr + 1) * LANES] = out.astype(o_ref.dtype)


def flash_heads(q, k, v, *, mode, batch, seq, tq, tk, rc, q_head0, k_head0, v_head0,
                kv_shared, out_dtype, selb=None, onehot=None, gate=None, name="flash"):
    nh = FLASH_HEADS
    ngroups = NSA_HEADS // nh
    nq = seq // tq
    assert tq % tk == 0 and tq % rc == 0
    tabs = _flash_steps(mode, nq, tq, tk)
    nsteps = tabs[0].shape[0]
    if kv_shared:
        kv_spec = lambda h0: pl.BlockSpec(
            (None, 1, tk, LANES), lambda b, g, st, qi, kt, fi, la: (b, h0 + g, kt[st], 0))
    else:
        kv_spec = lambda h0: pl.BlockSpec(
            (None, nh, tk, LANES), lambda b, g, st, qi, kt, fi, la: (b, h0 // nh + g, kt[st], 0))
    in_specs = [pl.BlockSpec((None, nh, tq, LANES),
                             lambda b, g, st, qi, kt, fi, la: (b, q_head0 // nh + g, qi[st], 0)),
                kv_spec(k_head0), kv_spec(v_head0)]
    args = [q, k, v]
    scratch = [pltpu.VMEM((nh, tq, LANES), F32), pltpu.VMEM((nh, tq, LANES), F32),
               pltpu.VMEM((nh, tq, tk), F32), pltpu.VMEM((nh, tq, tk), BF16)]
    if mode == "select":
        in_specs += [pl.BlockSpec((None, None, tq, LANES),
                                  lambda b, g, st, qi, kt, fi, la: (b, g, qi[st], 0)),
                     pl.BlockSpec((tk, LANES), lambda b, g, st, qi, kt, fi, la: (kt[st], 0))]
        args += [selb, onehot]
        scratch.append(pltpu.VMEM((nh, tq, 2 * LANES), BF16))
    if gate is not None:
        misc, glog_blk, expand, branch = gate
        in_specs += [pl.BlockSpec((tq, LANES),
                                  lambda b, g, st, qi, kt, fi, la: (b * nq + qi[st], glog_blk)),
                     pl.BlockSpec((None, nh // 2, LANES, LANES),
                                  lambda b, g, st, qi, kt, fi, la: (branch, g, 0, 0))]
        args += [misc, expand]
    kern = functools.partial(_flash_kernel, mode=mode, tq=tq, tk=tk, rc=rc,
                             kv_shared=kv_shared, gated=gate is not None)
    ow = nh * HALF
    grid_spec = pltpu.PrefetchScalarGridSpec(
        num_scalar_prefetch=4, grid=(batch, ngroups, nsteps), in_specs=in_specs,
        out_specs=pl.BlockSpec((tq, ow), lambda b, g, st, qi, kt, fi, la: (b * nq + qi[st], g)),
        scratch_shapes=scratch)
    return pl.pallas_call(
        kern,
        out_shape=jax.ShapeDtypeStruct((batch * seq, ngroups * ow), out_dtype),
        grid_spec=grid_spec,
        compiler_params=_cparams("parallel", "parallel", "arbitrary"),
        name=name,
    )(*tabs, *args)


def _mix_kernel(oc_ref, os_ref, ow_ref, ob_ref, ga_ref, gb_ref, wa_ref, wb_ref, o_ref):
    oa = (oc_ref[...].astype(F32) + os_ref[...].astype(F32) + ow_ref[...].astype(F32)).astype(BF16)
    ya = jnp.dot(oa, wa_ref[...], preferred_element_type=F32)
    yb = jnp.dot(ob_ref[...], wb_ref[...], preferred_element_type=F32)
    o_ref[...] = (ga_ref[...].astype(F32) * ya + gb_ref[...].astype(F32) * yb).astype(o_ref.dtype)


def gated_mix(oc, osel, ow, ob, gates, wa, wb, *, tm, tn):
    n, ka = oc.shape
    d = wa.shape[1]
    nb = d // tn
    row = lambda w: pl.BlockSpec((tm, w), lambda i, j: (i, 0))
    return pl.pallas_call(
        _mix_kernel,
        out_shape=jax.ShapeDtypeStruct((n, d), BF16),
        grid=(n // tm, nb),
        in_specs=[row(ka), row(ka), row(ka), row(ob.shape[1]),
                  pl.BlockSpec((tm, tn), lambda i, j: (i, j)),
                  pl.BlockSpec((tm, tn), lambda i, j: (i, nb + j)),
                  pl.BlockSpec((ka, tn), lambda i, j: (0, j)),
                  pl.BlockSpec((wb.shape[0], tn), lambda i, j: (0, j))],
        out_specs=pl.BlockSpec((tm, tn), lambda i, j: (i, j)),
        compiler_params=_cparams("parallel", "arbitrary"),
        name="gated_mix",
    )(oc, osel, ow, ob, gates, gates, wa, wb)


def _xattn_kernel(h_ref, g_ref, wq_ref, kv_ref, wo_ref, o_ref):
    h = h_ref[...]
    hn = _rms(h, g_ref[...]).astype(BF16)
    q = jnp.dot(hn, wq_ref[...], preferred_element_type=F32).astype(BF16)
    kv = kv_ref[...]
    outs = []
    for hd in range(XA_HEADS):
        qh = q[:, hd * XA_DIM:(hd + 1) * XA_DIM]
        kh = kv[:, hd * XA_DIM:(hd + 1) * XA_DIM]
        vh = kv[:, (XA_HEADS + hd) * XA_DIM:(XA_HEADS + hd + 1) * XA_DIM]
        s = _dot_nt(qh, kh) * (XA_DIM ** -0.5)
        e = jnp.exp(s - jnp.max(s, axis=-1, keepdims=True))
        p = e / jnp.sum(e, axis=-1, keepdims=True)
        outs.append(jnp.dot(p.astype(BF16), vh, preferred_element_type=F32))
    o = jnp.concatenate(outs, axis=1).astype(BF16)
    o_ref[...] = h + jnp.dot(o, wo_ref[...], preferred_element_type=F32)


def memory_xattn(h, g, wq, kv, wo, *, seq, tm):
    n, d = h.shape
    s_t = seq // tm
    dq = wq.shape[1]
    return pl.pallas_call(
        _xattn_kernel,
        out_shape=jax.ShapeDtypeStruct((n, d), F32),
        grid=(n // tm,),
        in_specs=[pl.BlockSpec((tm, d), lambda i: (i, 0)),
                  pl.BlockSpec((1, d), lambda i: (0, 0)),
                  pl.BlockSpec((d, dq), lambda i: (0, 0)),
                  pl.BlockSpec((MEM_LEN, 2 * dq), lambda i: (i // s_t, 0)),
                  pl.BlockSpec((dq, d), lambda i: (0, 0))],
        out_specs=pl.BlockSpec((tm, d), lambda i: (i, 0)),
        compiler_params=_cparams("parallel"),
        name="memory_xattn",
    )(h, g.reshape(1, d), wq, kv, wo)


def _mlp_kernel(h_ref, g_ref, w1_ref, w2_ref, gf_ref, o_ref, hn_sc, acc_sc, *, nf):
    f = pl.program_id(1)

    @pl.when(f == 0)
    def _init():
        hn_sc[...] = _rms(h_ref[...], g_ref[...]).astype(BF16)
        acc_sc[...] = jnp.zeros(acc_sc.shape, F32)

    u = jnp.dot(hn_sc[...], w1_ref[...], preferred_element_type=F32)
    u = jnp.square(jnp.maximum(u, 0.0))
    acc_sc[...] += jnp.dot(u.astype(BF16), w2_ref[...], preferred_element_type=F32)

    @pl.when(f == nf - 1)
    def _finish():
        o_ref[...] = _rms(h_ref[...] + acc_sc[...], gf_ref[...])


def mlp_final(h, g, w1, w2, gf, *, tm, tf):
    n, d = h.shape
    dff = w1.shape[1]
    nf = dff // tf
    return pl.pallas_call(
        functools.partial(_mlp_kernel, nf=nf),
        out_shape=jax.ShapeDtypeStruct((n, d), F32),
        grid=(n // tm, nf),
        in_specs=[pl.BlockSpec((tm, d), lambda i, f: (i, 0)),
                  pl.BlockSpec((1, d), lambda i, f: (0, 0)),
                  pl.BlockSpec((d, tf), lambda i, f: (0, f)),
                  pl.BlockSpec((tf, d), lambda i, f: (f, 0)),
                  pl.BlockSpec((1, d), lambda i, f: (0, 0))],
        out_specs=pl.BlockSpec((tm, d), lambda i, f: (i, 0)),
        scratch_shapes=[pltpu.VMEM((tm, d), BF16), pltpu.VMEM((tm, d), F32)],
        compiler_params=_cparams("parallel", "arbitrary"),
        name="mlp_final",
    )(h, g.reshape(1, d), w1, w2, gf.reshape(1, d))


def _rot_partner(w, half):
    return jnp.concatenate([-w[..., half:], w[..., :half]], axis=-1)


def _pad_heads_rope(w, heads, hd, rot0, rot_dim):
    k = w.shape[0]
    w = w.reshape(k, heads, hd)
    partner = _rot_partner(w[:, :, rot0:rot0 + rot_dim], rot_dim // 2)
    pad = jnp.zeros((k, heads, LANES - hd - rot_dim), w.dtype)
    return jnp.concatenate([w, partner, pad], axis=-1).reshape(k, heads * LANES)


def _rope_lane_tables(seq, rot0, rot_dim, hd):
    inv = 1.0 / (ROPE_THETA ** (jnp.arange(0, rot_dim, 2, dtype=F32) / rot_dim))
    ang = jnp.arange(seq, dtype=F32)[:, None] * inv[None, :]
    cos = jnp.concatenate([jnp.cos(ang), jnp.cos(ang)], axis=1)
    sin = jnp.concatenate([jnp.sin(ang), jnp.sin(ang)], axis=1)
    cos_t = jnp.concatenate([jnp.ones((seq, rot0), F32), cos,
                             jnp.ones((seq, hd - rot0 - rot_dim), F32),
                             jnp.zeros((seq, LANES - hd), F32)], axis=1)
    sin_t = jnp.concatenate([jnp.zeros((seq, rot0), F32), sin,
                             jnp.zeros((seq, LANES - rot0 - rot_dim), F32)], axis=1)
    return cos_t, sin_t


def _pad_lanes(w, groups, width):
    k = w.shape[0]
    w = w.reshape(k, groups, width)
    return jnp.pad(w, ((0, 0), (0, 0), (0, LANES - width))).reshape(k, groups * LANES)


def _compress_params(pos, w1, w2, d):
    half = CMP_BLOCK // 2
    pos_p = jnp.pad(pos, ((0, 0), (0, LANES - d)))
    pos_a = pos_p[:half].reshape(1, half * LANES)
    pos_b = pos_p[half:].reshape(1, half * LANES)
    w1p = jnp.pad(w1.reshape(CMP_BLOCK, d, d), ((0, 0), (0, LANES - d), (0, LANES - d)))
    w1a = w1p[:half].reshape(half * LANES, LANES).astype(BF16)
    w1b = w1p[half:].reshape(half * LANES, LANES).astype(BF16)
    w2p = jnp.pad(w2, ((0, LANES - d), (0, LANES - d)))
    return pos_a, pos_b, w1a, w1b, w2p.astype(BF16)


def _cmp_to_sel(nr, nsb):
    cs = np.arange(nr) * CMP_STRIDE
    ce = cs + CMP_BLOCK
    ss = np.arange(LANES) * SEL_BLOCK
    se = ss + SEL_BLOCK
    ov = np.clip(np.minimum(ce[:, None], se[None, :]) - np.maximum(cs[:, None], ss[None, :]), 0, None)
    ov = ov.astype(np.float32) / np.float32(CMP_BLOCK)
    ov[:, nsb:] = 0.0
    ov[nr - 1:, :] = 0.0
    return jnp.asarray(ov, BF16)


def _gate_expand():
    e = np.zeros((3, NSA_HEADS // 2, LANES, LANES), np.float32)
    for br in range(3):
        for hp in range(NSA_HEADS // 2):
            for hh in range(2):
                e[br, hp, 3 * (2 * hp + hh) + br, hh * HALF:(hh + 1) * HALF] = 1.0
    return jnp.asarray(e, BF16)


def _key_block_onehot(seq):
    e = (np.arange(seq)[:, None] // SEL_BLOCK) == np.arange(LANES)[None, :]
    return jnp.asarray(e.astype(np.float32), BF16)


def kernel(x, mem, g_mix, w_in, cmp_pos_k, cmp_w1_k, cmp_w2_k, cmp_pos_v, cmp_w1_v, cmp_w2_v,
           mla_g_q, mla_w_uq, mla_g_kv, mla_w_uk, mla_w_uv, w_o_nsa, w_o_mla, w_out,
           g_xattn, g_mem, xa_wq, xa_wkv, xa_wo, g_mlp, w_ff1, w_ff2, g_final):
    b, s, d = x.shape
    assert d == D_MODEL and s % (CMP_STRIDE * 8) == 0 and s // SEL_BLOCK <= LANES
    assert g_mix.shape[0] == 1
    n = b * s
    T = _tiles(s)
    tm, tq, tk, rc = T["tm"], T["tq"], T["tk"], T["rc"]
    G = NSA_KV_HEADS
    bounds = [int(v) for v in np.cumsum(SPLITS)[:-1]]

    (w_qa, w_kc, w_vc, w_ks, w_vs, w_kw, w_vw, w_gn, w_cq, w_ckv, w_kr,
     w_ga, w_gb) = jnp.split(w_in[0], bounds, axis=1)
    nsa_rope = lambda w, heads: _pad_heads_rope(w, heads, NSA_DK, 0, NSA_ROT)
    w_rope = jnp.concatenate([nsa_rope(w_qa, NSA_HEADS), nsa_rope(w_kc, G), nsa_rope(w_ks, G),
                              nsa_rope(w_kw, G)], axis=1).astype(BF16)
    w_krp = jnp.concatenate([jnp.zeros((d, MLA_NOPE), F32), w_kr, _rot_partner(w_kr, MLA_ROPE // 2)],
                            axis=1).astype(BF16)
    w_vsw = jnp.concatenate([_pad_lanes(w_vs, G, NSA_DV), _pad_lanes(w_vw, G, NSA_DV)], axis=1).astype(BF16)
    w_vcp = _pad_lanes(w_vc, G, NSA_DV).astype(BF16)
    w_misc = jnp.concatenate([w_cq, w_ckv, jnp.pad(w_gn, ((0, 0), (0, LANES - w_gn.shape[1])))],
                             axis=1).astype(BF16)
    glog_blk = (MLA_Q_RANK + MLA_KV_RANK) // LANES
    w_gates = jnp.concatenate([w_ga, w_gb], axis=1).astype(BF16)
    cos_a, sin_a = _rope_lane_tables(s, 0, NSA_ROT, NSA_DK)
    cos_b, sin_b = _rope_lane_tables(s, MLA_NOPE, MLA_ROPE, MLA_NOPE + MLA_ROPE)
    shift_a = LANES - NSA_DK
    shift_b = LANES - (NSA_DK - MLA_NOPE)
    w_uq = _pad_heads_rope(mla_w_uq[0], MLA_HEADS, MLA_NOPE + MLA_ROPE, MLA_NOPE, MLA_ROPE).astype(BF16)
    w_uk = _pad_lanes(mla_w_uk[0], MLA_HEADS, MLA_NOPE).astype(BF16)
    w_uv = _pad_lanes(mla_w_uv[0], MLA_HEADS, MLA_DV).astype(BF16)

    x2 = x.reshape(n, d)
    xn = rmsnorm_rows(x2, g_mix[0], tm)
    log2e = float(np.log2(np.e))
    tmb = T["tm_big"]
    qk = projection(xn, w_rope, tm=tmb, tn=512, out_dtype=BF16, rope=(cos_a, sin_a, shift_a),
                    out_scale=(NSA_DK ** -0.5 * log2e, NSA_HEADS * LANES // 512),
                    head_major=(b, s), name="proj_qk_rope")
    krp = projection(xn, w_krp, tm=tm, tn=LANES, out_dtype=BF16, rope=(cos_b, sin_b, shift_b),
                     name="proj_k_rope_mla")
    vsw = projection(xn, w_vsw, tm=tmb, tn=512, out_dtype=BF16, head_major=(b, s), name="proj_v")
    vcp = projection(xn, w_vcp, tm=tmb, tn=512, out_dtype=BF16, head_major=(b, s), name="proj_vc")
    misc = projection(xn, w_misc, tm=tm, tn=w_misc.shape[1], out_dtype=F32, name="proj_misc")
    gates = projection(xn, w_gates, tm=tmb, tn=512, out_dtype=BF16, act="sigmoid", name="proj_gates")

    nr = s // CMP_STRIDE
    pk = _compress_params(cmp_pos_k[0], cmp_w1_k[0], cmp_w2_k[0], NSA_DK)
    pv = _compress_params(cmp_pos_v[0], cmp_w1_v[0], cmp_w2_v[0], NSA_DV)
    kc_rows = qk[:, QK_KC0:QK_KC0 + G].reshape(b, G, nr, CMP_STRIDE * LANES)
    kcc = compress(kc_rows, 0, *pk)
    vcc = compress(vcp.reshape(b, G, nr, CMP_STRIDE * LANES), 0, *pv)
    expand = _gate_expand()
    o_c, selb = nsa_compressed(qk, kcc, vcc, _cmp_to_sel(nr, s // SEL_BLOCK), misc, glog_blk, expand,
                               batch=b, seq=s, tq=T["tq_cmp"], rc=rc)
    o_s = flash_heads(qk, qk, vsw, mode="select", batch=b, seq=s, tq=tq, tk=tk, rc=rc,
                      q_head0=QK_Q0, k_head0=QK_KS0, v_head0=0, kv_shared=True, out_dtype=BF16,
                      selb=selb, onehot=_key_block_onehot(s), gate=(misc, glog_blk, expand, 1),
                      name="nsa_selected")
    o_w = flash_heads(qk, qk, vsw, mode="window", batch=b, seq=s, tq=T["tq_win"], tk=tk, rc=rc,
                      q_head0=QK_Q0, k_head0=QK_KW0, v_head0=G, kv_shared=True, out_dtype=BF16,
                      gate=(misc, glog_blk, expand, 2), name="nsa_window")

    qm = projection(misc, w_uq, tm=tm, tn=512, out_dtype=BF16, a_cols=(MLA_Q_RANK, 0), gain=mla_g_q[0],
                    rope=(cos_b, sin_b, shift_b), head_major=(b, s), name="mla_q",
                    out_scale=((MLA_NOPE + MLA_ROPE) ** -0.5 * log2e, MLA_HEADS * LANES // 512))
    km = projection(misc, w_uk, tm=tm, tn=512, out_dtype=BF16,
                    a_cols=(MLA_KV_RANK, MLA_Q_RANK // MLA_KV_RANK), gain=mla_g_kv[0], add=krp,
                    head_major=(b, s), name="mla_k")
    vm = projection(misc, w_uv, tm=tm, tn=512, out_dtype=BF16,
                    a_cols=(MLA_KV_RANK, MLA_Q_RANK // MLA_KV_RANK), gain=mla_g_kv[0],
                    head_major=(b, s), name="mla_v")
    o_b = flash_heads(qm, km, vm, mode="causal", batch=b, seq=s, tq=tq, tk=tk, rc=rc,
                      q_head0=0, k_head0=0, v_head0=0,
                      kv_shared=False, out_dtype=BF16, name="mla_attention")

    mixed = gated_mix(o_c, o_s, o_w, o_b, gates, w_o_nsa[0].astype(BF16), w_o_mla[0].astype(BF16),
                      tm=tm, tn=512)
    h1 = projection(mixed, w_out[0].astype(BF16), tm=tmb, tn=512, out_dtype=F32, res=x2, name="proj_out")

    kv_mem = projection(mem.reshape(b * MEM_LEN, d), xa_wkv[0].astype(BF16), tm=MEM_LEN, tn=512,
                        out_dtype=BF16, gain=g_mem[0], name="xattn_kv")
    h2 = memory_xattn(h1, g_xattn[0], xa_wq[0].astype(BF16), kv_mem, xa_wo[0].astype(BF16), seq=s,
                      tm=T["tm_x"])

    out = mlp_final(h2, g_mlp[0], w_ff1[0].astype(BF16), w_ff2[0].astype(BF16), g_final,
                    tm=T["tm_mlp"], tf=T["tf"])
    return out.reshape(b, s, d)
```

```python
import functools

import numpy as np
import jax
import jax.numpy as jnp
from jax import lax
from jax.experimental import pallas as pl
from jax.experimental.pallas import tpu as pltpu

F32 = jnp.float32
BF16 = jnp.bfloat16

D_MODEL = 2048
MEM_LEN = 256
ROPE_THETA = 500000.0
EPS = 1e-6
NEG = -1e30

NSA_HEADS = 16
NSA_KV_HEADS = 4
NSA_HPG = NSA_HEADS // NSA_KV_HEADS
NSA_DK = 96
NSA_DV = 64
NSA_ROT = NSA_DK // 4
CMP_BLOCK = 32
CMP_STRIDE = 16
SEL_BLOCK = 64
SEL_TOPK = 16
N_FORCED = 3
WINDOW = 512

MLA_HEADS = 16
MLA_NOPE = 64
MLA_ROPE = 32
MLA_DV = 64
MLA_Q_RANK = 512
MLA_KV_RANK = 256

XA_HEADS = 4
XA_DIM = 128
D_FF = 4 * D_MODEL

SPLITS = (NSA_HEADS * NSA_DK,
          NSA_KV_HEADS * NSA_DK, NSA_KV_HEADS * NSA_DV,
          NSA_KV_HEADS * NSA_DK, NSA_KV_HEADS * NSA_DV,
          NSA_KV_HEADS * NSA_DK, NSA_KV_HEADS * NSA_DV,
          NSA_HEADS * 3,
          MLA_Q_RANK, MLA_KV_RANK, MLA_ROPE,
          D_MODEL, D_MODEL)

LANES = 128
HALF = LANES // 2
VMEM_LIMIT = 56 * 1024 * 1024
QK_Q0 = 0
QK_KC0 = NSA_HEADS
QK_KS0 = NSA_HEADS + NSA_KV_HEADS
QK_KW0 = NSA_HEADS + 2 * NSA_KV_HEADS
QK_HEADS = NSA_HEADS + 3 * NSA_KV_HEADS
FLASH_HEADS = NSA_HPG


def _cparams(*sem):
    return pltpu.CompilerParams(dimension_semantics=sem, vmem_limit_bytes=VMEM_LIMIT)


def _tiles(seq):
    return dict(
        tm=min(1024, seq),
        tm_big=min(2048, seq),
        tm_x=min(512, seq),
        tm_mlp=min(512, seq),
        tq=min(2048, seq),
        tq_win=min(512, seq),
        tk=min(512, seq),
        rc=32,
        tq_cmp=min(256, seq),
        tf=1024,
    )


def _rms(x, g):
    return x * lax.rsqrt(jnp.mean(x * x, axis=-1, keepdims=True) + EPS) * g


def _rmsnorm_kernel(x_ref, g_ref, o_ref):
    o_ref[...] = _rms(x_ref[...], g_ref[...]).astype(o_ref.dtype)


def rmsnorm_rows(x, g, tm):
    n, d = x.shape
    return pl.pallas_call(
        _rmsnorm_kernel,
        out_shape=jax.ShapeDtypeStruct((n, d), BF16),
        grid=(n // tm,),
        in_specs=[pl.BlockSpec((tm, d), lambda i: (i, 0)),
                  pl.BlockSpec((1, d), lambda i: (0, 0))],
        out_specs=pl.BlockSpec((tm, d), lambda i: (i, 0)),
        compiler_params=_cparams("parallel"),
        name="rmsnorm_rows",
    )(x, g.reshape(1, d))


def _proj_kernel(*refs, has_gain, has_rope, roll_shift, out_scale, has_add, has_res, head_major):
    it = iter(refs)
    a_ref = next(it)
    g_ref = next(it) if has_gain else None
    w_ref = next(it)
    cos_ref = next(it) if has_rope else None
    sin_ref = next(it) if has_rope else None
    add_ref = next(it) if has_add else None
    res_ref = next(it) if has_res else None
    o_ref = next(it)

    a = a_ref[...]
    if has_gain:
        a = _rms(a, g_ref[...]).astype(BF16)
    y = jnp.dot(a, w_ref[...], preferred_element_type=F32)
    if out_scale is not None:
        factor, n_tiles = out_scale
        y = y * jnp.where(pl.program_id(1) < n_tiles, factor, 1.0)
    if has_rope or has_add or head_major:
        for h in range(y.shape[1] // LANES):
            yh = y[:, h * LANES:(h + 1) * LANES]
            if has_rope:
                yh = yh * cos_ref[...] + pltpu.roll(yh, roll_shift, 1) * sin_ref[...]
            if has_add:
                yh = yh + add_ref[...].astype(F32)
            if head_major:
                o_ref[h] = yh.astype(o_ref.dtype)
            else:
                o_ref[:, h * LANES:(h + 1) * LANES] = yh.astype(o_ref.dtype)
    else:
        if has_res:
            y = res_ref[...] + y
        o_ref[...] = y.astype(o_ref.dtype)


def projection(a, w, *, tm, tn, out_dtype, a_cols=None, gain=None, rope=None, out_scale=None, add=None,
               res=None, head_major=None, name="projection"):
    n = a.shape[0]
    k, nc = w.shape
    a_w, a_blk = a_cols if a_cols is not None else (a.shape[1], 0)
    assert a_w == k and n % tm == 0 and nc % tn == 0
    grid = (n // tm, nc // tn)
    in_specs = [pl.BlockSpec((tm, k), lambda i, j: (i, a_blk))]
    args = [a]
    if gain is not None:
        in_specs.append(pl.BlockSpec((1, k), lambda i, j: (0, 0)))
        args.append(gain.reshape(1, k))
    in_specs.append(pl.BlockSpec((k, tn), lambda i, j: (0, j)))
    args.append(w)
    roll_shift = 0
    if rope is not None:
        cos, sin, roll_shift = rope
        s_tiles = cos.shape[0] // tm
        for t in (cos, sin):
            in_specs.append(pl.BlockSpec((tm, LANES), lambda i, j: (i % s_tiles, 0)))
            args.append(t)
    if add is not None:
        in_specs.append(pl.BlockSpec((tm, LANES), lambda i, j: (i, 0)))
        args.append(add)
    if res is not None:
        in_specs.append(pl.BlockSpec((tm, tn), lambda i, j: (i, j)))
        args.append(res)
    if head_major is not None:
        b, s = head_major
        s_t = s // tm
        hpt = tn // LANES
        out_shape = jax.ShapeDtypeStruct((b, nc // LANES, s, LANES), out_dtype)
        out_spec = pl.BlockSpec((None, hpt, tm, LANES), lambda i, j: (i // s_t, j, i % s_t, 0))
    else:
        out_shape = jax.ShapeDtypeStruct((n, nc), out_dtype)
        out_spec = pl.BlockSpec((tm, tn), lambda i, j: (i, j))
    kern = functools.partial(
        _proj_kernel, has_gain=gain is not None, has_rope=rope is not None, roll_shift=roll_shift,
        out_scale=out_scale, has_add=add is not None, has_res=res is not None,
        head_major=head_major is not None)
    return pl.pallas_call(
        kern, out_shape=out_shape, grid=grid, in_specs=in_specs, out_specs=out_spec,
        compiler_params=_cparams("parallel", "arbitrary"), name=name,
    )(*args)


def _compress_kernel(r_ref, pa_ref, pb_ref, w1a_ref, w1b_ref, w2_ref, o_ref):
    r = r_ref[...].astype(F32)
    a = jnp.dot((r + pa_ref[...]).astype(BF16), w1a_ref[...], preferred_element_type=F32)
    b = jnp.dot((r + pb_ref[...]).astype(BF16), w1b_ref[...], preferred_element_type=F32)
    nr = a.shape[0]
    hid = a + pltpu.roll(b, nr - 1, 0)
    hid = jax.nn.gelu(hid)
    o_ref[...] = jnp.dot(hid.astype(BF16), w2_ref[...], preferred_element_type=F32).astype(o_ref.dtype)


def compress(r, head0, pos_a, pos_b, w1a, w1b, w2):
    b, _, nr, kk = r.shape
    g = NSA_KV_HEADS
    full = lambda shape: pl.BlockSpec(shape, lambda bi, gi: (0,) * len(shape))
    return pl.pallas_call(
        _compress_kernel,
        out_shape=jax.ShapeDtypeStruct((b, g, nr, LANES), BF16),
        grid=(b, g),
        in_specs=[pl.BlockSpec((None, None, nr, kk), lambda bi, gi: (bi, head0 + gi, 0, 0)),
                  full((1, kk)), full((1, kk)), full((kk, LANES)), full((kk, LANES)),
                  full((LANES, LANES))],
        out_specs=pl.BlockSpec((None, None, nr, LANES), lambda bi, gi: (bi, gi, 0, 0)),
        compiler_params=_cparams("parallel", "parallel"),
        name="nsa_compress",
    )(r, pos_a, pos_b, w1a, w1b, w2)


def _dot_nt(a, b):
    return lax.dot_general(a, b, (((1,), (1,)), ((), ())), preferred_element_type=F32)


def _split_dot(x, w):
    hi = x.astype(BF16)
    lo = (x - hi.astype(F32)).astype(BF16)
    return (jnp.dot(hi, w, preferred_element_type=F32) + jnp.dot(lo, w, preferred_element_type=F32))


def _nsa_cmp_kernel(q_ref, kcc_ref, vcc_ref, msel_ref, glog_ref, e_ref, oc_ref, sel_ref, s_sc, p_sc, imp_sc,
                    *, tq, rc, topk):
    s0 = pl.program_id(2) * tq
    nr = kcc_ref.shape[0]

    def attend(ncol):
        kcc = kcc_ref[0:ncol, :]
        for h in range(NSA_HPG):
            s_sc[h, :, 0:ncol] = _dot_nt(q_ref[h], kcc)
        lane_v = lax.broadcasted_iota(jnp.int32, (ncol, LANES), 1)
        vcc = vcc_ref[0:ncol, :]
        vm = jnp.concatenate([jnp.where(lane_v == HALF, jnp.ones_like(vcc), vcc), msel_ref[0:ncol, :]],
                             axis=1)
        dmat = (lax.broadcasted_iota(jnp.int32, (rc, ncol), 0)
                - lax.broadcasted_iota(jnp.int32, (rc, ncol), 1) * CMP_STRIDE)
        visible = lambda c: dmat + (s0 + c * rc - (CMP_BLOCK - 1)) >= 0
        outs = []
        imp = jnp.zeros((tq, LANES), F32)
        for h in range(NSA_HPG):
            row_max = []
            for c in range(tq // rc):
                s = jnp.where(visible(c), s_sc[h, pl.ds(c * rc, rc), 0:ncol], NEG)
                row_max.append(jnp.max(s, axis=-1, keepdims=True))
            for c in range(tq // rc):
                rows = pl.ds(c * rc, rc)
                p = jnp.exp2(s_sc[h, rows, 0:ncol] - row_max[c])
                p_sc[h, rows, 0:ncol] = jnp.where(visible(c), p, 0.0).astype(BF16)
            a = jnp.dot(p_sc[h, :, 0:ncol], vm, preferred_element_type=F32)
            denom = jnp.maximum(a[:, HALF:HALF + 1], 1e-30)
            outs.append(a[:, :LANES] / denom)
            imp = imp + a[:, LANES:] / denom
        imp_sc[...] = imp
        glog = glog_ref[...]
        lane = lax.broadcasted_iota(jnp.int32, (tq, LANES), 1)
        for pr in range(NSA_HPG // 2):
            gate = jax.nn.sigmoid(_split_dot(glog, e_ref[pr]))
            pair = jnp.where(lane < HALF, outs[2 * pr], pltpu.roll(outs[2 * pr + 1], HALF, 1))
            oc_ref[:, pr * LANES:(pr + 1) * LANES] = (gate * pair).astype(oc_ref.dtype)

    ntiles = (s0 + (tq - CMP_BLOCK)) // (CMP_STRIDE * LANES) + 1
    for k in range(1, nr // LANES + 1):
        @pl.when(ntiles == k)
        def _width(k=k):
            attend(k * LANES)

    blk = lax.broadcasted_iota(jnp.int32, (tq, LANES), 1)
    cur = (s0 + lax.broadcasted_iota(jnp.int32, (tq, LANES), 0)) // SEL_BLOCK
    valid = blk <= cur
    forced = (blk == 0) | (blk == cur) | (blk == cur - 1)
    score = jnp.where(valid, jnp.where(forced, -jnp.inf, imp_sc[...]), -1.0)
    sc = score.T
    rowid = lax.broadcasted_iota(jnp.int32, (LANES, tq), 0).astype(F32)

    def pick_one(_, sc):
        cm = jnp.max(sc, axis=0, keepdims=True)
        first = jnp.min(jnp.where(sc == cm, rowid, float(LANES)), axis=0, keepdims=True)
        return jnp.where(rowid == first, -jnp.inf, sc)

    picked = lax.fori_loop(0, topk - N_FORCED, pick_one, sc) == -jnp.inf
    sel_ref[...] = jnp.where(picked, 0.0, NEG).T.astype(sel_ref.dtype)


def nsa_compressed(qk, kcc, vcc, msel, misc, glog_blk, expand, *, batch, seq, tq, rc):
    g = NSA_KV_HEADS
    nq = seq // tq
    nr = kcc.shape[2]
    topk = min(SEL_TOPK, seq // SEL_BLOCK)
    assert topk > N_FORCED and nr % LANES == 0
    kern = functools.partial(_nsa_cmp_kernel, tq=tq, rc=rc, topk=topk)
    return pl.pallas_call(
        kern,
        out_shape=(jax.ShapeDtypeStruct((batch * seq, NSA_HEADS * NSA_DV), BF16),
                   jax.ShapeDtypeStruct((batch, g, seq, LANES), BF16)),
        grid=(batch, g, nq),
        in_specs=[
            pl.BlockSpec((None, NSA_HPG, tq, LANES), lambda b, gi, qi: (b, gi, qi, 0)),
            pl.BlockSpec((None, None, nr, LANES), lambda b, gi, qi: (b, gi, 0, 0)),
            pl.BlockSpec((None, None, nr, LANES), lambda b, gi, qi: (b, gi, 0, 0)),
            pl.BlockSpec((nr, LANES), lambda b, gi, qi: (0, 0)),
            pl.BlockSpec((tq, LANES), lambda b, gi, qi: (b * nq + qi, glog_blk)),
            pl.BlockSpec((None, 2, LANES, LANES), lambda b, gi, qi: (0, gi, 0, 0)),
        ],
        out_specs=(pl.BlockSpec((tq, 2 * LANES), lambda b, gi, qi: (b * nq + qi, gi)),
                   pl.BlockSpec((None, None, tq, LANES), lambda b, gi, qi: (b, gi, qi, 0))),
        scratch_shapes=[pltpu.VMEM((NSA_HPG, tq, nr), F32), pltpu.VMEM((NSA_HPG, tq, nr), BF16),
                        pltpu.VMEM((tq, LANES), F32)],
        compiler_params=_cparams("parallel", "parallel", "parallel"),
        name="nsa_compressed_select",
    )(qk, kcc, vcc, msel, misc, expand)


def _flash_steps(mode, nq, tq, tk):
    r = tq // tk
    qi_l, kt_l, first_l, last_l = [], [], [], []
    for qi in range(nq):
        hi = qi * r + r - 1
        lo = max(0, qi * r - (-(-(WINDOW - 1) // tk))) if mode == "window" else 0
        for kt in range(lo, hi + 1):
            qi_l.append(qi)
            kt_l.append(kt)
            first_l.append(int(kt == lo))
            last_l.append(int(kt == hi))
    return tuple(jnp.asarray(np.asarray(a, np.int32)) for a in (qi_l, kt_l, first_l, last_l))


def _flash_kernel(*refs, mode, tq, tk, rc, kv_shared, gated):
    it = iter(refs)
    qi_ref, kt_ref, first_ref, last_ref = next(it), next(it), next(it), next(it)
    q_ref, k_ref, v_ref = next(it), next(it), next(it)
    selb_ref = next(it) if mode == "select" else None
    oh_ref = next(it) if mode == "select" else None
    glog_ref = next(it) if gated else None
    e_ref = next(it) if gated else None
    o_ref = next(it)
    m_sc, acc_sc, s_sc, p_sc = next(it), next(it), next(it), next(it)
    qa_sc = next(it) if mode == "select" else None

    st = pl.program_id(2)
    s0 = qi_ref[st] * tq
    k0 = kt_ref[st] * tk

    @pl.when(first_ref[st] == 1)
    def _init():
        m_sc[...] = jnp.full(m_sc.shape, NEG, F32)
        acc_sc[...] = jnp.zeros(acc_sc.shape, F32)
        if mode == "select":
            for hh in range(FLASH_HEADS):
                qa_sc[hh] = jnp.concatenate([q_ref[hh], selb_ref[...]], axis=1)

    def chunk_cols(off, r0):
        if off is None:
            return 0, tk, True
        r1 = r0 + rc
        hi = min(tk, r1 - off)
        lo = max(0, r0 - off - WINDOW + 1) if mode == "window" else 0
        lo, hi = lo // LANES * LANES, -(-hi // LANES) * LANES
        clear = tk - 1 <= r0 - off and (mode != "window" or r1 - 1 - off < WINDOW)
        return lo, hi, clear

    def tile(off):
        nh = FLASH_HEADS
        r_lo = max(0, off) if off is not None else 0
        nrow = tq - r_lo
        if kv_shared:
            if mode == "select":
                qs = qa_sc[:, r_lo:, :].reshape(nh * nrow, 2 * LANES)
                ks = jnp.concatenate([k_ref[0], oh_ref[...]], axis=1)
            else:
                qs, ks = q_ref[:, r_lo:, :].reshape(nh * nrow, LANES), k_ref[0]
            s_sc[:, r_lo:, :] = _dot_nt(qs, ks).reshape(nh, nrow, tk)
        else:
            for hh in range(nh):
                s_sc[hh, r_lo:, :] = _dot_nt(q_ref[hh, r_lo:, :], k_ref[hh])
        lane_v = lax.broadcasted_iota(jnp.int32, (tk, LANES), 1)
        if off is not None:
            dmat = (lax.broadcasted_iota(jnp.int32, (rc, tk), 0)
                    - lax.broadcasted_iota(jnp.int32, (rc, tk), 1))
        for hh in range(FLASH_HEADS):
            v = v_ref[0 if kv_shared else hh]
            v_aug = jnp.where(lane_v == HALF, jnp.ones_like(v), v)

            for c in range(r_lo // rc, tq // rc):
                r0 = c * rc
                rows = pl.ds(r0, rc)
                lo, hi, clear = chunk_cols(off, r0)
                if lo > 0:
                    p_sc[hh, rows, 0:lo] = jnp.zeros((rc, lo), BF16)
                if hi < tk:
                    p_sc[hh, rows, max(hi, 0):tk] = jnp.zeros((rc, tk - max(hi, 0)), BF16)
                if hi <= lo:
                    continue
                s = s_sc[hh, rows, lo:hi]
                if not clear:
                    d = dmat[:, lo:hi] + (r0 - off)
                    msk = (d >= 0) & (d < WINDOW) if mode == "window" else d >= 0
                    s = jnp.where(msk, s, NEG)
                m_old = m_sc[hh, rows, :]
                m_new = jnp.maximum(m_old, jnp.max(s, axis=-1, keepdims=True))
                m_sc[hh, rows, :] = m_new
                acc_sc[hh, rows, :] = acc_sc[hh, rows, :] * jnp.exp2(m_old - m_new)
                p = jnp.exp2(s_sc[hh, rows, lo:hi] - jnp.concatenate([m_new] * ((hi - lo) // LANES), axis=1))
                if not clear:
                    p = jnp.where(msk, p, 0.0)
                p_sc[hh, rows, lo:hi] = p.astype(BF16)
            if not kv_shared:
                acc_sc[hh, r_lo:, :] += jnp.dot(p_sc[hh, r_lo:, :], v_aug, preferred_element_type=F32)
        if kv_shared:
            pv = jnp.dot(p_sc[:, r_lo:, :].reshape(nh * nrow, tk), v_aug, preferred_element_type=F32)
            acc_sc[:, r_lo:, :] += pv.reshape(nh, nrow, LANES)

    if mode == "window":
        offsets = [(j - (-(-(WINDOW - 1) // tk))) * tk for j in range(-(-(WINDOW - 1) // tk) + tq // tk)]
    else:
        offsets = [j * tk for j in range(tq // tk)]

        @pl.when(k0 + tk - 1 <= s0)
        def _interior():
            tile(None)

    for off in offsets:
        @pl.when(k0 - s0 == off)
        def _partial(off=off):
            tile(off)

    @pl.when(last_ref[st] == 1)
    def _finish():
        lane = lax.broadcasted_iota(jnp.int32, (tq, LANES), 1)
        if gated:
            glog = glog_ref[...]
        for pr in range(FLASH_HEADS // 2):
            outs = []
            for hh in (2 * pr, 2 * pr + 1):
                a = acc_sc[hh]
                outs.append(a / jnp.maximum(a[:, HALF:HALF + 1], 1e-30))
            out = jnp.where(lane < HALF, outs[0], pltpu.roll(outs[1], HALF, 1))
            if gated:
                out = jax.nn.sigmoid(_split_dot(glog, e_ref[pr])) * out
            o_ref[:, pr * LANES:(pr + 1) * LANES] = out.astype(o_ref.dtype)


def flash_heads(q, k, v, *, mode, batch, seq, tq, tk, rc, q_head0, k_head0, v_head0,
                kv_shared, out_dtype, selb=None, onehot=None, gate=None, name="flash"):
    nh = FLASH_HEADS
    ngroups = NSA_HEADS // nh
    nq = seq // tq
    assert tq % tk == 0 and tq % rc == 0
    tabs = _flash_steps(mode, nq, tq, tk)
    nsteps = tabs[0].shape[0]
    if kv_shared:
        kv_spec = lambda h0: pl.BlockSpec(
            (None, 1, tk, LANES), lambda b, g, st, qi, kt, fi, la: (b, h0 + g, kt[st], 0))
    else:
        kv_spec = lambda h0: pl.BlockSpec(
            (None, nh, tk, LANES), lambda b, g, st, qi, kt, fi, la: (b, h0 // nh + g, kt[st], 0))
    in_specs = [pl.BlockSpec((None, nh, tq, LANES),
                             lambda b, g, st, qi, kt, fi, la: (b, q_head0 // nh + g, qi[st], 0)),
                kv_spec(k_head0), kv_spec(v_head0)]
    args = [q, k, v]
    scratch = [pltpu.VMEM((nh, tq, LANES), F32), pltpu.VMEM((nh, tq, LANES), F32),
               pltpu.VMEM((nh, tq, tk), F32), pltpu.VMEM((nh, tq, tk), BF16)]
    if mode == "select":
        in_specs += [pl.BlockSpec((None, None, tq, LANES),
                                  lambda b, g, st, qi, kt, fi, la: (b, g, qi[st], 0)),
                     pl.BlockSpec((tk, LANES), lambda b, g, st, qi, kt, fi, la: (kt[st], 0))]
        args += [selb, onehot]
        scratch.append(pltpu.VMEM((nh, tq, 2 * LANES), BF16))
    if gate is not None:
        misc, glog_blk, expand, branch = gate
        in_specs += [pl.BlockSpec((tq, LANES),
                                  lambda b, g, st, qi, kt, fi, la: (b * nq + qi[st], glog_blk)),
                     pl.BlockSpec((None, nh // 2, LANES, LANES),
                                  lambda b, g, st, qi, kt, fi, la: (branch, g, 0, 0))]
        args += [misc, expand]
    kern = functools.partial(_flash_kernel, mode=mode, tq=tq, tk=tk, rc=rc,
                             kv_shared=kv_shared, gated=gate is not None)
    ow = nh * HALF
    grid_spec = pltpu.PrefetchScalarGridSpec(
        num_scalar_prefetch=4, grid=(batch, ngroups, nsteps), in_specs=in_specs,
        out_specs=pl.BlockSpec((tq, ow), lambda b, g, st, qi, kt, fi, la: (b * nq + qi[st], g)),
        scratch_shapes=scratch)
    return pl.pallas_call(
        kern,
        out_shape=jax.ShapeDtypeStruct((batch * seq, ngroups * ow), out_dtype),
        grid_spec=grid_spec,
        compiler_params=_cparams("parallel", "parallel", "arbitrary"),
        name=name,
    )(*tabs, *args)


def _mix_kernel(oc_ref, os_ref, ow_ref, ob_ref, xn_ref, wga_ref, wgb_ref, wa_ref, wb_ref, o_ref):
    oa = (oc_ref[...].astype(F32) + os_ref[...].astype(F32) + ow_ref[...].astype(F32)).astype(BF16)
    xn = xn_ref[...]
    ga = jax.nn.sigmoid(jnp.dot(xn, wga_ref[...], preferred_element_type=F32))
    ya = ga * jnp.dot(oa, wa_ref[...], preferred_element_type=F32)
    gb = jax.nn.sigmoid(jnp.dot(xn, wgb_ref[...], preferred_element_type=F32))
    yb = gb * jnp.dot(ob_ref[...], wb_ref[...], preferred_element_type=F32)
    o_ref[...] = (ya + yb).astype(o_ref.dtype)


def gated_mix(oc, osel, ow, ob, xn, wga, wgb, wa, wb, *, tm, tn):
    n, ka = oc.shape
    d = wa.shape[1]
    row = lambda w: pl.BlockSpec((tm, w), lambda i, j: (i, 0))
    col = lambda k: pl.BlockSpec((k, tn), lambda i, j: (0, j))
    return pl.pallas_call(
        _mix_kernel,
        out_shape=jax.ShapeDtypeStruct((n, d), BF16),
        grid=(n // tm, d // tn),
        in_specs=[row(ka), row(ka), row(ka), row(ob.shape[1]), row(xn.shape[1]),
                  col(wga.shape[0]), col(wgb.shape[0]), col(ka), col(wb.shape[0])],
        out_specs=pl.BlockSpec((tm, tn), lambda i, j: (i, j)),
        compiler_params=_cparams("parallel", "arbitrary"),
        name="gated_mix",
    )(oc, osel, ow, ob, xn, wga, wgb, wa, wb)


def _xattn_kernel(h_ref, g_ref, wq_ref, kv_ref, wo_ref, o_ref):
    h = h_ref[...]
    hn = _rms(h, g_ref[...]).astype(BF16)
    q = jnp.dot(hn, wq_ref[...], preferred_element_type=F32).astype(BF16)
    kv = kv_ref[...]
    outs = []
    for hd in range(XA_HEADS):
        qh = q[:, hd * XA_DIM:(hd + 1) * XA_DIM]
        kh = kv[:, hd * XA_DIM:(hd + 1) * XA_DIM]
        vh = kv[:, (XA_HEADS + hd) * XA_DIM:(XA_HEADS + hd + 1) * XA_DIM]
        s = _dot_nt(qh, kh) * (XA_DIM ** -0.5)
        e = jnp.exp(s - jnp.max(s, axis=-1, keepdims=True))
        p = e / jnp.sum(e, axis=-1, keepdims=True)
        outs.append(jnp.dot(p.astype(BF16), vh, preferred_element_type=F32))
    o = jnp.concatenate(outs, axis=1).astype(BF16)
    o_ref[...] = h + jnp.dot(o, wo_ref[...], preferred_element_type=F32)


def memory_xattn(h, g, wq, kv, wo, *, seq, tm):
    n, d = h.shape
    s_t = seq // tm
    dq = wq.shape[1]
    return pl.pallas_call(
        _xattn_kernel,
        out_shape=jax.ShapeDtypeStruct((n, d), F32),
        grid=(n // tm,),
        in_specs=[pl.BlockSpec((tm, d), lambda i: (i, 0)),
                  pl.BlockSpec((1, d), lambda i: (0, 0)),
                  pl.BlockSpec((d, dq), lambda i: (0, 0)),
                  pl.BlockSpec((MEM_LEN, 2 * dq), lambda i: (i // s_t, 0)),
                  pl.BlockSpec((dq, d), lambda i: (0, 0))],
        out_specs=pl.BlockSpec((tm, d), lambda i: (i, 0)),
        compiler_params=_cparams("parallel"),
        name="memory_xattn",
    )(h, g.reshape(1, d), wq, kv, wo)


def _mlp_kernel(h_ref, g_ref, w1_ref, w2_ref, gf_ref, o_ref, hn_sc, acc_sc, *, nf):
    f = pl.program_id(1)

    @pl.when(f == 0)
    def _init():
        hn_sc[...] = _rms(h_ref[...], g_ref[...]).astype(BF16)
        acc_sc[...] = jnp.zeros(acc_sc.shape, F32)

    u = jnp.dot(hn_sc[...], w1_ref[...], preferred_element_type=F32)
    u = jnp.square(jnp.maximum(u, 0.0))
    acc_sc[...] += jnp.dot(u.astype(BF16), w2_ref[...], preferred_element_type=F32)

    @pl.when(f == nf - 1)
    def _finish():
        o_ref[...] = _rms(h_ref[...] + acc_sc[...], gf_ref[...])


def mlp_final(h, g, w1, w2, gf, *, tm, tf):
    n, d = h.shape
    dff = w1.shape[1]
    nf = dff // tf
    return pl.pallas_call(
        functools.partial(_mlp_kernel, nf=nf),
        out_shape=jax.ShapeDtypeStruct((n, d), F32),
        grid=(n // tm, nf),
        in_specs=[pl.BlockSpec((tm, d), lambda i, f: (i, 0)),
                  pl.BlockSpec((1, d), lambda i, f: (0, 0)),
                  pl.BlockSpec((d, tf), lambda i, f: (0, f)),
                  pl.BlockSpec((tf, d), lambda i, f: (f, 0)),
                  pl.BlockSpec((1, d), lambda i, f: (0, 0))],
        out_specs=pl.BlockSpec((tm, d), lambda i, f: (i, 0)),
        scratch_shapes=[pltpu.VMEM((tm, d), BF16), pltpu.VMEM((tm, d), F32)],
        compiler_params=_cparams("parallel", "arbitrary"),
        name="mlp_final",
    )(h, g.reshape(1, d), w1, w2, gf.reshape(1, d))


def _rot_partner(w, half):
    return jnp.concatenate([-w[..., half:], w[..., :half]], axis=-1)


def _pad_heads_rope(w, heads, hd, rot0, rot_dim):
    k = w.shape[0]
    w = w.reshape(k, heads, hd)
    partner = _rot_partner(w[:, :, rot0:rot0 + rot_dim], rot_dim // 2)
    pad = jnp.zeros((k, heads, LANES - hd - rot_dim), w.dtype)
    return jnp.concatenate([w, partner, pad], axis=-1).reshape(k, heads * LANES)


def _rope_lane_tables(seq, rot0, rot_dim, hd):
    inv = 1.0 / (ROPE_THETA ** (jnp.arange(0, rot_dim, 2, dtype=F32) / rot_dim))
    ang = jnp.arange(seq, dtype=F32)[:, None] * inv[None, :]
    cos = jnp.concatenate([jnp.cos(ang), jnp.cos(ang)], axis=1)
    sin = jnp.concatenate([jnp.sin(ang), jnp.sin(ang)], axis=1)
    cos_t = jnp.concatenate([jnp.ones((seq, rot0), F32), cos,
                             jnp.ones((seq, hd - rot0 - rot_dim), F32),
                             jnp.zeros((seq, LANES - hd), F32)], axis=1)
    sin_t = jnp.concatenate([jnp.zeros((seq, rot0), F32), sin,
                             jnp.zeros((seq, LANES - rot0 - rot_dim), F32)], axis=1)
    return cos_t, sin_t


def _pad_lanes(w, groups, width):
    k = w.shape[0]
    w = w.reshape(k, groups, width)
    return jnp.pad(w, ((0, 0), (0, 0), (0, LANES - width))).reshape(k, groups * LANES)


def _compress_params(pos, w1, w2, d):
    half = CMP_BLOCK // 2
    pos_p = jnp.pad(pos, ((0, 0), (0, LANES - d)))
    pos_a = pos_p[:half].reshape(1, half * LANES)
    pos_b = pos_p[half:].reshape(1, half * LANES)
    w1p = jnp.pad(w1.reshape(CMP_BLOCK, d, d), ((0, 0), (0, LANES - d), (0, LANES - d)))
    w1a = w1p[:half].reshape(half * LANES, LANES).astype(BF16)
    w1b = w1p[half:].reshape(half * LANES, LANES).astype(BF16)
    w2p = jnp.pad(w2, ((0, LANES - d), (0, LANES - d)))
    return pos_a, pos_b, w1a, w1b, w2p.astype(BF16)


def _cmp_to_sel(nr, nsb):
    cs = np.arange(nr) * CMP_STRIDE
    ce = cs + CMP_BLOCK
    ss = np.arange(LANES) * SEL_BLOCK
    se = ss + SEL_BLOCK
    ov = np.clip(np.minimum(ce[:, None], se[None, :]) - np.maximum(cs[:, None], ss[None, :]), 0, None)
    ov = ov.astype(np.float32) / np.float32(CMP_BLOCK)
    ov[:, nsb:] = 0.0
    ov[nr - 1:, :] = 0.0
    return jnp.asarray(ov, BF16)


def _gate_expand():
    e = np.zeros((3, NSA_HEADS // 2, LANES, LANES), np.float32)
    for br in range(3):
        for hp in range(NSA_HEADS // 2):
            for hh in range(2):
                e[br, hp, 3 * (2 * hp + hh) + br, hh * HALF:(hh + 1) * HALF] = 1.0
    return jnp.asarray(e, BF16)


def _key_block_onehot(seq):
    e = (np.arange(seq)[:, None] // SEL_BLOCK) == np.arange(LANES)[None, :]
    return jnp.asarray(e.astype(np.float32), BF16)


def kernel(x, mem, g_mix, w_in, cmp_pos_k, cmp_w1_k, cmp_w2_k, cmp_pos_v, cmp_w1_v, cmp_w2_v,
           mla_g_q, mla_w_uq, mla_g_kv, mla_w_uk, mla_w_uv, w_o_nsa, w_o_mla, w_out,
           g_xattn, g_mem, xa_wq, xa_wkv, xa_wo, g_mlp, w_ff1, w_ff2, g_final):
    b, s, d = x.shape
    assert d == D_MODEL and s % (CMP_STRIDE * 8) == 0 and s // SEL_BLOCK <= LANES
    assert g_mix.shape[0] == 1
    n = b * s
    T = _tiles(s)
    tm, tq, tk, rc = T["tm"], T["tq"], T["tk"], T["rc"]
    G = NSA_KV_HEADS
    bounds = [int(v) for v in np.cumsum(SPLITS)[:-1]]

    (w_qa, w_kc, w_vc, w_ks, w_vs, w_kw, w_vw, w_gn, w_cq, w_ckv, w_kr,
     w_ga, w_gb) = jnp.split(w_in[0], bounds, axis=1)
    nsa_rope = lambda w, heads: _pad_heads_rope(w, heads, NSA_DK, 0, NSA_ROT)
    w_rope = jnp.concatenate([nsa_rope(w_qa, NSA_HEADS), nsa_rope(w_kc, G), nsa_rope(w_ks, G),
                              nsa_rope(w_kw, G)], axis=1).astype(BF16)
    w_krp = jnp.concatenate([jnp.zeros((d, MLA_NOPE), F32), w_kr, _rot_partner(w_kr, MLA_ROPE // 2)],
                            axis=1).astype(BF16)
    w_vsw = jnp.concatenate([_pad_lanes(w_vs, G, NSA_DV), _pad_lanes(w_vw, G, NSA_DV)], axis=1).astype(BF16)
    w_vcp = _pad_lanes(w_vc, G, NSA_DV).astype(BF16)
    w_misc = jnp.concatenate([w_cq, w_ckv, jnp.pad(w_gn, ((0, 0), (0, LANES - w_gn.shape[1])))],
                             axis=1).astype(BF16)
    glog_blk = (MLA_Q_RANK + MLA_KV_RANK) // LANES
    cos_a, sin_a = _rope_lane_tables(s, 0, NSA_ROT, NSA_DK)
    cos_b, sin_b = _rope_lane_tables(s, MLA_NOPE, MLA_ROPE, MLA_NOPE + MLA_ROPE)
    shift_a = LANES - NSA_DK
    shift_b = LANES - (NSA_DK - MLA_NOPE)
    w_uq = _pad_heads_rope(mla_w_uq[0], MLA_HEADS, MLA_NOPE + MLA_ROPE, MLA_NOPE, MLA_ROPE).astype(BF16)
    w_uk = _pad_lanes(mla_w_uk[0], MLA_HEADS, MLA_NOPE).astype(BF16)
    w_uv = _pad_lanes(mla_w_uv[0], MLA_HEADS, MLA_DV).astype(BF16)

    x2 = x.reshape(n, d)
    xn = rmsnorm_rows(x2, g_mix[0], tm)
    log2e = float(np.log2(np.e))
    tmb = T["tm_big"]
    qk = projection(xn, w_rope, tm=tmb, tn=512, out_dtype=BF16, rope=(cos_a, sin_a, shift_a),
                    out_scale=(NSA_DK ** -0.5 * log2e, NSA_HEADS * LANES // 512),
                    head_major=(b, s), name="proj_qk_rope")
    krp = projection(xn, w_krp, tm=tm, tn=LANES, out_dtype=BF16, rope=(cos_b, sin_b, shift_b),
                     name="proj_k_rope_mla")
    vsw = projection(xn, w_vsw, tm=tmb, tn=512, out_dtype=BF16, head_major=(b, s), name="proj_v")
    vcp = projection(xn, w_vcp, tm=tmb, tn=512, out_dtype=BF16, head_major=(b, s), name="proj_vc")
    misc = projection(xn, w_misc, tm=tm, tn=w_misc.shape[1], out_dtype=F32, name="proj_misc")

    nr = s // CMP_STRIDE
    pk = _compress_params(cmp_pos_k[0], cmp_w1_k[0], cmp_w2_k[0], NSA_DK)
    pv = _compress_params(cmp_pos_v[0], cmp_w1_v[0], cmp_w2_v[0], NSA_DV)
    kc_rows = qk[:, QK_KC0:QK_KC0 + G].reshape(b, G, nr, CMP_STRIDE * LANES)
    kcc = compress(kc_rows, 0, *pk)
    vcc = compress(vcp.reshape(b, G, nr, CMP_STRIDE * LANES), 0, *pv)
    expand = _gate_expand()
    o_c, selb = nsa_compressed(qk, kcc, vcc, _cmp_to_sel(nr, s // SEL_BLOCK), misc, glog_blk, expand,
                               batch=b, seq=s, tq=T["tq_cmp"], rc=rc)
    o_s = flash_heads(qk, qk, vsw, mode="select", batch=b, seq=s, tq=tq, tk=tk, rc=rc,
                      q_head0=QK_Q0, k_head0=QK_KS0, v_head0=0, kv_shared=True, out_dtype=BF16,
                      selb=selb, onehot=_key_block_onehot(s), gate=(misc, glog_blk, expand, 1),
                      name="nsa_selected")
    o_w = flash_heads(qk, qk, vsw, mode="window", batch=b, seq=s, tq=T["tq_win"], tk=tk, rc=rc,
                      q_head0=QK_Q0, k_head0=QK_KW0, v_head0=G, kv_shared=True, out_dtype=BF16,
                      gate=(misc, glog_blk, expand, 2), name="nsa_window")

    qm = projection(misc, w_uq, tm=tm, tn=512, out_dtype=BF16, a_cols=(MLA_Q_RANK, 0), gain=mla_g_q[0],
                    rope=(cos_b, sin_b, shift_b), head_major=(b, s), name="mla_q",
                    out_scale=((MLA_NOPE + MLA_ROPE) ** -0.5 * log2e, MLA_HEADS * LANES // 512))
    km = projection(misc, w_uk, tm=tm, tn=512, out_dtype=BF16,
                    a_cols=(MLA_KV_RANK, MLA_Q_RANK // MLA_KV_RANK), gain=mla_g_kv[0], add=krp,
                    head_major=(b, s), name="mla_k")
    vm = projection(misc, w_uv, tm=tm, tn=512, out_dtype=BF16,
                    a_cols=(MLA_KV_RANK, MLA_Q_RANK // MLA_KV_RANK), gain=mla_g_kv[0],
                    head_major=(b, s), name="mla_v")
    o_b = flash_heads(qm, km, vm, mode="causal", batch=b, seq=s, tq=tq, tk=tk, rc=rc,
                      q_head0=0, k_head0=0, v_head0=0,
                      kv_shared=False, out_dtype=BF16, name="mla_attention")

    mixed = gated_mix(o_c, o_s, o_w, o_b, xn, w_ga.astype(BF16), w_gb.astype(BF16),
                      w_o_nsa[0].astype(BF16), w_o_mla[0].astype(BF16), tm=tm, tn=512)
    h1 = projection(mixed, w_out[0].astype(BF16), tm=tmb, tn=512, out_dtype=F32, res=x2, name="proj_out")

    kv_mem = projection(mem.reshape(b * MEM_LEN, d), xa_wkv[0].astype(BF16), tm=MEM_LEN, tn=512,
                        out_dtype=BF16, gain=g_mem[0], name="xattn_kv")
    h2 = memory_xattn(h1, g_xattn[0], xa_wq[0].astype(BF16), kv_mem, xa_wo[0].astype(BF16), seq=s,
                      tm=T["tm_x"])

    out = mlp_final(h2, g_mlp[0], w_ff1[0].astype(BF16), w_ff2[0].astype(BF16), g_final,
                    tm=T["tm_mlp"], tf=T["tf"])
    return out.reshape(b, s, d)
```

```python
import functools

import numpy as np
import jax
import jax.numpy as jnp
from jax import lax
from jax.experimental import pallas as pl
from jax.experimental.pallas import tpu as pltpu

F32 = jnp.float32
BF16 = jnp.bfloat16

D_MODEL = 2048
MEM_LEN = 256
ROPE_THETA = 500000.0
EPS = 1e-6
NEG = -1e30

NSA_HEADS = 16
NSA_KV_HEADS = 4
NSA_HPG = NSA_HEADS // NSA_KV_HEADS
NSA_DK = 96
NSA_DV = 64
NSA_ROT = NSA_DK // 4
CMP_BLOCK = 32
CMP_STRIDE = 16
SEL_BLOCK = 64
SEL_TOPK = 16
N_FORCED = 3
WINDOW = 512

MLA_HEADS = 16
MLA_NOPE = 64
MLA_ROPE = 32
MLA_DV = 64
MLA_Q_RANK = 512
MLA_KV_RANK = 256

XA_HEADS = 4
XA_DIM = 128
D_FF = 4 * D_MODEL

SPLITS = (NSA_HEADS * NSA_DK,
          NSA_KV_HEADS * NSA_DK, NSA_KV_HEADS * NSA_DV,
          NSA_KV_HEADS * NSA_DK, NSA_KV_HEADS * NSA_DV,
          NSA_KV_HEADS * NSA_DK, NSA_KV_HEADS * NSA_DV,
          NSA_HEADS * 3,
          MLA_Q_RANK, MLA_KV_RANK, MLA_ROPE,
          D_MODEL, D_MODEL)

LANES = 128
HALF = LANES // 2
VMEM_LIMIT = 56 * 1024 * 1024
QK_Q0 = 0
QK_KC0 = NSA_HEADS
QK_KS0 = NSA_HEADS + NSA_KV_HEADS
QK_KW0 = NSA_HEADS + 2 * NSA_KV_HEADS
QK_HEADS = NSA_HEADS + 3 * NSA_KV_HEADS
FLASH_HEADS = NSA_HPG


def _cparams(*sem):
    return pltpu.CompilerParams(dimension_semantics=sem, vmem_limit_bytes=VMEM_LIMIT)


def _tiles(seq):
    return dict(
        tm=min(1024, seq),
        tm_big=min(2048, seq),
        tm_x=min(512, seq),
        tm_mlp=min(512, seq),
        tq=min(2048, seq),
        tq_win=min(512, seq),
        tk=min(512, seq),
        rc=32,
        tq_cmp=min(256, seq),
        tf=1024,
    )


def _rms(x, g):
    return x * lax.rsqrt(jnp.mean(x * x, axis=-1, keepdims=True) + EPS) * g


def _rmsnorm_kernel(x_ref, g_ref, o_ref):
    o_ref[...] = _rms(x_ref[...], g_ref[...]).astype(o_ref.dtype)


def rmsnorm_rows(x, g, tm):
    n, d = x.shape
    return pl.pallas_call(
        _rmsnorm_kernel,
        out_shape=jax.ShapeDtypeStruct((n, d), BF16),
        grid=(n // tm,),
        in_specs=[pl.BlockSpec((tm, d), lambda i: (i, 0)),
                  pl.BlockSpec((1, d), lambda i: (0, 0))],
        out_specs=pl.BlockSpec((tm, d), lambda i: (i, 0)),
        compiler_params=_cparams("parallel"),
        name="rmsnorm_rows",
    )(x, g.reshape(1, d))


def _proj_kernel(*refs, has_gain, has_rope, roll_shift, rope_add, out_scale, has_add, has_res, head_major):
    it = iter(refs)
    a_ref = next(it)
    g_ref = next(it) if has_gain else None
    w_ref = next(it)
    cos_ref = next(it) if has_rope else None
    sin_ref = next(it) if has_rope else None
    add_ref = next(it) if has_add else None
    res_ref = next(it) if has_res else None
    o_ref = next(it)

    if has_gain:
        an_sc = next(it)

        @pl.when(pl.program_id(1) == 0)
        def _norm():
            an_sc[...] = _rms(a_ref[...], g_ref[...]).astype(BF16)

        a = an_sc[...]
    else:
        a = a_ref[...]
    y = jnp.dot(a, w_ref[...], preferred_element_type=F32)
    if out_scale is not None:
        factor, n_tiles = out_scale
        y = y * jnp.where(pl.program_id(1) < n_tiles, factor, 1.0)
    rope = lambda t: t * cos_ref[...] + pltpu.roll(t, roll_shift, 1) * sin_ref[...]
    if has_add:
        chunk = add_ref[...].astype(F32)
        if rope_add:
            chunk = rope(chunk)
    if has_rope or has_add or head_major:
        for h in range(y.shape[1] // LANES):
            yh = y[:, h * LANES:(h + 1) * LANES]
            if has_rope and not rope_add:
                yh = rope(yh)
            if has_add:
                yh = yh + chunk
            if head_major:
                o_ref[h] = yh.astype(o_ref.dtype)
            else:
                o_ref[:, h * LANES:(h + 1) * LANES] = yh.astype(o_ref.dtype)
    else:
        if has_res:
            y = res_ref[...] + y
        o_ref[...] = y.astype(o_ref.dtype)


def projection(a, w, *, tm, tn, out_dtype, a_cols=None, gain=None, rope=None, rope_add=False,
               out_scale=None, add=None, res=None, head_major=None, name="projection"):
    n = a.shape[0]
    k, nc = w.shape
    a_w, a_blk = a_cols if a_cols is not None else (a.shape[1], 0)
    assert a_w == k and n % tm == 0 and nc % tn == 0
    grid = (n // tm, nc // tn)
    in_specs = [pl.BlockSpec((tm, k), lambda i, j: (i, a_blk))]
    args = [a]
    if gain is not None:
        in_specs.append(pl.BlockSpec((1, k), lambda i, j: (0, 0)))
        args.append(gain.reshape(1, k))
    in_specs.append(pl.BlockSpec((k, tn), lambda i, j: (0, j)))
    args.append(w)
    roll_shift = 0
    if rope is not None:
        cos, sin, roll_shift = rope
        s_tiles = cos.shape[0] // tm
        for t in (cos, sin):
            in_specs.append(pl.BlockSpec((tm, LANES), lambda i, j: (i % s_tiles, 0)))
            args.append(t)
    if add is not None:
        add_arr, add_blk = add
        in_specs.append(pl.BlockSpec((tm, LANES), lambda i, j: (i, add_blk)))
        args.append(add_arr)
    if res is not None:
        in_specs.append(pl.BlockSpec((tm, tn), lambda i, j: (i, j)))
        args.append(res)
    if head_major is not None:
        b, s = head_major
        s_t = s // tm
        hpt = tn // LANES
        out_shape = jax.ShapeDtypeStruct((b, nc // LANES, s, LANES), out_dtype)
        out_spec = pl.BlockSpec((None, hpt, tm, LANES), lambda i, j: (i // s_t, j, i % s_t, 0))
    else:
        out_shape = jax.ShapeDtypeStruct((n, nc), out_dtype)
        out_spec = pl.BlockSpec((tm, tn), lambda i, j: (i, j))
    kern = functools.partial(
        _proj_kernel, has_gain=gain is not None, has_rope=rope is not None, roll_shift=roll_shift,
        rope_add=rope_add, out_scale=out_scale, has_add=add is not None, has_res=res is not None,
        head_major=head_major is not None)
    return pl.pallas_call(
        kern, out_shape=out_shape, grid=grid, in_specs=in_specs, out_specs=out_spec,
        scratch_shapes=[pltpu.VMEM((tm, k), BF16)] if gain is not None else [],
        compiler_params=_cparams("parallel", "arbitrary"), name=name,
    )(*args)


def _compress_kernel(r_ref, pa_ref, pb_ref, w1a_ref, w1b_ref, w2_ref, o_ref):
    r = r_ref[...].astype(F32)
    a = jnp.dot((r + pa_ref[...]).astype(BF16), w1a_ref[...], preferred_element_type=F32)
    b = jnp.dot((r + pb_ref[...]).astype(BF16), w1b_ref[...], preferred_element_type=F32)
    nr = a.shape[0]
    hid = a + pltpu.roll(b, nr - 1, 0)
    hid = jax.nn.gelu(hid)
    o_ref[...] = jnp.dot(hid.astype(BF16), w2_ref[...], preferred_element_type=F32).astype(o_ref.dtype)


def compress(r, head0, pos_a, pos_b, w1a, w1b, w2):
    b, _, nr, kk = r.shape
    g = NSA_KV_HEADS
    full = lambda shape: pl.BlockSpec(shape, lambda bi, gi: (0,) * len(shape))
    return pl.pallas_call(
        _compress_kernel,
        out_shape=jax.ShapeDtypeStruct((b, g, nr, LANES), BF16),
        grid=(b, g),
        in_specs=[pl.BlockSpec((None, None, nr, kk), lambda bi, gi: (bi, head0 + gi, 0, 0)),
                  full((1, kk)), full((1, kk)), full((kk, LANES)), full((kk, LANES)),
                  full((LANES, LANES))],
        out_specs=pl.BlockSpec((None, None, nr, LANES), lambda bi, gi: (bi, gi, 0, 0)),
        compiler_params=_cparams("parallel", "parallel"),
        name="nsa_compress",
    )(r, pos_a, pos_b, w1a, w1b, w2)


def _dot_nt(a, b):
    return lax.dot_general(a, b, (((1,), (1,)), ((), ())), preferred_element_type=F32)


def _split_dot(x, w):
    hi = x.astype(BF16)
    lo = (x - hi.astype(F32)).astype(BF16)
    return (jnp.dot(hi, w, preferred_element_type=F32) + jnp.dot(lo, w, preferred_element_type=F32))


def _nsa_cmp_kernel(q_ref, kcc_ref, vcc_ref, msel_ref, glog_ref, e_ref, oc_ref, sel_ref, s_sc, p_sc, imp_sc,
                    *, tq, rc, topk):
    s0 = pl.program_id(2) * tq
    nr = kcc_ref.shape[0]

    def attend(ncol):
        kcc = kcc_ref[0:ncol, :]
        for h in range(NSA_HPG):
            s_sc[h, :, 0:ncol] = _dot_nt(q_ref[h], kcc)
        lane_v = lax.broadcasted_iota(jnp.int32, (ncol, LANES), 1)
        vcc = vcc_ref[0:ncol, :]
        vm = jnp.concatenate([jnp.where(lane_v == HALF, jnp.ones_like(vcc), vcc), msel_ref[0:ncol, :]],
                             axis=1)
        dmat = (lax.broadcasted_iota(jnp.int32, (rc, ncol), 0)
                - lax.broadcasted_iota(jnp.int32, (rc, ncol), 1) * CMP_STRIDE)
        visible = lambda c: dmat + (s0 + c * rc - (CMP_BLOCK - 1)) >= 0
        outs = []
        imp = jnp.zeros((tq, LANES), F32)
        for h in range(NSA_HPG):
            row_max = []
            for c in range(tq // rc):
                s = jnp.where(visible(c), s_sc[h, pl.ds(c * rc, rc), 0:ncol], NEG)
                row_max.append(jnp.max(s, axis=-1, keepdims=True))
            for c in range(tq // rc):
                rows = pl.ds(c * rc, rc)
                p = jnp.exp2(s_sc[h, rows, 0:ncol] - row_max[c])
                p_sc[h, rows, 0:ncol] = jnp.where(visible(c), p, 0.0).astype(BF16)
            a = jnp.dot(p_sc[h, :, 0:ncol], vm, preferred_element_type=F32)
            denom = jnp.maximum(a[:, HALF:HALF + 1], 1e-30)
            outs.append(a[:, :LANES] / denom)
            imp = imp + a[:, LANES:] / denom
        imp_sc[...] = imp
        glog = glog_ref[...]
        lane = lax.broadcasted_iota(jnp.int32, (tq, LANES), 1)
        for pr in range(NSA_HPG // 2):
            gate = jax.nn.sigmoid(_split_dot(glog, e_ref[pr]))
            pair = jnp.where(lane < HALF, outs[2 * pr], pltpu.roll(outs[2 * pr + 1], HALF, 1))
            oc_ref[:, pr * LANES:(pr + 1) * LANES] = (gate * pair).astype(oc_ref.dtype)

    ntiles = (s0 + (tq - CMP_BLOCK)) // (CMP_STRIDE * LANES) + 1
    for k in range(1, nr // LANES + 1):
        @pl.when(ntiles == k)
        def _width(k=k):
            attend(k * LANES)

    blk = lax.broadcasted_iota(jnp.int32, (tq, LANES), 1)
    cur = (s0 + lax.broadcasted_iota(jnp.int32, (tq, LANES), 0)) // SEL_BLOCK
    valid = blk <= cur
    forced = (blk == 0) | (blk == cur) | (blk == cur - 1)
    score = jnp.where(valid, jnp.where(forced, -jnp.inf, imp_sc[...]), -1.0)
    sc = score.T
    rowid = lax.broadcasted_iota(jnp.int32, (LANES, tq), 0).astype(F32)

    def pick_one(_, sc):
        cm = jnp.max(sc, axis=0, keepdims=True)
        first = jnp.min(jnp.where(sc == cm, rowid, float(LANES)), axis=0, keepdims=True)
        return jnp.where(rowid == first, -jnp.inf, sc)

    picked = lax.fori_loop(0, topk - N_FORCED, pick_one, sc) == -jnp.inf
    sel_ref[...] = jnp.where(picked, 0.0, NEG).T.astype(sel_ref.dtype)


def nsa_compressed(qk, kcc, vcc, msel, misc, glog_blk, expand, *, batch, seq, tq, rc):
    g = NSA_KV_HEADS
    nq = seq // tq
    nr = kcc.shape[2]
    topk = min(SEL_TOPK, seq // SEL_BLOCK)
    assert topk > N_FORCED and nr % LANES == 0
    kern = functools.partial(_nsa_cmp_kernel, tq=tq, rc=rc, topk=topk)
    return pl.pallas_call(
        kern,
        out_shape=(jax.ShapeDtypeStruct((batch * seq, NSA_HEADS * NSA_DV), BF16),
                   jax.ShapeDtypeStruct((batch, g, seq, LANES), BF16)),
        grid=(batch, g, nq),
        in_specs=[
            pl.BlockSpec((None, NSA_HPG, tq, LANES), lambda b, gi, qi: (b, gi, qi, 0)),
            pl.BlockSpec((None, None, nr, LANES), lambda b, gi, qi: (b, gi, 0, 0)),
            pl.BlockSpec((None, None, nr, LANES), lambda b, gi, qi: (b, gi, 0, 0)),
            pl.BlockSpec((nr, LANES), lambda b, gi, qi: (0, 0)),
            pl.BlockSpec((tq, LANES), lambda b, gi, qi: (b * nq + qi, glog_blk)),
            pl.BlockSpec((None, 2, LANES, LANES), lambda b, gi, qi: (0, gi, 0, 0)),
        ],
        out_specs=(pl.BlockSpec((tq, 2 * LANES), lambda b, gi, qi: (b * nq + qi, gi)),
                   pl.BlockSpec((None, None, tq, LANES), lambda b, gi, qi: (b, gi, qi, 0))),
        scratch_shapes=[pltpu.VMEM((NSA_HPG, tq, nr), F32), pltpu.VMEM((NSA_HPG, tq, nr), BF16),
                        pltpu.VMEM((tq, LANES), F32)],
        compiler_params=_cparams("parallel", "parallel", "parallel"),
        name="nsa_compressed_select",
    )(qk, kcc, vcc, msel, misc, expand)


def _flash_steps(mode, nq, tq, tk):
    r = tq // tk
    qi_l, kt_l, first_l, last_l = [], [], [], []
    for qi in range(nq):
        hi = qi * r + r - 1
        lo = max(0, qi * r - (-(-(WINDOW - 1) // tk))) if mode == "window" else 0
        for kt in range(lo, hi + 1):
            qi_l.append(qi)
            kt_l.append(kt)
            first_l.append(int(kt == lo))
            last_l.append(int(kt == hi))
    return tuple(jnp.asarray(np.asarray(a, np.int32)) for a in (qi_l, kt_l, first_l, last_l))


def _flash_kernel(*refs, mode, tq, tk, rc, kv_shared, gated):
    it = iter(refs)
    qi_ref, kt_ref, first_ref, last_ref = next(it), next(it), next(it), next(it)
    q_ref, k_ref, v_ref = next(it), next(it), next(it)
    selb_ref = next(it) if mode == "select" else None
    oh_ref = next(it) if mode == "select" else None
    glog_ref = next(it) if gated else None
    e_ref = next(it) if gated else None
    o_ref = next(it)
    m_sc, acc_sc, s_sc, p_sc = next(it), next(it), next(it), next(it)
    qa_sc = next(it) if mode == "select" else None

    st = pl.program_id(2)
    s0 = qi_ref[st] * tq
    k0 = kt_ref[st] * tk

    @pl.when(first_ref[st] == 1)
    def _init():
        m_sc[...] = jnp.full(m_sc.shape, NEG, F32)
        acc_sc[...] = jnp.zeros(acc_sc.shape, F32)
        if mode == "select":
            for hh in range(FLASH_HEADS):
                qa_sc[hh] = jnp.concatenate([q_ref[hh], selb_ref[...]], axis=1)

    def chunk_cols(off, r0):
        if off is None:
            return 0, tk, True
        r1 = r0 + rc
        hi = min(tk, r1 - off)
        lo = max(0, r0 - off - WINDOW + 1) if mode == "window" else 0
        lo, hi = lo // LANES * LANES, -(-hi // LANES) * LANES
        clear = tk - 1 <= r0 - off and (mode != "window" or r1 - 1 - off < WINDOW)
        return lo, hi, clear

    def tile(off):
        nh = FLASH_HEADS
        r_lo = max(0, off) if off is not None else 0
        nrow = tq - r_lo
        if kv_shared:
            if mode == "select":
                qs = qa_sc[:, r_lo:, :].reshape(nh * nrow, 2 * LANES)
                ks = jnp.concatenate([k_ref[0], oh_ref[...]], axis=1)
            else:
                qs, ks = q_ref[:, r_lo:, :].reshape(nh * nrow, LANES), k_ref[0]
            s_sc[:, r_lo:, :] = _dot_nt(qs, ks).reshape(nh, nrow, tk)
        else:
            for hh in range(nh):
                s_sc[hh, r_lo:, :] = _dot_nt(q_ref[hh, r_lo:, :], k_ref[hh])
        lane_v = lax.broadcasted_iota(jnp.int32, (tk, LANES), 1)
        if off is not None:
            dmat = (lax.broadcasted_iota(jnp.int32, (rc, tk), 0)
                    - lax.broadcasted_iota(jnp.int32, (rc, tk), 1))
        for hh in range(FLASH_HEADS):
            v = v_ref[0 if kv_shared else hh]
            v_aug = jnp.where(lane_v == HALF, jnp.ones_like(v), v)

            for c in range(r_lo // rc, tq // rc):
                r0 = c * rc
                rows = pl.ds(r0, rc)
                lo, hi, clear = chunk_cols(off, r0)
                if lo > 0:
                    p_sc[hh, rows, 0:lo] = jnp.zeros((rc, lo), BF16)
                if hi < tk:
                    p_sc[hh, rows, max(hi, 0):tk] = jnp.zeros((rc, tk - max(hi, 0)), BF16)
                if hi <= lo:
                    continue
                s = s_sc[hh, rows, lo:hi]
                if not clear:
                    d = dmat[:, lo:hi] + (r0 - off)
                    msk = (d >= 0) & (d < WINDOW) if mode == "window" else d >= 0
                    s = jnp.where(msk, s, NEG)
                m_old = m_sc[hh, rows, :]
                m_new = jnp.maximum(m_old, jnp.max(s, axis=-1, keepdims=True))
                m_sc[hh, rows, :] = m_new
                acc_sc[hh, rows, :] = acc_sc[hh, rows, :] * jnp.exp2(m_old - m_new)
                p = jnp.exp2(s_sc[hh, rows, lo:hi] - jnp.concatenate([m_new] * ((hi - lo) // LANES), axis=1))
                if not clear:
                    p = jnp.where(msk, p, 0.0)
                p_sc[hh, rows, lo:hi] = p.astype(BF16)
            if not kv_shared:
                acc_sc[hh, r_lo:, :] += jnp.dot(p_sc[hh, r_lo:, :], v_aug, preferred_element_type=F32)
        if kv_shared:
            pv = jnp.dot(p_sc[:, r_lo:, :].reshape(nh * nrow, tk), v_aug, preferred_element_type=F32)
            acc_sc[:, r_lo:, :] += pv.reshape(nh, nrow, LANES)

    if mode == "window":
        offsets = [(j - (-(-(WINDOW - 1) // tk))) * tk for j in range(-(-(WINDOW - 1) // tk) + tq // tk)]
    else:
        offsets = [j * tk for j in range(tq // tk)]

        @pl.when(k0 + tk - 1 <= s0)
        def _interior():
            tile(None)

    for off in offsets:
        @pl.when(k0 - s0 == off)
        def _partial(off=off):
            tile(off)

    @pl.when(last_ref[st] == 1)
    def _finish():
        lane = lax.broadcasted_iota(jnp.int32, (tq, LANES), 1)
        if gated:
            glog = glog_ref[...]
        for pr in range(FLASH_HEADS // 2):
            outs = []
            for hh in (2 * pr, 2 * pr + 1):
                a = acc_sc[hh]
                outs.append(a / jnp.maximum(a[:, HALF:HALF + 1], 1e-30))
            out = jnp.where(lane < HALF, outs[0], pltpu.roll(outs[1], HALF, 1))
            if gated:
                out = jax.nn.sigmoid(_split_dot(glog, e_ref[pr])) * out
            o_ref[:, pr * LANES:(pr + 1) * LANES] = out.astype(o_ref.dtype)


def flash_heads(q, k, v, *, mode, batch, seq, tq, tk, rc, q_head0, k_head0, v_head0,
                kv_shared, out_dtype, selb=None, onehot=None, gate=None, name="flash"):
    nh = FLASH_HEADS
    ngroups = NSA_HEADS // nh
    nq = seq // tq
    assert tq % tk == 0 and tq % rc == 0
    tabs = _flash_steps(mode, nq, tq, tk)
    nsteps = tabs[0].shape[0]
    if kv_shared:
        kv_spec = lambda h0: pl.BlockSpec(
            (None, 1, tk, LANES), lambda b, g, st, qi, kt, fi, la: (b, h0 + g, kt[st], 0))
    else:
        kv_spec = lambda h0: pl.BlockSpec(
            (None, nh, tk, LANES), lambda b, g, st, qi, kt, fi, la: (b, h0 // nh + g, kt[st], 0))
    in_specs = [pl.BlockSpec((None, nh, tq, LANES),
                             lambda b, g, st, qi, kt, fi, la: (b, q_head0 // nh + g, qi[st], 0)),
                kv_spec(k_head0), kv_spec(v_head0)]
    args = [q, k, v]
    scratch = [pltpu.VMEM((nh, tq, LANES), F32), pltpu.VMEM((nh, tq, LANES), F32),
               pltpu.VMEM((nh, tq, tk), F32), pltpu.VMEM((nh, tq, tk), BF16)]
    if mode == "select":
        in_specs += [pl.BlockSpec((None, None, tq, LANES),
                                  lambda b, g, st, qi, kt, fi, la: (b, g, qi[st], 0)),
                     pl.BlockSpec((tk, LANES), lambda b, g, st, qi, kt, fi, la: (kt[st], 0))]
        args += [selb, onehot]
        scratch.append(pltpu.VMEM((nh, tq, 2 * LANES), BF16))
    if gate is not None:
        misc, glog_blk, expand, branch = gate
        in_specs += [pl.BlockSpec((tq, LANES),
                                  lambda b, g, st, qi, kt, fi, la: (b * nq + qi[st], glog_blk)),
                     pl.BlockSpec((None, nh // 2, LANES, LANES),
                                  lambda b, g, st, qi, kt, fi, la: (branch, g, 0, 0))]
        args += [misc, expand]
    kern = functools.partial(_flash_kernel, mode=mode, tq=tq, tk=tk, rc=rc,
                             kv_shared=kv_shared, gated=gate is not None)
    ow = nh * HALF
    grid_spec = pltpu.PrefetchScalarGridSpec(
        num_scalar_prefetch=4, grid=(batch, ngroups, nsteps), in_specs=in_specs,
        out_specs=pl.BlockSpec((tq, ow), lambda b, g, st, qi, kt, fi, la: (b * nq + qi[st], g)),
        scratch_shapes=scratch)
    return pl.pallas_call(
        kern,
        out_shape=jax.ShapeDtypeStruct((batch * seq, ngroups * ow), out_dtype),
        grid_spec=grid_spec,
        compiler_params=_cparams("parallel", "parallel", "arbitrary"),
        name=name,
    )(*tabs, *args)


def _mix_kernel(oc_ref, os_ref, ow_ref, ob_ref, xn_ref, wga_ref, wgb_ref, wa_ref, wb_ref, o_ref):
    oa = (oc_ref[...].astype(F32) + os_ref[...].astype(F32) + ow_ref[...].astype(F32)).astype(BF16)
    xn = xn_ref[...]
    ga = jax.nn.sigmoid(jnp.dot(xn, wga_ref[...], preferred_element_type=F32))
    ya = ga * jnp.dot(oa, wa_ref[...], preferred_element_type=F32)
    gb = jax.nn.sigmoid(jnp.dot(xn, wgb_ref[...], preferred_element_type=F32))
    yb = gb * jnp.dot(ob_ref[...], wb_ref[...], preferred_element_type=F32)
    o_ref[...] = (ya + yb).astype(o_ref.dtype)


def gated_mix(oc, osel, ow, ob, xn, wga, wgb, wa, wb, *, tm, tn):
    n, ka = oc.shape
    d = wa.shape[1]
    row = lambda w: pl.BlockSpec((tm, w), lambda i, j: (i, 0))
    col = lambda k: pl.BlockSpec((k, tn), lambda i, j: (0, j))
    return pl.pallas_call(
        _mix_kernel,
        out_shape=jax.ShapeDtypeStruct((n, d), BF16),
        grid=(n // tm, d // tn),
        in_specs=[row(ka), row(ka), row(ka), row(ob.shape[1]), row(xn.shape[1]),
                  col(wga.shape[0]), col(wgb.shape[0]), col(ka), col(wb.shape[0])],
        out_specs=pl.BlockSpec((tm, tn), lambda i, j: (i, j)),
        compiler_params=_cparams("parallel", "arbitrary"),
        name="gated_mix",
    )(oc, osel, ow, ob, xn, wga, wgb, wa, wb)


def _xattn_kernel(h_ref, g_ref, wq_ref, kv_ref, wo_ref, o_ref):
    h = h_ref[...]
    hn = _rms(h, g_ref[...]).astype(BF16)
    q = jnp.dot(hn, wq_ref[...], preferred_element_type=F32).astype(BF16)
    kv = kv_ref[...]
    outs = []
    for hd in range(XA_HEADS):
        qh = q[:, hd * XA_DIM:(hd + 1) * XA_DIM]
        kh = kv[:, hd * XA_DIM:(hd + 1) * XA_DIM]
        vh = kv[:, (XA_HEADS + hd) * XA_DIM:(XA_HEADS + hd + 1) * XA_DIM]
        s = _dot_nt(qh, kh) * (XA_DIM ** -0.5)
        e = jnp.exp(s - jnp.max(s, axis=-1, keepdims=True))
        p = e / jnp.sum(e, axis=-1, keepdims=True)
        outs.append(jnp.dot(p.astype(BF16), vh, preferred_element_type=F32))
    o = jnp.concatenate(outs, axis=1).astype(BF16)
    o_ref[...] = h + jnp.dot(o, wo_ref[...], preferred_element_type=F32)


def memory_xattn(h, g, wq, kv, wo, *, seq, tm):
    n, d = h.shape
    s_t = seq // tm
    dq = wq.shape[1]
    return pl.pallas_call(
        _xattn_kernel,
        out_shape=jax.ShapeDtypeStruct((n, d), F32),
        grid=(n // tm,),
        in_specs=[pl.BlockSpec((tm, d), lambda i: (i, 0)),
                  pl.BlockSpec((1, d), lambda i: (0, 0)),
                  pl.BlockSpec((d, dq), lambda i: (0, 0)),
                  pl.BlockSpec((MEM_LEN, 2 * dq), lambda i: (i // s_t, 0)),
                  pl.BlockSpec((dq, d), lambda i: (0, 0))],
        out_specs=pl.BlockSpec((tm, d), lambda i: (i, 0)),
        compiler_params=_cparams("parallel"),
        name="memory_xattn",
    )(h, g.reshape(1, d), wq, kv, wo)


def _mlp_kernel(h_ref, g_ref, w1_ref, w2_ref, gf_ref, o_ref, hn_sc, acc_sc, *, nf):
    f = pl.program_id(1)

    @pl.when(f == 0)
    def _init():
        hn_sc[...] = _rms(h_ref[...], g_ref[...]).astype(BF16)
        acc_sc[...] = jnp.zeros(acc_sc.shape, F32)

    u = jnp.dot(hn_sc[...], w1_ref[...], preferred_element_type=F32)
    u = jnp.square(jnp.maximum(u, 0.0))
    acc_sc[...] += jnp.dot(u.astype(BF16), w2_ref[...], preferred_element_type=F32)

    @pl.when(f == nf - 1)
    def _finish():
        o_ref[...] = _rms(h_ref[...] + acc_sc[...], gf_ref[...])


def mlp_final(h, g, w1, w2, gf, *, tm, tf):
    n, d = h.shape
    dff = w1.shape[1]
    nf = dff // tf
    return pl.pallas_call(
        functools.partial(_mlp_kernel, nf=nf),
        out_shape=jax.ShapeDtypeStruct((n, d), F32),
        grid=(n // tm, nf),
        in_specs=[pl.BlockSpec((tm, d), lambda i, f: (i, 0)),
                  pl.BlockSpec((1, d), lambda i, f: (0, 0)),
                  pl.BlockSpec((d, tf), lambda i, f: (0, f)),
                  pl.BlockSpec((tf, d), lambda i, f: (f, 0)),
                  pl.BlockSpec((1, d), lambda i, f: (0, 0))],
        out_specs=pl.BlockSpec((tm, d), lambda i, f: (i, 0)),
        scratch_shapes=[pltpu.VMEM((tm, d), BF16), pltpu.VMEM((tm, d), F32)],
        compiler_params=_cparams("parallel", "arbitrary"),
        name="mlp_final",
    )(h, g.reshape(1, d), w1, w2, gf.reshape(1, d))


def _rot_partner(w, half):
    return jnp.concatenate([-w[..., half:], w[..., :half]], axis=-1)


def _pad_heads_rope(w, heads, hd, rot0, rot_dim):
    k = w.shape[0]
    w = w.reshape(k, heads, hd)
    partner = _rot_partner(w[:, :, rot0:rot0 + rot_dim], rot_dim // 2)
    pad = jnp.zeros((k, heads, LANES - hd - rot_dim), w.dtype)
    return jnp.concatenate([w, partner, pad], axis=-1).reshape(k, heads * LANES)


def _rope_lane_tables(seq, rot0, rot_dim, hd):
    inv = 1.0 / (ROPE_THETA ** (jnp.arange(0, rot_dim, 2, dtype=F32) / rot_dim))
    ang = jnp.arange(seq, dtype=F32)[:, None] * inv[None, :]
    cos = jnp.concatenate([jnp.cos(ang), jnp.cos(ang)], axis=1)
    sin = jnp.concatenate([jnp.sin(ang), jnp.sin(ang)], axis=1)
    cos_t = jnp.concatenate([jnp.ones((seq, rot0), F32), cos,
                             jnp.ones((seq, hd - rot0 - rot_dim), F32),
                             jnp.zeros((seq, LANES - hd), F32)], axis=1)
    sin_t = jnp.concatenate([jnp.zeros((seq, rot0), F32), sin,
                             jnp.zeros((seq, LANES - rot0 - rot_dim), F32)], axis=1)
    return cos_t, sin_t


def _pad_lanes(w, groups, width):
    k = w.shape[0]
    w = w.reshape(k, groups, width)
    return jnp.pad(w, ((0, 0), (0, 0), (0, LANES - width))).reshape(k, groups * LANES)


def _compress_params(pos, w1, w2, d):
    half = CMP_BLOCK // 2
    pos_p = jnp.pad(pos, ((0, 0), (0, LANES - d)))
    pos_a = pos_p[:half].reshape(1, half * LANES)
    pos_b = pos_p[half:].reshape(1, half * LANES)
    w1p = jnp.pad(w1.reshape(CMP_BLOCK, d, d), ((0, 0), (0, LANES - d), (0, LANES - d)))
    w1a = w1p[:half].reshape(half * LANES, LANES).astype(BF16)
    w1b = w1p[half:].reshape(half * LANES, LANES).astype(BF16)
    w2p = jnp.pad(w2, ((0, LANES - d), (0, LANES - d)))
    return pos_a, pos_b, w1a, w1b, w2p.astype(BF16)


def _cmp_to_sel(nr, nsb):
    cs = np.arange(nr) * CMP_STRIDE
    ce = cs + CMP_BLOCK
    ss = np.arange(LANES) * SEL_BLOCK
    se = ss + SEL_BLOCK
    ov = np.clip(np.minimum(ce[:, None], se[None, :]) - np.maximum(cs[:, None], ss[None, :]), 0, None)
    ov = ov.astype(np.float32) / np.float32(CMP_BLOCK)
    ov[:, nsb:] = 0.0
    ov[nr - 1:, :] = 0.0
    return jnp.asarray(ov, BF16)


def _gate_expand():
    e = np.zeros((3, NSA_HEADS // 2, LANES, LANES), np.float32)
    for br in range(3):
        for hp in range(NSA_HEADS // 2):
            for hh in range(2):
                e[br, hp, 3 * (2 * hp + hh) + br, hh * HALF:(hh + 1) * HALF] = 1.0
    return jnp.asarray(e, BF16)


def _key_block_onehot(seq):
    e = (np.arange(seq)[:, None] // SEL_BLOCK) == np.arange(LANES)[None, :]
    return jnp.asarray(e.astype(np.float32), BF16)


def kernel(x, mem, g_mix, w_in, cmp_pos_k, cmp_w1_k, cmp_w2_k, cmp_pos_v, cmp_w1_v, cmp_w2_v,
           mla_g_q, mla_w_uq, mla_g_kv, mla_w_uk, mla_w_uv, w_o_nsa, w_o_mla, w_out,
           g_xattn, g_mem, xa_wq, xa_wkv, xa_wo, g_mlp, w_ff1, w_ff2, g_final):
    b, s, d = x.shape
    assert d == D_MODEL and s % (CMP_STRIDE * 8) == 0 and s // SEL_BLOCK <= LANES
    assert g_mix.shape[0] == 1
    n = b * s
    T = _tiles(s)
    tm, tq, tk, rc = T["tm"], T["tq"], T["tk"], T["rc"]
    G = NSA_KV_HEADS
    bounds = [int(v) for v in np.cumsum(SPLITS)[:-1]]

    (w_qa, w_kc, w_vc, w_ks, w_vs, w_kw, w_vw, w_gn, w_cq, w_ckv, w_kr,
     w_ga, w_gb) = jnp.split(w_in[0], bounds, axis=1)
    nsa_rope = lambda w, heads: _pad_heads_rope(w, heads, NSA_DK, 0, NSA_ROT)
    w_rope = jnp.concatenate([nsa_rope(w_qa, NSA_HEADS), nsa_rope(w_kc, G), nsa_rope(w_ks, G),
                              nsa_rope(w_kw, G)], axis=1).astype(BF16)
    w_krp = jnp.concatenate([jnp.zeros((d, MLA_NOPE), F32), w_kr, _rot_partner(w_kr, MLA_ROPE // 2)], axis=1)
    w_vsw = jnp.concatenate([_pad_lanes(w_vs, G, NSA_DV), _pad_lanes(w_vw, G, NSA_DV)], axis=1).astype(BF16)
    w_vcp = _pad_lanes(w_vc, G, NSA_DV).astype(BF16)
    w_misc = jnp.concatenate([w_cq, w_ckv, jnp.pad(w_gn, ((0, 0), (0, LANES - w_gn.shape[1]))), w_krp],
                             axis=1).astype(BF16)
    glog_blk = (MLA_Q_RANK + MLA_KV_RANK) // LANES
    krp_blk = glog_blk + 1
    cos_a, sin_a = _rope_lane_tables(s, 0, NSA_ROT, NSA_DK)
    cos_b, sin_b = _rope_lane_tables(s, MLA_NOPE, MLA_ROPE, MLA_NOPE + MLA_ROPE)
    shift_a = LANES - NSA_DK
    shift_b = LANES - (NSA_DK - MLA_NOPE)
    w_uq = _pad_heads_rope(mla_w_uq[0], MLA_HEADS, MLA_NOPE + MLA_ROPE, MLA_NOPE, MLA_ROPE).astype(BF16)
    w_uk = _pad_lanes(mla_w_uk[0], MLA_HEADS, MLA_NOPE).astype(BF16)
    w_uv = _pad_lanes(mla_w_uv[0], MLA_HEADS, MLA_DV).astype(BF16)

    x2 = x.reshape(n, d)
    xn = rmsnorm_rows(x2, g_mix[0], tm)
    log2e = float(np.log2(np.e))
    tmb = T["tm_big"]
    qk = projection(xn, w_rope, tm=tmb, tn=512, out_dtype=BF16, rope=(cos_a, sin_a, shift_a),
                    out_scale=(NSA_DK ** -0.5 * log2e, NSA_HEADS * LANES // 512),
                    head_major=(b, s), name="proj_qk_rope")
    vsw = projection(xn, w_vsw, tm=tmb, tn=512, out_dtype=BF16, head_major=(b, s), name="proj_v")
    vcp = projection(xn, w_vcp, tm=tmb, tn=512, out_dtype=BF16, head_major=(b, s), name="proj_vc")
    misc = projection(xn, w_misc, tm=tm, tn=w_misc.shape[1], out_dtype=F32, name="proj_misc")

    nr = s // CMP_STRIDE
    pk = _compress_params(cmp_pos_k[0], cmp_w1_k[0], cmp_w2_k[0], NSA_DK)
    pv = _compress_params(cmp_pos_v[0], cmp_w1_v[0], cmp_w2_v[0], NSA_DV)
    kc_rows = qk[:, QK_KC0:QK_KC0 + G].reshape(b, G, nr, CMP_STRIDE * LANES)
    kcc = compress(kc_rows, 0, *pk)
    vcc = compress(vcp.reshape(b, G, nr, CMP_STRIDE * LANES), 0, *pv)
    expand = _gate_expand()
    o_c, selb = nsa_compressed(qk, kcc, vcc, _cmp_to_sel(nr, s // SEL_BLOCK), misc, glog_blk, expand,
                               batch=b, seq=s, tq=T["tq_cmp"], rc=rc)
    o_s = flash_heads(qk, qk, vsw, mode="select", batch=b, seq=s, tq=tq, tk=tk, rc=rc,
                      q_head0=QK_Q0, k_head0=QK_KS0, v_head0=0, kv_shared=True, out_dtype=BF16,
                      selb=selb, onehot=_key_block_onehot(s), gate=(misc, glog_blk, expand, 1),
                      name="nsa_selected")
    o_w = flash_heads(qk, qk, vsw, mode="window", batch=b, seq=s, tq=T["tq_win"], tk=tk, rc=rc,
                      q_head0=QK_Q0, k_head0=QK_KW0, v_head0=G, kv_shared=True, out_dtype=BF16,
                      gate=(misc, glog_blk, expand, 2), name="nsa_window")

    qm = projection(misc, w_uq, tm=tm, tn=512, out_dtype=BF16, a_cols=(MLA_Q_RANK, 0), gain=mla_g_q[0],
                    rope=(cos_b, sin_b, shift_b), head_major=(b, s), name="mla_q",
                    out_scale=((MLA_NOPE + MLA_ROPE) ** -0.5 * log2e, MLA_HEADS * LANES // 512))
    km = projection(misc, w_uk, tm=tm, tn=512, out_dtype=BF16,
                    a_cols=(MLA_KV_RANK, MLA_Q_RANK // MLA_KV_RANK), gain=mla_g_kv[0],
                    add=(misc, krp_blk), rope=(cos_b, sin_b, shift_b), rope_add=True,
                    head_major=(b, s), name="mla_k")
    vm = projection(misc, w_uv, tm=tm, tn=512, out_dtype=BF16,
                    a_cols=(MLA_KV_RANK, MLA_Q_RANK // MLA_KV_RANK), gain=mla_g_kv[0],
                    head_major=(b, s), name="mla_v")
    o_b = flash_heads(qm, km, vm, mode="causal", batch=b, seq=s, tq=tq, tk=tk, rc=rc,
                      q_head0=0, k_head0=0, v_head0=0,
                      kv_shared=False, out_dtype=BF16, name="mla_attention")

    mixed = gated_mix(o_c, o_s, o_w, o_b, xn, w_ga.astype(BF16), w_gb.astype(BF16),
                      w_o_nsa[0].astype(BF16), w_o_mla[0].astype(BF16), tm=tm, tn=512)
    h1 = projection(mixed, w_out[0].astype(BF16), tm=tmb, tn=512, out_dtype=F32, res=x2, name="proj_out")

    kv_mem = projection(mem.reshape(b * MEM_LEN, d), xa_wkv[0].astype(BF16), tm=MEM_LEN, tn=512,
                        out_dtype=BF16, gain=g_mem[0], name="xattn_kv")
    h2 = memory_xattn(h1, g_xattn[0], xa_wq[0].astype(BF16), kv_mem, xa_wo[0].astype(BF16), seq=s,
                      tm=T["tm_x"])

    out = mlp_final(h2, g_mlp[0], w_ff1[0].astype(BF16), w_ff2[0].astype(BF16), g_final,
                    tm=T["tm_mlp"], tf=T["tf"])
    return out.reshape(b, s, d)
```

```python
import functools

import numpy as np
import jax
import jax.numpy as jnp
from jax import lax
from jax.experimental import pallas as pl
from jax.experimental.pallas import tpu as pltpu

F32 = jnp.float32
BF16 = jnp.bfloat16

D_MODEL = 2048
MEM_LEN = 256
ROPE_THETA = 500000.0
EPS = 1e-6
NEG = -1e30

NSA_HEADS = 16
NSA_KV_HEADS = 4
NSA_HPG = NSA_HEADS // NSA_KV_HEADS
NSA_DK = 96
NSA_DV = 64
NSA_ROT = NSA_DK // 4
CMP_BLOCK = 32
CMP_STRIDE = 16
SEL_BLOCK = 64
SEL_TOPK = 16
N_FORCED = 3
WINDOW = 512

MLA_HEADS = 16
MLA_NOPE = 64
MLA_ROPE = 32
MLA_DV = 64
MLA_Q_RANK = 512
MLA_KV_RANK = 256

XA_HEADS = 4
XA_DIM = 128
D_FF = 4 * D_MODEL

SPLITS = (NSA_HEADS * NSA_DK,
          NSA_KV_HEADS * NSA_DK, NSA_KV_HEADS * NSA_DV,
          NSA_KV_HEADS * NSA_DK, NSA_KV_HEADS * NSA_DV,
          NSA_KV_HEADS * NSA_DK, NSA_KV_HEADS * NSA_DV,
          NSA_HEADS * 3,
          MLA_Q_RANK, MLA_KV_RANK, MLA_ROPE,
          D_MODEL, D_MODEL)

LANES = 128
HALF = LANES // 2
VMEM_LIMIT = 56 * 1024 * 1024
QK_Q0 = 0
QK_KC0 = NSA_HEADS
QK_KS0 = NSA_HEADS + NSA_KV_HEADS
QK_KW0 = NSA_HEADS + 2 * NSA_KV_HEADS
QK_HEADS = NSA_HEADS + 3 * NSA_KV_HEADS
FLASH_HEADS = NSA_HPG
TOPK_ROWS = 32


def _cparams(*sem):
    return pltpu.CompilerParams(dimension_semantics=sem, vmem_limit_bytes=VMEM_LIMIT)


def _tiles(seq):
    return dict(
        tm=min(1024, seq),
        tm_big=min(2048, seq),
        tm_x=min(512, seq),
        tm_mlp=min(512, seq),
        tq=min(2048, seq),
        tq_win=min(512, seq),
        tk=min(512, seq),
        rc=32,
        tq_cmp=min(256, seq),
        tf=1024,
    )


def _rms(x, g):
    return x * lax.rsqrt(jnp.mean(x * x, axis=-1, keepdims=True) + EPS) * g


def _rmsnorm_kernel(x_ref, g_ref, o_ref):
    o_ref[...] = _rms(x_ref[...], g_ref[...]).astype(o_ref.dtype)


def rmsnorm_rows(x, g, tm):
    n, d = x.shape
    return pl.pallas_call(
        _rmsnorm_kernel,
        out_shape=jax.ShapeDtypeStruct((n, d), BF16),
        grid=(n // tm,),
        in_specs=[pl.BlockSpec((tm, d), lambda i: (i, 0)),
                  pl.BlockSpec((1, d), lambda i: (0, 0))],
        out_specs=pl.BlockSpec((tm, d), lambda i: (i, 0)),
        compiler_params=_cparams("parallel"),
        name="rmsnorm_rows",
    )(x, g.reshape(1, d))


def _proj_kernel(*refs, has_gain, has_rope, roll_shift, rope_add, out_scale, has_add, head_major):
    it = iter(refs)
    a_ref = next(it)
    g_ref = next(it) if has_gain else None
    w_ref = next(it)
    cos_ref = next(it) if has_rope else None
    sin_ref = next(it) if has_rope else None
    add_ref = next(it) if has_add else None
    o_ref = next(it)

    if has_gain:
        an_sc = next(it)

        @pl.when(pl.program_id(1) == 0)
        def _norm():
            an_sc[...] = _rms(a_ref[...], g_ref[...]).astype(BF16)

        a = an_sc[...]
    else:
        a = a_ref[...]
    y = jnp.dot(a, w_ref[...], preferred_element_type=F32)
    if out_scale is not None:
        factor, n_tiles = out_scale
        y = y * jnp.where(pl.program_id(1) < n_tiles, factor, 1.0)
    rope = lambda t: t * cos_ref[...] + pltpu.roll(t, roll_shift, 1) * sin_ref[...]
    if has_add:
        chunk = add_ref[...].astype(F32)
        if rope_add:
            chunk = rope(chunk)
    if has_rope or has_add or head_major:
        for h in range(y.shape[1] // LANES):
            yh = y[:, h * LANES:(h + 1) * LANES]
            if has_rope and not rope_add:
                yh = rope(yh)
            if has_add:
                yh = yh + chunk
            if head_major:
                o_ref[h] = yh.astype(o_ref.dtype)
            else:
                o_ref[:, h * LANES:(h + 1) * LANES] = yh.astype(o_ref.dtype)
    else:
        o_ref[...] = y.astype(o_ref.dtype)


def projection(a, w, *, tm, tn, out_dtype, a_cols=None, gain=None, rope=None, rope_add=False,
               out_scale=None, add=None, head_major=None, name="projection"):
    n = a.shape[0]
    k, nc = w.shape
    a_w, a_blk = a_cols if a_cols is not None else (a.shape[1], 0)
    assert a_w == k and n % tm == 0 and nc % tn == 0
    grid = (n // tm, nc // tn)
    in_specs = [pl.BlockSpec((tm, k), lambda i, j: (i, a_blk))]
    args = [a]
    if gain is not None:
        in_specs.append(pl.BlockSpec((1, k), lambda i, j: (0, 0)))
        args.append(gain.reshape(1, k))
    in_specs.append(pl.BlockSpec((k, tn), lambda i, j: (0, j)))
    args.append(w)
    roll_shift = 0
    if rope is not None:
        cos, sin, roll_shift = rope
        s_tiles = cos.shape[0] // tm
        for t in (cos, sin):
            in_specs.append(pl.BlockSpec((tm, LANES), lambda i, j: (i % s_tiles, 0)))
            args.append(t)
    if add is not None:
        add_arr, add_blk = add
        in_specs.append(pl.BlockSpec((tm, LANES), lambda i, j: (i, add_blk)))
        args.append(add_arr)
    if head_major is not None:
        b, s = head_major
        s_t = s // tm
        hpt = tn // LANES
        out_shape = jax.ShapeDtypeStruct((b, nc // LANES, s, LANES), out_dtype)
        out_spec = pl.BlockSpec((None, hpt, tm, LANES), lambda i, j: (i // s_t, j, i % s_t, 0))
    else:
        out_shape = jax.ShapeDtypeStruct((n, nc), out_dtype)
        out_spec = pl.BlockSpec((tm, tn), lambda i, j: (i, j))
    kern = functools.partial(
        _proj_kernel, has_gain=gain is not None, has_rope=rope is not None, roll_shift=roll_shift,
        rope_add=rope_add, out_scale=out_scale, has_add=add is not None,
        head_major=head_major is not None)
    return pl.pallas_call(
        kern, out_shape=out_shape, grid=grid, in_specs=in_specs, out_specs=out_spec,
        scratch_shapes=[pltpu.VMEM((tm, k), BF16)] if gain is not None else [],
        compiler_params=_cparams("parallel", "arbitrary"), name=name,
    )(*args)


def _compress_kernel(r_ref, pa_ref, pb_ref, w1a_ref, w1b_ref, w2_ref, o_ref):
    r = r_ref[...].astype(F32)
    a = jnp.dot((r + pa_ref[...]).astype(BF16), w1a_ref[...], preferred_element_type=F32)
    b = jnp.dot((r + pb_ref[...]).astype(BF16), w1b_ref[...], preferred_element_type=F32)
    nr = a.shape[0]
    hid = a + pltpu.roll(b, nr - 1, 0)
    hid = jax.nn.gelu(hid)
    o_ref[...] = jnp.dot(hid.astype(BF16), w2_ref[...], preferred_element_type=F32).astype(o_ref.dtype)


def compress(r, head0, pos_a, pos_b, w1a, w1b, w2):
    b, _, nr, kk = r.shape
    g = NSA_KV_HEADS
    full = lambda shape: pl.BlockSpec(shape, lambda bi, gi: (0,) * len(shape))
    return pl.pallas_call(
        _compress_kernel,
        out_shape=jax.ShapeDtypeStruct((b, g, nr, LANES), BF16),
        grid=(b, g),
        in_specs=[pl.BlockSpec((None, None, nr, kk), lambda bi, gi: (bi, head0 + gi, 0, 0)),
                  full((1, kk)), full((1, kk)), full((kk, LANES)), full((kk, LANES)),
                  full((LANES, LANES))],
        out_specs=pl.BlockSpec((None, None, nr, LANES), lambda bi, gi: (bi, gi, 0, 0)),
        compiler_params=_cparams("parallel", "parallel"),
        name="nsa_compress",
    )(r, pos_a, pos_b, w1a, w1b, w2)


def _dot_nt(a, b):
    return lax.dot_general(a, b, (((1,), (1,)), ((), ())), preferred_element_type=F32)


def _split_dot(x, w):
    hi = x.astype(BF16)
    lo = (x - hi.astype(F32)).astype(BF16)
    return (jnp.dot(hi, w, preferred_element_type=F32) + jnp.dot(lo, w, preferred_element_type=F32))


def _nsa_cmp_kernel(q_ref, kcc_ref, vcc_ref, msel_ref, glog_ref, e_ref, oc_ref, sel_ref, s_sc, p_sc, imp_sc,
                    *, tq, rc, topk):
    s0 = pl.program_id(2) * tq
    nr = kcc_ref.shape[0]

    def attend(ncol):
        kcc = kcc_ref[0:ncol, :]
        for h in range(NSA_HPG):
            s_sc[h, :, 0:ncol] = _dot_nt(q_ref[h], kcc)
        lane_v = lax.broadcasted_iota(jnp.int32, (ncol, LANES), 1)
        vcc = vcc_ref[0:ncol, :]
        vm = jnp.concatenate([jnp.where(lane_v == HALF, jnp.ones_like(vcc), vcc), msel_ref[0:ncol, :]],
                             axis=1)
        dmat = (lax.broadcasted_iota(jnp.int32, (rc, ncol), 0)
                - lax.broadcasted_iota(jnp.int32, (rc, ncol), 1) * CMP_STRIDE)
        visible = lambda c: dmat + (s0 + c * rc - (CMP_BLOCK - 1)) >= 0
        outs = []
        imp = jnp.zeros((tq, LANES), F32)
        for h in range(NSA_HPG):
            row_max = []
            for c in range(tq // rc):
                s = jnp.where(visible(c), s_sc[h, pl.ds(c * rc, rc), 0:ncol], NEG)
                row_max.append(jnp.max(s, axis=-1, keepdims=True))
            for c in range(tq // rc):
                rows = pl.ds(c * rc, rc)
                p = jnp.exp2(s_sc[h, rows, 0:ncol] - row_max[c])
                p_sc[h, rows, 0:ncol] = jnp.where(visible(c), p, 0.0).astype(BF16)
            a = jnp.dot(p_sc[h, :, 0:ncol], vm, preferred_element_type=F32)
            denom = jnp.maximum(a[:, HALF:HALF + 1], 1e-30)
            outs.append(a[:, :LANES] / denom)
            imp = imp + a[:, LANES:] / denom
        imp_sc[...] = imp
        glog = glog_ref[...]
        lane = lax.broadcasted_iota(jnp.int32, (tq, LANES), 1)
        for pr in range(NSA_HPG // 2):
            gate = jax.nn.sigmoid(_split_dot(glog, e_ref[pr]))
            pair = jnp.where(lane < HALF, outs[2 * pr], pltpu.roll(outs[2 * pr + 1], HALF, 1))
            oc_ref[:, pr * LANES:(pr + 1) * LANES] = (gate * pair).astype(oc_ref.dtype)

    ntiles = (s0 + (tq - CMP_BLOCK)) // (CMP_STRIDE * LANES) + 1
    for k in range(1, nr // LANES + 1):
        @pl.when(ntiles == k)
        def _width(k=k):
            attend(k * LANES)

    blk = lax.broadcasted_iota(jnp.int32, (tq, LANES), 1)
    cur = (s0 + lax.broadcasted_iota(jnp.int32, (tq, LANES), 0)) // SEL_BLOCK
    valid = blk <= cur
    forced = (blk == 0) | (blk == cur) | (blk == cur - 1)
    score = jnp.where(valid, jnp.where(forced, -jnp.inf, imp_sc[...]), -1.0)
    score_t = score.T
    nblk = (s0 + (tq - 1)) // SEL_BLOCK + 1
    for nrows in range(TOPK_ROWS, LANES + 1, TOPK_ROWS):
        @pl.when((nblk + (TOPK_ROWS - 1)) // TOPK_ROWS == nrows // TOPK_ROWS)
        def _select(nrows=nrows):
            rowid = lax.broadcasted_iota(jnp.int32, (nrows, tq), 0).astype(F32)

            def pick_one(_, sc):
                cm = jnp.max(sc, axis=0, keepdims=True)
                first = jnp.min(jnp.where(sc == cm, rowid, float(LANES)), axis=0, keepdims=True)
                return jnp.where(rowid == first, -jnp.inf, sc)

            picked = lax.fori_loop(0, topk - N_FORCED, pick_one, score_t[0:nrows]) == -jnp.inf
            bias = jnp.where(picked, 0.0, NEG)
            if nrows < LANES:
                bias = jnp.concatenate([bias, jnp.full((LANES - nrows, tq), NEG, F32)], axis=0)
            sel_ref[...] = bias.T.astype(sel_ref.dtype)


def nsa_compressed(qk, kcc, vcc, msel, misc, glog_blk, expand, *, batch, seq, tq, rc):
    g = NSA_KV_HEADS
    nq = seq // tq
    nr = kcc.shape[2]
    topk = min(SEL_TOPK, seq // SEL_BLOCK)
    assert topk > N_FORCED and nr % LANES == 0
    kern = functools.partial(_nsa_cmp_kernel, tq=tq, rc=rc, topk=topk)
    return pl.pallas_call(
        kern,
        out_shape=(jax.ShapeDtypeStruct((batch * seq, NSA_HEADS * NSA_DV), BF16),
                   jax.ShapeDtypeStruct((batch, g, seq, LANES), BF16)),
        grid=(batch, g, nq),
        in_specs=[
            pl.BlockSpec((None, NSA_HPG, tq, LANES), lambda b, gi, qi: (b, gi, qi, 0)),
            pl.BlockSpec((None, None, nr, LANES), lambda b, gi, qi: (b, gi, 0, 0)),
            pl.BlockSpec((None, None, nr, LANES), lambda b, gi, qi: (b, gi, 0, 0)),
            pl.BlockSpec((nr, LANES), lambda b, gi, qi: (0, 0)),
            pl.BlockSpec((tq, LANES), lambda b, gi, qi: (b * nq + qi, glog_blk)),
            pl.BlockSpec((None, 2, LANES, LANES), lambda b, gi, qi: (0, gi, 0, 0)),
        ],
        out_specs=(pl.BlockSpec((tq, 2 * LANES), lambda b, gi, qi: (b * nq + qi, gi)),
                   pl.BlockSpec((None, None, tq, LANES), lambda b, gi, qi: (b, gi, qi, 0))),
        scratch_shapes=[pltpu.VMEM((NSA_HPG, tq, nr), F32), pltpu.VMEM((NSA_HPG, tq, nr), BF16),
                        pltpu.VMEM((tq, LANES), F32)],
        compiler_params=_cparams("parallel", "parallel", "parallel"),
        name="nsa_compressed_select",
    )(qk, kcc, vcc, msel, misc, expand)


def _flash_steps(mode, nq, tq, tk):
    r = tq // tk
    qi_l, kt_l, first_l, last_l = [], [], [], []
    for qi in range(nq):
        hi = qi * r + r - 1
        lo = max(0, qi * r - (-(-(WINDOW - 1) // tk))) if mode == "window" else 0
        for kt in range(lo, hi + 1):
            qi_l.append(qi)
            kt_l.append(kt)
            first_l.append(int(kt == lo))
            last_l.append(int(kt == hi))
    return tuple(jnp.asarray(np.asarray(a, np.int32)) for a in (qi_l, kt_l, first_l, last_l))


def _flash_kernel(*refs, mode, tq, tk, rc, kv_shared, gated):
    it = iter(refs)
    qi_ref, kt_ref, first_ref, last_ref = next(it), next(it), next(it), next(it)
    q_ref, k_ref, v_ref = next(it), next(it), next(it)
    selb_ref = next(it) if mode == "select" else None
    oh_ref = next(it) if mode == "select" else None
    glog_ref = next(it) if gated else None
    e_ref = next(it) if gated else None
    o_ref = next(it)
    m_sc, acc_sc, s_sc, p_sc = next(it), next(it), next(it), next(it)
    qa_sc = next(it) if mode == "select" else None

    st = pl.program_id(2)
    s0 = qi_ref[st] * tq
    k0 = kt_ref[st] * tk

    @pl.when(first_ref[st] == 1)
    def _init():
        m_sc[...] = jnp.full(m_sc.shape, NEG, F32)
        acc_sc[...] = jnp.zeros(acc_sc.shape, F32)
        if mode == "select":
            for hh in range(FLASH_HEADS):
                qa_sc[hh] = jnp.concatenate([q_ref[hh], selb_ref[...]], axis=1)

    def chunk_cols(off, r0):
        if off is None:
            return 0, tk, True
        r1 = r0 + rc
        hi = min(tk, r1 - off)
        lo = max(0, r0 - off - WINDOW + 1) if mode == "window" else 0
        lo, hi = lo // LANES * LANES, -(-hi // LANES) * LANES
        clear = tk - 1 <= r0 - off and (mode != "window" or r1 - 1 - off < WINDOW)
        return lo, hi, clear

    def tile(off):
        nh = FLASH_HEADS
        r_lo = max(0, off) if off is not None else 0
        nrow = tq - r_lo
        if kv_shared:
            if mode == "select":
                qs = qa_sc[:, r_lo:, :].reshape(nh * nrow, 2 * LANES)
                ks = jnp.concatenate([k_ref[0], oh_ref[...]], axis=1)
            else:
                qs, ks = q_ref[:, r_lo:, :].reshape(nh * nrow, LANES), k_ref[0]
            s_sc[:, r_lo:, :] = _dot_nt(qs, ks).reshape(nh, nrow, tk)
        else:
            for hh in range(nh):
                s_sc[hh, r_lo:, :] = _dot_nt(q_ref[hh, r_lo:, :], k_ref[hh])
        lane_v = lax.broadcasted_iota(jnp.int32, (tk, LANES), 1)
        if off is not None:
            dmat = (lax.broadcasted_iota(jnp.int32, (rc, tk), 0)
                    - lax.broadcasted_iota(jnp.int32, (rc, tk), 1))
        for hh in range(FLASH_HEADS):
            v = v_ref[0 if kv_shared else hh]
            v_aug = jnp.where(lane_v == HALF, jnp.ones_like(v), v)

            for c in range(r_lo // rc, tq // rc):
                r0 = c * rc
                rows = pl.ds(r0, rc)
                lo, hi, clear = chunk_cols(off, r0)
                if lo > 0:
                    p_sc[hh, rows, 0:lo] = jnp.zeros((rc, lo), BF16)
                if hi < tk:
                    p_sc[hh, rows, max(hi, 0):tk] = jnp.zeros((rc, tk - max(hi, 0)), BF16)
                if hi <= lo:
                    continue
                s = s_sc[hh, rows, lo:hi]
                if not clear:
                    d = dmat[:, lo:hi] + (r0 - off)
                    msk = (d >= 0) & (d < WINDOW) if mode == "window" else d >= 0
                    s = jnp.where(msk, s, NEG)
                m_old = m_sc[hh, rows, :]
                m_new = jnp.maximum(m_old, jnp.max(s, axis=-1, keepdims=True))
                m_sc[hh, rows, :] = m_new
                acc_sc[hh, rows, :] = acc_sc[hh, rows, :] * jnp.exp2(m_old - m_new)
                p = jnp.exp2(s_sc[hh, rows, lo:hi] - jnp.concatenate([m_new] * ((hi - lo) // LANES), axis=1))
                if not clear:
                    p = jnp.where(msk, p, 0.0)
                p_sc[hh, rows, lo:hi] = p.astype(BF16)
            if not kv_shared:
                acc_sc[hh, r_lo:, :] += jnp.dot(p_sc[hh, r_lo:, :], v_aug, preferred_element_type=F32)
        if kv_shared:
            pv = jnp.dot(p_sc[:, r_lo:, :].reshape(nh * nrow, tk), v_aug, preferred_element_type=F32)
            acc_sc[:, r_lo:, :] += pv.reshape(nh, nrow, LANES)

    if mode == "window":
        offsets = [(j - (-(-(WINDOW - 1) // tk))) * tk for j in range(-(-(WINDOW - 1) // tk) + tq // tk)]
    else:
        offsets = [j * tk for j in range(tq // tk)]

        @pl.when(k0 + tk - 1 <= s0)
        def _interior():
            tile(None)

    for off in offsets:
        @pl.when(k0 - s0 == off)
        def _partial(off=off):
            tile(off)

    @pl.when(last_ref[st] == 1)
    def _finish():
        lane = lax.broadcasted_iota(jnp.int32, (tq, LANES), 1)
        if gated:
            glog = glog_ref[...]
        for pr in range(FLASH_HEADS // 2):
            outs = []
            for hh in (2 * pr, 2 * pr + 1):
                a = acc_sc[hh]
                outs.append(a / jnp.maximum(a[:, HALF:HALF + 1], 1e-30))
            out = jnp.where(lane < HALF, outs[0], pltpu.roll(outs[1], HALF, 1))
            if gated:
                out = jax.nn.sigmoid(_split_dot(glog, e_ref[pr])) * out
            o_ref[:, pr * LANES:(pr + 1) * LANES] = out.astype(o_ref.dtype)


def flash_heads(q, k, v, *, mode, batch, seq, tq, tk, rc, q_head0, k_head0, v_head0,
                kv_shared, out_dtype, selb=None, onehot=None, gate=None, name="flash"):
    nh = FLASH_HEADS
    ngroups = NSA_HEADS // nh
    nq = seq // tq
    assert tq % tk == 0 and tq % rc == 0
    tabs = _flash_steps(mode, nq, tq, tk)
    nsteps = tabs[0].shape[0]
    if kv_shared:
        kv_spec = lambda h0: pl.BlockSpec(
            (None, 1, tk, LANES), lambda b, g, st, qi, kt, fi, la: (b, h0 + g, kt[st], 0))
    else:
        kv_spec = lambda h0: pl.BlockSpec(
            (None, nh, tk, LANES), lambda b, g, st, qi, kt, fi, la: (b, h0 // nh + g, kt[st], 0))
    in_specs = [pl.BlockSpec((None, nh, tq, LANES),
                             lambda b, g, st, qi, kt, fi, la: (b, q_head0 // nh + g, qi[st], 0)),
                kv_spec(k_head0), kv_spec(v_head0)]
    args = [q, k, v]
    scratch = [pltpu.VMEM((nh, tq, LANES), F32), pltpu.VMEM((nh, tq, LANES), F32),
               pltpu.VMEM((nh, tq, tk), F32), pltpu.VMEM((nh, tq, tk), BF16)]
    if mode == "select":
        in_specs += [pl.BlockSpec((None, None, tq, LANES),
                                  lambda b, g, st, qi, kt, fi, la: (b, g, qi[st], 0)),
                     pl.BlockSpec((tk, LANES), lambda b, g, st, qi, kt, fi, la: (kt[st], 0))]
        args += [selb, onehot]
        scratch.append(pltpu.VMEM((nh, tq, 2 * LANES), BF16))
    if gate is not None:
        misc, glog_blk, expand, branch = gate
        in_specs += [pl.BlockSpec((tq, LANES),
                                  lambda b, g, st, qi, kt, fi, la: (b * nq + qi[st], glog_blk)),
                     pl.BlockSpec((None, nh // 2, LANES, LANES),
                                  lambda b, g, st, qi, kt, fi, la: (branch, g, 0, 0))]
        args += [misc, expand]
    kern = functools.partial(_flash_kernel, mode=mode, tq=tq, tk=tk, rc=rc,
                             kv_shared=kv_shared, gated=gate is not None)
    ow = nh * HALF
    grid_spec = pltpu.PrefetchScalarGridSpec(
        num_scalar_prefetch=4, grid=(batch, ngroups, nsteps), in_specs=in_specs,
        out_specs=pl.BlockSpec((tq, ow), lambda b, g, st, qi, kt, fi, la: (b * nq + qi[st], g)),
        scratch_shapes=scratch)
    return pl.pallas_call(
        kern,
        out_shape=jax.ShapeDtypeStruct((batch * seq, ngroups * ow), out_dtype),
        grid_spec=grid_spec,
        compiler_params=_cparams("parallel", "parallel", "arbitrary"),
        name=name,
    )(*tabs, *args)


def _mix_kernel(oc_ref, os_ref, ow_ref, ob_ref, xn_ref, wga_ref, wgb_ref, wa_ref, wb_ref, o_ref):
    oa = (oc_ref[...].astype(F32) + os_ref[...].astype(F32) + ow_ref[...].astype(F32)).astype(BF16)
    xn = xn_ref[...]
    ga = jax.nn.sigmoid(jnp.dot(xn, wga_ref[...], preferred_element_type=F32))
    ya = ga * jnp.dot(oa, wa_ref[...], preferred_element_type=F32)
    gb = jax.nn.sigmoid(jnp.dot(xn, wgb_ref[...], preferred_element_type=F32))
    yb = gb * jnp.dot(ob_ref[...], wb_ref[...], preferred_element_type=F32)
    o_ref[...] = (ya + yb).astype(o_ref.dtype)


def gated_mix(oc, osel, ow, ob, xn, wga, wgb, wa, wb, *, tm, tn):
    n, ka = oc.shape
    d = wa.shape[1]
    row = lambda w: pl.BlockSpec((tm, w), lambda i, j: (i, 0))
    col = lambda k: pl.BlockSpec((k, tn), lambda i, j: (0, j))
    return pl.pallas_call(
        _mix_kernel,
        out_shape=jax.ShapeDtypeStruct((n, d), BF16),
        grid=(n // tm, d // tn),
        in_specs=[row(ka), row(ka), row(ka), row(ob.shape[1]), row(xn.shape[1]),
                  col(wga.shape[0]), col(wgb.shape[0]), col(ka), col(wb.shape[0])],
        out_specs=pl.BlockSpec((tm, tn), lambda i, j: (i, j)),
        compiler_params=_cparams("parallel", "arbitrary"),
        name="gated_mix",
    )(oc, osel, ow, ob, xn, wga, wgb, wa, wb)


def _xattn_kernel(x_ref, mix_ref, wout_ref, g_ref, wq_ref, kv_ref, wo_ref, o_ref):
    h = x_ref[...] + jnp.dot(mix_ref[...], wout_ref[...], preferred_element_type=F32)
    hn = _rms(h, g_ref[...]).astype(BF16)
    q = jnp.dot(hn, wq_ref[...], preferred_element_type=F32).astype(BF16)
    kv = kv_ref[...]
    outs = []
    for hd in range(XA_HEADS):
        qh = q[:, hd * XA_DIM:(hd + 1) * XA_DIM]
        kh = kv[:, hd * XA_DIM:(hd + 1) * XA_DIM]
        vh = kv[:, (XA_HEADS + hd) * XA_DIM:(XA_HEADS + hd + 1) * XA_DIM]
        s = _dot_nt(qh, kh) * (XA_DIM ** -0.5)
        e = jnp.exp(s - jnp.max(s, axis=-1, keepdims=True))
        p = e / jnp.sum(e, axis=-1, keepdims=True)
        outs.append(jnp.dot(p.astype(BF16), vh, preferred_element_type=F32))
    o = jnp.concatenate(outs, axis=1).astype(BF16)
    o_ref[...] = h + jnp.dot(o, wo_ref[...], preferred_element_type=F32)


def out_proj_xattn(x, mixed, wout, g, wq, kv, wo, *, seq, tm):
    n, d = x.shape
    s_t = seq // tm
    dq = wq.shape[1]
    resident = lambda shape: pl.BlockSpec(shape, lambda i: (0, 0), pipeline_mode=pl.Buffered(1))
    return pl.pallas_call(
        _xattn_kernel,
        out_shape=jax.ShapeDtypeStruct((n, d), F32),
        grid=(n // tm,),
        in_specs=[pl.BlockSpec((tm, d), lambda i: (i, 0)),
                  pl.BlockSpec((tm, d), lambda i: (i, 0)),
                  resident((d, d)),
                  pl.BlockSpec((1, d), lambda i: (0, 0)),
                  resident((d, dq)),
                  pl.BlockSpec((MEM_LEN, 2 * dq), lambda i: (i // s_t, 0)),
                  resident((dq, d))],
        out_specs=pl.BlockSpec((tm, d), lambda i: (i, 0)),
        compiler_params=_cparams("parallel"),
        name="out_proj_xattn",
    )(x, mixed, wout, g.reshape(1, d), wq, kv, wo)


def _mlp_kernel(h_ref, g_ref, w1_ref, w2_ref, gf_ref, o_ref, hn_sc, acc_sc, *, nf):
    f = pl.program_id(1)

    @pl.when(f == 0)
    def _init():
        hn_sc[...] = _rms(h_ref[...], g_ref[...]).astype(BF16)
        acc_sc[...] = jnp.zeros(acc_sc.shape, F32)

    u = jnp.dot(hn_sc[...], w1_ref[...], preferred_element_type=F32)
    u = jnp.square(jnp.maximum(u, 0.0))
    acc_sc[...] += jnp.dot(u.astype(BF16), w2_ref[...], preferred_element_type=F32)

    @pl.when(f == nf - 1)
    def _finish():
        o_ref[...] = _rms(h_ref[...] + acc_sc[...], gf_ref[...])


def mlp_final(h, g, w1, w2, gf, *, tm, tf):
    n, d = h.shape
    dff = w1.shape[1]
    nf = dff // tf
    return pl.pallas_call(
        functools.partial(_mlp_kernel, nf=nf),
        out_shape=jax.ShapeDtypeStruct((n, d), F32),
        grid=(n // tm, nf),
        in_specs=[pl.BlockSpec((tm, d), lambda i, f: (i, 0)),
                  pl.BlockSpec((1, d), lambda i, f: (0, 0)),
                  pl.BlockSpec((d, tf), lambda i, f: (0, f)),
                  pl.BlockSpec((tf, d), lambda i, f: (f, 0)),
                  pl.BlockSpec((1, d), lambda i, f: (0, 0))],
        out_specs=pl.BlockSpec((tm, d), lambda i, f: (i, 0)),
        scratch_shapes=[pltpu.VMEM((tm, d), BF16), pltpu.VMEM((tm, d), F32)],
        compiler_params=_cparams("parallel", "arbitrary"),
        name="mlp_final",
    )(h, g.reshape(1, d), w1, w2, gf.reshape(1, d))


def _rot_partner(w, half):
    return jnp.concatenate([-w[..., half:], w[..., :half]], axis=-1)


def _pad_heads_rope(w, heads, hd, rot0, rot_dim):
    k = w.shape[0]
    w = w.reshape(k, heads, hd)
    partner = _rot_partner(w[:, :, rot0:rot0 + rot_dim], rot_dim // 2)
    pad = jnp.zeros((k, heads, LANES - hd - rot_dim), w.dtype)
    return jnp.concatenate([w, partner, pad], axis=-1).reshape(k, heads * LANES)


def _rope_lane_tables(seq, rot0, rot_dim, hd):
    inv = 1.0 / (ROPE_THETA ** (jnp.arange(0, rot_dim, 2, dtype=F32) / rot_dim))
    ang = jnp.arange(seq, dtype=F32)[:, None] * inv[None, :]
    cos = jnp.concatenate([jnp.cos(ang), jnp.cos(ang)], axis=1)
    sin = jnp.concatenate([jnp.sin(ang), jnp.sin(ang)], axis=1)
    cos_t = jnp.concatenate([jnp.ones((seq, rot0), F32), cos,
                             jnp.ones((seq, hd - rot0 - rot_dim), F32),
                             jnp.zeros((seq, LANES - hd), F32)], axis=1)
    sin_t = jnp.concatenate([jnp.zeros((seq, rot0), F32), sin,
                             jnp.zeros((seq, LANES - rot0 - rot_dim), F32)], axis=1)
    return cos_t, sin_t


def _pad_lanes(w, groups, width):
    k = w.shape[0]
    w = w.reshape(k, groups, width)
    return jnp.pad(w, ((0, 0), (0, 0), (0, LANES - width))).reshape(k, groups * LANES)


def _compress_params(pos, w1, w2, d):
    half = CMP_BLOCK // 2
    pos_p = jnp.pad(pos, ((0, 0), (0, LANES - d)))
    pos_a = pos_p[:half].reshape(1, half * LANES)
    pos_b = pos_p[half:].reshape(1, half * LANES)
    w1p = jnp.pad(w1.reshape(CMP_BLOCK, d, d), ((0, 0), (0, LANES - d), (0, LANES - d)))
    w1a = w1p[:half].reshape(half * LANES, LANES).astype(BF16)
    w1b = w1p[half:].reshape(half * LANES, LANES).astype(BF16)
    w2p = jnp.pad(w2, ((0, LANES - d), (0, LANES - d)))
    return pos_a, pos_b, w1a, w1b, w2p.astype(BF16)


def _cmp_to_sel(nr, nsb):
    cs = np.arange(nr) * CMP_STRIDE
    ce = cs + CMP_BLOCK
    ss = np.arange(LANES) * SEL_BLOCK
    se = ss + SEL_BLOCK
    ov = np.clip(np.minimum(ce[:, None], se[None, :]) - np.maximum(cs[:, None], ss[None, :]), 0, None)
    ov = ov.astype(np.float32) / np.float32(CMP_BLOCK)
    ov[:, nsb:] = 0.0
    ov[nr - 1:, :] = 0.0
    return jnp.asarray(ov, BF16)


def _gate_expand():
    e = np.zeros((3, NSA_HEADS // 2, LANES, LANES), np.float32)
    for br in range(3):
        for hp in range(NSA_HEADS // 2):
            for hh in range(2):
                e[br, hp, 3 * (2 * hp + hh) + br, hh * HALF:(hh + 1) * HALF] = 1.0
    return jnp.asarray(e, BF16)


def _key_block_onehot(seq):
    e = (np.arange(seq)[:, None] // SEL_BLOCK) == np.arange(LANES)[None, :]
    return jnp.asarray(e.astype(np.float32), BF16)


def kernel(x, mem, g_mix, w_in, cmp_pos_k, cmp_w1_k, cmp_w2_k, cmp_pos_v, cmp_w1_v, cmp_w2_v,
           mla_g_q, mla_w_uq, mla_g_kv, mla_w_uk, mla_w_uv, w_o_nsa, w_o_mla, w_out,
           g_xattn, g_mem, xa_wq, xa_wkv, xa_wo, g_mlp, w_ff1, w_ff2, g_final):
    b, s, d = x.shape
    assert d == D_MODEL and s % (CMP_STRIDE * 8) == 0 and s // SEL_BLOCK <= LANES
    assert g_mix.shape[0] == 1
    n = b * s
    T = _tiles(s)
    tm, tq, tk, rc = T["tm"], T["tq"], T["tk"], T["rc"]
    G = NSA_KV_HEADS
    bounds = [int(v) for v in np.cumsum(SPLITS)[:-1]]

    (w_qa, w_kc, w_vc, w_ks, w_vs, w_kw, w_vw, w_gn, w_cq, w_ckv, w_kr,
     w_ga, w_gb) = jnp.split(w_in[0], bounds, axis=1)
    nsa_rope = lambda w, heads: _pad_heads_rope(w, heads, NSA_DK, 0, NSA_ROT)
    w_rope = jnp.concatenate([nsa_rope(w_qa, NSA_HEADS), nsa_rope(w_kc, G), nsa_rope(w_ks, G),
                              nsa_rope(w_kw, G)], axis=1).astype(BF16)
    w_krp = jnp.concatenate([jnp.zeros((d, MLA_NOPE), F32), w_kr, _rot_partner(w_kr, MLA_ROPE // 2)], axis=1)
    w_vsw = jnp.concatenate([_pad_lanes(w_vs, G, NSA_DV), _pad_lanes(w_vw, G, NSA_DV)], axis=1).astype(BF16)
    w_vcp = _pad_lanes(w_vc, G, NSA_DV).astype(BF16)
    w_misc = jnp.concatenate([w_cq, w_ckv, jnp.pad(w_gn, ((0, 0), (0, LANES - w_gn.shape[1]))), w_krp],
                             axis=1).astype(BF16)
    glog_blk = (MLA_Q_RANK + MLA_KV_RANK) // LANES
    krp_blk = glog_blk + 1
    cos_a, sin_a = _rope_lane_tables(s, 0, NSA_ROT, NSA_DK)
    cos_b, sin_b = _rope_lane_tables(s, MLA_NOPE, MLA_ROPE, MLA_NOPE + MLA_ROPE)
    shift_a = LANES - NSA_DK
    shift_b = LANES - (NSA_DK - MLA_NOPE)
    w_uq = _pad_heads_rope(mla_w_uq[0], MLA_HEADS, MLA_NOPE + MLA_ROPE, MLA_NOPE, MLA_ROPE).astype(BF16)
    w_uk = _pad_lanes(mla_w_uk[0], MLA_HEADS, MLA_NOPE).astype(BF16)
    w_uv = _pad_lanes(mla_w_uv[0], MLA_HEADS, MLA_DV).astype(BF16)

    x2 = x.reshape(n, d)
    xn = rmsnorm_rows(x2, g_mix[0], tm)
    log2e = float(np.log2(np.e))
    tmb = T["tm_big"]
    qk = projection(xn, w_rope, tm=tmb, tn=512, out_dtype=BF16, rope=(cos_a, sin_a, shift_a),
                    out_scale=(NSA_DK ** -0.5 * log2e, NSA_HEADS * LANES // 512),
                    head_major=(b, s), name="proj_qk_rope")
    vsw = projection(xn, w_vsw, tm=tmb, tn=512, out_dtype=BF16, head_major=(b, s), name="proj_v")
    vcp = projection(xn, w_vcp, tm=tmb, tn=512, out_dtype=BF16, head_major=(b, s), name="proj_vc")
    misc = projection(xn, w_misc, tm=tm, tn=w_misc.shape[1], out_dtype=F32, name="proj_misc")

    nr = s // CMP_STRIDE
    pk = _compress_params(cmp_pos_k[0], cmp_w1_k[0], cmp_w2_k[0], NSA_DK)
    pv = _compress_params(cmp_pos_v[0], cmp_w1_v[0], cmp_w2_v[0], NSA_DV)
    kc_rows = qk[:, QK_KC0:QK_KC0 + G].reshape(b, G, nr, CMP_STRIDE * LANES)
    kcc = compress(kc_rows, 0, *pk)
    vcc = compress(vcp.reshape(b, G, nr, CMP_STRIDE * LANES), 0, *pv)
    expand = _gate_expand()
    o_c, selb = nsa_compressed(qk, kcc, vcc, _cmp_to_sel(nr, s // SEL_BLOCK), misc, glog_blk, expand,
                               batch=b, seq=s, tq=T["tq_cmp"], rc=rc)
    o_s = flash_heads(qk, qk, vsw, mode="select", batch=b, seq=s, tq=tq, tk=tk, rc=rc,
                      q_head0=QK_Q0, k_head0=QK_KS0, v_head0=0, kv_shared=True, out_dtype=BF16,
                      selb=selb, onehot=_key_block_onehot(s), gate=(misc, glog_blk, expand, 1),
                      name="nsa_selected")
    o_w = flash_heads(qk, qk, vsw, mode="window", batch=b, seq=s, tq=T["tq_win"], tk=tk, rc=rc,
                      q_head0=QK_Q0, k_head0=QK_KW0, v_head0=G, kv_shared=True, out_dtype=BF16,
                      gate=(misc, glog_blk, expand, 2), name="nsa_window")

    qm = projection(misc, w_uq, tm=tm, tn=512, out_dtype=BF16, a_cols=(MLA_Q_RANK, 0), gain=mla_g_q[0],
                    rope=(cos_b, sin_b, shift_b), head_major=(b, s), name="mla_q",
                    out_scale=((MLA_NOPE + MLA_ROPE) ** -0.5 * log2e, MLA_HEADS * LANES // 512))
    km = projection(misc, w_uk, tm=tm, tn=512, out_dtype=BF16,
                    a_cols=(MLA_KV_RANK, MLA_Q_RANK // MLA_KV_RANK), gain=mla_g_kv[0],
                    add=(misc, krp_blk), rope=(cos_b, sin_b, shift_b), rope_add=True,
                    head_major=(b, s), name="mla_k")
    vm = projection(misc, w_uv, tm=tm, tn=512, out_dtype=BF16,
                    a_cols=(MLA_KV_RANK, MLA_Q_RANK // MLA_KV_RANK), gain=mla_g_kv[0],
                    head_major=(b, s), name="mla_v")
    o_b = flash_heads(qm, km, vm, mode="causal", batch=b, seq=s, tq=tq, tk=tk, rc=rc,
                      q_head0=0, k_head0=0, v_head0=0,
                      kv_shared=False, out_dtype=BF16, name="mla_attention")

    mixed = gated_mix(o_c, o_s, o_w, o_b, xn, w_ga.astype(BF16), w_gb.astype(BF16),
                      w_o_nsa[0].astype(BF16), w_o_mla[0].astype(BF16), tm=tm, tn=512)

    kv_mem = projection(mem.reshape(b * MEM_LEN, d), xa_wkv[0].astype(BF16), tm=MEM_LEN, tn=512,
                        out_dtype=BF16, gain=g_mem[0], name="xattn_kv")
    h2 = out_proj_xattn(x2, mixed, w_out[0].astype(BF16), g_xattn[0], xa_wq[0].astype(BF16), kv_mem,
                        xa_wo[0].astype(BF16), seq=s, tm=T["tm_x"])

    out = mlp_final(h2, g_mlp[0], w_ff1[0].astype(BF16), w_ff2[0].astype(BF16), g_final,
                    tm=T["tm_mlp"], tf=T["tf"])
    return out.reshape(b, s, d)
```

```python
import functools

import numpy as np
import jax
import jax.numpy as jnp
from jax import lax
from jax.experimental import pallas as pl
from jax.experimental.pallas import tpu as pltpu

F32 = jnp.float32
BF16 = jnp.bfloat16

D_MODEL = 2048
MEM_LEN = 256
ROPE_THETA = 500000.0
EPS = 1e-6
NEG = -1e30

NSA_HEADS = 16
NSA_KV_HEADS = 4
NSA_HPG = NSA_HEADS // NSA_KV_HEADS
NSA_DK = 96
NSA_DV = 64
NSA_ROT = NSA_DK // 4
CMP_BLOCK = 32
CMP_STRIDE = 16
SEL_BLOCK = 64
SEL_TOPK = 16
N_FORCED = 3
WINDOW = 512

MLA_HEADS = 16
MLA_NOPE = 64
MLA_ROPE = 32
MLA_DV = 64
MLA_Q_RANK = 512
MLA_KV_RANK = 256

XA_HEADS = 4
XA_DIM = 128
D_FF = 4 * D_MODEL

SPLITS = (NSA_HEADS * NSA_DK,
          NSA_KV_HEADS * NSA_DK, NSA_KV_HEADS * NSA_DV,
          NSA_KV_HEADS * NSA_DK, NSA_KV_HEADS * NSA_DV,
          NSA_KV_HEADS * NSA_DK, NSA_KV_HEADS * NSA_DV,
          NSA_HEADS * 3,
          MLA_Q_RANK, MLA_KV_RANK, MLA_ROPE,
          D_MODEL, D_MODEL)

LANES = 128
HALF = LANES // 2
VMEM_LIMIT = 56 * 1024 * 1024

QK_Q0 = 0
QK_KC0 = NSA_HEADS
QK_KS0 = NSA_HEADS + NSA_KV_HEADS
QK_KW0 = NSA_HEADS + 2 * NSA_KV_HEADS
QK_HEADS = NSA_HEADS + 3 * NSA_KV_HEADS
FLASH_HEADS = NSA_HPG
TOPK_ROWS = 32


def _cparams(*sem):
    return pltpu.CompilerParams(dimension_semantics=sem, vmem_limit_bytes=VMEM_LIMIT)


def _tiles(seq):
    return dict(
        tm=min(1024, seq),
        tm_big=min(2048, seq),
        tn=4 * LANES,
        tm_x=min(512, seq),
        tm_mlp=min(512, seq),
        tq=min(2048, seq),
        tq_win=min(512, seq),
        tk=min(512, seq),
        rc=32,
        tq_cmp=min(256, seq),
        tf=1024,
    )


def _rms(x, g):
    return x * lax.rsqrt(jnp.mean(x * x, axis=-1, keepdims=True) + EPS) * g


def _rmsnorm_kernel(x_ref, g_ref, o_ref):
    o_ref[...] = _rms(x_ref[...], g_ref[...]).astype(o_ref.dtype)


def rmsnorm_rows(x, g, tm):
    n, d = x.shape
    return pl.pallas_call(
        _rmsnorm_kernel,
        out_shape=jax.ShapeDtypeStruct((n, d), BF16),
        grid=(n // tm,),
        in_specs=[pl.BlockSpec((tm, d), lambda i: (i, 0)),
                  pl.BlockSpec((1, d), lambda i: (0, 0))],
        out_specs=pl.BlockSpec((tm, d), lambda i: (i, 0)),
        compiler_params=_cparams("parallel"),
        name="rmsnorm_rows",
    )(x, g.reshape(1, d))


def _proj_kernel(*refs, has_gain, has_rope, roll_shift, rope_add, out_scale, has_add, head_major):
    it = iter(refs)
    a_ref = next(it)
    g_ref = next(it) if has_gain else None
    w_ref = next(it)
    cos_ref = next(it) if has_rope else None
    sin_ref = next(it) if has_rope else None
    add_ref = next(it) if has_add else None
    o_ref = next(it)

    if has_gain:
        an_sc = next(it)

        @pl.when(pl.program_id(1) == 0)
        def _norm():
            an_sc[...] = _rms(a_ref[...], g_ref[...]).astype(BF16)

        a = an_sc[...]
    else:
        a = a_ref[...]
    y = jnp.dot(a, w_ref[...], preferred_element_type=F32)
    if out_scale is not None:
        factor, n_tiles = out_scale
        y = y * jnp.where(pl.program_id(1) < n_tiles, factor, 1.0)
    rope = lambda t: t * cos_ref[...] + pltpu.roll(t, roll_shift, 1) * sin_ref[...]
    if has_add:
        chunk = add_ref[...].astype(F32)
        if rope_add:
            chunk = rope(chunk)
    if has_rope or has_add or head_major:
        for h in range(y.shape[1] // LANES):
            yh = y[:, h * LANES:(h + 1) * LANES]
            if has_rope and not rope_add:
                yh = rope(yh)
            if has_add:
                yh = yh + chunk
            if head_major:
                o_ref[h] = yh.astype(o_ref.dtype)
            else:
                o_ref[:, h * LANES:(h + 1) * LANES] = yh.astype(o_ref.dtype)
    else:
        o_ref[...] = y.astype(o_ref.dtype)


def projection(a, w, *, tm, tn, out_dtype, a_cols=None, gain=None, rope=None, rope_add=False,
               out_scale=None, add=None, head_major=None, name="projection"):
    n = a.shape[0]
    k, nc = w.shape
    a_w, a_blk = a_cols if a_cols is not None else (a.shape[1], 0)
    assert a_w == k and n % tm == 0 and nc % tn == 0
    grid = (n // tm, nc // tn)
    in_specs = [pl.BlockSpec((tm, k), lambda i, j: (i, a_blk))]
    args = [a]
    if gain is not None:
        in_specs.append(pl.BlockSpec((1, k), lambda i, j: (0, 0)))
        args.append(gain.reshape(1, k))
    in_specs.append(pl.BlockSpec((k, tn), lambda i, j: (0, j)))
    args.append(w)
    roll_shift = 0
    if rope is not None:
        cos, sin, roll_shift = rope
        s_tiles = cos.shape[0] // tm
        for t in (cos, sin):
            in_specs.append(pl.BlockSpec((tm, LANES), lambda i, j: (i % s_tiles, 0)))
            args.append(t)
    if add is not None:
        add_arr, add_blk = add
        in_specs.append(pl.BlockSpec((tm, LANES), lambda i, j: (i, add_blk)))
        args.append(add_arr)
    if head_major is not None:
        b, s = head_major
        s_t = s // tm
        hpt = tn // LANES
        out_shape = jax.ShapeDtypeStruct((b, nc // LANES, s, LANES), out_dtype)
        out_spec = pl.BlockSpec((None, hpt, tm, LANES), lambda i, j: (i // s_t, j, i % s_t, 0))
    else:
        out_shape = jax.ShapeDtypeStruct((n, nc), out_dtype)
        out_spec = pl.BlockSpec((tm, tn), lambda i, j: (i, j))
    kern = functools.partial(
        _proj_kernel, has_gain=gain is not None, has_rope=rope is not None, roll_shift=roll_shift,
        rope_add=rope_add, out_scale=out_scale, has_add=add is not None,
        head_major=head_major is not None)
    return pl.pallas_call(
        kern, out_shape=out_shape, grid=grid, in_specs=in_specs, out_specs=out_spec,
        scratch_shapes=[pltpu.VMEM((tm, k), BF16)] if gain is not None else [],
        compiler_params=_cparams("parallel", "arbitrary"), name=name,
    )(*args)


def _compress_kernel(r_ref, pa_ref, pb_ref, w1a_ref, w1b_ref, w2_ref, o_ref):
    r = r_ref[...].astype(F32)
    a = jnp.dot((r + pa_ref[...]).astype(BF16), w1a_ref[...], preferred_element_type=F32)
    b = jnp.dot((r + pb_ref[...]).astype(BF16), w1b_ref[...], preferred_element_type=F32)
    nr = a.shape[0]
    hid = a + pltpu.roll(b, nr - 1, 0)
    hid = jax.nn.gelu(hid)
    o_ref[...] = jnp.dot(hid.astype(BF16), w2_ref[...], preferred_element_type=F32).astype(o_ref.dtype)


def compress(r, head0, pos_a, pos_b, w1a, w1b, w2):
    b, _, nr, kk = r.shape
    g = NSA_KV_HEADS
    full = lambda shape: pl.BlockSpec(shape, lambda bi, gi: (0,) * len(shape))
    return pl.pallas_call(
        _compress_kernel,
        out_shape=jax.ShapeDtypeStruct((b, g, nr, LANES), BF16),
        grid=(b, g),
        in_specs=[pl.BlockSpec((None, None, nr, kk), lambda bi, gi: (bi, head0 + gi, 0, 0)),
                  full((1, kk)), full((1, kk)), full((kk, LANES)), full((kk, LANES)),
                  full((LANES, LANES))],
        out_specs=pl.BlockSpec((None, None, nr, LANES), lambda bi, gi: (bi, gi, 0, 0)),
        compiler_params=_cparams("parallel", "parallel"),
        name="nsa_compress",
    )(r, pos_a, pos_b, w1a, w1b, w2)


def _dot_nt(a, b):
    return lax.dot_general(a, b, (((1,), (1,)), ((), ())), preferred_element_type=F32)


def _split_dot(x, w):
    hi = x.astype(BF16)
    lo = (x - hi.astype(F32)).astype(BF16)
    return (jnp.dot(hi, w, preferred_element_type=F32) + jnp.dot(lo, w, preferred_element_type=F32))


def _nsa_cmp_kernel(q_ref, kcc_ref, vcc_ref, msel_ref, glog_ref, e_ref, oc_ref, sel_ref, s_sc, p_sc, imp_sc,
                    *, tq, rc, topk):
    s0 = pl.program_id(2) * tq
    nr = kcc_ref.shape[0]

    def attend(ncol):
        kcc = kcc_ref[0:ncol, :]
        for h in range(NSA_HPG):
            s_sc[h, :, 0:ncol] = _dot_nt(q_ref[h], kcc)
        lane_v = lax.broadcasted_iota(jnp.int32, (ncol, LANES), 1)
        vcc = vcc_ref[0:ncol, :]
        vm = jnp.concatenate([jnp.where(lane_v == HALF, jnp.ones_like(vcc), vcc), msel_ref[0:ncol, :]],
                             axis=1)
        dmat = (lax.broadcasted_iota(jnp.int32, (rc, ncol), 0)
                - lax.broadcasted_iota(jnp.int32, (rc, ncol), 1) * CMP_STRIDE)
        visible = lambda c: dmat + (s0 + c * rc - (CMP_BLOCK - 1)) >= 0
        outs = []
        imp = jnp.zeros((tq, LANES), F32)
        for h in range(NSA_HPG):
            row_max = []
            for c in range(tq // rc):
                s = jnp.where(visible(c), s_sc[h, pl.ds(c * rc, rc), 0:ncol], NEG)
                row_max.append(jnp.max(s, axis=-1, keepdims=True))
            for c in range(tq // rc):
                rows = pl.ds(c * rc, rc)
                p = jnp.exp2(s_sc[h, rows, 0:ncol] - row_max[c])
                p_sc[h, rows, 0:ncol] = jnp.where(visible(c), p, 0.0).astype(BF16)
            a = jnp.dot(p_sc[h, :, 0:ncol], vm, preferred_element_type=F32)
            denom = jnp.maximum(a[:, HALF:HALF + 1], 1e-30)
            outs.append(a[:, :LANES] / denom)
            imp = imp + a[:, LANES:] / denom
        imp_sc[...] = imp
        glog = glog_ref[...]
        lane = lax.broadcasted_iota(jnp.int32, (tq, LANES), 1)
        for pr in range(NSA_HPG // 2):
            gate = jax.nn.sigmoid(_split_dot(glog, e_ref[pr]))
            pair = jnp.where(lane < HALF, outs[2 * pr], pltpu.roll(outs[2 * pr + 1], HALF, 1))
            oc_ref[:, pr * LANES:(pr + 1) * LANES] = (gate * pair).astype(oc_ref.dtype)

    ntiles = (s0 + (tq - CMP_BLOCK)) // (CMP_STRIDE * LANES) + 1
    for k in range(1, nr // LANES + 1):
        @pl.when(ntiles == k)
        def _width(k=k):
            attend(k * LANES)

    blk = lax.broadcasted_iota(jnp.int32, (tq, LANES), 1)
    cur = (s0 + lax.broadcasted_iota(jnp.int32, (tq, LANES), 0)) // SEL_BLOCK
    valid = blk <= cur
    forced = (blk == 0) | (blk == cur) | (blk == cur - 1)
    score = jnp.where(valid, jnp.where(forced, -jnp.inf, imp_sc[...]), -1.0)
    score_t = score.T
    nblk = (s0 + (tq - 1)) // SEL_BLOCK + 1
    for nrows in range(TOPK_ROWS, LANES + 1, TOPK_ROWS):
        @pl.when((nblk + (TOPK_ROWS - 1)) // TOPK_ROWS == nrows // TOPK_ROWS)
        def _select(nrows=nrows):
            rowid = lax.broadcasted_iota(jnp.int32, (nrows, tq), 0).astype(F32)

            def pick_one(_, sc):
                cm = jnp.max(sc, axis=0, keepdims=True)
                first = jnp.min(jnp.where(sc == cm, rowid, float(LANES)), axis=0, keepdims=True)
                return jnp.where(rowid == first, -jnp.inf, sc)

            picked = lax.fori_loop(0, topk - N_FORCED, pick_one, score_t[0:nrows]) == -jnp.inf
            bias = jnp.where(picked, 0.0, NEG)
            if nrows < LANES:
                bias = jnp.concatenate([bias, jnp.full((LANES - nrows, tq), NEG, F32)], axis=0)
            sel_ref[...] = bias.T.astype(sel_ref.dtype)


def nsa_compressed(qk, kcc, vcc, msel, misc, glog_blk, expand, *, batch, seq, tq, rc):
    g = NSA_KV_HEADS
    nq = seq // tq
    nr = kcc.shape[2]
    topk = min(SEL_TOPK, seq // SEL_BLOCK)
    assert topk > N_FORCED and nr % LANES == 0
    kern = functools.partial(_nsa_cmp_kernel, tq=tq, rc=rc, topk=topk)
    return pl.pallas_call(
        kern,
        out_shape=(jax.ShapeDtypeStruct((batch * seq, NSA_HEADS * NSA_DV), BF16),
                   jax.ShapeDtypeStruct((batch, g, seq, LANES), BF16)),
        grid=(batch, g, nq),
        in_specs=[
            pl.BlockSpec((None, NSA_HPG, tq, LANES), lambda b, gi, qi: (b, gi, qi, 0)),
            pl.BlockSpec((None, None, nr, LANES), lambda b, gi, qi: (b, gi, 0, 0)),
            pl.BlockSpec((None, None, nr, LANES), lambda b, gi, qi: (b, gi, 0, 0)),
            pl.BlockSpec((nr, LANES), lambda b, gi, qi: (0, 0)),
            pl.BlockSpec((tq, LANES), lambda b, gi, qi: (b * nq + qi, glog_blk)),
            pl.BlockSpec((None, 2, LANES, LANES), lambda b, gi, qi: (0, gi, 0, 0)),
        ],
        out_specs=(pl.BlockSpec((tq, 2 * LANES), lambda b, gi, qi: (b * nq + qi, gi)),
                   pl.BlockSpec((None, None, tq, LANES), lambda b, gi, qi: (b, gi, qi, 0))),
        scratch_shapes=[pltpu.VMEM((NSA_HPG, tq, nr), F32), pltpu.VMEM((NSA_HPG, tq, nr), BF16),
                        pltpu.VMEM((tq, LANES), F32)],
        compiler_params=_cparams("parallel", "parallel", "parallel"),
        name="nsa_compressed_select",
    )(qk, kcc, vcc, msel, misc, expand)


def _flash_steps(mode, nq, tq, tk):
    r = tq // tk
    qi_l, kt_l, first_l, last_l = [], [], [], []
    for qi in range(nq):
        hi = qi * r + r - 1
        lo = max(0, qi * r - (-(-(WINDOW - 1) // tk))) if mode == "window" else 0
        for kt in range(lo, hi + 1):
            qi_l.append(qi)
            kt_l.append(kt)
            first_l.append(int(kt == lo))
            last_l.append(int(kt == hi))
    return tuple(jnp.asarray(np.asarray(a, np.int32)) for a in (qi_l, kt_l, first_l, last_l))


def _flash_kernel(*refs, mode, tq, tk, rc, kv_shared, gated):
    it = iter(refs)
    qi_ref, kt_ref, first_ref, last_ref = next(it), next(it), next(it), next(it)
    q_ref, k_ref, v_ref = next(it), next(it), next(it)
    selb_ref = next(it) if mode == "select" else None
    oh_ref = next(it) if mode == "select" else None
    glog_ref = next(it) if gated else None
    e_ref = next(it) if gated else None
    o_ref = next(it)
    m_sc, acc_sc, s_sc, p_sc = next(it), next(it), next(it), next(it)
    qa_sc = next(it) if mode == "select" else None

    st = pl.program_id(2)
    s0 = qi_ref[st] * tq
    k0 = kt_ref[st] * tk

    @pl.when(first_ref[st] == 1)
    def _init():
        m_sc[...] = jnp.full(m_sc.shape, NEG, F32)
        acc_sc[...] = jnp.zeros(acc_sc.shape, F32)
        if mode == "select":
            for hh in range(FLASH_HEADS):
                qa_sc[hh] = jnp.concatenate([q_ref[hh], selb_ref[...]], axis=1)

    def chunk_cols(off, r0):
        if off is None:
            return 0, tk, True
        r1 = r0 + rc
        hi = min(tk, r1 - off)
        lo = max(0, r0 - off - WINDOW + 1) if mode == "window" else 0
        lo, hi = lo // LANES * LANES, -(-hi // LANES) * LANES
        clear = tk - 1 <= r0 - off and (mode != "window" or r1 - 1 - off < WINDOW)
        return lo, hi, clear

    def tile(off):
        nh = FLASH_HEADS
        r_lo = max(0, off) if off is not None else 0
        nrow = tq - r_lo
        if kv_shared:
            if mode == "select":
                qs = qa_sc[:, r_lo:, :].reshape(nh * nrow, 2 * LANES)
                ks = jnp.concatenate([k_ref[0], oh_ref[...]], axis=1)
            else:
                qs, ks = q_ref[:, r_lo:, :].reshape(nh * nrow, LANES), k_ref[0]
            s_sc[:, r_lo:, :] = _dot_nt(qs, ks).reshape(nh, nrow, tk)
        else:
            for hh in range(nh):
                s_sc[hh, r_lo:, :] = _dot_nt(q_ref[hh, r_lo:, :], k_ref[hh])
        lane_v = lax.broadcasted_iota(jnp.int32, (tk, LANES), 1)
        if off is not None:
            dmat = (lax.broadcasted_iota(jnp.int32, (rc, tk), 0)
                    - lax.broadcasted_iota(jnp.int32, (rc, tk), 1))
        for hh in range(FLASH_HEADS):
            v = v_ref[0 if kv_shared else hh]
            v_aug = jnp.where(lane_v == HALF, jnp.ones_like(v), v)

            for c in range(r_lo // rc, tq // rc):
                r0 = c * rc
                rows = pl.ds(r0, rc)
                lo, hi, clear = chunk_cols(off, r0)
                if lo > 0:
                    p_sc[hh, rows, 0:lo] = jnp.zeros((rc, lo), BF16)
                if hi < tk:
                    p_sc[hh, rows, max(hi, 0):tk] = jnp.zeros((rc, tk - max(hi, 0)), BF16)
                if hi <= lo:
                    continue
                s = s_sc[hh, rows, lo:hi]
                if not clear:
                    d = dmat[:, lo:hi] + (r0 - off)
                    msk = (d >= 0) & (d < WINDOW) if mode == "window" else d >= 0
                    s = jnp.where(msk, s, NEG)
                m_old = m_sc[hh, rows, :]
                m_new = jnp.maximum(m_old, jnp.max(s, axis=-1, keepdims=True))
                m_sc[hh, rows, :] = m_new
                acc_sc[hh, rows, :] = acc_sc[hh, rows, :] * jnp.exp2(m_old - m_new)
                p = jnp.exp2(s_sc[hh, rows, lo:hi] - jnp.concatenate([m_new] * ((hi - lo) // LANES), axis=1))
                if not clear:
                    p = jnp.where(msk, p, 0.0)
                p_sc[hh, rows, lo:hi] = p.astype(BF16)
            if not kv_shared:
                acc_sc[hh, r_lo:, :] += jnp.dot(p_sc[hh, r_lo:, :], v_aug, preferred_element_type=F32)
        if kv_shared:
            pv = jnp.dot(p_sc[:, r_lo:, :].reshape(nh * nrow, tk), v_aug, preferred_element_type=F32)
            acc_sc[:, r_lo:, :] += pv.reshape(nh, nrow, LANES)

    if mode == "window":
        offsets = [(j - (-(-(WINDOW - 1) // tk))) * tk for j in range(-(-(WINDOW - 1) // tk) + tq // tk)]
    else:
        offsets = [j * tk for j in range(tq // tk)]

        @pl.when(k0 + tk - 1 <= s0)
        def _interior():
            tile(None)

    for off in offsets:
        @pl.when(k0 - s0 == off)
        def _partial(off=off):
            tile(off)

    @pl.when(last_ref[st] == 1)
    def _finish():
        lane = lax.broadcasted_iota(jnp.int32, (tq, LANES), 1)
        if gated:
            glog = glog_ref[...]
        for pr in range(FLASH_HEADS // 2):
            outs = []
            for hh in (2 * pr, 2 * pr + 1):
                a = acc_sc[hh]
                outs.append(a / jnp.maximum(a[:, HALF:HALF + 1], 1e-30))
            out = jnp.where(lane < HALF, outs[0], pltpu.roll(outs[1], HALF, 1))
            if gated:
                out = jax.nn.sigmoid(_split_dot(glog, e_ref[pr])) * out
            o_ref[:, pr * LANES:(pr + 1) * LANES] = out.astype(o_ref.dtype)


def flash_heads(q, k, v, *, mode, batch, seq, tq, tk, rc, q_head0, k_head0, v_head0,
                kv_shared, out_dtype, selb=None, onehot=None, gate=None, name="flash"):
    nh = FLASH_HEADS
    ngroups = NSA_HEADS // nh
    nq = seq // tq
    assert tq % tk == 0 and tq % rc == 0
    tabs = _flash_steps(mode, nq, tq, tk)
    nsteps = tabs[0].shape[0]
    if kv_shared:
        kv_spec = lambda h0: pl.BlockSpec(
            (None, 1, tk, LANES), lambda b, g, st, qi, kt, fi, la: (b, h0 + g, kt[st], 0))
    else:
        kv_spec = lambda h0: pl.BlockSpec(
            (None, nh, tk, LANES), lambda b, g, st, qi, kt, fi, la: (b, h0 // nh + g, kt[st], 0))
    in_specs = [pl.BlockSpec((None, nh, tq, LANES),
                             lambda b, g, st, qi, kt, fi, la: (b, q_head0 // nh + g, qi[st], 0)),
                kv_spec(k_head0), kv_spec(v_head0)]
    args = [q, k, v]
    scratch = [pltpu.VMEM((nh, tq, LANES), F32), pltpu.VMEM((nh, tq, LANES), F32),
               pltpu.VMEM((nh, tq, tk), F32), pltpu.VMEM((nh, tq, tk), BF16)]
    if mode == "select":
        in_specs += [pl.BlockSpec((None, None, tq, LANES),
                                  lambda b, g, st, qi, kt, fi, la: (b, g, qi[st], 0)),
                     pl.BlockSpec((tk, LANES), lambda b, g, st, qi, kt, fi, la: (kt[st], 0))]
        args += [selb, onehot]
        scratch.append(pltpu.VMEM((nh, tq, 2 * LANES), BF16))
    if gate is not None:
        misc, glog_blk, expand, branch = gate
        in_specs += [pl.BlockSpec((tq, LANES),
                                  lambda b, g, st, qi, kt, fi, la: (b * nq + qi[st], glog_blk)),
                     pl.BlockSpec((None, nh // 2, LANES, LANES),
                                  lambda b, g, st, qi, kt, fi, la: (branch, g, 0, 0))]
        args += [misc, expand]
    kern = functools.partial(_flash_kernel, mode=mode, tq=tq, tk=tk, rc=rc,
                             kv_shared=kv_shared, gated=gate is not None)
    ow = nh * HALF
    grid_spec = pltpu.PrefetchScalarGridSpec(
        num_scalar_prefetch=4, grid=(batch, ngroups, nsteps), in_specs=in_specs,
        out_specs=pl.BlockSpec((tq, ow), lambda b, g, st, qi, kt, fi, la: (b * nq + qi[st], g)),
        scratch_shapes=scratch)
    return pl.pallas_call(
        kern,
        out_shape=jax.ShapeDtypeStruct((batch * seq, ngroups * ow), out_dtype),
        grid_spec=grid_spec,
        compiler_params=_cparams("parallel", "parallel", "arbitrary"),
        name=name,
    )(*tabs, *args)


def _mix_kernel(oc_ref, os_ref, ow_ref, ob_ref, xn_ref, wga_ref, wgb_ref, wa_ref, wb_ref, o_ref):
    oa = (oc_ref[...].astype(F32) + os_ref[...].astype(F32) + ow_ref[...].astype(F32)).astype(BF16)
    xn = xn_ref[...]
    ga = jax.nn.sigmoid(jnp.dot(xn, wga_ref[...], preferred_element_type=F32))
    ya = ga * jnp.dot(oa, wa_ref[...], preferred_element_type=F32)
    gb = jax.nn.sigmoid(jnp.dot(xn, wgb_ref[...], preferred_element_type=F32))
    yb = gb * jnp.dot(ob_ref[...], wb_ref[...], preferred_element_type=F32)
    o_ref[...] = (ya + yb).astype(o_ref.dtype)


def gated_mix(oc, osel, ow, ob, xn, wga, wgb, wa, wb, *, tm, tn):
    n, ka = oc.shape
    d = wa.shape[1]
    row = lambda w: pl.BlockSpec((tm, w), lambda i, j: (i, 0))
    col = lambda k: pl.BlockSpec((k, tn), lambda i, j: (0, j))
    return pl.pallas_call(
        _mix_kernel,
        out_shape=jax.ShapeDtypeStruct((n, d), BF16),
        grid=(n // tm, d // tn),
        in_specs=[row(ka), row(ka), row(ka), row(ob.shape[1]), row(xn.shape[1]),
                  col(wga.shape[0]), col(wgb.shape[0]), col(ka), col(wb.shape[0])],
        out_specs=pl.BlockSpec((tm, tn), lambda i, j: (i, j)),
        compiler_params=_cparams("parallel", "arbitrary"),
        name="gated_mix",
    )(oc, osel, ow, ob, xn, wga, wgb, wa, wb)


def _xattn_kernel(x_ref, mix_ref, wout_ref, g_ref, wq_ref, kv_ref, wo_ref, o_ref):
    h = x_ref[...] + jnp.dot(mix_ref[...], wout_ref[...], preferred_element_type=F32)
    hn = _rms(h, g_ref[...]).astype(BF16)
    q = jnp.dot(hn, wq_ref[...], preferred_element_type=F32).astype(BF16)
    kv = kv_ref[...]
    outs = []
    for hd in range(XA_HEADS):
        qh = q[:, hd * XA_DIM:(hd + 1) * XA_DIM]
        kh = kv[:, hd * XA_DIM:(hd + 1) * XA_DIM]
        vh = kv[:, (XA_HEADS + hd) * XA_DIM:(XA_HEADS + hd + 1) * XA_DIM]
        s = _dot_nt(qh, kh) * (XA_DIM ** -0.5)
        e = jnp.exp(s - jnp.max(s, axis=-1, keepdims=True))
        p = e / jnp.sum(e, axis=-1, keepdims=True)
        outs.append(jnp.dot(p.astype(BF16), vh, preferred_element_type=F32))
    o = jnp.concatenate(outs, axis=1).astype(BF16)
    o_ref[...] = h + jnp.dot(o, wo_ref[...], preferred_element_type=F32)


def out_proj_xattn(x, mixed, wout, g, wq, kv, wo, *, seq, tm):
    n, d = x.shape
    s_t = seq // tm
    dq = wq.shape[1]
    resident = lambda shape: pl.BlockSpec(shape, lambda i: (0, 0), pipeline_mode=pl.Buffered(1))
    return pl.pallas_call(
        _xattn_kernel,
        out_shape=jax.ShapeDtypeStruct((n, d), F32),
        grid=(n // tm,),
        in_specs=[pl.BlockSpec((tm, d), lambda i: (i, 0)),
                  pl.BlockSpec((tm, d), lambda i: (i, 0)),
                  resident((d, d)),
                  pl.BlockSpec((1, d), lambda i: (0, 0)),
                  resident((d, dq)),
                  pl.BlockSpec((MEM_LEN, 2 * dq), lambda i: (i // s_t, 0)),
                  resident((dq, d))],
        out_specs=pl.BlockSpec((tm, d), lambda i: (i, 0)),
        compiler_params=_cparams("parallel"),
        name="out_proj_xattn",
    )(x, mixed, wout, g.reshape(1, d), wq, kv, wo)


def _mlp_kernel(h_ref, g_ref, w1_ref, w2_ref, gf_ref, o_ref, hn_sc, acc_sc, *, nf):
    f = pl.program_id(1)

    @pl.when(f == 0)
    def _init():
        hn_sc[...] = _rms(h_ref[...], g_ref[...]).astype(BF16)
        acc_sc[...] = jnp.zeros(acc_sc.shape, F32)

    u = jnp.dot(hn_sc[...], w1_ref[...], preferred_element_type=F32)
    u = jnp.square(jnp.maximum(u, 0.0))
    acc_sc[...] += jnp.dot(u.astype(BF16), w2_ref[...], preferred_element_type=F32)

    @pl.when(f == nf - 1)
    def _finish():
        o_ref[...] = _rms(h_ref[...] + acc_sc[...], gf_ref[...])


def mlp_final(h, g, w1, w2, gf, *, tm, tf):
    n, d = h.shape
    dff = w1.shape[1]
    nf = dff // tf
    return pl.pallas_call(
        functools.partial(_mlp_kernel, nf=nf),
        out_shape=jax.ShapeDtypeStruct((n, d), F32),
        grid=(n // tm, nf),
        in_specs=[pl.BlockSpec((tm, d), lambda i, f: (i, 0)),
                  pl.BlockSpec((1, d), lambda i, f: (0, 0)),
                  pl.BlockSpec((d, tf), lambda i, f: (0, f)),
                  pl.BlockSpec((tf, d), lambda i, f: (f, 0)),
                  pl.BlockSpec((1, d), lambda i, f: (0, 0))],
        out_specs=pl.BlockSpec((tm, d), lambda i, f: (i, 0)),
        scratch_shapes=[pltpu.VMEM((tm, d), BF16), pltpu.VMEM((tm, d), F32)],
        compiler_params=_cparams("parallel", "arbitrary"),
        name="mlp_final",
    )(h, g.reshape(1, d), w1, w2, gf.reshape(1, d))


def _rot_partner(w, half):
    return jnp.concatenate([-w[..., half:], w[..., :half]], axis=-1)


def _pad_heads_rope(w, heads, hd, rot0, rot_dim):
    k = w.shape[0]
    w = w.reshape(k, heads, hd)
    partner = _rot_partner(w[:, :, rot0:rot0 + rot_dim], rot_dim // 2)
    pad = jnp.zeros((k, heads, LANES - hd - rot_dim), w.dtype)
    return jnp.concatenate([w, partner, pad], axis=-1).reshape(k, heads * LANES)


def _rope_lane_tables(seq, rot0, rot_dim, hd):
    inv = 1.0 / (ROPE_THETA ** (jnp.arange(0, rot_dim, 2, dtype=F32) / rot_dim))
    ang = jnp.arange(seq, dtype=F32)[:, None] * inv[None, :]
    cos = jnp.concatenate([jnp.cos(ang), jnp.cos(ang)], axis=1)
    sin = jnp.concatenate([jnp.sin(ang), jnp.sin(ang)], axis=1)
    cos_t = jnp.concatenate([jnp.ones((seq, rot0), F32), cos,
                             jnp.ones((seq, hd - rot0 - rot_dim), F32),
                             jnp.zeros((seq, LANES - hd), F32)], axis=1)
    sin_t = jnp.concatenate([jnp.zeros((seq, rot0), F32), sin,
                             jnp.zeros((seq, LANES - rot0 - rot_dim), F32)], axis=1)
    return cos_t, sin_t


def _pad_lanes(w, groups, width):
    k = w.shape[0]
    w = w.reshape(k, groups, width)
    return jnp.pad(w, ((0, 0), (0, 0), (0, LANES - width))).reshape(k, groups * LANES)


def _compress_params(pos, w1, w2, d):
    half = CMP_BLOCK // 2
    pos_p = jnp.pad(pos, ((0, 0), (0, LANES - d)))
    pos_a = pos_p[:half].reshape(1, half * LANES)
    pos_b = pos_p[half:].reshape(1, half * LANES)
    w1p = jnp.pad(w1.reshape(CMP_BLOCK, d, d), ((0, 0), (0, LANES - d), (0, LANES - d)))
    w1a = w1p[:half].reshape(half * LANES, LANES).astype(BF16)
    w1b = w1p[half:].reshape(half * LANES, LANES).astype(BF16)
    w2p = jnp.pad(w2, ((0, LANES - d), (0, LANES - d)))
    return pos_a, pos_b, w1a, w1b, w2p.astype(BF16)


def _cmp_to_sel(nr, nsb):
    cs = np.arange(nr) * CMP_STRIDE
    ce = cs + CMP_BLOCK
    ss = np.arange(LANES) * SEL_BLOCK
    se = ss + SEL_BLOCK
    ov = np.clip(np.minimum(ce[:, None], se[None, :]) - np.maximum(cs[:, None], ss[None, :]), 0, None)
    ov = ov.astype(np.float32) / np.float32(CMP_BLOCK)
    ov[:, nsb:] = 0.0
    ov[nr - 1:, :] = 0.0
    return jnp.asarray(ov, BF16)


def _gate_expand():
    e = np.zeros((3, NSA_HEADS // 2, LANES, LANES), np.float32)
    for br in range(3):
        for hp in range(NSA_HEADS // 2):
            for hh in range(2):
                e[br, hp, 3 * (2 * hp + hh) + br, hh * HALF:(hh + 1) * HALF] = 1.0
    return jnp.asarray(e, BF16)


def _key_block_onehot(seq):
    e = (np.arange(seq)[:, None] // SEL_BLOCK) == np.arange(LANES)[None, :]
    return jnp.asarray(e.astype(np.float32), BF16)


def kernel(x, mem, g_mix, w_in, cmp_pos_k, cmp_w1_k, cmp_w2_k, cmp_pos_v, cmp_w1_v, cmp_w2_v,
           mla_g_q, mla_w_uq, mla_g_kv, mla_w_uk, mla_w_uv, w_o_nsa, w_o_mla, w_out,
           g_xattn, g_mem, xa_wq, xa_wkv, xa_wo, g_mlp, w_ff1, w_ff2, g_final):
    b, s, d = x.shape
    assert d == D_MODEL and s % (CMP_STRIDE * 8) == 0 and s // SEL_BLOCK <= LANES
    assert g_mix.shape[0] == 1
    n = b * s
    T = _tiles(s)
    tm, tn, tq, tk, rc = T["tm"], T["tn"], T["tq"], T["tk"], T["rc"]
    G = NSA_KV_HEADS
    bounds = [int(v) for v in np.cumsum(SPLITS)[:-1]]

    (w_qa, w_kc, w_vc, w_ks, w_vs, w_kw, w_vw, w_gn, w_cq, w_ckv, w_kr,
     w_ga, w_gb) = jnp.split(w_in[0], bounds, axis=1)
    nsa_rope = lambda w, heads: _pad_heads_rope(w, heads, NSA_DK, 0, NSA_ROT)
    w_rope = jnp.concatenate([nsa_rope(w_qa, NSA_HEADS), nsa_rope(w_kc, G), nsa_rope(w_ks, G),
                              nsa_rope(w_kw, G)], axis=1).astype(BF16)
    w_krp = jnp.concatenate([jnp.zeros((d, MLA_NOPE), F32), w_kr, _rot_partner(w_kr, MLA_ROPE // 2)], axis=1)
    w_vsw = jnp.concatenate([_pad_lanes(w_vs, G, NSA_DV), _pad_lanes(w_vw, G, NSA_DV),
                             _pad_lanes(w_vc, G, NSA_DV)], axis=1).astype(BF16)
    w_misc = jnp.concatenate([w_cq, w_ckv, jnp.pad(w_gn, ((0, 0), (0, LANES - w_gn.shape[1]))), w_krp],
                             axis=1).astype(BF16)
    glog_blk = (MLA_Q_RANK + MLA_KV_RANK) // LANES
    krp_blk = glog_blk + 1
    cos_a, sin_a = _rope_lane_tables(s, 0, NSA_ROT, NSA_DK)
    cos_b, sin_b = _rope_lane_tables(s, MLA_NOPE, MLA_ROPE, MLA_NOPE + MLA_ROPE)
    shift_a = LANES - NSA_DK
    shift_b = LANES - (NSA_DK - MLA_NOPE)
    w_uq = _pad_heads_rope(mla_w_uq[0], MLA_HEADS, MLA_NOPE + MLA_ROPE, MLA_NOPE, MLA_ROPE).astype(BF16)
    w_uk = _pad_lanes(mla_w_uk[0], MLA_HEADS, MLA_NOPE).astype(BF16)
    w_uv = _pad_lanes(mla_w_uv[0], MLA_HEADS, MLA_DV).astype(BF16)

    x2 = x.reshape(n, d)
    xn = rmsnorm_rows(x2, g_mix[0], tm)
    log2e = float(np.log2(np.e))
    tmb = T["tm_big"]
    qk = projection(xn, w_rope, tm=tmb, tn=tn, out_dtype=BF16, rope=(cos_a, sin_a, shift_a),
                    out_scale=(NSA_DK ** -0.5 * log2e, NSA_HEADS * LANES // tn),
                    head_major=(b, s), name="proj_qk_rope")
    vsw = projection(xn, w_vsw, tm=tmb, tn=tn, out_dtype=BF16, head_major=(b, s), name="proj_v")
    vcp = vsw[:, 2 * G:3 * G]
    misc = projection(xn, w_misc, tm=tm, tn=w_misc.shape[1], out_dtype=F32, name="proj_misc")

    nr = s // CMP_STRIDE
    pk = _compress_params(cmp_pos_k[0], cmp_w1_k[0], cmp_w2_k[0], NSA_DK)
    pv = _compress_params(cmp_pos_v[0], cmp_w1_v[0], cmp_w2_v[0], NSA_DV)
    kc_rows = qk[:, QK_KC0:QK_KC0 + G].reshape(b, G, nr, CMP_STRIDE * LANES)
    kcc = compress(kc_rows, 0, *pk)
    vcc = compress(vcp.reshape(b, G, nr, CMP_STRIDE * LANES), 0, *pv)
    expand = _gate_expand()
    o_c, selb = nsa_compressed(qk, kcc, vcc, _cmp_to_sel(nr, s // SEL_BLOCK), misc, glog_blk, expand,
                               batch=b, seq=s, tq=T["tq_cmp"], rc=rc)
    o_s = flash_heads(qk, qk, vsw, mode="select", batch=b, seq=s, tq=tq, tk=tk, rc=rc,
                      q_head0=QK_Q0, k_head0=QK_KS0, v_head0=0, kv_shared=True, out_dtype=BF16,
                      selb=selb, onehot=_key_block_onehot(s), gate=(misc, glog_blk, expand, 1),
                      name="nsa_selected")
    o_w = flash_heads(qk, qk, vsw, mode="window", batch=b, seq=s, tq=T["tq_win"], tk=tk, rc=rc,
                      q_head0=QK_Q0, k_head0=QK_KW0, v_head0=G, kv_shared=True, out_dtype=BF16,
                      gate=(misc, glog_blk, expand, 2), name="nsa_window")

    qm = projection(misc, w_uq, tm=tm, tn=tn, out_dtype=BF16, a_cols=(MLA_Q_RANK, 0), gain=mla_g_q[0],
                    rope=(cos_b, sin_b, shift_b), head_major=(b, s), name="mla_q",
                    out_scale=((MLA_NOPE + MLA_ROPE) ** -0.5 * log2e, MLA_HEADS * LANES // tn))
    km = projection(misc, w_uk, tm=tm, tn=tn, out_dtype=BF16,
                    a_cols=(MLA_KV_RANK, MLA_Q_RANK // MLA_KV_RANK), gain=mla_g_kv[0],
                    add=(misc, krp_blk), rope=(cos_b, sin_b, shift_b), rope_add=True,
                    head_major=(b, s), name="mla_k")
    vm = projection(misc, w_uv, tm=tm, tn=tn, out_dtype=BF16,
                    a_cols=(MLA_KV_RANK, MLA_Q_RANK // MLA_KV_RANK), gain=mla_g_kv[0],
                    head_major=(b, s), name="mla_v")
    o_b = flash_heads(qm, km, vm, mode="causal", batch=b, seq=s, tq=tq, tk=tk, rc=rc,
                      q_head0=0, k_head0=0, v_head0=0,
                      kv_shared=False, out_dtype=BF16, name="mla_attention")

    mixed = gated_mix(o_c, o_s, o_w, o_b, xn, w_ga.astype(BF16), w_gb.astype(BF16),
                      w_o_nsa[0].astype(BF16), w_o_mla[0].astype(BF16), tm=tm, tn=512)

    kv_mem = projection(mem.reshape(b * MEM_LEN, d), xa_wkv[0].astype(BF16), tm=MEM_LEN, tn=tn,
                        out_dtype=BF16, gain=g_mem[0], name="xattn_kv")
    h2 = out_proj_xattn(x2, mixed, w_out[0].astype(BF16), g_xattn[0], xa_wq[0].astype(BF16), kv_mem,
                        xa_wo[0].astype(BF16), seq=s, tm=T["tm_x"])

    out = mlp_final(h2, g_mlp[0], w_ff1[0].astype(BF16), w_ff2[0].astype(BF16), g_final,
                    tm=T["tm_mlp"], tf=T["tf"])
    return out.reshape(b, s, d)
```

```python
import functools

import numpy as np
import jax
import jax.numpy as jnp
from jax import lax
from jax.experimental import pallas as pl
from jax.experimental.pallas import tpu as pltpu

F32 = jnp.float32
BF16 = jnp.bfloat16

D_MODEL = 2048
MEM_LEN = 256
ROPE_THETA = 500000.0
EPS = 1e-6
NEG = -1e30

NSA_HEADS = 16
NSA_KV_HEADS = 4
NSA_HPG = NSA_HEADS // NSA_KV_HEADS
NSA_DK = 96
NSA_DV = 64
NSA_ROT = NSA_DK // 4
CMP_BLOCK = 32
CMP_STRIDE = 16
SEL_BLOCK = 64
SEL_TOPK = 16
N_FORCED = 3
WINDOW = 512

MLA_HEADS = 16
MLA_NOPE = 64
MLA_ROPE = 32
MLA_DV = 64
MLA_Q_RANK = 512
MLA_KV_RANK = 256

XA_HEADS = 4
XA_DIM = 128
D_FF = 4 * D_MODEL

SPLITS = (NSA_HEADS * NSA_DK,
          NSA_KV_HEADS * NSA_DK, NSA_KV_HEADS * NSA_DV,
          NSA_KV_HEADS * NSA_DK, NSA_KV_HEADS * NSA_DV,
          NSA_KV_HEADS * NSA_DK, NSA_KV_HEADS * NSA_DV,
          NSA_HEADS * 3,
          MLA_Q_RANK, MLA_KV_RANK, MLA_ROPE,
          D_MODEL, D_MODEL)

LANES = 128
HALF = LANES // 2
VMEM_LIMIT = 56 * 1024 * 1024

QK_Q0 = 0
QK_KC0 = NSA_HEADS
QK_KS0 = NSA_HEADS + NSA_KV_HEADS
QK_KW0 = NSA_HEADS + 2 * NSA_KV_HEADS
QK_HEADS = NSA_HEADS + 3 * NSA_KV_HEADS
FLASH_HEADS = NSA_HPG
TOPK_ROWS = 32
TOPK_QUERIES = 256


def _cparams(*sem):
    return pltpu.CompilerParams(dimension_semantics=sem, vmem_limit_bytes=VMEM_LIMIT)


def _tiles(seq):
    return dict(
        tm=min(1024, seq),
        tm_big=min(2048, seq),
        tn=4 * LANES,
        tm_x=min(512, seq),
        tm_mlp=min(512, seq),
        tq=min(2048, seq),
        tq_win=min(512, seq),
        tk=min(512, seq),
        rc=32,
        tq_cmp=min(512, seq),
        tf=1024,
    )


def _rms(x, g):
    return x * lax.rsqrt(jnp.mean(x * x, axis=-1, keepdims=True) + EPS) * g


def _rmsnorm_kernel(x_ref, g_ref, o_ref):
    o_ref[...] = _rms(x_ref[...], g_ref[...]).astype(o_ref.dtype)


def rmsnorm_rows(x, g, tm):
    n, d = x.shape
    return pl.pallas_call(
        _rmsnorm_kernel,
        out_shape=jax.ShapeDtypeStruct((n, d), BF16),
        grid=(n // tm,),
        in_specs=[pl.BlockSpec((tm, d), lambda i: (i, 0)),
                  pl.BlockSpec((1, d), lambda i: (0, 0))],
        out_specs=pl.BlockSpec((tm, d), lambda i: (i, 0)),
        compiler_params=_cparams("parallel"),
        name="rmsnorm_rows",
    )(x, g.reshape(1, d))


def _proj_kernel(*refs, has_gain, has_rope, roll_shift, rope_add, out_scale, has_add, head_major):
    it = iter(refs)
    a_ref = next(it)
    g_ref = next(it) if has_gain else None
    w_ref = next(it)
    cos_ref = next(it) if has_rope else None
    sin_ref = next(it) if has_rope else None
    add_ref = next(it) if has_add else None
    o_ref = next(it)

    if has_gain:
        an_sc = next(it)

        @pl.when(pl.program_id(1) == 0)
        def _norm():
            an_sc[...] = _rms(a_ref[...], g_ref[...]).astype(BF16)

        a = an_sc[...]
    else:
        a = a_ref[...]
    y = jnp.dot(a, w_ref[...], preferred_element_type=F32)
    if out_scale is not None:
        factor, n_tiles = out_scale
        y = y * jnp.where(pl.program_id(1) < n_tiles, factor, 1.0)
    rope = lambda t: t * cos_ref[...] + pltpu.roll(t, roll_shift, 1) * sin_ref[...]
    if has_add:
        chunk = add_ref[...].astype(F32)
        if rope_add:
            chunk = rope(chunk)
    if has_rope or has_add or head_major:
        for h in range(y.shape[1] // LANES):
            yh = y[:, h * LANES:(h + 1) * LANES]
            if has_rope and not rope_add:
                yh = rope(yh)
            if has_add:
                yh = yh + chunk
            if head_major:
                o_ref[h] = yh.astype(o_ref.dtype)
            else:
                o_ref[:, h * LANES:(h + 1) * LANES] = yh.astype(o_ref.dtype)
    else:
        o_ref[...] = y.astype(o_ref.dtype)


def projection(a, w, *, tm, tn, out_dtype, a_cols=None, gain=None, rope=None, rope_add=False,
               out_scale=None, add=None, head_major=None, name="projection"):
    n = a.shape[0]
    k, nc = w.shape
    a_w, a_blk = a_cols if a_cols is not None else (a.shape[1], 0)
    assert a_w == k and n % tm == 0 and nc % tn == 0
    grid = (n // tm, nc // tn)
    in_specs = [pl.BlockSpec((tm, k), lambda i, j: (i, a_blk))]
    args = [a]
    if gain is not None:
        in_specs.append(pl.BlockSpec((1, k), lambda i, j: (0, 0)))
        args.append(gain.reshape(1, k))
    in_specs.append(pl.BlockSpec((k, tn), lambda i, j: (0, j)))
    args.append(w)
    roll_shift = 0
    if rope is not None:
        cos, sin, roll_shift = rope
        s_tiles = cos.shape[0] // tm
        for t in (cos, sin):
            in_specs.append(pl.BlockSpec((tm, LANES), lambda i, j: (i % s_tiles, 0)))
            args.append(t)
    if add is not None:
        add_arr, add_blk = add
        in_specs.append(pl.BlockSpec((tm, LANES), lambda i, j: (i, add_blk)))
        args.append(add_arr)
    if head_major is not None:
        b, s = head_major
        s_t = s // tm
        hpt = tn // LANES
        out_shape = jax.ShapeDtypeStruct((b, nc // LANES, s, LANES), out_dtype)
        out_spec = pl.BlockSpec((None, hpt, tm, LANES), lambda i, j: (i // s_t, j, i % s_t, 0))
    else:
        out_shape = jax.ShapeDtypeStruct((n, nc), out_dtype)
        out_spec = pl.BlockSpec((tm, tn), lambda i, j: (i, j))
    kern = functools.partial(
        _proj_kernel, has_gain=gain is not None, has_rope=rope is not None, roll_shift=roll_shift,
        rope_add=rope_add, out_scale=out_scale, has_add=add is not None,
        head_major=head_major is not None)
    return pl.pallas_call(
        kern, out_shape=out_shape, grid=grid, in_specs=in_specs, out_specs=out_spec,
        scratch_shapes=[pltpu.VMEM((tm, k), BF16)] if gain is not None else [],
        compiler_params=_cparams("parallel", "arbitrary"), name=name,
    )(*args)


def _compress_kernel(r_ref, pa_ref, pb_ref, w1a_ref, w1b_ref, w2_ref, o_ref):
    r = r_ref[...].astype(F32)
    a = jnp.dot((r + pa_ref[...]).astype(BF16), w1a_ref[...], preferred_element_type=F32)
    b = jnp.dot((r + pb_ref[...]).astype(BF16), w1b_ref[...], preferred_element_type=F32)
    nr = a.shape[0]
    hid = a + pltpu.roll(b, nr - 1, 0)
    hid = jax.nn.gelu(hid)
    o_ref[...] = jnp.dot(hid.astype(BF16), w2_ref[...], preferred_element_type=F32).astype(o_ref.dtype)


def compress(r, head0, pos_a, pos_b, w1a, w1b, w2):
    b, _, nr, kk = r.shape
    g = NSA_KV_HEADS
    full = lambda shape: pl.BlockSpec(shape, lambda bi, gi: (0,) * len(shape))
    return pl.pallas_call(
        _compress_kernel,
        out_shape=jax.ShapeDtypeStruct((b, g, nr, LANES), BF16),
        grid=(b, g),
        in_specs=[pl.BlockSpec((None, None, nr, kk), lambda bi, gi: (bi, head0 + gi, 0, 0)),
                  full((1, kk)), full((1, kk)), full((kk, LANES)), full((kk, LANES)),
                  full((LANES, LANES))],
        out_specs=pl.BlockSpec((None, None, nr, LANES), lambda bi, gi: (bi, gi, 0, 0)),
        compiler_params=_cparams("parallel", "parallel"),
        name="nsa_compress",
    )(r, pos_a, pos_b, w1a, w1b, w2)


def _dot_nt(a, b):
    return lax.dot_general(a, b, (((1,), (1,)), ((), ())), preferred_element_type=F32)


def _split_dot(x, w):
    hi = x.astype(BF16)
    lo = (x - hi.astype(F32)).astype(BF16)
    return (jnp.dot(hi, w, preferred_element_type=F32) + jnp.dot(lo, w, preferred_element_type=F32))


def _nsa_cmp_kernel(q_ref, kcc_ref, vcc_ref, msel_ref, glog_ref, e_ref, oc_ref, sel_ref, s_sc, p_sc, imp_sc,
                    *, tq, rc, topk):
    s0 = pl.program_id(2) * tq
    nr = kcc_ref.shape[0]

    def attend(ncol):
        kcc = kcc_ref[0:ncol, :]
        for h in range(NSA_HPG):
            s_sc[h, :, 0:ncol] = _dot_nt(q_ref[h], kcc)
        lane_v = lax.broadcasted_iota(jnp.int32, (ncol, LANES), 1)
        vcc = vcc_ref[0:ncol, :]
        vm = jnp.concatenate([jnp.where(lane_v == HALF, jnp.ones_like(vcc), vcc), msel_ref[0:ncol, :]],
                             axis=1)
        dmat = (lax.broadcasted_iota(jnp.int32, (rc, ncol), 0)
                - lax.broadcasted_iota(jnp.int32, (rc, ncol), 1) * CMP_STRIDE)
        visible = lambda c: dmat + (s0 + c * rc - (CMP_BLOCK - 1)) >= 0
        outs = []
        imp = jnp.zeros((tq, LANES), F32)
        for h in range(NSA_HPG):
            row_max = []
            for c in range(tq // rc):
                s = jnp.where(visible(c), s_sc[h, pl.ds(c * rc, rc), 0:ncol], NEG)
                row_max.append(jnp.max(s, axis=-1, keepdims=True))
            for c in range(tq // rc):
                rows = pl.ds(c * rc, rc)
                p = jnp.exp2(s_sc[h, rows, 0:ncol] - row_max[c])
                p_sc[h, rows, 0:ncol] = jnp.where(visible(c), p, 0.0).astype(BF16)
            a = jnp.dot(p_sc[h, :, 0:ncol], vm, preferred_element_type=F32)
            denom = jnp.maximum(a[:, HALF:HALF + 1], 1e-30)
            outs.append(a[:, :LANES] / denom)
            imp = imp + a[:, LANES:] / denom
        imp_sc[...] = imp
        glog = glog_ref[...]
        lane = lax.broadcasted_iota(jnp.int32, (tq, LANES), 1)
        for pr in range(NSA_HPG // 2):
            gate = jax.nn.sigmoid(_split_dot(glog, e_ref[pr]))
            pair = jnp.where(lane < HALF, outs[2 * pr], pltpu.roll(outs[2 * pr + 1], HALF, 1))
            oc_ref[:, pr * LANES:(pr + 1) * LANES] = (gate * pair).astype(oc_ref.dtype)

    ntiles = (s0 + (tq - CMP_BLOCK)) // (CMP_STRIDE * LANES) + 1
    for k in range(1, nr // LANES + 1):
        @pl.when(ntiles == k)
        def _width(k=k):
            attend(k * LANES)

    blk = lax.broadcasted_iota(jnp.int32, (tq, LANES), 1)
    cur = (s0 + lax.broadcasted_iota(jnp.int32, (tq, LANES), 0)) // SEL_BLOCK
    valid = blk <= cur
    forced = (blk == 0) | (blk == cur) | (blk == cur - 1)
    score = jnp.where(valid, jnp.where(forced, -jnp.inf, imp_sc[...]), -1.0)
    score_t = score.T
    nblk = (s0 + (tq - 1)) // SEL_BLOCK + 1
    for nrows in range(TOPK_ROWS, LANES + 1, TOPK_ROWS):
        @pl.when((nblk + (TOPK_ROWS - 1)) // TOPK_ROWS == nrows // TOPK_ROWS)
        def _select(nrows=nrows):
            tw = min(tq, TOPK_QUERIES)
            rowid = lax.broadcasted_iota(jnp.int32, (nrows, tw), 0).astype(F32)

            def pick_one(_, sc):
                cm = jnp.max(sc, axis=0, keepdims=True)
                first = jnp.min(jnp.where(sc == cm, rowid, float(LANES)), axis=0, keepdims=True)
                return jnp.where(rowid == first, -jnp.inf, sc)

            for w in range(tq // tw):
                cols = slice(w * tw, (w + 1) * tw)
                picked = lax.fori_loop(0, topk - N_FORCED, pick_one, score_t[0:nrows, cols]) == -jnp.inf
                bias = jnp.where(picked, 0.0, NEG)
                if nrows < LANES:
                    bias = jnp.concatenate([bias, jnp.full((LANES - nrows, tw), NEG, F32)], axis=0)
                sel_ref[cols, :] = bias.T.astype(sel_ref.dtype)


def nsa_compressed(qk, kcc, vcc, msel, misc, glog_blk, expand, *, batch, seq, tq, rc):
    g = NSA_KV_HEADS
    nq = seq // tq
    nr = kcc.shape[2]
    topk = min(SEL_TOPK, seq // SEL_BLOCK)
    assert topk > N_FORCED and nr % LANES == 0
    kern = functools.partial(_nsa_cmp_kernel, tq=tq, rc=rc, topk=topk)
    return pl.pallas_call(
        kern,
        out_shape=(jax.ShapeDtypeStruct((batch * seq, NSA_HEADS * NSA_DV), BF16),
                   jax.ShapeDtypeStruct((batch, g, seq, LANES), BF16)),
        grid=(batch, g, nq),
        in_specs=[
            pl.BlockSpec((None, NSA_HPG, tq, LANES), lambda b, gi, qi: (b, gi, qi, 0)),
            pl.BlockSpec((None, None, nr, LANES), lambda b, gi, qi: (b, gi, 0, 0)),
            pl.BlockSpec((None, None, nr, LANES), lambda b, gi, qi: (b, gi, 0, 0)),
            pl.BlockSpec((nr, LANES), lambda b, gi, qi: (0, 0)),
            pl.BlockSpec((tq, LANES), lambda b, gi, qi: (b * nq + qi, glog_blk)),
            pl.BlockSpec((None, 2, LANES, LANES), lambda b, gi, qi: (0, gi, 0, 0)),
        ],
        out_specs=(pl.BlockSpec((tq, 2 * LANES), lambda b, gi, qi: (b * nq + qi, gi)),
                   pl.BlockSpec((None, None, tq, LANES), lambda b, gi, qi: (b, gi, qi, 0))),
        scratch_shapes=[pltpu.VMEM((NSA_HPG, tq, nr), F32), pltpu.VMEM((NSA_HPG, tq, nr), BF16),
                        pltpu.VMEM((tq, LANES), F32)],
        compiler_params=_cparams("parallel", "parallel", "parallel"),
        name="nsa_compressed_select",
    )(qk, kcc, vcc, msel, misc, expand)


def _flash_steps(mode, nq, tq, tk):
    r = tq // tk
    qi_l, kt_l, first_l, last_l = [], [], [], []
    for qi in range(nq):
        hi = qi * r + r - 1
        lo = max(0, qi * r - (-(-(WINDOW - 1) // tk))) if mode == "window" else 0
        for kt in range(lo, hi + 1):
            qi_l.append(qi)
            kt_l.append(kt)
            first_l.append(int(kt == lo))
            last_l.append(int(kt == hi))
    return tuple(jnp.asarray(np.asarray(a, np.int32)) for a in (qi_l, kt_l, first_l, last_l))


def _flash_kernel(*refs, mode, tq, tk, rc, kv_shared, gated):
    it = iter(refs)
    qi_ref, kt_ref, first_ref, last_ref = next(it), next(it), next(it), next(it)
    q_ref, k_ref, v_ref = next(it), next(it), next(it)
    selb_ref = next(it) if mode == "select" else None
    oh_ref = next(it) if mode == "select" else None
    glog_ref = next(it) if gated else None
    e_ref = next(it) if gated else None
    o_ref = next(it)
    m_sc, acc_sc, s_sc, p_sc = next(it), next(it), next(it), next(it)
    qa_sc = next(it) if mode == "select" else None

    st = pl.program_id(2)
    s0 = qi_ref[st] * tq
    k0 = kt_ref[st] * tk

    @pl.when(first_ref[st] == 1)
    def _init():
        m_sc[...] = jnp.full(m_sc.shape, NEG, F32)
        acc_sc[...] = jnp.zeros(acc_sc.shape, F32)
        if mode == "select":
            for hh in range(FLASH_HEADS):
                qa_sc[hh] = jnp.concatenate([q_ref[hh], selb_ref[...]], axis=1)

    def chunk_cols(off, r0):
        if off is None:
            return 0, tk, True
        r1 = r0 + rc
        hi = min(tk, r1 - off)
        lo = max(0, r0 - off - WINDOW + 1) if mode == "window" else 0
        lo, hi = lo // LANES * LANES, -(-hi // LANES) * LANES
        clear = tk - 1 <= r0 - off and (mode != "window" or r1 - 1 - off < WINDOW)
        return lo, hi, clear

    def tile(off):
        nh = FLASH_HEADS
        r_lo = max(0, off) if off is not None else 0
        nrow = tq - r_lo
        if kv_shared:
            if mode == "select":
                qs = qa_sc[:, r_lo:, :].reshape(nh * nrow, 2 * LANES)
                ks = jnp.concatenate([k_ref[0], oh_ref[...]], axis=1)
            else:
                qs, ks = q_ref[:, r_lo:, :].reshape(nh * nrow, LANES), k_ref[0]
            s_sc[:, r_lo:, :] = _dot_nt(qs, ks).reshape(nh, nrow, tk)
        else:
            for hh in range(nh):
                s_sc[hh, r_lo:, :] = _dot_nt(q_ref[hh, r_lo:, :], k_ref[hh])
        lane_v = lax.broadcasted_iota(jnp.int32, (tk, LANES), 1)
        if off is not None:
            dmat = (lax.broadcasted_iota(jnp.int32, (rc, tk), 0)
                    - lax.broadcasted_iota(jnp.int32, (rc, tk), 1))
        for hh in range(FLASH_HEADS):
            v = v_ref[0 if kv_shared else hh]
            v_aug = jnp.where(lane_v == HALF, jnp.ones_like(v), v)

            for c in range(r_lo // rc, tq // rc):
                r0 = c * rc
                rows = pl.ds(r0, rc)
                lo, hi, clear = chunk_cols(off, r0)
                if lo > 0:
                    p_sc[hh, rows, 0:lo] = jnp.zeros((rc, lo), BF16)
                if hi < tk:
                    p_sc[hh, rows, max(hi, 0):tk] = jnp.zeros((rc, tk - max(hi, 0)), BF16)
                if hi <= lo:
                    continue
                s = s_sc[hh, rows, lo:hi]
                if not clear:
                    d = dmat[:, lo:hi] + (r0 - off)
                    msk = (d >= 0) & (d < WINDOW) if mode == "window" else d >= 0
                    s = jnp.where(msk, s, NEG)
                m_old = m_sc[hh, rows, :]
                m_new = jnp.maximum(m_old, jnp.max(s, axis=-1, keepdims=True))
                m_sc[hh, rows, :] = m_new
                acc_sc[hh, rows, :] = acc_sc[hh, rows, :] * jnp.exp2(m_old - m_new)
                p = jnp.exp2(s_sc[hh, rows, lo:hi] - jnp.concatenate([m_new] * ((hi - lo) // LANES), axis=1))
                if not clear:
                    p = jnp.where(msk, p, 0.0)
                p_sc[hh, rows, lo:hi] = p.astype(BF16)
            if not kv_shared:
                acc_sc[hh, r_lo:, :] += jnp.dot(p_sc[hh, r_lo:, :], v_aug, preferred_element_type=F32)
        if kv_shared:
            pv = jnp.dot(p_sc[:, r_lo:, :].reshape(nh * nrow, tk), v_aug, preferred_element_type=F32)
            acc_sc[:, r_lo:, :] += pv.reshape(nh, nrow, LANES)

    if mode == "window":
        offsets = [(j - (-(-(WINDOW - 1) // tk))) * tk for j in range(-(-(WINDOW - 1) // tk) + tq // tk)]
    else:
        offsets = [j * tk for j in range(tq // tk)]

        @pl.when(k0 + tk - 1 <= s0)
        def _interior():
            tile(None)

    for off in offsets:
        @pl.when(k0 - s0 == off)
        def _partial(off=off):
            tile(off)

    @pl.when(last_ref[st] == 1)
    def _finish():
        lane = lax.broadcasted_iota(jnp.int32, (tq, LANES), 1)
        if gated:
            glog = glog_ref[...]
        for pr in range(FLASH_HEADS // 2):
            outs = []
            for hh in (2 * pr, 2 * pr + 1):
                a = acc_sc[hh]
                outs.append(a / jnp.maximum(a[:, HALF:HALF + 1], 1e-30))
            out = jnp.where(lane < HALF, outs[0], pltpu.roll(outs[1], HALF, 1))
            if gated:
                out = jax.nn.sigmoid(_split_dot(glog, e_ref[pr])) * out
            o_ref[:, pr * LANES:(pr + 1) * LANES] = out.astype(o_ref.dtype)


def flash_heads(q, k, v, *, mode, batch, seq, tq, tk, rc, q_head0, k_head0, v_head0,
                kv_shared, out_dtype, selb=None, onehot=None, gate=None, name="flash"):
    nh = FLASH_HEADS
    ngroups = NSA_HEADS // nh
    nq = seq // tq
    assert tq % tk == 0 and tq % rc == 0
    tabs = _flash_steps(mode, nq, tq, tk)
    nsteps = tabs[0].shape[0]
    if kv_shared:
        kv_spec = lambda h0: pl.BlockSpec(
            (None, 1, tk, LANES), lambda b, g, st, qi, kt, fi, la: (b, h0 + g, kt[st], 0))
    else:
        kv_spec = lambda h0: pl.BlockSpec(
            (None, nh, tk, LANES), lambda b, g, st, qi, kt, fi, la: (b, h0 // nh + g, kt[st], 0))
    in_specs = [pl.BlockSpec((None, nh, tq, LANES),
                             lambda b, g, st, qi, kt, fi, la: (b, q_head0 // nh + g, qi[st], 0)),
                kv_spec(k_head0), kv_spec(v_head0)]
    args = [q, k, v]
    scratch = [pltpu.VMEM((nh, tq, LANES), F32), pltpu.VMEM((nh, tq, LANES), F32),
               pltpu.VMEM((nh, tq, tk), F32), pltpu.VMEM((nh, tq, tk), BF16)]
    if mode == "select":
        in_specs += [pl.BlockSpec((None, None, tq, LANES),
                                  lambda b, g, st, qi, kt, fi, la: (b, g, qi[st], 0)),
                     pl.BlockSpec((tk, LANES), lambda b, g, st, qi, kt, fi, la: (kt[st], 0))]
        args += [selb, onehot]
        scratch.append(pltpu.VMEM((nh, tq, 2 * LANES), BF16))
    if gate is not None:
        misc, glog_blk, expand, branch = gate
        in_specs += [pl.BlockSpec((tq, LANES),
                                  lambda b, g, st, qi, kt, fi, la: (b * nq + qi[st], glog_blk)),
                     pl.BlockSpec((None, nh // 2, LANES, LANES),
                                  lambda b, g, st, qi, kt, fi, la: (branch, g, 0, 0))]
        args += [misc, expand]
    kern = functools.partial(_flash_kernel, mode=mode, tq=tq, tk=tk, rc=rc,
                             kv_shared=kv_shared, gated=gate is not None)
    ow = nh * HALF
    grid_spec = pltpu.PrefetchScalarGridSpec(
        num_scalar_prefetch=4, grid=(batch, ngroups, nsteps), in_specs=in_specs,
        out_specs=pl.BlockSpec((tq, ow), lambda b, g, st, qi, kt, fi, la: (b * nq + qi[st], g)),
        scratch_shapes=scratch)
    return pl.pallas_call(
        kern,
        out_shape=jax.ShapeDtypeStruct((batch * seq, ngroups * ow), out_dtype),
        grid_spec=grid_spec,
        compiler_params=_cparams("parallel", "parallel", "arbitrary"),
        name=name,
    )(*tabs, *args)


def _mix_kernel(oc_ref, os_ref, ow_ref, ob_ref, xn_ref, wga_ref, wgb_ref, wa_ref, wb_ref, o_ref):
    oa = (oc_ref[...].astype(F32) + os_ref[...].astype(F32) + ow_ref[...].astype(F32)).astype(BF16)
    xn = xn_ref[...]
    ga = jax.nn.sigmoid(jnp.dot(xn, wga_ref[...], preferred_element_type=F32))
    ya = ga * jnp.dot(oa, wa_ref[...], preferred_element_type=F32)
    gb = jax.nn.sigmoid(jnp.dot(xn, wgb_ref[...], preferred_element_type=F32))
    yb = gb * jnp.dot(ob_ref[...], wb_ref[...], preferred_element_type=F32)
    o_ref[...] = (ya + yb).astype(o_ref.dtype)


def gated_mix(oc, osel, ow, ob, xn, wga, wgb, wa, wb, *, tm, tn):
    n, ka = oc.shape
    d = wa.shape[1]
    row = lambda w: pl.BlockSpec((tm, w), lambda i, j: (i, 0))
    col = lambda k: pl.BlockSpec((k, tn), lambda i, j: (0, j))
    return pl.pallas_call(
        _mix_kernel,
        out_shape=jax.ShapeDtypeStruct((n, d), BF16),
        grid=(n // tm, d // tn),
        in_specs=[row(ka), row(ka), row(ka), row(ob.shape[1]), row(xn.shape[1]),
                  col(wga.shape[0]), col(wgb.shape[0]), col(ka), col(wb.shape[0])],
        out_specs=pl.BlockSpec((tm, tn), lambda i, j: (i, j)),
        compiler_params=_cparams("parallel", "arbitrary"),
        name="gated_mix",
    )(oc, osel, ow, ob, xn, wga, wgb, wa, wb)


def _xattn_kernel(x_ref, mix_ref, wout_ref, g_ref, wq_ref, kv_ref, wo_ref, o_ref):
    h = x_ref[...] + jnp.dot(mix_ref[...], wout_ref[...], preferred_element_type=F32)
    hn = _rms(h, g_ref[...]).astype(BF16)
    q = jnp.dot(hn, wq_ref[...], preferred_element_type=F32).astype(BF16)
    kv = kv_ref[...]
    outs = []
    for hd in range(XA_HEADS):
        qh = q[:, hd * XA_DIM:(hd + 1) * XA_DIM]
        kh = kv[:, hd * XA_DIM:(hd + 1) * XA_DIM]
        vh = kv[:, (XA_HEADS + hd) * XA_DIM:(XA_HEADS + hd + 1) * XA_DIM]
        s = _dot_nt(qh, kh) * (XA_DIM ** -0.5)
        e = jnp.exp(s - jnp.max(s, axis=-1, keepdims=True))
        p = e / jnp.sum(e, axis=-1, keepdims=True)
        outs.append(jnp.dot(p.astype(BF16), vh, preferred_element_type=F32))
    o = jnp.concatenate(outs, axis=1).astype(BF16)
    o_ref[...] = h + jnp.dot(o, wo_ref[...], preferred_element_type=F32)


def out_proj_xattn(x, mixed, wout, g, wq, kv, wo, *, seq, tm):
    n, d = x.shape
    s_t = seq // tm
    dq = wq.shape[1]
    resident = lambda shape: pl.BlockSpec(shape, lambda i: (0, 0), pipeline_mode=pl.Buffered(1))
    return pl.pallas_call(
        _xattn_kernel,
        out_shape=jax.ShapeDtypeStruct((n, d), F32),
        grid=(n // tm,),
        in_specs=[pl.BlockSpec((tm, d), lambda i: (i, 0)),
                  pl.BlockSpec((tm, d), lambda i: (i, 0)),
                  resident((d, d)),
                  pl.BlockSpec((1, d), lambda i: (0, 0)),
                  resident((d, dq)),
                  pl.BlockSpec((MEM_LEN, 2 * dq), lambda i: (i // s_t, 0)),
                  resident((dq, d))],
        out_specs=pl.BlockSpec((tm, d), lambda i: (i, 0)),
        compiler_params=_cparams("parallel"),
        name="out_proj_xattn",
    )(x, mixed, wout, g.reshape(1, d), wq, kv, wo)


def _mlp_kernel(h_ref, g_ref, w1_ref, w2_ref, gf_ref, o_ref, hn_sc, acc_sc, *, nf):
    f = pl.program_id(1)

    @pl.when(f == 0)
    def _init():
        hn_sc[...] = _rms(h_ref[...], g_ref[...]).astype(BF16)
        acc_sc[...] = jnp.zeros(acc_sc.shape, F32)

    u = jnp.dot(hn_sc[...], w1_ref[...], preferred_element_type=F32)
    u = jnp.square(jnp.maximum(u, 0.0))
    acc_sc[...] += jnp.dot(u.astype(BF16), w2_ref[...], preferred_element_type=F32)

    @pl.when(f == nf - 1)
    def _finish():
        o_ref[...] = _rms(h_ref[...] + acc_sc[...], gf_ref[...])


def mlp_final(h, g, w1, w2, gf, *, tm, tf):
    n, d = h.shape
    dff = w1.shape[1]
    nf = dff // tf
    return pl.pallas_call(
        functools.partial(_mlp_kernel, nf=nf),
        out_shape=jax.ShapeDtypeStruct((n, d), F32),
        grid=(n // tm, nf),
        in_specs=[pl.BlockSpec((tm, d), lambda i, f: (i, 0)),
                  pl.BlockSpec((1, d), lambda i, f: (0, 0)),
                  pl.BlockSpec((d, tf), lambda i, f: (0, f)),
                  pl.BlockSpec((tf, d), lambda i, f: (f, 0)),
                  pl.BlockSpec((1, d), lambda i, f: (0, 0))],
        out_specs=pl.BlockSpec((tm, d), lambda i, f: (i, 0)),
        scratch_shapes=[pltpu.VMEM((tm, d), BF16), pltpu.VMEM((tm, d), F32)],
        compiler_params=_cparams("parallel", "arbitrary"),
        name="mlp_final",
    )(h, g.reshape(1, d), w1, w2, gf.reshape(1, d))


def _rot_partner(w, half):
    return jnp.concatenate([-w[..., half:], w[..., :half]], axis=-1)


def _pad_heads_rope(w, heads, hd, rot0, rot_dim):
    k = w.shape[0]
    w = w.reshape(k, heads, hd)
    partner = _rot_partner(w[:, :, rot0:rot0 + rot_dim], rot_dim // 2)
    pad = jnp.zeros((k, heads, LANES - hd - rot_dim), w.dtype)
    return jnp.concatenate([w, partner, pad], axis=-1).reshape(k, heads * LANES)


def _rope_lane_tables(seq, rot0, rot_dim, hd):
    inv = 1.0 / (ROPE_THETA ** (jnp.arange(0, rot_dim, 2, dtype=F32) / rot_dim))
    ang = jnp.arange(seq, dtype=F32)[:, None] * inv[None, :]
    cos = jnp.concatenate([jnp.cos(ang), jnp.cos(ang)], axis=1)
    sin = jnp.concatenate([jnp.sin(ang), jnp.sin(ang)], axis=1)
    cos_t = jnp.concatenate([jnp.ones((seq, rot0), F32), cos,
                             jnp.ones((seq, hd - rot0 - rot_dim), F32),
                             jnp.zeros((seq, LANES - hd), F32)], axis=1)
    sin_t = jnp.concatenate([jnp.zeros((seq, rot0), F32), sin,
                             jnp.zeros((seq, LANES - rot0 - rot_dim), F32)], axis=1)
    return cos_t, sin_t


def _pad_lanes(w, groups, width):
    k = w.shape[0]
    w = w.reshape(k, groups, width)
    return jnp.pad(w, ((0, 0), (0, 0), (0, LANES - width))).reshape(k, groups * LANES)


def _compress_params(pos, w1, w2, d):
    half = CMP_BLOCK // 2
    pos_p = jnp.pad(pos, ((0, 0), (0, LANES - d)))
    pos_a = pos_p[:half].reshape(1, half * LANES)
    pos_b = pos_p[half:].reshape(1, half * LANES)
    w1p = jnp.pad(w1.reshape(CMP_BLOCK, d, d), ((0, 0), (0, LANES - d), (0, LANES - d)))
    w1a = w1p[:half].reshape(half * LANES, LANES).astype(BF16)
    w1b = w1p[half:].reshape(half * LANES, LANES).astype(BF16)
    w2p = jnp.pad(w2, ((0, LANES - d), (0, LANES - d)))
    return pos_a, pos_b, w1a, w1b, w2p.astype(BF16)


def _cmp_to_sel(nr, nsb):
    cs = np.arange(nr) * CMP_STRIDE
    ce = cs + CMP_BLOCK
    ss = np.arange(LANES) * SEL_BLOCK
    se = ss + SEL_BLOCK
    ov = np.clip(np.minimum(ce[:, None], se[None, :]) - np.maximum(cs[:, None], ss[None, :]), 0, None)
    ov = ov.astype(np.float32) / np.float32(CMP_BLOCK)
    ov[:, nsb:] = 0.0
    ov[nr - 1:, :] = 0.0
    return jnp.asarray(ov, BF16)


def _gate_expand():
    e = np.zeros((3, NSA_HEADS // 2, LANES, LANES), np.float32)
    for br in range(3):
        for hp in range(NSA_HEADS // 2):
            for hh in range(2):
                e[br, hp, 3 * (2 * hp + hh) + br, hh * HALF:(hh + 1) * HALF] = 1.0
    return jnp.asarray(e, BF16)


def _key_block_onehot(seq):
    e = (np.arange(seq)[:, None] // SEL_BLOCK) == np.arange(LANES)[None, :]
    return jnp.asarray(e.astype(np.float32), BF16)


def kernel(x, mem, g_mix, w_in, cmp_pos_k, cmp_w1_k, cmp_w2_k, cmp_pos_v, cmp_w1_v, cmp_w2_v,
           mla_g_q, mla_w_uq, mla_g_kv, mla_w_uk, mla_w_uv, w_o_nsa, w_o_mla, w_out,
           g_xattn, g_mem, xa_wq, xa_wkv, xa_wo, g_mlp, w_ff1, w_ff2, g_final):
    b, s, d = x.shape
    assert d == D_MODEL and s % (CMP_STRIDE * 8) == 0 and s // SEL_BLOCK <= LANES
    assert g_mix.shape[0] == 1
    n = b * s
    T = _tiles(s)
    tm, tn, tq, tk, rc = T["tm"], T["tn"], T["tq"], T["tk"], T["rc"]
    G = NSA_KV_HEADS
    bounds = [int(v) for v in np.cumsum(SPLITS)[:-1]]

    (w_qa, w_kc, w_vc, w_ks, w_vs, w_kw, w_vw, w_gn, w_cq, w_ckv, w_kr,
     w_ga, w_gb) = jnp.split(w_in[0], bounds, axis=1)
    nsa_rope = lambda w, heads: _pad_heads_rope(w, heads, NSA_DK, 0, NSA_ROT)
    w_rope = jnp.concatenate([nsa_rope(w_qa, NSA_HEADS), nsa_rope(w_kc, G), nsa_rope(w_ks, G),
                              nsa_rope(w_kw, G)], axis=1).astype(BF16)
    w_krp = jnp.concatenate([jnp.zeros((d, MLA_NOPE), F32), w_kr, _rot_partner(w_kr, MLA_ROPE // 2)], axis=1)
    w_vsw = jnp.concatenate([_pad_lanes(w_vs, G, NSA_DV), _pad_lanes(w_vw, G, NSA_DV),
                             _pad_lanes(w_vc, G, NSA_DV)], axis=1).astype(BF16)
    w_misc = jnp.concatenate([w_cq, w_ckv, jnp.pad(w_gn, ((0, 0), (0, LANES - w_gn.shape[1]))), w_krp],
                             axis=1).astype(BF16)
    glog_blk = (MLA_Q_RANK + MLA_KV_RANK) // LANES
    krp_blk = glog_blk + 1
    cos_a, sin_a = _rope_lane_tables(s, 0, NSA_ROT, NSA_DK)
    cos_b, sin_b = _rope_lane_tables(s, MLA_NOPE, MLA_ROPE, MLA_NOPE + MLA_ROPE)
    shift_a = LANES - NSA_DK
    shift_b = LANES - (NSA_DK - MLA_NOPE)
    w_uq = _pad_heads_rope(mla_w_uq[0], MLA_HEADS, MLA_NOPE + MLA_ROPE, MLA_NOPE, MLA_ROPE).astype(BF16)
    w_uk = _pad_lanes(mla_w_uk[0], MLA_HEADS, MLA_NOPE).astype(BF16)
    w_uv = _pad_lanes(mla_w_uv[0], MLA_HEADS, MLA_DV).astype(BF16)

    x2 = x.reshape(n, d)
    xn = rmsnorm_rows(x2, g_mix[0], tm)
    log2e = float(np.log2(np.e))
    tmb = T["tm_big"]
    qk = projection(xn, w_rope, tm=tmb, tn=tn, out_dtype=BF16, rope=(cos_a, sin_a, shift_a),
                    out_scale=(NSA_DK ** -0.5 * log2e, NSA_HEADS * LANES // tn),
                    head_major=(b, s), name="proj_qk_rope")
    vsw = projection(xn, w_vsw, tm=tmb, tn=tn, out_dtype=BF16, head_major=(b, s), name="proj_v")
    vcp = vsw[:, 2 * G:3 * G]
    misc = projection(xn, w_misc, tm=tm, tn=w_misc.shape[1], out_dtype=F32, name="proj_misc")

    nr = s // CMP_STRIDE
    pk = _compress_params(cmp_pos_k[0], cmp_w1_k[0], cmp_w2_k[0], NSA_DK)
    pv = _compress_params(cmp_pos_v[0], cmp_w1_v[0], cmp_w2_v[0], NSA_DV)
    kc_rows = qk[:, QK_KC0:QK_KC0 + G].reshape(b, G, nr, CMP_STRIDE * LANES)
    kcc = compress(kc_rows, 0, *pk)
    vcc = compress(vcp.reshape(b, G, nr, CMP_STRIDE * LANES), 0, *pv)
    expand = _gate_expand()
    o_c, selb = nsa_compressed(qk, kcc, vcc, _cmp_to_sel(nr, s // SEL_BLOCK), misc, glog_blk, expand,
                               batch=b, seq=s, tq=T["tq_cmp"], rc=rc)
    o_s = flash_heads(qk, qk, vsw, mode="select", batch=b, seq=s, tq=tq, tk=tk, rc=rc,
                      q_head0=QK_Q0, k_head0=QK_KS0, v_head0=0, kv_shared=True, out_dtype=BF16,
                      selb=selb, onehot=_key_block_onehot(s), gate=(misc, glog_blk, expand, 1),
                      name="nsa_selected")
    o_w = flash_heads(qk, qk, vsw, mode="window", batch=b, seq=s, tq=T["tq_win"], tk=tk, rc=rc,
                      q_head0=QK_Q0, k_head0=QK_KW0, v_head0=G, kv_shared=True, out_dtype=BF16,
                      gate=(misc, glog_blk, expand, 2), name="nsa_window")

    qm = projection(misc, w_uq, tm=tm, tn=tn, out_dtype=BF16, a_cols=(MLA_Q_RANK, 0), gain=mla_g_q[0],
                    rope=(cos_b, sin_b, shift_b), head_major=(b, s), name="mla_q",
                    out_scale=((MLA_NOPE + MLA_ROPE) ** -0.5 * log2e, MLA_HEADS * LANES // tn))
    km = projection(misc, w_uk, tm=tm, tn=tn, out_dtype=BF16,
                    a_cols=(MLA_KV_RANK, MLA_Q_RANK // MLA_KV_RANK), gain=mla_g_kv[0],
                    add=(misc, krp_blk), rope=(cos_b, sin_b, shift_b), rope_add=True,
                    head_major=(b, s), name="mla_k")
    vm = projection(misc, w_uv, tm=tm, tn=tn, out_dtype=BF16,
                    a_cols=(MLA_KV_RANK, MLA_Q_RANK // MLA_KV_RANK), gain=mla_g_kv[0],
                    head_major=(b, s), name="mla_v")
    o_b = flash_heads(qm, km, vm, mode="causal", batch=b, seq=s, tq=tq, tk=tk, rc=rc,
                      q_head0=0, k_head0=0, v_head0=0,
                      kv_shared=False, out_dtype=BF16, name="mla_attention")

    mixed = gated_mix(o_c, o_s, o_w, o_b, xn, w_ga.astype(BF16), w_gb.astype(BF16),
                      w_o_nsa[0].astype(BF16), w_o_mla[0].astype(BF16), tm=tm, tn=512)

    kv_mem = projection(mem.reshape(b * MEM_LEN, d), xa_wkv[0].astype(BF16), tm=MEM_LEN, tn=tn,
                        out_dtype=BF16, gain=g_mem[0], name="xattn_kv")
    h2 = out_proj_xattn(x2, mixed, w_out[0].astype(BF16), g_xattn[0], xa_wq[0].astype(BF16), kv_mem,
                        xa_wo[0].astype(BF16), seq=s, tm=T["tm_x"])

    out = mlp_final(h2, g_mlp[0], w_ff1[0].astype(BF16), w_ff2[0].astype(BF16), g_final,
                    tm=T["tm_mlp"], tf=T["tf"])
    return out.reshape(b, s, d)
```

```python
import functools

import numpy as np
import jax
import jax.numpy as jnp
from jax import lax
from jax.experimental import pallas as pl
from jax.experimental.pallas import tpu as pltpu

F32 = jnp.float32
BF16 = jnp.bfloat16

D_MODEL = 2048
MEM_LEN = 256
ROPE_THETA = 500000.0
EPS = 1e-6
NEG = -1e30

NSA_HEADS = 16
NSA_KV_HEADS = 4
NSA_HPG = NSA_HEADS // NSA_KV_HEADS
NSA_DK = 96
NSA_DV = 64
NSA_ROT = NSA_DK // 4
CMP_BLOCK = 32
CMP_STRIDE = 16
SEL_BLOCK = 64
SEL_TOPK = 16
N_FORCED = 3
WINDOW = 512

MLA_HEADS = 16
MLA_NOPE = 64
MLA_ROPE = 32
MLA_DV = 64
MLA_Q_RANK = 512
MLA_KV_RANK = 256

XA_HEADS = 4
XA_DIM = 128
D_FF = 4 * D_MODEL

SPLITS = (NSA_HEADS * NSA_DK,
          NSA_KV_HEADS * NSA_DK, NSA_KV_HEADS * NSA_DV,
          NSA_KV_HEADS * NSA_DK, NSA_KV_HEADS * NSA_DV,
          NSA_KV_HEADS * NSA_DK, NSA_KV_HEADS * NSA_DV,
          NSA_HEADS * 3,
          MLA_Q_RANK, MLA_KV_RANK, MLA_ROPE,
          D_MODEL, D_MODEL)

LANES = 128
HALF = LANES // 2
VMEM_LIMIT = 56 * 1024 * 1024

QK_Q0 = 0
QK_KC0 = NSA_HEADS
QK_KS0 = NSA_HEADS + NSA_KV_HEADS
QK_KW0 = NSA_HEADS + 2 * NSA_KV_HEADS
QK_HEADS = NSA_HEADS + 3 * NSA_KV_HEADS
FLASH_HEADS = NSA_HPG
TOPK_ROWS = 32
TOPK_QUERIES = 256


def _cparams(*sem):
    return pltpu.CompilerParams(dimension_semantics=sem, vmem_limit_bytes=VMEM_LIMIT)


def _tiles(seq):
    return dict(
        tm=min(1024, seq),
        tm_big=min(2048, seq),
        tn=4 * LANES,
        tm_x=min(512, seq),
        tm_mlp=min(512, seq),
        tq=min(2048, seq),
        tq_win=min(512, seq),
        tk=min(512, seq),
        rc=32,
        tq_cmp=min(512, seq),
        tf=1024,
    )


def _rms(x, g):
    return x * lax.rsqrt(jnp.mean(x * x, axis=-1, keepdims=True) + EPS) * g


def _rmsnorm_kernel(x_ref, g_ref, o_ref):
    o_ref[...] = _rms(x_ref[...], g_ref[...]).astype(o_ref.dtype)


def rmsnorm_rows(x, g, tm):
    n, d = x.shape
    return pl.pallas_call(
        _rmsnorm_kernel,
        out_shape=jax.ShapeDtypeStruct((n, d), BF16),
        grid=(n // tm,),
        in_specs=[pl.BlockSpec((tm, d), lambda i: (i, 0)),
                  pl.BlockSpec((1, d), lambda i: (0, 0))],
        out_specs=pl.BlockSpec((tm, d), lambda i: (i, 0)),
        compiler_params=_cparams("parallel"),
        name="rmsnorm_rows",
    )(x, g.reshape(1, d))


def _proj_kernel(*refs, has_gain, has_rope, roll_shift, rope_add, out_scale, has_add, head_major):
    it = iter(refs)
    a_ref = next(it)
    g_ref = next(it) if has_gain else None
    w_ref = next(it)
    cos_ref = next(it) if has_rope else None
    sin_ref = next(it) if has_rope else None
    add_ref = next(it) if has_add else None
    o_ref = next(it)

    if has_gain:
        an_sc = next(it)

        @pl.when(pl.program_id(1) == 0)
        def _norm():
            an_sc[...] = _rms(a_ref[...], g_ref[...]).astype(BF16)

        a = an_sc[...]
    else:
        a = a_ref[...]
    y = jnp.dot(a, w_ref[...], preferred_element_type=F32)
    if out_scale is not None:
        factor, n_tiles = out_scale
        y = y * jnp.where(pl.program_id(1) < n_tiles, factor, 1.0)
    rope = lambda t: t * cos_ref[...] + pltpu.roll(t, roll_shift, 1) * sin_ref[...]
    if has_add:
        chunk = add_ref[...].astype(F32)
        if rope_add:
            chunk = rope(chunk)
    if has_rope or has_add or head_major:
        for h in range(y.shape[1] // LANES):
            yh = y[:, h * LANES:(h + 1) * LANES]
            if has_rope and not rope_add:
                yh = rope(yh)
            if has_add:
                yh = yh + chunk
            if head_major:
                o_ref[h] = yh.astype(o_ref.dtype)
            else:
                o_ref[:, h * LANES:(h + 1) * LANES] = yh.astype(o_ref.dtype)
    else:
        o_ref[...] = y.astype(o_ref.dtype)


def projection(a, w, *, tm, tn, out_dtype, a_cols=None, gain=None, rope=None, rope_add=False,
               out_scale=None, add=None, head_major=None, name="projection"):
    n = a.shape[0]
    k, nc = w.shape
    a_w, a_blk = a_cols if a_cols is not None else (a.shape[1], 0)
    assert a_w == k and n % tm == 0 and nc % tn == 0
    grid = (n // tm, nc // tn)
    in_specs = [pl.BlockSpec((tm, k), lambda i, j: (i, a_blk))]
    args = [a]
    if gain is not None:
        in_specs.append(pl.BlockSpec((1, k), lambda i, j: (0, 0)))
        args.append(gain.reshape(1, k))
    in_specs.append(pl.BlockSpec((k, tn), lambda i, j: (0, j)))
    args.append(w)
    roll_shift = 0
    if rope is not None:
        cos, sin, roll_shift = rope
        s_tiles = cos.shape[0] // tm
        for t in (cos, sin):
            in_specs.append(pl.BlockSpec((tm, LANES), lambda i, j: (i % s_tiles, 0)))
            args.append(t)
    if add is not None:
        add_arr, add_blk = add
        in_specs.append(pl.BlockSpec((tm, LANES), lambda i, j: (i, add_blk)))
        args.append(add_arr)
    if head_major is not None:
        b, s = head_major
        s_t = s // tm
        hpt = tn // LANES
        out_shape = jax.ShapeDtypeStruct((b, nc // LANES, s, LANES), out_dtype)
        out_spec = pl.BlockSpec((None, hpt, tm, LANES), lambda i, j: (i // s_t, j, i % s_t, 0))
    else:
        out_shape = jax.ShapeDtypeStruct((n, nc), out_dtype)
        out_spec = pl.BlockSpec((tm, tn), lambda i, j: (i, j))
    kern = functools.partial(
        _proj_kernel, has_gain=gain is not None, has_rope=rope is not None, roll_shift=roll_shift,
        rope_add=rope_add, out_scale=out_scale, has_add=add is not None,
        head_major=head_major is not None)
    return pl.pallas_call(
        kern, out_shape=out_shape, grid=grid, in_specs=in_specs, out_specs=out_spec,
        scratch_shapes=[pltpu.VMEM((tm, k), BF16)] if gain is not None else [],
        compiler_params=_cparams("parallel", "arbitrary"), name=name,
    )(*args)


def _compress_kernel(r_ref, pa_ref, pb_ref, w1a_ref, w1b_ref, w2_ref, o_ref):
    r = r_ref[...].astype(F32)
    a = jnp.dot((r + pa_ref[...]).astype(BF16), w1a_ref[...], preferred_element_type=F32)
    b = jnp.dot((r + pb_ref[...]).astype(BF16), w1b_ref[...], preferred_element_type=F32)
    nr = a.shape[0]
    hid = a + pltpu.roll(b, nr - 1, 0)
    hid = jax.nn.gelu(hid)
    o_ref[...] = jnp.dot(hid.astype(BF16), w2_ref[...], preferred_element_type=F32).astype(o_ref.dtype)


def compress(r, head0, pos_a, pos_b, w1a, w1b, w2):
    b, _, nr, kk = r.shape
    g = NSA_KV_HEADS
    full = lambda shape: pl.BlockSpec(shape, lambda bi, gi: (0,) * len(shape))
    return pl.pallas_call(
        _compress_kernel,
        out_shape=jax.ShapeDtypeStruct((b, g, nr, LANES), BF16),
        grid=(b, g),
        in_specs=[pl.BlockSpec((None, None, nr, kk), lambda bi, gi: (bi, head0 + gi, 0, 0)),
                  full((1, kk)), full((1, kk)), full((kk, LANES)), full((kk, LANES)),
                  full((LANES, LANES))],
        out_specs=pl.BlockSpec((None, None, nr, LANES), lambda bi, gi: (bi, gi, 0, 0)),
        compiler_params=_cparams("parallel", "parallel"),
        name="nsa_compress",
    )(r, pos_a, pos_b, w1a, w1b, w2)


def _dot_nt(a, b):
    return lax.dot_general(a, b, (((1,), (1,)), ((), ())), preferred_element_type=F32)


def _split_dot(x, w):
    hi = x.astype(BF16)
    lo = (x - hi.astype(F32)).astype(BF16)
    return (jnp.dot(hi, w, preferred_element_type=F32) + jnp.dot(lo, w, preferred_element_type=F32))


def _nsa_cmp_kernel(q_ref, kcc_ref, vcc_ref, msel_ref, glog_ref, e_ref, oc_ref, sel_ref, s_sc, p_sc, imp_sc,
                    *, tq, rc, topk):
    s0 = pl.program_id(2) * tq
    nr = kcc_ref.shape[0]

    def attend(ncol):
        kcc = kcc_ref[0:ncol, :]
        for h in range(NSA_HPG):
            s_sc[h, :, 0:ncol] = _dot_nt(q_ref[h], kcc)
        lane_v = lax.broadcasted_iota(jnp.int32, (ncol, LANES), 1)
        vcc = vcc_ref[0:ncol, :]
        vm = jnp.concatenate([jnp.where(lane_v == HALF, jnp.ones_like(vcc), vcc), msel_ref[0:ncol, :]],
                             axis=1)
        dmat = (lax.broadcasted_iota(jnp.int32, (rc, ncol), 0)
                - lax.broadcasted_iota(jnp.int32, (rc, ncol), 1) * CMP_STRIDE)
        visible = lambda c: dmat + (s0 + c * rc - (CMP_BLOCK - 1)) >= 0
        outs = []
        imp = jnp.zeros((tq, LANES), F32)
        for h in range(NSA_HPG):
            row_max = []
            for c in range(tq // rc):
                s = jnp.where(visible(c), s_sc[h, pl.ds(c * rc, rc), 0:ncol], NEG)
                row_max.append(jnp.max(s, axis=-1, keepdims=True))
            for c in range(tq // rc):
                rows = pl.ds(c * rc, rc)
                p = jnp.exp2(s_sc[h, rows, 0:ncol] - row_max[c])
                p_sc[h, rows, 0:ncol] = jnp.where(visible(c), p, 0.0).astype(BF16)
            a = jnp.dot(p_sc[h, :, 0:ncol], vm, preferred_element_type=F32)
            denom = jnp.maximum(a[:, HALF:HALF + 1], 1e-30)
            outs.append(a[:, :LANES] / denom)
            imp = imp + a[:, LANES:] / denom
        imp_sc[...] = imp
        glog = glog_ref[...]
        lane = lax.broadcasted_iota(jnp.int32, (tq, LANES), 1)
        for pr in range(NSA_HPG // 2):
            gate = jax.nn.sigmoid(_split_dot(glog, e_ref[pr]))
            pair = jnp.where(lane < HALF, outs[2 * pr], pltpu.roll(outs[2 * pr + 1], HALF, 1))
            oc_ref[:, pr * LANES:(pr + 1) * LANES] = (gate * pair).astype(oc_ref.dtype)

    ntiles = (s0 + (tq - CMP_BLOCK)) // (CMP_STRIDE * LANES) + 1
    for k in range(1, nr // LANES + 1):
        @pl.when(ntiles == k)
        def _width(k=k):
            attend(k * LANES)

    blk = lax.broadcasted_iota(jnp.int32, (tq, LANES), 1)
    cur = (s0 + lax.broadcasted_iota(jnp.int32, (tq, LANES), 0)) // SEL_BLOCK
    valid = blk <= cur
    forced = (blk == 0) | (blk == cur) | (blk == cur - 1)
    score = jnp.where(valid, jnp.where(forced, -jnp.inf, imp_sc[...]), -1.0)
    score_t = score.T
    nblk = (s0 + (tq - 1)) // SEL_BLOCK + 1
    for nrows in range(TOPK_ROWS, LANES + 1, TOPK_ROWS):
        @pl.when((nblk + (TOPK_ROWS - 1)) // TOPK_ROWS == nrows // TOPK_ROWS)
        def _select(nrows=nrows):
            tw = min(tq, TOPK_QUERIES)
            rowid = lax.broadcasted_iota(jnp.int32, (nrows, tw), 0).astype(F32)

            def pick_one(_, sc):
                cm = jnp.max(sc, axis=0, keepdims=True)
                first = jnp.min(jnp.where(sc == cm, rowid, float(LANES)), axis=0, keepdims=True)
                return jnp.where(rowid == first, -jnp.inf, sc)

            for w in range(tq // tw):
                cols = slice(w * tw, (w + 1) * tw)
                picked = lax.fori_loop(0, topk - N_FORCED, pick_one, score_t[0:nrows, cols]) == -jnp.inf
                bias = jnp.where(picked, 0.0, NEG)
                if nrows < LANES:
                    bias = jnp.concatenate([bias, jnp.full((LANES - nrows, tw), NEG, F32)], axis=0)
                sel_ref[cols, :] = bias.T.astype(sel_ref.dtype)


def nsa_compressed(qk, kcc, vcc, msel, misc, glog_blk, expand, *, batch, seq, tq, rc):
    g = NSA_KV_HEADS
    nq = seq // tq
    nr = kcc.shape[2]
    topk = min(SEL_TOPK, seq // SEL_BLOCK)
    assert topk > N_FORCED and nr % LANES == 0
    kern = functools.partial(_nsa_cmp_kernel, tq=tq, rc=rc, topk=topk)
    return pl.pallas_call(
        kern,
        out_shape=(jax.ShapeDtypeStruct((batch * seq, NSA_HEADS * NSA_DV), BF16),
                   jax.ShapeDtypeStruct((batch, g, seq, LANES), BF16)),
        grid=(batch, g, nq),
        in_specs=[
            pl.BlockSpec((None, NSA_HPG, tq, LANES), lambda b, gi, qi: (b, gi, qi, 0)),
            pl.BlockSpec((None, None, nr, LANES), lambda b, gi, qi: (b, gi, 0, 0)),
            pl.BlockSpec((None, None, nr, LANES), lambda b, gi, qi: (b, gi, 0, 0)),
            pl.BlockSpec((nr, LANES), lambda b, gi, qi: (0, 0)),
            pl.BlockSpec((tq, LANES), lambda b, gi, qi: (b * nq + qi, glog_blk)),
            pl.BlockSpec((None, 2, LANES, LANES), lambda b, gi, qi: (0, gi, 0, 0)),
        ],
        out_specs=(pl.BlockSpec((tq, 2 * LANES), lambda b, gi, qi: (b * nq + qi, gi)),
                   pl.BlockSpec((None, None, tq, LANES), lambda b, gi, qi: (b, gi, qi, 0))),
        scratch_shapes=[pltpu.VMEM((NSA_HPG, tq, nr), F32), pltpu.VMEM((NSA_HPG, tq, nr), BF16),
                        pltpu.VMEM((tq, LANES), F32)],
        compiler_params=_cparams("parallel", "parallel", "parallel"),
        name="nsa_compressed_select",
    )(qk, kcc, vcc, msel, misc, expand)


def _flash_steps(mode, nq, tq, tk):
    r = tq // tk
    qi_l, kt_l, first_l, last_l = [], [], [], []
    for qi in range(nq):
        hi = qi * r + r - 1
        lo = max(0, qi * r - (-(-(WINDOW - 1) // tk))) if mode == "window" else 0
        for kt in range(lo, hi + 1):
            qi_l.append(qi)
            kt_l.append(kt)
            first_l.append(int(kt == lo))
            last_l.append(int(kt == hi))
    return tuple(jnp.asarray(np.asarray(a, np.int32)) for a in (qi_l, kt_l, first_l, last_l))


def _flash_kernel(*refs, mode, tq, tk, rc, kv_shared, gated):
    it = iter(refs)
    qi_ref, kt_ref, first_ref, last_ref = next(it), next(it), next(it), next(it)
    q_ref, k_ref, v_ref = next(it), next(it), next(it)
    selb_ref = next(it) if mode == "select" else None
    oh_ref = next(it) if mode == "select" else None
    glog_ref = next(it) if gated else None
    e_ref = next(it) if gated else None
    o_ref = next(it)
    m_sc, acc_sc, s_sc, p_sc = next(it), next(it), next(it), next(it)
    qa_sc = next(it) if mode == "select" else None

    st = pl.program_id(2)
    s0 = qi_ref[st] * tq
    k0 = kt_ref[st] * tk

    @pl.when(first_ref[st] == 1)
    def _init():
        m_sc[...] = jnp.full(m_sc.shape, NEG, F32)
        acc_sc[...] = jnp.zeros(acc_sc.shape, F32)
        if mode == "select":
            for hh in range(FLASH_HEADS):
                qa_sc[hh] = jnp.concatenate([q_ref[hh], selb_ref[...]], axis=1)

    def chunk_cols(off, r0):
        if off is None:
            return 0, tk, True
        r1 = r0 + rc
        hi = min(tk, r1 - off)
        lo = max(0, r0 - off - WINDOW + 1) if mode == "window" else 0
        lo, hi = lo // LANES * LANES, -(-hi // LANES) * LANES
        clear = tk - 1 <= r0 - off and (mode != "window" or r1 - 1 - off < WINDOW)
        return lo, hi, clear

    def tile(off):
        nh = FLASH_HEADS
        r_lo = max(0, off) if off is not None else 0
        nrow = tq - r_lo
        if kv_shared:
            if mode == "select":
                qs = qa_sc[:, r_lo:, :].reshape(nh * nrow, 2 * LANES)
                ks = jnp.concatenate([k_ref[0], oh_ref[...]], axis=1)
            else:
                qs, ks = q_ref[:, r_lo:, :].reshape(nh * nrow, LANES), k_ref[0]
            s_sc[:, r_lo:, :] = _dot_nt(qs, ks).reshape(nh, nrow, tk)
        else:
            for hh in range(nh):
                s_sc[hh, r_lo:, :] = _dot_nt(q_ref[hh, r_lo:, :], k_ref[hh])
        lane_v = lax.broadcasted_iota(jnp.int32, (tk, LANES), 1)
        if off is not None:
            dmat = (lax.broadcasted_iota(jnp.int32, (rc, tk), 0)
                    - lax.broadcasted_iota(jnp.int32, (rc, tk), 1))
        for hh in range(FLASH_HEADS):
            v = v_ref[0 if kv_shared else hh]
            v_aug = jnp.where(lane_v == HALF, jnp.ones_like(v), v)

            for c in range(r_lo // rc, tq // rc):
                r0 = c * rc
                rows = pl.ds(r0, rc)
                lo, hi, clear = chunk_cols(off, r0)
                if lo > 0:
                    p_sc[hh, rows, 0:lo] = jnp.zeros((rc, lo), BF16)
                if hi < tk:
                    p_sc[hh, rows, max(hi, 0):tk] = jnp.zeros((rc, tk - max(hi, 0)), BF16)
                if hi <= lo:
                    continue
                s = s_sc[hh, rows, lo:hi]
                if not clear:
                    d = dmat[:, lo:hi] + (r0 - off)
                    msk = (d >= 0) & (d < WINDOW) if mode == "window" else d >= 0
                    s = jnp.where(msk, s, NEG)
                m_old = m_sc[hh, rows, :]
                m_new = jnp.maximum(m_old, jnp.max(s, axis=-1, keepdims=True))
                m_sc[hh, rows, :] = m_new
                acc_sc[hh, rows, :] = acc_sc[hh, rows, :] * jnp.exp2(m_old - m_new)
                p = jnp.exp2(s_sc[hh, rows, lo:hi] - jnp.concatenate([m_new] * ((hi - lo) // LANES), axis=1))
                if not clear:
                    p = jnp.where(msk, p, 0.0)
                p_sc[hh, rows, lo:hi] = p.astype(BF16)
            if not kv_shared:
                acc_sc[hh, r_lo:, :] += jnp.dot(p_sc[hh, r_lo:, :], v_aug, preferred_element_type=F32)
        if kv_shared:
            pv = jnp.dot(p_sc[:, r_lo:, :].reshape(nh * nrow, tk), v_aug, preferred_element_type=F32)
            acc_sc[:, r_lo:, :] += pv.reshape(nh, nrow, LANES)

    if mode == "window":
        offsets = [(j - (-(-(WINDOW - 1) // tk))) * tk for j in range(-(-(WINDOW - 1) // tk) + tq // tk)]
    else:
        offsets = [j * tk for j in range(tq // tk)]

        @pl.when(k0 + tk - 1 <= s0)
        def _interior():
            tile(None)

    for off in offsets:
        @pl.when(k0 - s0 == off)
        def _partial(off=off):
            tile(off)

    @pl.when(last_ref[st] == 1)
    def _finish():
        lane = lax.broadcasted_iota(jnp.int32, (tq, LANES), 1)
        if gated:
            glog = glog_ref[...]
        for pr in range(FLASH_HEADS // 2):
            outs = []
            for hh in (2 * pr, 2 * pr + 1):
                a = acc_sc[hh]
                outs.append(a / jnp.maximum(a[:, HALF:HALF + 1], 1e-30))
            out = jnp.where(lane < HALF, outs[0], pltpu.roll(outs[1], HALF, 1))
            if gated:
                out = jax.nn.sigmoid(_split_dot(glog, e_ref[pr])) * out
            o_ref[:, pr * LANES:(pr + 1) * LANES] = out.astype(o_ref.dtype)


def flash_heads(q, k, v, *, mode, batch, seq, tq, tk, rc, q_head0, k_head0, v_head0,
                kv_shared, out_dtype, selb=None, onehot=None, gate=None, name="flash"):
    nh = FLASH_HEADS
    ngroups = NSA_HEADS // nh
    nq = seq // tq
    assert tq % tk == 0 and tq % rc == 0
    tabs = _flash_steps(mode, nq, tq, tk)
    nsteps = tabs[0].shape[0]
    if kv_shared:
        kv_spec = lambda h0: pl.BlockSpec(
            (None, 1, tk, LANES), lambda b, g, st, qi, kt, fi, la: (b, h0 + g, kt[st], 0))
    else:
        kv_spec = lambda h0: pl.BlockSpec(
            (None, nh, tk, LANES), lambda b, g, st, qi, kt, fi, la: (b, h0 // nh + g, kt[st], 0))
    in_specs = [pl.BlockSpec((None, nh, tq, LANES),
                             lambda b, g, st, qi, kt, fi, la: (b, q_head0 // nh + g, qi[st], 0)),
                kv_spec(k_head0), kv_spec(v_head0)]
    args = [q, k, v]
    scratch = [pltpu.VMEM((nh, tq, LANES), F32), pltpu.VMEM((nh, tq, LANES), F32),
               pltpu.VMEM((nh, tq, tk), F32), pltpu.VMEM((nh, tq, tk), BF16)]
    if mode == "select":
        in_specs += [pl.BlockSpec((None, None, tq, LANES),
                                  lambda b, g, st, qi, kt, fi, la: (b, g, qi[st], 0)),
                     pl.BlockSpec((tk, LANES), lambda b, g, st, qi, kt, fi, la: (kt[st], 0))]
        args += [selb, onehot]
        scratch.append(pltpu.VMEM((nh, tq, 2 * LANES), BF16))
    if gate is not None:
        misc, glog_blk, expand, branch = gate
        in_specs += [pl.BlockSpec((tq, LANES),
                                  lambda b, g, st, qi, kt, fi, la: (b * nq + qi[st], glog_blk)),
                     pl.BlockSpec((None, nh // 2, LANES, LANES),
                                  lambda b, g, st, qi, kt, fi, la: (branch, g, 0, 0))]
        args += [misc, expand]
    kern = functools.partial(_flash_kernel, mode=mode, tq=tq, tk=tk, rc=rc,
                             kv_shared=kv_shared, gated=gate is not None)
    ow = nh * HALF
    grid_spec = pltpu.PrefetchScalarGridSpec(
        num_scalar_prefetch=4, grid=(batch, ngroups, nsteps), in_specs=in_specs,
        out_specs=pl.BlockSpec((tq, ow), lambda b, g, st, qi, kt, fi, la: (b * nq + qi[st], g)),
        scratch_shapes=scratch)
    return pl.pallas_call(
        kern,
        out_shape=jax.ShapeDtypeStruct((batch * seq, ngroups * ow), out_dtype),
        grid_spec=grid_spec,
        compiler_params=_cparams("parallel", "parallel", "arbitrary"),
        name=name,
    )(*tabs, *args)


def _mix_kernel(oc_ref, os_ref, ow_ref, ob_ref, xn_ref, wga_ref, wgb_ref, wa_ref, wb_ref, o_ref):
    oa = (oc_ref[...].astype(F32) + os_ref[...].astype(F32) + ow_ref[...].astype(F32)).astype(BF16)
    xn = xn_ref[...]
    ga = jax.nn.sigmoid(jnp.dot(xn, wga_ref[...], preferred_element_type=F32))
    ya = ga * jnp.dot(oa, wa_ref[...], preferred_element_type=F32)
    gb = jax.nn.sigmoid(jnp.dot(xn, wgb_ref[...], preferred_element_type=F32))
    yb = gb * jnp.dot(ob_ref[...], wb_ref[...], preferred_element_type=F32)
    o_ref[...] = (ya + yb).astype(o_ref.dtype)


def gated_mix(oc, osel, ow, ob, xn, wga, wgb, wa, wb, *, tm, tn):
    n, ka = oc.shape
    d = wa.shape[1]
    row = lambda w: pl.BlockSpec((tm, w), lambda i, j: (i, 0))
    col = lambda k: pl.BlockSpec((k, tn), lambda i, j: (0, j))
    return pl.pallas_call(
        _mix_kernel,
        out_shape=jax.ShapeDtypeStruct((n, d), BF16),
        grid=(n // tm, d // tn),
        in_specs=[row(ka), row(ka), row(ka), row(ob.shape[1]), row(xn.shape[1]),
                  col(wga.shape[0]), col(wgb.shape[0]), col(ka), col(wb.shape[0])],
        out_specs=pl.BlockSpec((tm, tn), lambda i, j: (i, j)),
        compiler_params=_cparams("parallel", "arbitrary"),
        name="gated_mix",
    )(oc, osel, ow, ob, xn, wga, wgb, wa, wb)


def _xattn_kernel(x_ref, mix_ref, wout_ref, g_ref, wq_ref, kv_ref, wo_ref, o_ref):
    h = x_ref[...] + jnp.dot(mix_ref[...], wout_ref[...], preferred_element_type=F32)
    hn = _rms(h, g_ref[...]).astype(BF16)
    q = jnp.dot(hn, wq_ref[...], preferred_element_type=F32).astype(BF16)
    kv = kv_ref[...]
    outs = []
    for hd in range(XA_HEADS):
        qh = q[:, hd * XA_DIM:(hd + 1) * XA_DIM]
        kh = kv[:, hd * XA_DIM:(hd + 1) * XA_DIM]
        vh = kv[:, (XA_HEADS + hd) * XA_DIM:(XA_HEADS + hd + 1) * XA_DIM]
        s = _dot_nt(qh, kh) * (XA_DIM ** -0.5)
        e = jnp.exp(s - jnp.max(s, axis=-1, keepdims=True))
        p = e / jnp.sum(e, axis=-1, keepdims=True)
        outs.append(jnp.dot(p.astype(BF16), vh, preferred_element_type=F32))
    o = jnp.concatenate(outs, axis=1).astype(BF16)
    o_ref[...] = h + jnp.dot(o, wo_ref[...], preferred_element_type=F32)


def out_proj_xattn(x, mixed, wout, g, wq, kv, wo, *, seq, tm):
    n, d = x.shape
    s_t = seq // tm
    dq = wq.shape[1]
    resident = lambda shape: pl.BlockSpec(shape, lambda i: (0, 0), pipeline_mode=pl.Buffered(1))
    return pl.pallas_call(
        _xattn_kernel,
        out_shape=jax.ShapeDtypeStruct((n, d), F32),
        grid=(n // tm,),
        in_specs=[pl.BlockSpec((tm, d), lambda i: (i, 0)),
                  pl.BlockSpec((tm, d), lambda i: (i, 0)),
                  resident((d, d)),
                  pl.BlockSpec((1, d), lambda i: (0, 0)),
                  resident((d, dq)),
                  pl.BlockSpec((MEM_LEN, 2 * dq), lambda i: (i // s_t, 0)),
                  resident((dq, d))],
        out_specs=pl.BlockSpec((tm, d), lambda i: (i, 0)),
        compiler_params=_cparams("parallel"),
        name="out_proj_xattn",
    )(x, mixed, wout, g.reshape(1, d), wq, kv, wo)


def _mlp_kernel(h_ref, g_ref, w1_ref, w2_ref, gf_ref, o_ref, hn_sc, acc_sc, *, nf):
    f = pl.program_id(1)

    @pl.when(f == 0)
    def _init():
        hn_sc[...] = _rms(h_ref[...], g_ref[...]).astype(BF16)
        acc_sc[...] = jnp.zeros(acc_sc.shape, F32)

    u = jnp.dot(hn_sc[...], w1_ref[...], preferred_element_type=F32)
    u = jnp.square(jnp.maximum(u, 0.0))
    acc_sc[...] += jnp.dot(u.astype(BF16), w2_ref[...], preferred_element_type=F32)

    @pl.when(f == nf - 1)
    def _finish():
        o_ref[...] = _rms(h_ref[...] + acc_sc[...], gf_ref[...])


def mlp_final(h, g, w1, w2, gf, *, tm, tf):
    n, d = h.shape
    dff = w1.shape[1]
    nf = dff // tf
    return pl.pallas_call(
        functools.partial(_mlp_kernel, nf=nf),
        out_shape=jax.ShapeDtypeStruct((n, d), F32),
        grid=(n // tm, nf),
        in_specs=[pl.BlockSpec((tm, d), lambda i, f: (i, 0)),
                  pl.BlockSpec((1, d), lambda i, f: (0, 0)),
                  pl.BlockSpec((d, tf), lambda i, f: (0, f)),
                  pl.BlockSpec((tf, d), lambda i, f: (f, 0)),
                  pl.BlockSpec((1, d), lambda i, f: (0, 0))],
        out_specs=pl.BlockSpec((tm, d), lambda i, f: (i, 0)),
        scratch_shapes=[pltpu.VMEM((tm, d), BF16), pltpu.VMEM((tm, d), F32)],
        compiler_params=_cparams("parallel", "arbitrary"),
        name="mlp_final",
    )(h, g.reshape(1, d), w1, w2, gf.reshape(1, d))


def _rot_partner(w, half):
    return jnp.concatenate([-w[..., half:], w[..., :half]], axis=-1)


def _pad_heads_rope(w, heads, hd, rot0, rot_dim):
    k = w.shape[0]
    w = w.reshape(k, heads, hd)
    partner = _rot_partner(w[:, :, rot0:rot0 + rot_dim], rot_dim // 2)
    pad = jnp.zeros((k, heads, LANES - hd - rot_dim), w.dtype)
    return jnp.concatenate([w, partner, pad], axis=-1).reshape(k, heads * LANES)


def _rope_lane_tables(seq, rot0, rot_dim, hd):
    inv = 1.0 / (ROPE_THETA ** (jnp.arange(0, rot_dim, 2, dtype=F32) / rot_dim))
    ang = jnp.arange(seq, dtype=F32)[:, None] * inv[None, :]
    cos = jnp.concatenate([jnp.cos(ang), jnp.cos(ang)], axis=1)
    sin = jnp.concatenate([jnp.sin(ang), jnp.sin(ang)], axis=1)
    cos_t = jnp.concatenate([jnp.ones((seq, rot0), F32), cos,
                             jnp.ones((seq, hd - rot0 - rot_dim), F32),
                             jnp.zeros((seq, LANES - hd), F32)], axis=1)
    sin_t = jnp.concatenate([jnp.zeros((seq, rot0), F32), sin,
                             jnp.zeros((seq, LANES - rot0 - rot_dim), F32)], axis=1)
    return cos_t, sin_t


def _pad_lanes(w, groups, width):
    k = w.shape[0]
    w = w.reshape(k, groups, width)
    return jnp.pad(w, ((0, 0), (0, 0), (0, LANES - width))).reshape(k, groups * LANES)


def _compress_params(pos, w1, w2, d):
    half = CMP_BLOCK // 2
    pos_p = jnp.pad(pos, ((0, 0), (0, LANES - d)))
    pos_a = pos_p[:half].reshape(1, half * LANES)
    pos_b = pos_p[half:].reshape(1, half * LANES)
    w1p = jnp.pad(w1.reshape(CMP_BLOCK, d, d), ((0, 0), (0, LANES - d), (0, LANES - d)))
    w1a = w1p[:half].reshape(half * LANES, LANES).astype(BF16)
    w1b = w1p[half:].reshape(half * LANES, LANES).astype(BF16)
    w2p = jnp.pad(w2, ((0, LANES - d), (0, LANES - d)))
    return pos_a, pos_b, w1a, w1b, w2p.astype(BF16)


def _cmp_to_sel(nr, nsb):
    cs = np.arange(nr) * CMP_STRIDE
    ce = cs + CMP_BLOCK
    ss = np.arange(LANES) * SEL_BLOCK
    se = ss + SEL_BLOCK
    ov = np.clip(np.minimum(ce[:, None], se[None, :]) - np.maximum(cs[:, None], ss[None, :]), 0, None)
    ov = ov.astype(np.float32) / np.float32(CMP_BLOCK)
    ov[:, nsb:] = 0.0
    ov[nr - 1:, :] = 0.0
    return jnp.asarray(ov, BF16)


def _gate_expand():
    e = np.zeros((3, NSA_HEADS // 2, LANES, LANES), np.float32)
    for br in range(3):
        for hp in range(NSA_HEADS // 2):
            for hh in range(2):
                e[br, hp, 3 * (2 * hp + hh) + br, hh * HALF:(hh + 1) * HALF] = 1.0
    return jnp.asarray(e, BF16)


def _key_block_onehot(seq):
    e = (np.arange(seq)[:, None] // SEL_BLOCK) == np.arange(LANES)[None, :]
    return jnp.asarray(e.astype(np.float32), BF16)


def kernel(x, mem, g_mix, w_in, cmp_pos_k, cmp_w1_k, cmp_w2_k, cmp_pos_v, cmp_w1_v, cmp_w2_v,
           mla_g_q, mla_w_uq, mla_g_kv, mla_w_uk, mla_w_uv, w_o_nsa, w_o_mla, w_out,
           g_xattn, g_mem, xa_wq, xa_wkv, xa_wo, g_mlp, w_ff1, w_ff2, g_final):
    b, s, d = x.shape
    assert d == D_MODEL and s % (CMP_STRIDE * 8) == 0 and s // SEL_BLOCK <= LANES
    assert g_mix.shape[0] == 1
    n = b * s
    T = _tiles(s)
    tm, tn, tq, tk, rc = T["tm"], T["tn"], T["tq"], T["tk"], T["rc"]
    G = NSA_KV_HEADS
    bounds = [int(v) for v in np.cumsum(SPLITS)[:-1]]

    (w_qa, w_kc, w_vc, w_ks, w_vs, w_kw, w_vw, w_gn, w_cq, w_ckv, w_kr,
     w_ga, w_gb) = jnp.split(w_in[0], bounds, axis=1)
    nsa_rope = lambda w, heads: _pad_heads_rope(w, heads, NSA_DK, 0, NSA_ROT)
    w_rope = jnp.concatenate([nsa_rope(w_qa, NSA_HEADS), nsa_rope(w_kc, G), nsa_rope(w_ks, G),
                              nsa_rope(w_kw, G)], axis=1).astype(BF16)
    w_krp = jnp.concatenate([jnp.zeros((d, MLA_NOPE), F32), w_kr, _rot_partner(w_kr, MLA_ROPE // 2)], axis=1)
    w_vsw = jnp.concatenate([_pad_lanes(w_vs, G, NSA_DV), _pad_lanes(w_vw, G, NSA_DV),
                             _pad_lanes(w_vc, G, NSA_DV)], axis=1).astype(BF16)
    w_misc = jnp.concatenate([w_cq, w_ckv, jnp.pad(w_gn, ((0, 0), (0, LANES - w_gn.shape[1]))), w_krp],
                             axis=1).astype(BF16)
    glog_blk = (MLA_Q_RANK + MLA_KV_RANK) // LANES
    krp_blk = glog_blk + 1
    cos_a, sin_a = _rope_lane_tables(s, 0, NSA_ROT, NSA_DK)
    cos_b, sin_b = _rope_lane_tables(s, MLA_NOPE, MLA_ROPE, MLA_NOPE + MLA_ROPE)
    shift_a = LANES - NSA_DK
    shift_b = LANES - (NSA_DK - MLA_NOPE)
    w_uq = _pad_heads_rope(mla_w_uq[0], MLA_HEADS, MLA_NOPE + MLA_ROPE, MLA_NOPE, MLA_ROPE).astype(BF16)
    w_uk = _pad_lanes(mla_w_uk[0], MLA_HEADS, MLA_NOPE).astype(BF16)
    w_uv = _pad_lanes(mla_w_uv[0], MLA_HEADS, MLA_DV).astype(BF16)

    x2 = x.reshape(n, d)
    xn = rmsnorm_rows(x2, g_mix[0], tm)
    log2e = float(np.log2(np.e))
    tmb = T["tm_big"]
    qk = projection(xn, w_rope, tm=tmb, tn=tn, out_dtype=BF16, rope=(cos_a, sin_a, shift_a),
                    out_scale=(NSA_DK ** -0.5 * log2e, NSA_HEADS * LANES // tn),
                    head_major=(b, s), name="proj_qk_rope")
    vsw = projection(xn, w_vsw, tm=tmb, tn=tn, out_dtype=BF16, head_major=(b, s), name="proj_v")
    vcp = vsw[:, 2 * G:3 * G]
    misc = projection(xn, w_misc, tm=tm, tn=w_misc.shape[1], out_dtype=F32, name="proj_misc")

    nr = s // CMP_STRIDE
    pk = _compress_params(cmp_pos_k[0], cmp_w1_k[0], cmp_w2_k[0], NSA_DK)
    pv = _compress_params(cmp_pos_v[0], cmp_w1_v[0], cmp_w2_v[0], NSA_DV)
    kc_rows = qk[:, QK_KC0:QK_KC0 + G].reshape(b, G, nr, CMP_STRIDE * LANES)
    kcc = compress(kc_rows, 0, *pk)
    vcc = compress(vcp.reshape(b, G, nr, CMP_STRIDE * LANES), 0, *pv)
    expand = _gate_expand()
    o_c, selb = nsa_compressed(qk, kcc, vcc, _cmp_to_sel(nr, s // SEL_BLOCK), misc, glog_blk, expand,
                               batch=b, seq=s, tq=T["tq_cmp"], rc=rc)
    o_s = flash_heads(qk, qk, vsw, mode="select", batch=b, seq=s, tq=tq, tk=tk, rc=rc,
                      q_head0=QK_Q0, k_head0=QK_KS0, v_head0=0, kv_shared=True, out_dtype=BF16,
                      selb=selb, onehot=_key_block_onehot(s), gate=(misc, glog_blk, expand, 1),
                      name="nsa_selected")
    o_w = flash_heads(qk, qk, vsw, mode="window", batch=b, seq=s, tq=T["tq_win"], tk=tk, rc=rc,
                      q_head0=QK_Q0, k_head0=QK_KW0, v_head0=G, kv_shared=True, out_dtype=BF16,
                      gate=(misc, glog_blk, expand, 2), name="nsa_window")

    qm = projection(misc, w_uq, tm=tmb, tn=tn, out_dtype=BF16, a_cols=(MLA_Q_RANK, 0), gain=mla_g_q[0],
                    rope=(cos_b, sin_b, shift_b), head_major=(b, s), name="mla_q",
                    out_scale=((MLA_NOPE + MLA_ROPE) ** -0.5 * log2e, MLA_HEADS * LANES // tn))
    km = projection(misc, w_uk, tm=tmb, tn=tn, out_dtype=BF16,
                    a_cols=(MLA_KV_RANK, MLA_Q_RANK // MLA_KV_RANK), gain=mla_g_kv[0],
                    add=(misc, krp_blk), rope=(cos_b, sin_b, shift_b), rope_add=True,
                    head_major=(b, s), name="mla_k")
    vm = projection(misc, w_uv, tm=tmb, tn=tn, out_dtype=BF16,
                    a_cols=(MLA_KV_RANK, MLA_Q_RANK // MLA_KV_RANK), gain=mla_g_kv[0],
                    head_major=(b, s), name="mla_v")
    o_b = flash_heads(qm, km, vm, mode="causal", batch=b, seq=s, tq=tq, tk=tk, rc=rc,
                      q_head0=0, k_head0=0, v_head0=0,
                      kv_shared=False, out_dtype=BF16, name="mla_attention")

    mixed = gated_mix(o_c, o_s, o_w, o_b, xn, w_ga.astype(BF16), w_gb.astype(BF16),
                      w_o_nsa[0].astype(BF16), w_o_mla[0].astype(BF16), tm=tm, tn=tn)

    kv_mem = projection(mem.reshape(b * MEM_LEN, d), xa_wkv[0].astype(BF16), tm=MEM_LEN, tn=tn,
                        out_dtype=BF16, gain=g_mem[0], name="xattn_kv")
    h2 = out_proj_xattn(x2, mixed, w_out[0].astype(BF16), g_xattn[0], xa_wq[0].astype(BF16), kv_mem,
                        xa_wo[0].astype(BF16), seq=s, tm=T["tm_x"])

    out = mlp_final(h2, g_mlp[0], w_ff1[0].astype(BF16), w_ff2[0].astype(BF16), g_final,
                    tm=T["tm_mlp"], tf=T["tf"])
    return out.reshape(b, s, d)
```

```python
import functools

import numpy as np
import jax
import jax.numpy as jnp
from jax import lax
from jax.experimental import pallas as pl
from jax.experimental.pallas import tpu as pltpu

F32 = jnp.float32
BF16 = jnp.bfloat16

D_MODEL = 2048
MEM_LEN = 256
ROPE_THETA = 500000.0
EPS = 1e-6
NEG = -1e30

NSA_HEADS = 16
NSA_KV_HEADS = 4
NSA_HPG = NSA_HEADS // NSA_KV_HEADS
NSA_DK = 96
NSA_DV = 64
NSA_ROT = NSA_DK // 4
CMP_BLOCK = 32
CMP_STRIDE = 16
SEL_BLOCK = 64
SEL_TOPK = 16
N_FORCED = 3
WINDOW = 512

MLA_HEADS = 16
MLA_NOPE = 64
MLA_ROPE = 32
MLA_DV = 64
MLA_Q_RANK = 512
MLA_KV_RANK = 256

XA_HEADS = 4
XA_DIM = 128
D_FF = 4 * D_MODEL

SPLITS = (NSA_HEADS * NSA_DK,
          NSA_KV_HEADS * NSA_DK, NSA_KV_HEADS * NSA_DV,
          NSA_KV_HEADS * NSA_DK, NSA_KV_HEADS * NSA_DV,
          NSA_KV_HEADS * NSA_DK, NSA_KV_HEADS * NSA_DV,
          NSA_HEADS * 3,
          MLA_Q_RANK, MLA_KV_RANK, MLA_ROPE,
          D_MODEL, D_MODEL)

LANES = 128
HALF = LANES // 2
VMEM_LIMIT = 56 * 1024 * 1024

QK_Q0 = 0
QK_KC0 = NSA_HEADS
QK_KS0 = NSA_HEADS + NSA_KV_HEADS
QK_KW0 = NSA_HEADS + 2 * NSA_KV_HEADS
QK_HEADS = NSA_HEADS + 3 * NSA_KV_HEADS
FLASH_HEADS = NSA_HPG
TOPK_ROWS = 32
TOPK_QUERIES = 256


def _cparams(*sem):
    return pltpu.CompilerParams(dimension_semantics=sem, vmem_limit_bytes=VMEM_LIMIT)


def _tiles(seq):
    return dict(
        tm=min(1024, seq),
        tm_big=min(2048, seq),
        tn=4 * LANES,
        tm_x=min(512, seq),
        tm_mlp=min(512, seq),
        tq=min(2048, seq),
        tq_win=min(512, seq),
        tk=min(512, seq),
        rc=32,
        tq_cmp=min(512, seq),
        tf=1024,
    )


def _rms(x, g):
    return x * lax.rsqrt(jnp.mean(x * x, axis=-1, keepdims=True) + EPS) * g


def _rmsnorm_kernel(x_ref, g_ref, o_ref):
    o_ref[...] = _rms(x_ref[...], g_ref[...]).astype(o_ref.dtype)


def rmsnorm_rows(x, g, tm):
    n, d = x.shape
    return pl.pallas_call(
        _rmsnorm_kernel,
        out_shape=jax.ShapeDtypeStruct((n, d), BF16),
        grid=(n // tm,),
        in_specs=[pl.BlockSpec((tm, d), lambda i: (i, 0)),
                  pl.BlockSpec((1, d), lambda i: (0, 0))],
        out_specs=pl.BlockSpec((tm, d), lambda i: (i, 0)),
        compiler_params=_cparams("parallel"),
        name="rmsnorm_rows",
    )(x, g.reshape(1, d))


def _proj_kernel(*refs, has_gain, has_rope, roll_shift, rope_add, out_scale, has_add, head_major):
    it = iter(refs)
    a_ref = next(it)
    g_ref = next(it) if has_gain else None
    w_ref = next(it)
    cos_ref = next(it) if has_rope else None
    sin_ref = next(it) if has_rope else None
    add_ref = next(it) if has_add else None
    o_ref = next(it)

    if has_gain:
        an_sc = next(it)

        @pl.when(pl.program_id(1) == 0)
        def _norm():
            an_sc[...] = _rms(a_ref[...], g_ref[...]).astype(BF16)

        a = an_sc[...]
    else:
        a = a_ref[...]
    y = jnp.dot(a, w_ref[...], preferred_element_type=F32)
    if out_scale is not None:
        factor, n_tiles = out_scale
        y = y * jnp.where(pl.program_id(1) < n_tiles, factor, 1.0)
    rope = lambda t: t * cos_ref[...] + pltpu.roll(t, roll_shift, 1) * sin_ref[...]
    if has_add:
        chunk = add_ref[...].astype(F32)
        if rope_add:
            chunk = rope(chunk)
    if has_rope or has_add or head_major:
        for h in range(y.shape[1] // LANES):
            yh = y[:, h * LANES:(h + 1) * LANES]
            if has_rope and not rope_add:
                yh = rope(yh)
            if has_add:
                yh = yh + chunk
            if head_major:
                o_ref[h] = yh.astype(o_ref.dtype)
            else:
                o_ref[:, h * LANES:(h + 1) * LANES] = yh.astype(o_ref.dtype)
    else:
        o_ref[...] = y.astype(o_ref.dtype)


def projection(a, w, *, tm, tn, out_dtype, a_cols=None, gain=None, rope=None, rope_add=False,
               out_scale=None, add=None, head_major=None, name="projection"):
    n = a.shape[0]
    k, nc = w.shape
    a_w, a_blk = a_cols if a_cols is not None else (a.shape[1], 0)
    assert a_w == k and n % tm == 0 and nc % tn == 0
    grid = (n // tm, nc // tn)
    in_specs = [pl.BlockSpec((tm, k), lambda i, j: (i, a_blk))]
    args = [a]
    if gain is not None:
        in_specs.append(pl.BlockSpec((1, k), lambda i, j: (0, 0)))
        args.append(gain.reshape(1, k))
    in_specs.append(pl.BlockSpec((k, tn), lambda i, j: (0, j)))
    args.append(w)
    roll_shift = 0
    if rope is not None:
        cos, sin, roll_shift = rope
        s_tiles = cos.shape[0] // tm
        for t in (cos, sin):
            in_specs.append(pl.BlockSpec((tm, LANES), lambda i, j: (i % s_tiles, 0)))
            args.append(t)
    if add is not None:
        add_arr, add_blk = add
        in_specs.append(pl.BlockSpec((tm, LANES), lambda i, j: (i, add_blk)))
        args.append(add_arr)
    if head_major is not None:
        b, s = head_major
        s_t = s // tm
        hpt = tn // LANES
        out_shape = jax.ShapeDtypeStruct((b, nc // LANES, s, LANES), out_dtype)
        out_spec = pl.BlockSpec((None, hpt, tm, LANES), lambda i, j: (i // s_t, j, i % s_t, 0))
    else:
        out_shape = jax.ShapeDtypeStruct((n, nc), out_dtype)
        out_spec = pl.BlockSpec((tm, tn), lambda i, j: (i, j))
    kern = functools.partial(
        _proj_kernel, has_gain=gain is not None, has_rope=rope is not None, roll_shift=roll_shift,
        rope_add=rope_add, out_scale=out_scale, has_add=add is not None,
        head_major=head_major is not None)
    return pl.pallas_call(
        kern, out_shape=out_shape, grid=grid, in_specs=in_specs, out_specs=out_spec,
        scratch_shapes=[pltpu.VMEM((tm, k), BF16)] if gain is not None else [],
        compiler_params=_cparams("parallel", "arbitrary"), name=name,
    )(*args)


def _compress_kernel(r_ref, pa_ref, pb_ref, w1a_ref, w1b_ref, w2_ref, o_ref):
    r = r_ref[...].astype(F32)
    a = jnp.dot((r + pa_ref[...]).astype(BF16), w1a_ref[...], preferred_element_type=F32)
    b = jnp.dot((r + pb_ref[...]).astype(BF16), w1b_ref[...], preferred_element_type=F32)
    nr = a.shape[0]
    hid = a + pltpu.roll(b, nr - 1, 0)
    hid = jax.nn.gelu(hid)
    o_ref[...] = jnp.dot(hid.astype(BF16), w2_ref[...], preferred_element_type=F32).astype(o_ref.dtype)


def compress(r, head0, pos_a, pos_b, w1a, w1b, w2):
    b, _, nr, kk = r.shape
    g = NSA_KV_HEADS
    full = lambda shape: pl.BlockSpec(shape, lambda bi, gi: (0,) * len(shape))
    return pl.pallas_call(
        _compress_kernel,
        out_shape=jax.ShapeDtypeStruct((b, g, nr, LANES), BF16),
        grid=(b, g),
        in_specs=[pl.BlockSpec((None, None, nr, kk), lambda bi, gi: (bi, head0 + gi, 0, 0)),
                  full((1, kk)), full((1, kk)), full((kk, LANES)), full((kk, LANES)),
                  full((LANES, LANES))],
        out_specs=pl.BlockSpec((None, None, nr, LANES), lambda bi, gi: (bi, gi, 0, 0)),
        compiler_params=_cparams("parallel", "parallel"),
        name="nsa_compress",
    )(r, pos_a, pos_b, w1a, w1b, w2)


def _dot_nt(a, b):
    return lax.dot_general(a, b, (((1,), (1,)), ((), ())), preferred_element_type=F32)


def _split_dot(x, w):
    hi = x.astype(BF16)
    lo = (x - hi.astype(F32)).astype(BF16)
    return (jnp.dot(hi, w, preferred_element_type=F32) + jnp.dot(lo, w, preferred_element_type=F32))


def _nsa_cmp_kernel(q_ref, kcc_ref, vcc_ref, msel_ref, glog_ref, e_ref, oc_ref, sel_ref, s_sc, p_sc, imp_sc,
                    *, tq, rc, topk):
    s0 = pl.program_id(2) * tq
    nr = kcc_ref.shape[0]

    def attend(ncol):
        kcc = kcc_ref[0:ncol, :]
        for h in range(NSA_HPG):
            s_sc[h, :, 0:ncol] = _dot_nt(q_ref[h], kcc)
        lane_v = lax.broadcasted_iota(jnp.int32, (ncol, LANES), 1)
        vcc = vcc_ref[0:ncol, :]
        vm = jnp.concatenate([jnp.where(lane_v == HALF, jnp.ones_like(vcc), vcc), msel_ref[0:ncol, :]],
                             axis=1)
        dmat = (lax.broadcasted_iota(jnp.int32, (rc, ncol), 0)
                - lax.broadcasted_iota(jnp.int32, (rc, ncol), 1) * CMP_STRIDE)
        visible = lambda c: dmat + (s0 + c * rc - (CMP_BLOCK - 1)) >= 0
        outs = []
        imp = jnp.zeros((tq, LANES), F32)
        for h in range(NSA_HPG):
            row_max = []
            for c in range(tq // rc):
                s = jnp.where(visible(c), s_sc[h, pl.ds(c * rc, rc), 0:ncol], NEG)
                row_max.append(jnp.max(s, axis=-1, keepdims=True))
            for c in range(tq // rc):
                rows = pl.ds(c * rc, rc)
                p = jnp.exp2(s_sc[h, rows, 0:ncol] - row_max[c])
                p_sc[h, rows, 0:ncol] = jnp.where(visible(c), p, 0.0).astype(BF16)
            a = jnp.dot(p_sc[h, :, 0:ncol], vm, preferred_element_type=F32)
            denom = jnp.maximum(a[:, HALF:HALF + 1], 1e-30)
            outs.append(a[:, :LANES] / denom)
            imp = imp + a[:, LANES:] / denom
        imp_sc[...] = imp
        glog = glog_ref[...]
        lane = lax.broadcasted_iota(jnp.int32, (tq, LANES), 1)
        for pr in range(NSA_HPG // 2):
            gate = jax.nn.sigmoid(_split_dot(glog, e_ref[pr]))
            pair = jnp.where(lane < HALF, outs[2 * pr], pltpu.roll(outs[2 * pr + 1], HALF, 1))
            oc_ref[:, pr * LANES:(pr + 1) * LANES] = (gate * pair).astype(oc_ref.dtype)

    ntiles = (s0 + (tq - CMP_BLOCK)) // (CMP_STRIDE * LANES) + 1
    for k in range(1, nr // LANES + 1):
        @pl.when(ntiles == k)
        def _width(k=k):
            attend(k * LANES)

    blk = lax.broadcasted_iota(jnp.int32, (tq, LANES), 1)
    cur = (s0 + lax.broadcasted_iota(jnp.int32, (tq, LANES), 0)) // SEL_BLOCK
    valid = blk <= cur
    forced = (blk == 0) | (blk == cur) | (blk == cur - 1)
    score = jnp.where(valid, jnp.where(forced, -jnp.inf, imp_sc[...]), -1.0)
    score_t = score.T
    nblk = (s0 + (tq - 1)) // SEL_BLOCK + 1
    for nrows in range(TOPK_ROWS, LANES + 1, TOPK_ROWS):
        @pl.when((nblk + (TOPK_ROWS - 1)) // TOPK_ROWS == nrows // TOPK_ROWS)
        def _select(nrows=nrows):
            tw = min(tq, TOPK_QUERIES)
            rowid = lax.broadcasted_iota(jnp.int32, (nrows, tw), 0).astype(F32)

            def pick_one(_, sc):
                cm = jnp.max(sc, axis=0, keepdims=True)
                first = jnp.min(jnp.where(sc == cm, rowid, float(LANES)), axis=0, keepdims=True)
                return jnp.where(rowid == first, -jnp.inf, sc)

            for w in range(tq // tw):
                cols = slice(w * tw, (w + 1) * tw)
                picked = lax.fori_loop(0, topk - N_FORCED, pick_one, score_t[0:nrows, cols]) == -jnp.inf
                bias = jnp.where(picked, 0.0, NEG)
                if nrows < LANES:
                    bias = jnp.concatenate([bias, jnp.full((LANES - nrows, tw), NEG, F32)], axis=0)
                sel_ref[cols, :] = bias.T.astype(sel_ref.dtype)


def nsa_compressed(qk, kcc, vcc, msel, misc, glog_blk, expand, *, batch, seq, tq, rc):
    g = NSA_KV_HEADS
    nq = seq // tq
    nr = kcc.shape[2]
    topk = min(SEL_TOPK, seq // SEL_BLOCK)
    assert topk > N_FORCED and nr % LANES == 0
    kern = functools.partial(_nsa_cmp_kernel, tq=tq, rc=rc, topk=topk)
    return pl.pallas_call(
        kern,
        out_shape=(jax.ShapeDtypeStruct((batch * seq, NSA_HEADS * NSA_DV), BF16),
                   jax.ShapeDtypeStruct((batch, g, seq, LANES), BF16)),
        grid=(batch, g, nq),
        in_specs=[
            pl.BlockSpec((None, NSA_HPG, tq, LANES), lambda b, gi, qi: (b, gi, qi, 0)),
            pl.BlockSpec((None, None, nr, LANES), lambda b, gi, qi: (b, gi, 0, 0)),
            pl.BlockSpec((None, None, nr, LANES), lambda b, gi, qi: (b, gi, 0, 0)),
            pl.BlockSpec((nr, LANES), lambda b, gi, qi: (0, 0)),
            pl.BlockSpec((tq, LANES), lambda b, gi, qi: (b * nq + qi, glog_blk)),
            pl.BlockSpec((None, 2, LANES, LANES), lambda b, gi, qi: (0, gi, 0, 0)),
        ],
        out_specs=(pl.BlockSpec((tq, 2 * LANES), lambda b, gi, qi: (b * nq + qi, gi)),
                   pl.BlockSpec((None, None, tq, LANES), lambda b, gi, qi: (b, gi, qi, 0))),
        scratch_shapes=[pltpu.VMEM((NSA_HPG, tq, nr), F32), pltpu.VMEM((NSA_HPG, tq, nr), BF16),
                        pltpu.VMEM((tq, LANES), F32)],
        compiler_params=_cparams("parallel", "parallel", "parallel"),
        name="nsa_compressed_select",
    )(qk, kcc, vcc, msel, misc, expand)


def _flash_steps(mode, nq, tq, tk):
    r = tq // tk
    qi_l, kt_l, first_l, last_l = [], [], [], []
    for qi in range(nq):
        hi = qi * r + r - 1
        lo = max(0, qi * r - (-(-(WINDOW - 1) // tk))) if mode == "window" else 0
        for kt in range(lo, hi + 1):
            qi_l.append(qi)
            kt_l.append(kt)
            first_l.append(int(kt == lo))
            last_l.append(int(kt == hi))
    return tuple(jnp.asarray(np.asarray(a, np.int32)) for a in (qi_l, kt_l, first_l, last_l))


def _flash_kernel(*refs, mode, tq, tk, rc, kv_shared, gated):
    it = iter(refs)
    qi_ref, kt_ref, first_ref, last_ref = next(it), next(it), next(it), next(it)
    q_ref, k_ref, v_ref = next(it), next(it), next(it)
    selb_ref = next(it) if mode == "select" else None
    oh_ref = next(it) if mode == "select" else None
    glog_ref = next(it) if gated else None
    e_ref = next(it) if gated else None
    o_ref = next(it)
    m_sc, acc_sc, s_sc, p_sc = next(it), next(it), next(it), next(it)
    qa_sc = next(it) if mode == "select" else None

    st = pl.program_id(2)
    s0 = qi_ref[st] * tq
    k0 = kt_ref[st] * tk

    @pl.when(first_ref[st] == 1)
    def _init():
        m_sc[...] = jnp.full(m_sc.shape, NEG, F32)
        acc_sc[...] = jnp.zeros(acc_sc.shape, F32)
        if mode == "select":
            for hh in range(FLASH_HEADS):
                qa_sc[hh] = jnp.concatenate([q_ref[hh], selb_ref[...]], axis=1)

    def chunk_cols(off, r0):
        if off is None:
            return 0, tk, True
        r1 = r0 + rc
        hi = min(tk, r1 - off)
        lo = max(0, r0 - off - WINDOW + 1) if mode == "window" else 0
        lo, hi = lo // LANES * LANES, -(-hi // LANES) * LANES
        clear = tk - 1 <= r0 - off and (mode != "window" or r1 - 1 - off < WINDOW)
        return lo, hi, clear

    def tile(off):
        nh = FLASH_HEADS
        r_lo = max(0, off) if off is not None else 0
        nrow = tq - r_lo
        if kv_shared:
            if mode == "select":
                qs = qa_sc[:, r_lo:, :].reshape(nh * nrow, 2 * LANES)
                ks = jnp.concatenate([k_ref[0], oh_ref[...]], axis=1)
            else:
                qs, ks = q_ref[:, r_lo:, :].reshape(nh * nrow, LANES), k_ref[0]
            s_sc[:, r_lo:, :] = _dot_nt(qs, ks).reshape(nh, nrow, tk)
        else:
            for hh in range(nh):
                s_sc[hh, r_lo:, :] = _dot_nt(q_ref[hh, r_lo:, :], k_ref[hh])
        lane_v = lax.broadcasted_iota(jnp.int32, (tk, LANES), 1)
        if off is not None:
            dmat = (lax.broadcasted_iota(jnp.int32, (rc, tk), 0)
                    - lax.broadcasted_iota(jnp.int32, (rc, tk), 1))
        for hh in range(FLASH_HEADS):
            v = v_ref[0 if kv_shared else hh]
            v_aug = jnp.where(lane_v == HALF, jnp.ones_like(v), v)

            for c in range(r_lo // rc, tq // rc):
                r0 = c * rc
                rows = pl.ds(r0, rc)
                lo, hi, clear = chunk_cols(off, r0)
                if lo > 0:
                    p_sc[hh, rows, 0:lo] = jnp.zeros((rc, lo), BF16)
                if hi < tk:
                    p_sc[hh, rows, max(hi, 0):tk] = jnp.zeros((rc, tk - max(hi, 0)), BF16)
                if hi <= lo:
                    continue
                s = s_sc[hh, rows, lo:hi]
                if not clear:
                    d = dmat[:, lo:hi] + (r0 - off)
                    msk = (d >= 0) & (d < WINDOW) if mode == "window" else d >= 0
                    s = jnp.where(msk, s, NEG)
                m_old = m_sc[hh, rows, :]
                m_new = jnp.maximum(m_old, jnp.max(s, axis=-1, keepdims=True))
                m_sc[hh, rows, :] = m_new
                acc_sc[hh, rows, :] = acc_sc[hh, rows, :] * jnp.exp2(m_old - m_new)
                p = jnp.exp2(s_sc[hh, rows, lo:hi] - jnp.concatenate([m_new] * ((hi - lo) // LANES), axis=1))
                if not clear:
                    p = jnp.where(msk, p, 0.0)
                p_sc[hh, rows, lo:hi] = p.astype(BF16)
            if not kv_shared:
                acc_sc[hh, r_lo:, :] += jnp.dot(p_sc[hh, r_lo:, :], v_aug, preferred_element_type=F32)
        if kv_shared:
            pv = jnp.dot(p_sc[:, r_lo:, :].reshape(nh * nrow, tk), v_aug, preferred_element_type=F32)
            acc_sc[:, r_lo:, :] += pv.reshape(nh, nrow, LANES)

    if mode == "window":
        offsets = [(j - (-(-(WINDOW - 1) // tk))) * tk for j in range(-(-(WINDOW - 1) // tk) + tq // tk)]
    else:
        offsets = [j * tk for j in range(tq // tk)]

        @pl.when(k0 + tk - 1 <= s0)
        def _interior():
            tile(None)

    for off in offsets:
        @pl.when(k0 - s0 == off)
        def _partial(off=off):
            tile(off)

    @pl.when(last_ref[st] == 1)
    def _finish():
        lane = lax.broadcasted_iota(jnp.int32, (tq, LANES), 1)
        if gated:
            glog = glog_ref[...]
        for pr in range(FLASH_HEADS // 2):
            outs = []
            for hh in (2 * pr, 2 * pr + 1):
                a = acc_sc[hh]
                outs.append(a / jnp.maximum(a[:, HALF:HALF + 1], 1e-30))
            out = jnp.where(lane < HALF, outs[0], pltpu.roll(outs[1], HALF, 1))
            if gated:
                out = jax.nn.sigmoid(_split_dot(glog, e_ref[pr])) * out
            o_ref[:, pr * LANES:(pr + 1) * LANES] = out.astype(o_ref.dtype)


def flash_heads(q, k, v, *, mode, batch, seq, tq, tk, rc, q_head0, k_head0, v_head0,
                kv_shared, out_dtype, selb=None, onehot=None, gate=None, name="flash"):
    nh = FLASH_HEADS
    ngroups = NSA_HEADS // nh
    nq = seq // tq
    assert tq % tk == 0 and tq % rc == 0
    tabs = _flash_steps(mode, nq, tq, tk)
    nsteps = tabs[0].shape[0]
    if kv_shared:
        kv_spec = lambda h0: pl.BlockSpec(
            (None, 1, tk, LANES), lambda b, g, st, qi, kt, fi, la: (b, h0 + g, kt[st], 0))
    else:
        kv_spec = lambda h0: pl.BlockSpec(
            (None, nh, tk, LANES), lambda b, g, st, qi, kt, fi, la: (b, h0 // nh + g, kt[st], 0))
    in_specs = [pl.BlockSpec((None, nh, tq, LANES),
                             lambda b, g, st, qi, kt, fi, la: (b, q_head0 // nh + g, qi[st], 0)),
                kv_spec(k_head0), kv_spec(v_head0)]
    args = [q, k, v]
    scratch = [pltpu.VMEM((nh, tq, LANES), F32), pltpu.VMEM((nh, tq, LANES), F32),
               pltpu.VMEM((nh, tq, tk), F32), pltpu.VMEM((nh, tq, tk), BF16)]
    if mode == "select":
        in_specs += [pl.BlockSpec((None, None, tq, LANES),
                                  lambda b, g, st, qi, kt, fi, la: (b, g, qi[st], 0)),
                     pl.BlockSpec((tk, LANES), lambda b, g, st, qi, kt, fi, la: (kt[st], 0))]
        args += [selb, onehot]
        scratch.append(pltpu.VMEM((nh, tq, 2 * LANES), BF16))
    if gate is not None:
        misc, glog_blk, expand, branch = gate
        in_specs += [pl.BlockSpec((tq, LANES),
                                  lambda b, g, st, qi, kt, fi, la: (b * nq + qi[st], glog_blk)),
                     pl.BlockSpec((None, nh // 2, LANES, LANES),
                                  lambda b, g, st, qi, kt, fi, la: (branch, g, 0, 0))]
        args += [misc, expand]
    kern = functools.partial(_flash_kernel, mode=mode, tq=tq, tk=tk, rc=rc,
                             kv_shared=kv_shared, gated=gate is not None)
    ow = nh * HALF
    grid_spec = pltpu.PrefetchScalarGridSpec(
        num_scalar_prefetch=4, grid=(batch, ngroups, nsteps), in_specs=in_specs,
        out_specs=pl.BlockSpec((tq, ow), lambda b, g, st, qi, kt, fi, la: (b * nq + qi[st], g)),
        scratch_shapes=scratch)
    return pl.pallas_call(
        kern,
        out_shape=jax.ShapeDtypeStruct((batch * seq, ngroups * ow), out_dtype),
        grid_spec=grid_spec,
        compiler_params=_cparams("parallel", "parallel", "arbitrary"),
        name=name,
    )(*tabs, *args)


def _mix_kernel(oc_ref, os_ref, ow_ref, ob_ref, xn_ref, wga_ref, wgb_ref, wa_ref, wb_ref, o_ref):
    oa = (oc_ref[...].astype(F32) + os_ref[...].astype(F32) + ow_ref[...].astype(F32)).astype(BF16)
    xn = xn_ref[...]
    ga = jax.nn.sigmoid(jnp.dot(xn, wga_ref[...], preferred_element_type=F32))
    ya = ga * jnp.dot(oa, wa_ref[...], preferred_element_type=F32)
    gb = jax.nn.sigmoid(jnp.dot(xn, wgb_ref[...], preferred_element_type=F32))
    yb = gb * jnp.dot(ob_ref[...], wb_ref[...], preferred_element_type=F32)
    o_ref[...] = (ya + yb).astype(o_ref.dtype)


def gated_mix(oc, osel, ow, ob, xn, wga, wgb, wa, wb, *, tm, tn):
    n, ka = oc.shape
    d = wa.shape[1]
    row = lambda w: pl.BlockSpec((tm, w), lambda i, j: (i, 0))
    col = lambda k: pl.BlockSpec((k, tn), lambda i, j: (0, j))
    return pl.pallas_call(
        _mix_kernel,
        out_shape=jax.ShapeDtypeStruct((n, d), BF16),
        grid=(n // tm, d // tn),
        in_specs=[row(ka), row(ka), row(ka), row(ob.shape[1]), row(xn.shape[1]),
                  col(wga.shape[0]), col(wgb.shape[0]), col(ka), col(wb.shape[0])],
        out_specs=pl.BlockSpec((tm, tn), lambda i, j: (i, j)),
        compiler_params=_cparams("parallel", "arbitrary"),
        name="gated_mix",
    )(oc, osel, ow, ob, xn, wga, wgb, wa, wb)


def _xattn_kernel(x_ref, mix_ref, wout_ref, g_ref, wq_ref, kv_ref, wo_ref, o_ref):
    h = x_ref[...] + jnp.dot(mix_ref[...], wout_ref[...], preferred_element_type=F32)
    hn = _rms(h, g_ref[...]).astype(BF16)
    q = jnp.dot(hn, wq_ref[...], preferred_element_type=F32).astype(BF16)
    kv = kv_ref[...]
    outs = []
    for hd in range(XA_HEADS):
        qh = q[:, hd * XA_DIM:(hd + 1) * XA_DIM]
        kh = kv[:, hd * XA_DIM:(hd + 1) * XA_DIM]
        vh = kv[:, (XA_HEADS + hd) * XA_DIM:(XA_HEADS + hd + 1) * XA_DIM]
        s = _dot_nt(qh, kh) * (XA_DIM ** -0.5)
        e = jnp.exp(s - jnp.max(s, axis=-1, keepdims=True))
        p = e / jnp.sum(e, axis=-1, keepdims=True)
        outs.append(jnp.dot(p.astype(BF16), vh, preferred_element_type=F32))
    o = jnp.concatenate(outs, axis=1).astype(BF16)
    o_ref[...] = h + jnp.dot(o, wo_ref[...], preferred_element_type=F32)


def out_proj_xattn(x, mixed, wout, g, wq, kv, wo, *, seq, tm):
    n, d = x.shape
    s_t = seq // tm
    dq = wq.shape[1]
    resident = lambda shape: pl.BlockSpec(shape, lambda i: (0, 0), pipeline_mode=pl.Buffered(1))
    return pl.pallas_call(
        _xattn_kernel,
        out_shape=jax.ShapeDtypeStruct((n, d), F32),
        grid=(n // tm,),
        in_specs=[pl.BlockSpec((tm, d), lambda i: (i, 0)),
                  pl.BlockSpec((tm, d), lambda i: (i, 0)),
                  resident((d, d)),
                  pl.BlockSpec((1, d), lambda i: (0, 0)),
                  resident((d, dq)),
                  pl.BlockSpec((MEM_LEN, 2 * dq), lambda i: (i // s_t, 0)),
                  resident((dq, d))],
        out_specs=pl.BlockSpec((tm, d), lambda i: (i, 0)),
        compiler_params=_cparams("parallel"),
        name="out_proj_xattn",
    )(x, mixed, wout, g.reshape(1, d), wq, kv, wo)


def _mlp_kernel(h_ref, g_ref, w1_ref, w2_ref, gf_ref, o_ref, hn_sc, acc_sc, *, nf):
    f = pl.program_id(1)

    @pl.when(f == 0)
    def _init():
        hn_sc[...] = _rms(h_ref[...], g_ref[...]).astype(BF16)
        acc_sc[...] = jnp.zeros(acc_sc.shape, F32)

    u = jnp.dot(hn_sc[...], w1_ref[...], preferred_element_type=F32)
    u = jnp.square(jnp.maximum(u, 0.0))
    acc_sc[...] += jnp.dot(u.astype(BF16), w2_ref[...], preferred_element_type=F32)

    @pl.when(f == nf - 1)
    def _finish():
        o_ref[...] = _rms(h_ref[...] + acc_sc[...], gf_ref[...])


def mlp_final(h, g, w1, w2, gf, *, tm, tf):
    n, d = h.shape
    dff = w1.shape[1]
    nf = dff // tf
    return pl.pallas_call(
        functools.partial(_mlp_kernel, nf=nf),
        out_shape=jax.ShapeDtypeStruct((n, d), F32),
        grid=(n // tm, nf),
        in_specs=[pl.BlockSpec((tm, d), lambda i, f: (i, 0)),
                  pl.BlockSpec((1, d), lambda i, f: (0, 0)),
                  pl.BlockSpec((d, tf), lambda i, f: (0, f)),
                  pl.BlockSpec((tf, d), lambda i, f: (f, 0)),
                  pl.BlockSpec((1, d), lambda i, f: (0, 0))],
        out_specs=pl.BlockSpec((tm, d), lambda i, f: (i, 0)),
        scratch_shapes=[pltpu.VMEM((tm, d), BF16), pltpu.VMEM((tm, d), F32)],
        compiler_params=_cparams("parallel", "arbitrary"),
        name="mlp_final",
    )(h, g.reshape(1, d), w1, w2, gf.reshape(1, d))


def _rot_partner(w, half):
    return jnp.concatenate([-w[..., half:], w[..., :half]], axis=-1)


def _pad_heads_rope(w, heads, hd, rot0, rot_dim):
    k = w.shape[0]
    w = w.reshape(k, heads, hd)
    partner = _rot_partner(w[:, :, rot0:rot0 + rot_dim], rot_dim // 2)
    pad = jnp.zeros((k, heads, LANES - hd - rot_dim), w.dtype)
    return jnp.concatenate([w, partner, pad], axis=-1).reshape(k, heads * LANES)


def _rope_lane_tables(seq, rot0, rot_dim, hd):
    inv = 1.0 / (ROPE_THETA ** (jnp.arange(0, rot_dim, 2, dtype=F32) / rot_dim))
    ang = jnp.arange(seq, dtype=F32)[:, None] * inv[None, :]
    cos = jnp.concatenate([jnp.cos(ang), jnp.cos(ang)], axis=1)
    sin = jnp.concatenate([jnp.sin(ang), jnp.sin(ang)], axis=1)
    cos_t = jnp.concatenate([jnp.ones((seq, rot0), F32), cos,
                             jnp.ones((seq, hd - rot0 - rot_dim), F32),
                             jnp.zeros((seq, LANES - hd), F32)], axis=1)
    sin_t = jnp.concatenate([jnp.zeros((seq, rot0), F32), sin,
                             jnp.zeros((seq, LANES - rot0 - rot_dim), F32)], axis=1)
    return cos_t, sin_t


def _pad_lanes(w, groups, width):
    k = w.shape[0]
    w = w.reshape(k, groups, width)
    return jnp.pad(w, ((0, 0), (0, 0), (0, LANES - width))).reshape(k, groups * LANES)


def _compress_params(pos, w1, w2, d):
    half = CMP_BLOCK // 2
    pos_p = jnp.pad(pos, ((0, 0), (0, LANES - d)))
    pos_a = pos_p[:half].reshape(1, half * LANES)
    pos_b = pos_p[half:].reshape(1, half * LANES)
    w1p = jnp.pad(w1.reshape(CMP_BLOCK, d, d), ((0, 0), (0, LANES - d), (0, LANES - d)))
    w1a = w1p[:half].reshape(half * LANES, LANES).astype(BF16)
    w1b = w1p[half:].reshape(half * LANES, LANES).astype(BF16)
    w2p = jnp.pad(w2, ((0, LANES - d), (0, LANES - d)))
    return pos_a, pos_b, w1a, w1b, w2p.astype(BF16)


def _cmp_to_sel(nr, nsb):
    cs = np.arange(nr) * CMP_STRIDE
    ce = cs + CMP_BLOCK
    ss = np.arange(LANES) * SEL_BLOCK
    se = ss + SEL_BLOCK
    ov = np.clip(np.minimum(ce[:, None], se[None, :]) - np.maximum(cs[:, None], ss[None, :]), 0, None)
    ov = ov.astype(np.float32) / np.float32(CMP_BLOCK)
    ov[:, nsb:] = 0.0
    ov[nr - 1:, :] = 0.0
    return jnp.asarray(ov, BF16)


def _gate_expand():
    e = np.zeros((3, NSA_HEADS // 2, LANES, LANES), np.float32)
    for br in range(3):
        for hp in range(NSA_HEADS // 2):
            for hh in range(2):
                e[br, hp, 3 * (2 * hp + hh) + br, hh * HALF:(hh + 1) * HALF] = 1.0
    return jnp.asarray(e, BF16)


def _key_block_onehot(seq):
    e = (np.arange(seq)[:, None] // SEL_BLOCK) == np.arange(LANES)[None, :]
    return jnp.asarray(e.astype(np.float32), BF16)


def kernel(x, mem, g_mix, w_in, cmp_pos_k, cmp_w1_k, cmp_w2_k, cmp_pos_v, cmp_w1_v, cmp_w2_v,
           mla_g_q, mla_w_uq, mla_g_kv, mla_w_uk, mla_w_uv, w_o_nsa, w_o_mla, w_out,
           g_xattn, g_mem, xa_wq, xa_wkv, xa_wo, g_mlp, w_ff1, w_ff2, g_final):
    b, s, d = x.shape
    assert d == D_MODEL and s % (CMP_STRIDE * 8) == 0 and s // SEL_BLOCK <= LANES
    assert g_mix.shape[0] == 1
    n = b * s
    T = _tiles(s)
    tm, tn, tq, tk, rc = T["tm"], T["tn"], T["tq"], T["tk"], T["rc"]
    G = NSA_KV_HEADS
    bounds = [int(v) for v in np.cumsum(SPLITS)[:-1]]

    (w_qa, w_kc, w_vc, w_ks, w_vs, w_kw, w_vw, w_gn, w_cq, w_ckv, w_kr,
     w_ga, w_gb) = jnp.split(w_in[0], bounds, axis=1)
    nsa_rope = lambda w, heads: _pad_heads_rope(w, heads, NSA_DK, 0, NSA_ROT)
    w_rope = jnp.concatenate([nsa_rope(w_qa, NSA_HEADS), nsa_rope(w_kc, G), nsa_rope(w_ks, G),
                              nsa_rope(w_kw, G)], axis=1).astype(BF16)
    w_krp = jnp.concatenate([jnp.zeros((d, MLA_NOPE), F32), w_kr, _rot_partner(w_kr, MLA_ROPE // 2)], axis=1)
    w_vsw = jnp.concatenate([_pad_lanes(w_vs, G, NSA_DV), _pad_lanes(w_vw, G, NSA_DV),
                             _pad_lanes(w_vc, G, NSA_DV)], axis=1).astype(BF16)
    w_misc = jnp.concatenate([w_cq, w_ckv, jnp.pad(w_gn, ((0, 0), (0, LANES - w_gn.shape[1]))), w_krp],
                             axis=1).astype(BF16)
    glog_blk = (MLA_Q_RANK + MLA_KV_RANK) // LANES
    krp_blk = glog_blk + 1
    cos_a, sin_a = _rope_lane_tables(s, 0, NSA_ROT, NSA_DK)
    cos_b, sin_b = _rope_lane_tables(s, MLA_NOPE, MLA_ROPE, MLA_NOPE + MLA_ROPE)
    shift_a = LANES - NSA_DK
    shift_b = LANES - (NSA_DK - MLA_NOPE)
    w_uq = _pad_heads_rope(mla_w_uq[0], MLA_HEADS, MLA_NOPE + MLA_ROPE, MLA_NOPE, MLA_ROPE).astype(BF16)
    w_uk = _pad_lanes(mla_w_uk[0], MLA_HEADS, MLA_NOPE).astype(BF16)
    w_uv = _pad_lanes(mla_w_uv[0], MLA_HEADS, MLA_DV).astype(BF16)

    x2 = x.reshape(n, d)
    xn = rmsnorm_rows(x2, g_mix[0], tm)
    log2e = float(np.log2(np.e))
    tmb = T["tm_big"]
    qk = projection(xn, w_rope, tm=tmb, tn=tn, out_dtype=BF16, rope=(cos_a, sin_a, shift_a),
                    out_scale=(NSA_DK ** -0.5 * log2e, NSA_HEADS * LANES // tn),
                    head_major=(b, s), name="proj_qk_rope")
    vsw = projection(xn, w_vsw, tm=tmb, tn=tn, out_dtype=BF16, head_major=(b, s), name="proj_v")
    vcp = vsw[:, 2 * G:3 * G]
    misc = projection(xn, w_misc, tm=tmb, tn=tn, out_dtype=F32, name="proj_misc")

    nr = s // CMP_STRIDE
    pk = _compress_params(cmp_pos_k[0], cmp_w1_k[0], cmp_w2_k[0], NSA_DK)
    pv = _compress_params(cmp_pos_v[0], cmp_w1_v[0], cmp_w2_v[0], NSA_DV)
    kc_rows = qk[:, QK_KC0:QK_KC0 + G].reshape(b, G, nr, CMP_STRIDE * LANES)
    kcc = compress(kc_rows, 0, *pk)
    vcc = compress(vcp.reshape(b, G, nr, CMP_STRIDE * LANES), 0, *pv)
    expand = _gate_expand()
    o_c, selb = nsa_compressed(qk, kcc, vcc, _cmp_to_sel(nr, s // SEL_BLOCK), misc, glog_blk, expand,
                               batch=b, seq=s, tq=T["tq_cmp"], rc=rc)
    o_s = flash_heads(qk, qk, vsw, mode="select", batch=b, seq=s, tq=tq, tk=tk, rc=rc,
                      q_head0=QK_Q0, k_head0=QK_KS0, v_head0=0, kv_shared=True, out_dtype=BF16,
                      selb=selb, onehot=_key_block_onehot(s), gate=(misc, glog_blk, expand, 1),
                      name="nsa_selected")
    o_w = flash_heads(qk, qk, vsw, mode="window", batch=b, seq=s, tq=T["tq_win"], tk=tk, rc=rc,
                      q_head0=QK_Q0, k_head0=QK_KW0, v_head0=G, kv_shared=True, out_dtype=BF16,
                      gate=(misc, glog_blk, expand, 2), name="nsa_window")

    qm = projection(misc, w_uq, tm=tmb, tn=tn, out_dtype=BF16, a_cols=(MLA_Q_RANK, 0), gain=mla_g_q[0],
                    rope=(cos_b, sin_b, shift_b), head_major=(b, s), name="mla_q",
                    out_scale=((MLA_NOPE + MLA_ROPE) ** -0.5 * log2e, MLA_HEADS * LANES // tn))
    km = projection(misc, w_uk, tm=tmb, tn=tn, out_dtype=BF16,
                    a_cols=(MLA_KV_RANK, MLA_Q_RANK // MLA_KV_RANK), gain=mla_g_kv[0],
                    add=(misc, krp_blk), rope=(cos_b, sin_b, shift_b), rope_add=True,
                    head_major=(b, s), name="mla_k")
    vm = projection(misc, w_uv, tm=tmb, tn=tn, out_dtype=BF16,
                    a_cols=(MLA_KV_RANK, MLA_Q_RANK // MLA_KV_RANK), gain=mla_g_kv[0],
                    head_major=(b, s), name="mla_v")
    o_b = flash_heads(qm, km, vm, mode="causal", batch=b, seq=s, tq=tq, tk=tk, rc=rc,
                      q_head0=0, k_head0=0, v_head0=0,
                      kv_shared=False, out_dtype=BF16, name="mla_attention")

    mixed = gated_mix(o_c, o_s, o_w, o_b, xn, w_ga.astype(BF16), w_gb.astype(BF16),
                      w_o_nsa[0].astype(BF16), w_o_mla[0].astype(BF16), tm=tm, tn=tn)

    kv_mem = projection(mem.reshape(b * MEM_LEN, d), xa_wkv[0].astype(BF16), tm=MEM_LEN, tn=tn,
                        out_dtype=BF16, gain=g_mem[0], name="xattn_kv")
    h2 = out_proj_xattn(x2, mixed, w_out[0].astype(BF16), g_xattn[0], xa_wq[0].astype(BF16), kv_mem,
                        xa_wo[0].astype(BF16), seq=s, tm=T["tm_x"])

    out = mlp_final(h2, g_mlp[0], w_ff1[0].astype(BF16), w_ff2[0].astype(BF16), g_final,
                    tm=T["tm_mlp"], tf=T["tf"])
    return out.reshape(b, s, d)
```

```python
import functools

import numpy as np
import jax
import jax.numpy as jnp
from jax import lax
from jax.experimental import pallas as pl
from jax.experimental.pallas import tpu as pltpu

F32 = jnp.float32
BF16 = jnp.bfloat16

D_MODEL = 2048
MEM_LEN = 256
ROPE_THETA = 500000.0
EPS = 1e-6
NEG = -1e30

NSA_HEADS = 16
NSA_KV_HEADS = 4
NSA_HPG = NSA_HEADS // NSA_KV_HEADS
NSA_DK = 96
NSA_DV = 64
NSA_ROT = NSA_DK // 4
CMP_BLOCK = 32
CMP_STRIDE = 16
SEL_BLOCK = 64
SEL_TOPK = 16
N_FORCED = 3
WINDOW = 512

MLA_HEADS = 16
MLA_NOPE = 64
MLA_ROPE = 32
MLA_DV = 64
MLA_Q_RANK = 512
MLA_KV_RANK = 256

XA_HEADS = 4
XA_DIM = 128
D_FF = 4 * D_MODEL

SPLITS = (NSA_HEADS * NSA_DK,
          NSA_KV_HEADS * NSA_DK, NSA_KV_HEADS * NSA_DV,
          NSA_KV_HEADS * NSA_DK, NSA_KV_HEADS * NSA_DV,
          NSA_KV_HEADS * NSA_DK, NSA_KV_HEADS * NSA_DV,
          NSA_HEADS * 3,
          MLA_Q_RANK, MLA_KV_RANK, MLA_ROPE,
          D_MODEL, D_MODEL)

LANES = 128
HALF = LANES // 2
VMEM_LIMIT = 56 * 1024 * 1024

QK_Q0 = 0
QK_KC0 = NSA_HEADS
QK_KS0 = NSA_HEADS + NSA_KV_HEADS
QK_KW0 = NSA_HEADS + 2 * NSA_KV_HEADS
QK_HEADS = NSA_HEADS + 3 * NSA_KV_HEADS
FLASH_HEADS = NSA_HPG
TOPK_ROWS = 32
TOPK_QUERIES = 256


def _cparams(*sem):
    return pltpu.CompilerParams(dimension_semantics=sem, vmem_limit_bytes=VMEM_LIMIT)


def _tiles(seq):
    return dict(
        tm=min(1024, seq),
        tm_big=min(2048, seq),
        tm_mla=min(4096, seq),
        tn=4 * LANES,
        tm_x=min(512, seq),
        tm_mlp=min(512, seq),
        tq=min(2048, seq),
        tq_win=min(512, seq),
        tk=min(512, seq),
        rc=32,
        tq_cmp=min(512, seq),
        tf=1024,
    )


def _rms(x, g):
    return x * lax.rsqrt(jnp.mean(x * x, axis=-1, keepdims=True) + EPS) * g


def _rmsnorm_kernel(x_ref, g_ref, o_ref):
    o_ref[...] = _rms(x_ref[...], g_ref[...]).astype(o_ref.dtype)


def rmsnorm_rows(x, g, tm):
    n, d = x.shape
    return pl.pallas_call(
        _rmsnorm_kernel,
        out_shape=jax.ShapeDtypeStruct((n, d), BF16),
        grid=(n // tm,),
        in_specs=[pl.BlockSpec((tm, d), lambda i: (i, 0)),
                  pl.BlockSpec((1, d), lambda i: (0, 0))],
        out_specs=pl.BlockSpec((tm, d), lambda i: (i, 0)),
        compiler_params=_cparams("parallel"),
        name="rmsnorm_rows",
    )(x, g.reshape(1, d))


def _proj_kernel(*refs, has_gain, has_rope, roll_shift, rope_add, out_scale, has_add, head_major):
    it = iter(refs)
    a_ref = next(it)
    g_ref = next(it) if has_gain else None
    w_ref = next(it)
    cos_ref = next(it) if has_rope else None
    sin_ref = next(it) if has_rope else None
    add_ref = next(it) if has_add else None
    o_ref = next(it)

    if has_gain:
        an_sc = next(it)

        @pl.when(pl.program_id(1) == 0)
        def _norm():
            an_sc[...] = _rms(a_ref[...], g_ref[...]).astype(BF16)

        a = an_sc[...]
    else:
        a = a_ref[...]
    y = jnp.dot(a, w_ref[...], preferred_element_type=F32)
    if out_scale is not None:
        factor, n_tiles = out_scale
        y = y * jnp.where(pl.program_id(1) < n_tiles, factor, 1.0)
    rope = lambda t: t * cos_ref[...] + pltpu.roll(t, roll_shift, 1) * sin_ref[...]
    if has_add:
        chunk = add_ref[...].astype(F32)
        if rope_add:
            chunk = rope(chunk)
    if has_rope or has_add or head_major:
        for h in range(y.shape[1] // LANES):
            yh = y[:, h * LANES:(h + 1) * LANES]
            if has_rope and not rope_add:
                yh = rope(yh)
            if has_add:
                yh = yh + chunk
            if head_major:
                o_ref[h] = yh.astype(o_ref.dtype)
            else:
                o_ref[:, h * LANES:(h + 1) * LANES] = yh.astype(o_ref.dtype)
    else:
        o_ref[...] = y.astype(o_ref.dtype)


def projection(a, w, *, tm, tn, out_dtype, a_cols=None, gain=None, rope=None, rope_add=False,
               out_scale=None, add=None, head_major=None, name="projection"):
    n = a.shape[0]
    k, nc = w.shape
    a_w, a_blk = a_cols if a_cols is not None else (a.shape[1], 0)
    assert a_w == k and n % tm == 0 and nc % tn == 0
    grid = (n // tm, nc // tn)
    in_specs = [pl.BlockSpec((tm, k), lambda i, j: (i, a_blk))]
    args = [a]
    if gain is not None:
        in_specs.append(pl.BlockSpec((1, k), lambda i, j: (0, 0)))
        args.append(gain.reshape(1, k))
    in_specs.append(pl.BlockSpec((k, tn), lambda i, j: (0, j)))
    args.append(w)
    roll_shift = 0
    if rope is not None:
        cos, sin, roll_shift = rope
        s_tiles = cos.shape[0] // tm
        for t in (cos, sin):
            in_specs.append(pl.BlockSpec((tm, LANES), lambda i, j: (i % s_tiles, 0)))
            args.append(t)
    if add is not None:
        add_arr, add_blk = add
        in_specs.append(pl.BlockSpec((tm, LANES), lambda i, j: (i, add_blk)))
        args.append(add_arr)
    if head_major is not None:
        b, s = head_major
        s_t = s // tm
        hpt = tn // LANES
        out_shape = jax.ShapeDtypeStruct((b, nc // LANES, s, LANES), out_dtype)
        out_spec = pl.BlockSpec((None, hpt, tm, LANES), lambda i, j: (i // s_t, j, i % s_t, 0))
    else:
        out_shape = jax.ShapeDtypeStruct((n, nc), out_dtype)
        out_spec = pl.BlockSpec((tm, tn), lambda i, j: (i, j))
    kern = functools.partial(
        _proj_kernel, has_gain=gain is not None, has_rope=rope is not None, roll_shift=roll_shift,
        rope_add=rope_add, out_scale=out_scale, has_add=add is not None,
        head_major=head_major is not None)
    return pl.pallas_call(
        kern, out_shape=out_shape, grid=grid, in_specs=in_specs, out_specs=out_spec,
        scratch_shapes=[pltpu.VMEM((tm, k), BF16)] if gain is not None else [],
        compiler_params=_cparams("parallel", "arbitrary"), name=name,
    )(*args)


def _compress_kernel(r_ref, pa_ref, pb_ref, w1a_ref, w1b_ref, w2_ref, o_ref):
    r = r_ref[...].astype(F32)
    a = jnp.dot((r + pa_ref[...]).astype(BF16), w1a_ref[...], preferred_element_type=F32)
    b = jnp.dot((r + pb_ref[...]).astype(BF16), w1b_ref[...], preferred_element_type=F32)
    nr = a.shape[0]
    hid = a + pltpu.roll(b, nr - 1, 0)
    hid = jax.nn.gelu(hid)
    o_ref[...] = jnp.dot(hid.astype(BF16), w2_ref[...], preferred_element_type=F32).astype(o_ref.dtype)


def compress(r, head0, pos_a, pos_b, w1a, w1b, w2):
    b, _, nr, kk = r.shape
    g = NSA_KV_HEADS
    full = lambda shape: pl.BlockSpec(shape, lambda bi, gi: (0,) * len(shape))
    return pl.pallas_call(
        _compress_kernel,
        out_shape=jax.ShapeDtypeStruct((b, g, nr, LANES), BF16),
        grid=(b, g),
        in_specs=[pl.BlockSpec((None, None, nr, kk), lambda bi, gi: (bi, head0 + gi, 0, 0)),
                  full((1, kk)), full((1, kk)), full((kk, LANES)), full((kk, LANES)),
                  full((LANES, LANES))],
        out_specs=pl.BlockSpec((None, None, nr, LANES), lambda bi, gi: (bi, gi, 0, 0)),
        compiler_params=_cparams("parallel", "parallel"),
        name="nsa_compress",
    )(r, pos_a, pos_b, w1a, w1b, w2)


def _dot_nt(a, b):
    return lax.dot_general(a, b, (((1,), (1,)), ((), ())), preferred_element_type=F32)


def _split_dot(x, w):
    hi = x.astype(BF16)
    lo = (x - hi.astype(F32)).astype(BF16)
    return (jnp.dot(hi, w, preferred_element_type=F32) + jnp.dot(lo, w, preferred_element_type=F32))


def _nsa_cmp_kernel(q_ref, kcc_ref, vcc_ref, msel_ref, glog_ref, e_ref, oc_ref, sel_ref, s_sc, p_sc, imp_sc,
                    *, tq, rc, topk):
    s0 = pl.program_id(2) * tq
    nr = kcc_ref.shape[0]

    def attend(ncol):
        kcc = kcc_ref[0:ncol, :]
        for h in range(NSA_HPG):
            s_sc[h, :, 0:ncol] = _dot_nt(q_ref[h], kcc)
        lane_v = lax.broadcasted_iota(jnp.int32, (ncol, LANES), 1)
        vcc = vcc_ref[0:ncol, :]
        vm = jnp.concatenate([jnp.where(lane_v == HALF, jnp.ones_like(vcc), vcc), msel_ref[0:ncol, :]],
                             axis=1)
        dmat = (lax.broadcasted_iota(jnp.int32, (rc, ncol), 0)
                - lax.broadcasted_iota(jnp.int32, (rc, ncol), 1) * CMP_STRIDE)
        visible = lambda c: dmat + (s0 + c * rc - (CMP_BLOCK - 1)) >= 0
        outs = []
        imp = jnp.zeros((tq, LANES), F32)
        for h in range(NSA_HPG):
            row_max = []
            for c in range(tq // rc):
                s = jnp.where(visible(c), s_sc[h, pl.ds(c * rc, rc), 0:ncol], NEG)
                row_max.append(jnp.max(s, axis=-1, keepdims=True))
            for c in range(tq // rc):
                rows = pl.ds(c * rc, rc)
                p = jnp.exp2(s_sc[h, rows, 0:ncol] - row_max[c])
                p_sc[h, rows, 0:ncol] = jnp.where(visible(c), p, 0.0).astype(BF16)
            a = jnp.dot(p_sc[h, :, 0:ncol], vm, preferred_element_type=F32)
            denom = jnp.maximum(a[:, HALF:HALF + 1], 1e-30)
            outs.append(a[:, :LANES] / denom)
            imp = imp + a[:, LANES:] / denom
        imp_sc[...] = imp
        glog = glog_ref[...]
        lane = lax.broadcasted_iota(jnp.int32, (tq, LANES), 1)
        for pr in range(NSA_HPG // 2):
            gate = jax.nn.sigmoid(_split_dot(glog, e_ref[pr]))
            pair = jnp.where(lane < HALF, outs[2 * pr], pltpu.roll(outs[2 * pr + 1], HALF, 1))
            oc_ref[:, pr * LANES:(pr + 1) * LANES] = (gate * pair).astype(oc_ref.dtype)

    ntiles = (s0 + (tq - CMP_BLOCK)) // (CMP_STRIDE * LANES) + 1
    for k in range(1, nr // LANES + 1):
        @pl.when(ntiles == k)
        def _width(k=k):
            attend(k * LANES)

    blk = lax.broadcasted_iota(jnp.int32, (tq, LANES), 1)
    cur = (s0 + lax.broadcasted_iota(jnp.int32, (tq, LANES), 0)) // SEL_BLOCK
    valid = blk <= cur
    forced = (blk == 0) | (blk == cur) | (blk == cur - 1)
    score = jnp.where(valid, jnp.where(forced, -jnp.inf, imp_sc[...]), -1.0)
    score_t = score.T
    nblk = (s0 + (tq - 1)) // SEL_BLOCK + 1
    for nrows in range(TOPK_ROWS, LANES + 1, TOPK_ROWS):
        @pl.when((nblk + (TOPK_ROWS - 1)) // TOPK_ROWS == nrows // TOPK_ROWS)
        def _select(nrows=nrows):
            tw = min(tq, TOPK_QUERIES)
            rowid = lax.broadcasted_iota(jnp.int32, (nrows, tw), 0).astype(F32)

            def pick_one(_, sc):
                cm = jnp.max(sc, axis=0, keepdims=True)
                first = jnp.min(jnp.where(sc == cm, rowid, float(LANES)), axis=0, keepdims=True)
                return jnp.where(rowid == first, -jnp.inf, sc)

            for w in range(tq // tw):
                cols = slice(w * tw, (w + 1) * tw)
                picked = lax.fori_loop(0, topk - N_FORCED, pick_one, score_t[0:nrows, cols]) == -jnp.inf
                bias = jnp.where(picked, 0.0, NEG)
                if nrows < LANES:
                    bias = jnp.concatenate([bias, jnp.full((LANES - nrows, tw), NEG, F32)], axis=0)
                sel_ref[cols, :] = bias.T.astype(sel_ref.dtype)


def nsa_compressed(qk, kcc, vcc, msel, misc, glog_blk, expand, *, batch, seq, tq, rc):
    g = NSA_KV_HEADS
    nq = seq // tq
    nr = kcc.shape[2]
    topk = min(SEL_TOPK, seq // SEL_BLOCK)
    assert topk > N_FORCED and nr % LANES == 0
    kern = functools.partial(_nsa_cmp_kernel, tq=tq, rc=rc, topk=topk)
    return pl.pallas_call(
        kern,
        out_shape=(jax.ShapeDtypeStruct((batch * seq, NSA_HEADS * NSA_DV), BF16),
                   jax.ShapeDtypeStruct((batch, g, seq, LANES), BF16)),
        grid=(batch, g, nq),
        in_specs=[
            pl.BlockSpec((None, NSA_HPG, tq, LANES), lambda b, gi, qi: (b, gi, qi, 0)),
            pl.BlockSpec((None, None, nr, LANES), lambda b, gi, qi: (b, gi, 0, 0)),
            pl.BlockSpec((None, None, nr, LANES), lambda b, gi, qi: (b, gi, 0, 0)),
            pl.BlockSpec((nr, LANES), lambda b, gi, qi: (0, 0)),
            pl.BlockSpec((tq, LANES), lambda b, gi, qi: (b * nq + qi, glog_blk)),
            pl.BlockSpec((None, 2, LANES, LANES), lambda b, gi, qi: (0, gi, 0, 0)),
        ],
        out_specs=(pl.BlockSpec((tq, 2 * LANES), lambda b, gi, qi: (b * nq + qi, gi)),
                   pl.BlockSpec((None, None, tq, LANES), lambda b, gi, qi: (b, gi, qi, 0))),
        scratch_shapes=[pltpu.VMEM((NSA_HPG, tq, nr), F32), pltpu.VMEM((NSA_HPG, tq, nr), BF16),
                        pltpu.VMEM((tq, LANES), F32)],
        compiler_params=_cparams("parallel", "parallel", "parallel"),
        name="nsa_compressed_select",
    )(qk, kcc, vcc, msel, misc, expand)


def _flash_steps(mode, nq, tq, tk):
    r = tq // tk
    qi_l, kt_l, first_l, last_l = [], [], [], []
    for qi in range(nq):
        hi = qi * r + r - 1
        lo = max(0, qi * r - (-(-(WINDOW - 1) // tk))) if mode == "window" else 0
        for kt in range(lo, hi + 1):
            qi_l.append(qi)
            kt_l.append(kt)
            first_l.append(int(kt == lo))
            last_l.append(int(kt == hi))
    return tuple(jnp.asarray(np.asarray(a, np.int32)) for a in (qi_l, kt_l, first_l, last_l))


def _flash_kernel(*refs, mode, tq, tk, rc, kv_shared, gated):
    it = iter(refs)
    qi_ref, kt_ref, first_ref, last_ref = next(it), next(it), next(it), next(it)
    q_ref, k_ref, v_ref = next(it), next(it), next(it)
    selb_ref = next(it) if mode == "select" else None
    oh_ref = next(it) if mode == "select" else None
    glog_ref = next(it) if gated else None
    e_ref = next(it) if gated else None
    o_ref = next(it)
    m_sc, acc_sc, s_sc, p_sc = next(it), next(it), next(it), next(it)
    qa_sc = next(it) if mode == "select" else None

    st = pl.program_id(2)
    s0 = qi_ref[st] * tq
    k0 = kt_ref[st] * tk

    @pl.when(first_ref[st] == 1)
    def _init():
        m_sc[...] = jnp.full(m_sc.shape, NEG, F32)
        acc_sc[...] = jnp.zeros(acc_sc.shape, F32)
        if mode == "select":
            for hh in range(FLASH_HEADS):
                qa_sc[hh] = jnp.concatenate([q_ref[hh], selb_ref[...]], axis=1)

    def chunk_cols(off, r0):
        if off is None:
            return 0, tk, True
        r1 = r0 + rc
        hi = min(tk, r1 - off)
        lo = max(0, r0 - off - WINDOW + 1) if mode == "window" else 0
        lo, hi = lo // LANES * LANES, -(-hi // LANES) * LANES
        clear = tk - 1 <= r0 - off and (mode != "window" or r1 - 1 - off < WINDOW)
        return lo, hi, clear

    def tile(off):
        nh = FLASH_HEADS
        r_lo = max(0, off) if off is not None else 0
        nrow = tq - r_lo
        if kv_shared:
            if mode == "select":
                qs = qa_sc[:, r_lo:, :].reshape(nh * nrow, 2 * LANES)
                ks = jnp.concatenate([k_ref[0], oh_ref[...]], axis=1)
            else:
                qs, ks = q_ref[:, r_lo:, :].reshape(nh * nrow, LANES), k_ref[0]
            s_sc[:, r_lo:, :] = _dot_nt(qs, ks).reshape(nh, nrow, tk)
        else:
            for hh in range(nh):
                s_sc[hh, r_lo:, :] = _dot_nt(q_ref[hh, r_lo:, :], k_ref[hh])
        lane_v = lax.broadcasted_iota(jnp.int32, (tk, LANES), 1)
        if off is not None:
            dmat = (lax.broadcasted_iota(jnp.int32, (rc, tk), 0)
                    - lax.broadcasted_iota(jnp.int32, (rc, tk), 1))
        for hh in range(FLASH_HEADS):
            v = v_ref[0 if kv_shared else hh]
            v_aug = jnp.where(lane_v == HALF, jnp.ones_like(v), v)

            for c in range(r_lo // rc, tq // rc):
                r0 = c * rc
                rows = pl.ds(r0, rc)
                lo, hi, clear = chunk_cols(off, r0)
                if lo > 0:
                    p_sc[hh, rows, 0:lo] = jnp.zeros((rc, lo), BF16)
                if hi < tk:
                    p_sc[hh, rows, max(hi, 0):tk] = jnp.zeros((rc, tk - max(hi, 0)), BF16)
                if hi <= lo:
                    continue
                s = s_sc[hh, rows, lo:hi]
                if not clear:
                    d = dmat[:, lo:hi] + (r0 - off)
                    msk = (d >= 0) & (d < WINDOW) if mode == "window" else d >= 0
                    s = jnp.where(msk, s, NEG)
                m_old = m_sc[hh, rows, :]
                m_new = jnp.maximum(m_old, jnp.max(s, axis=-1, keepdims=True))
                m_sc[hh, rows, :] = m_new
                acc_sc[hh, rows, :] = acc_sc[hh, rows, :] * jnp.exp2(m_old - m_new)
                p = jnp.exp2(s_sc[hh, rows, lo:hi] - jnp.concatenate([m_new] * ((hi - lo) // LANES), axis=1))
                if not clear:
                    p = jnp.where(msk, p, 0.0)
                p_sc[hh, rows, lo:hi] = p.astype(BF16)
            if not kv_shared:
                acc_sc[hh, r_lo:, :] += jnp.dot(p_sc[hh, r_lo:, :], v_aug, preferred_element_type=F32)
        if kv_shared:
            pv = jnp.dot(p_sc[:, r_lo:, :].reshape(nh * nrow, tk), v_aug, preferred_element_type=F32)
            acc_sc[:, r_lo:, :] += pv.reshape(nh, nrow, LANES)

    if mode == "window":
        offsets = [(j - (-(-(WINDOW - 1) // tk))) * tk for j in range(-(-(WINDOW - 1) // tk) + tq // tk)]
    else:
        offsets = [j * tk for j in range(tq // tk)]

        @pl.when(k0 + tk - 1 <= s0)
        def _interior():
            tile(None)

    for off in offsets:
        @pl.when(k0 - s0 == off)
        def _partial(off=off):
            tile(off)

    @pl.when(last_ref[st] == 1)
    def _finish():
        lane = lax.broadcasted_iota(jnp.int32, (tq, LANES), 1)
        if gated:
            glog = glog_ref[...]
        for pr in range(FLASH_HEADS // 2):
            outs = []
            for hh in (2 * pr, 2 * pr + 1):
                a = acc_sc[hh]
                outs.append(a / jnp.maximum(a[:, HALF:HALF + 1], 1e-30))
            out = jnp.where(lane < HALF, outs[0], pltpu.roll(outs[1], HALF, 1))
            if gated:
                out = jax.nn.sigmoid(_split_dot(glog, e_ref[pr])) * out
            o_ref[:, pr * LANES:(pr + 1) * LANES] = out.astype(o_ref.dtype)


def flash_heads(q, k, v, *, mode, batch, seq, tq, tk, rc, q_head0, k_head0, v_head0,
                kv_shared, out_dtype, selb=None, onehot=None, gate=None, name="flash"):
    nh = FLASH_HEADS
    ngroups = NSA_HEADS // nh
    nq = seq // tq
    assert tq % tk == 0 and tq % rc == 0
    tabs = _flash_steps(mode, nq, tq, tk)
    nsteps = tabs[0].shape[0]
    if kv_shared:
        kv_spec = lambda h0: pl.BlockSpec(
            (None, 1, tk, LANES), lambda b, g, st, qi, kt, fi, la: (b, h0 + g, kt[st], 0))
    else:
        kv_spec = lambda h0: pl.BlockSpec(
            (None, nh, tk, LANES), lambda b, g, st, qi, kt, fi, la: (b, h0 // nh + g, kt[st], 0))
    in_specs = [pl.BlockSpec((None, nh, tq, LANES),
                             lambda b, g, st, qi, kt, fi, la: (b, q_head0 // nh + g, qi[st], 0)),
                kv_spec(k_head0), kv_spec(v_head0)]
    args = [q, k, v]
    scratch = [pltpu.VMEM((nh, tq, LANES), F32), pltpu.VMEM((nh, tq, LANES), F32),
               pltpu.VMEM((nh, tq, tk), F32), pltpu.VMEM((nh, tq, tk), BF16)]
    if mode == "select":
        in_specs += [pl.BlockSpec((None, None, tq, LANES),
                                  lambda b, g, st, qi, kt, fi, la: (b, g, qi[st], 0)),
                     pl.BlockSpec((tk, LANES), lambda b, g, st, qi, kt, fi, la: (kt[st], 0))]
        args += [selb, onehot]
        scratch.append(pltpu.VMEM((nh, tq, 2 * LANES), BF16))
    if gate is not None:
        misc, glog_blk, expand, branch = gate
        in_specs += [pl.BlockSpec((tq, LANES),
                                  lambda b, g, st, qi, kt, fi, la: (b * nq + qi[st], glog_blk)),
                     pl.BlockSpec((None, nh // 2, LANES, LANES),
                                  lambda b, g, st, qi, kt, fi, la: (branch, g, 0, 0))]
        args += [misc, expand]
    kern = functools.partial(_flash_kernel, mode=mode, tq=tq, tk=tk, rc=rc,
                             kv_shared=kv_shared, gated=gate is not None)
    ow = nh * HALF
    grid_spec = pltpu.PrefetchScalarGridSpec(
        num_scalar_prefetch=4, grid=(batch, ngroups, nsteps), in_specs=in_specs,
        out_specs=pl.BlockSpec((tq, ow), lambda b, g, st, qi, kt, fi, la: (b * nq + qi[st], g)),
        scratch_shapes=scratch)
    return pl.pallas_call(
        kern,
        out_shape=jax.ShapeDtypeStruct((batch * seq, ngroups * ow), out_dtype),
        grid_spec=grid_spec,
        compiler_params=_cparams("parallel", "parallel", "arbitrary"),
        name=name,
    )(*tabs, *args)


def _mix_kernel(oc_ref, os_ref, ow_ref, ob_ref, xn_ref, wga_ref, wgb_ref, wa_ref, wb_ref, o_ref):
    oa = (oc_ref[...].astype(F32) + os_ref[...].astype(F32) + ow_ref[...].astype(F32)).astype(BF16)
    xn = xn_ref[...]
    ga = jax.nn.sigmoid(jnp.dot(xn, wga_ref[...], preferred_element_type=F32))
    ya = ga * jnp.dot(oa, wa_ref[...], preferred_element_type=F32)
    gb = jax.nn.sigmoid(jnp.dot(xn, wgb_ref[...], preferred_element_type=F32))
    yb = gb * jnp.dot(ob_ref[...], wb_ref[...], preferred_element_type=F32)
    o_ref[...] = (ya + yb).astype(o_ref.dtype)


def gated_mix(oc, osel, ow, ob, xn, wga, wgb, wa, wb, *, tm, tn):
    n, ka = oc.shape
    d = wa.shape[1]
    row = lambda w: pl.BlockSpec((tm, w), lambda i, j: (i, 0))
    col = lambda k: pl.BlockSpec((k, tn), lambda i, j: (0, j))
    return pl.pallas_call(
        _mix_kernel,
        out_shape=jax.ShapeDtypeStruct((n, d), BF16),
        grid=(n // tm, d // tn),
        in_specs=[row(ka), row(ka), row(ka), row(ob.shape[1]), row(xn.shape[1]),
                  col(wga.shape[0]), col(wgb.shape[0]), col(ka), col(wb.shape[0])],
        out_specs=pl.BlockSpec((tm, tn), lambda i, j: (i, j)),
        compiler_params=_cparams("parallel", "arbitrary"),
        name="gated_mix",
    )(oc, osel, ow, ob, xn, wga, wgb, wa, wb)


def _xattn_kernel(x_ref, mix_ref, wout_ref, g_ref, wq_ref, kv_ref, wo_ref, o_ref):
    h = x_ref[...] + jnp.dot(mix_ref[...], wout_ref[...], preferred_element_type=F32)
    hn = _rms(h, g_ref[...]).astype(BF16)
    q = jnp.dot(hn, wq_ref[...], preferred_element_type=F32).astype(BF16)
    kv = kv_ref[...]
    outs = []
    for hd in range(XA_HEADS):
        qh = q[:, hd * XA_DIM:(hd + 1) * XA_DIM]
        kh = kv[:, hd * XA_DIM:(hd + 1) * XA_DIM]
        vh = kv[:, (XA_HEADS + hd) * XA_DIM:(XA_HEADS + hd + 1) * XA_DIM]
        s = _dot_nt(qh, kh) * (XA_DIM ** -0.5)
        e = jnp.exp(s - jnp.max(s, axis=-1, keepdims=True))
        p = e / jnp.sum(e, axis=-1, keepdims=True)
        outs.append(jnp.dot(p.astype(BF16), vh, preferred_element_type=F32))
    o = jnp.concatenate(outs, axis=1).astype(BF16)
    o_ref[...] = h + jnp.dot(o, wo_ref[...], preferred_element_type=F32)


def out_proj_xattn(x, mixed, wout, g, wq, kv, wo, *, seq, tm):
    n, d = x.shape
    s_t = seq // tm
    dq = wq.shape[1]
    resident = lambda shape: pl.BlockSpec(shape, lambda i: (0, 0), pipeline_mode=pl.Buffered(1))
    return pl.pallas_call(
        _xattn_kernel,
        out_shape=jax.ShapeDtypeStruct((n, d), F32),
        grid=(n // tm,),
        in_specs=[pl.BlockSpec((tm, d), lambda i: (i, 0)),
                  pl.BlockSpec((tm, d), lambda i: (i, 0)),
                  resident((d, d)),
                  pl.BlockSpec((1, d), lambda i: (0, 0)),
                  resident((d, dq)),
                  pl.BlockSpec((MEM_LEN, 2 * dq), lambda i: (i // s_t, 0)),
                  resident((dq, d))],
        out_specs=pl.BlockSpec((tm, d), lambda i: (i, 0)),
        compiler_params=_cparams("parallel"),
        name="out_proj_xattn",
    )(x, mixed, wout, g.reshape(1, d), wq, kv, wo)


def _mlp_kernel(h_ref, g_ref, w1_ref, w2_ref, gf_ref, o_ref, hn_sc, acc_sc, *, nf):
    f = pl.program_id(1)

    @pl.when(f == 0)
    def _init():
        hn_sc[...] = _rms(h_ref[...], g_ref[...]).astype(BF16)
        acc_sc[...] = jnp.zeros(acc_sc.shape, F32)

    u = jnp.dot(hn_sc[...], w1_ref[...], preferred_element_type=F32)
    u = jnp.square(jnp.maximum(u, 0.0))
    acc_sc[...] += jnp.dot(u.astype(BF16), w2_ref[...], preferred_element_type=F32)

    @pl.when(f == nf - 1)
    def _finish():
        o_ref[...] = _rms(h_ref[...] + acc_sc[...], gf_ref[...])


def mlp_final(h, g, w1, w2, gf, *, tm, tf):
    n, d = h.shape
    dff = w1.shape[1]
    nf = dff // tf
    return pl.pallas_call(
        functools.partial(_mlp_kernel, nf=nf),
        out_shape=jax.ShapeDtypeStruct((n, d), F32),
        grid=(n // tm, nf),
        in_specs=[pl.BlockSpec((tm, d), lambda i, f: (i, 0)),
                  pl.BlockSpec((1, d), lambda i, f: (0, 0)),
                  pl.BlockSpec((d, tf), lambda i, f: (0, f)),
                  pl.BlockSpec((tf, d), lambda i, f: (f, 0)),
                  pl.BlockSpec((1, d), lambda i, f: (0, 0))],
        out_specs=pl.BlockSpec((tm, d), lambda i, f: (i, 0)),
        scratch_shapes=[pltpu.VMEM((tm, d), BF16), pltpu.VMEM((tm, d), F32)],
        compiler_params=_cparams("parallel", "arbitrary"),
        name="mlp_final",
    )(h, g.reshape(1, d), w1, w2, gf.reshape(1, d))


def _rot_partner(w, half):
    return jnp.concatenate([-w[..., half:], w[..., :half]], axis=-1)


def _pad_heads_rope(w, heads, hd, rot0, rot_dim):
    k = w.shape[0]
    w = w.reshape(k, heads, hd)
    partner = _rot_partner(w[:, :, rot0:rot0 + rot_dim], rot_dim // 2)
    pad = jnp.zeros((k, heads, LANES - hd - rot_dim), w.dtype)
    return jnp.concatenate([w, partner, pad], axis=-1).reshape(k, heads * LANES)


def _rope_lane_tables(seq, rot0, rot_dim, hd):
    inv = 1.0 / (ROPE_THETA ** (jnp.arange(0, rot_dim, 2, dtype=F32) / rot_dim))
    ang = jnp.arange(seq, dtype=F32)[:, None] * inv[None, :]
    cos = jnp.concatenate([jnp.cos(ang), jnp.cos(ang)], axis=1)
    sin = jnp.concatenate([jnp.sin(ang), jnp.sin(ang)], axis=1)
    cos_t = jnp.concatenate([jnp.ones((seq, rot0), F32), cos,
                             jnp.ones((seq, hd - rot0 - rot_dim), F32),
                             jnp.zeros((seq, LANES - hd), F32)], axis=1)
    sin_t = jnp.concatenate([jnp.zeros((seq, rot0), F32), sin,
                             jnp.zeros((seq, LANES - rot0 - rot_dim), F32)], axis=1)
    return cos_t, sin_t


def _pad_lanes(w, groups, width):
    k = w.shape[0]
    w = w.reshape(k, groups, width)
    return jnp.pad(w, ((0, 0), (0, 0), (0, LANES - width))).reshape(k, groups * LANES)


def _compress_params(pos, w1, w2, d):
    half = CMP_BLOCK // 2
    pos_p = jnp.pad(pos, ((0, 0), (0, LANES - d)))
    pos_a = pos_p[:half].reshape(1, half * LANES)
    pos_b = pos_p[half:].reshape(1, half * LANES)
    w1p = jnp.pad(w1.reshape(CMP_BLOCK, d, d), ((0, 0), (0, LANES - d), (0, LANES - d)))
    w1a = w1p[:half].reshape(half * LANES, LANES).astype(BF16)
    w1b = w1p[half:].reshape(half * LANES, LANES).astype(BF16)
    w2p = jnp.pad(w2, ((0, LANES - d), (0, LANES - d)))
    return pos_a, pos_b, w1a, w1b, w2p.astype(BF16)


def _cmp_to_sel(nr, nsb):
    cs = np.arange(nr) * CMP_STRIDE
    ce = cs + CMP_BLOCK
    ss = np.arange(LANES) * SEL_BLOCK
    se = ss + SEL_BLOCK
    ov = np.clip(np.minimum(ce[:, None], se[None, :]) - np.maximum(cs[:, None], ss[None, :]), 0, None)
    ov = ov.astype(np.float32) / np.float32(CMP_BLOCK)
    ov[:, nsb:] = 0.0
    ov[nr - 1:, :] = 0.0
    return jnp.asarray(ov, BF16)


def _gate_expand():
    e = np.zeros((3, NSA_HEADS // 2, LANES, LANES), np.float32)
    for br in range(3):
        for hp in range(NSA_HEADS // 2):
            for hh in range(2):
                e[br, hp, 3 * (2 * hp + hh) + br, hh * HALF:(hh + 1) * HALF] = 1.0
    return jnp.asarray(e, BF16)


def _key_block_onehot(seq):
    e = (np.arange(seq)[:, None] // SEL_BLOCK) == np.arange(LANES)[None, :]
    return jnp.asarray(e.astype(np.float32), BF16)


def kernel(x, mem, g_mix, w_in, cmp_pos_k, cmp_w1_k, cmp_w2_k, cmp_pos_v, cmp_w1_v, cmp_w2_v,
           mla_g_q, mla_w_uq, mla_g_kv, mla_w_uk, mla_w_uv, w_o_nsa, w_o_mla, w_out,
           g_xattn, g_mem, xa_wq, xa_wkv, xa_wo, g_mlp, w_ff1, w_ff2, g_final):
    b, s, d = x.shape
    assert d == D_MODEL and s % (CMP_STRIDE * 8) == 0 and s // SEL_BLOCK <= LANES
    assert g_mix.shape[0] == 1
    n = b * s
    T = _tiles(s)
    tm, tn, tq, tk, rc = T["tm"], T["tn"], T["tq"], T["tk"], T["rc"]
    G = NSA_KV_HEADS
    bounds = [int(v) for v in np.cumsum(SPLITS)[:-1]]

    (w_qa, w_kc, w_vc, w_ks, w_vs, w_kw, w_vw, w_gn, w_cq, w_ckv, w_kr,
     w_ga, w_gb) = jnp.split(w_in[0], bounds, axis=1)
    nsa_rope = lambda w, heads: _pad_heads_rope(w, heads, NSA_DK, 0, NSA_ROT)
    w_rope = jnp.concatenate([nsa_rope(w_qa, NSA_HEADS), nsa_rope(w_kc, G), nsa_rope(w_ks, G),
                              nsa_rope(w_kw, G)], axis=1).astype(BF16)
    w_krp = jnp.concatenate([jnp.zeros((d, MLA_NOPE), F32), w_kr, _rot_partner(w_kr, MLA_ROPE // 2)], axis=1)
    w_vsw = jnp.concatenate([_pad_lanes(w_vs, G, NSA_DV), _pad_lanes(w_vw, G, NSA_DV),
                             _pad_lanes(w_vc, G, NSA_DV)], axis=1).astype(BF16)
    w_misc = jnp.concatenate([w_cq, w_ckv, jnp.pad(w_gn, ((0, 0), (0, LANES - w_gn.shape[1]))), w_krp],
                             axis=1).astype(BF16)
    glog_blk = (MLA_Q_RANK + MLA_KV_RANK) // LANES
    krp_blk = glog_blk + 1
    cos_a, sin_a = _rope_lane_tables(s, 0, NSA_ROT, NSA_DK)
    cos_b, sin_b = _rope_lane_tables(s, MLA_NOPE, MLA_ROPE, MLA_NOPE + MLA_ROPE)
    shift_a = LANES - NSA_DK
    shift_b = LANES - (NSA_DK - MLA_NOPE)
    w_uq = _pad_heads_rope(mla_w_uq[0], MLA_HEADS, MLA_NOPE + MLA_ROPE, MLA_NOPE, MLA_ROPE).astype(BF16)
    w_uk = _pad_lanes(mla_w_uk[0], MLA_HEADS, MLA_NOPE).astype(BF16)
    w_uv = _pad_lanes(mla_w_uv[0], MLA_HEADS, MLA_DV).astype(BF16)

    x2 = x.reshape(n, d)
    xn = rmsnorm_rows(x2, g_mix[0], tm)
    log2e = float(np.log2(np.e))
    tmb = T["tm_big"]
    qk = projection(xn, w_rope, tm=tmb, tn=tn, out_dtype=BF16, rope=(cos_a, sin_a, shift_a),
                    out_scale=(NSA_DK ** -0.5 * log2e, NSA_HEADS * LANES // tn),
                    head_major=(b, s), name="proj_qk_rope")
    vsw = projection(xn, w_vsw, tm=tmb, tn=tn, out_dtype=BF16, head_major=(b, s), name="proj_v")
    vcp = vsw[:, 2 * G:3 * G]
    misc = projection(xn, w_misc, tm=tm, tn=w_misc.shape[1], out_dtype=F32, name="proj_misc")

    nr = s // CMP_STRIDE
    pk = _compress_params(cmp_pos_k[0], cmp_w1_k[0], cmp_w2_k[0], NSA_DK)
    pv = _compress_params(cmp_pos_v[0], cmp_w1_v[0], cmp_w2_v[0], NSA_DV)
    kc_rows = qk[:, QK_KC0:QK_KC0 + G].reshape(b, G, nr, CMP_STRIDE * LANES)
    kcc = compress(kc_rows, 0, *pk)
    vcc = compress(vcp.reshape(b, G, nr, CMP_STRIDE * LANES), 0, *pv)
    expand = _gate_expand()
    o_c, selb = nsa_compressed(qk, kcc, vcc, _cmp_to_sel(nr, s // SEL_BLOCK), misc, glog_blk, expand,
                               batch=b, seq=s, tq=T["tq_cmp"], rc=rc)
    o_s = flash_heads(qk, qk, vsw, mode="select", batch=b, seq=s, tq=tq, tk=tk, rc=rc,
                      q_head0=QK_Q0, k_head0=QK_KS0, v_head0=0, kv_shared=True, out_dtype=BF16,
                      selb=selb, onehot=_key_block_onehot(s), gate=(misc, glog_blk, expand, 1),
                      name="nsa_selected")
    o_w = flash_heads(qk, qk, vsw, mode="window", batch=b, seq=s, tq=T["tq_win"], tk=tk, rc=rc,
                      q_head0=QK_Q0, k_head0=QK_KW0, v_head0=G, kv_shared=True, out_dtype=BF16,
                      gate=(misc, glog_blk, expand, 2), name="nsa_window")

    qm = projection(misc, w_uq, tm=T["tm_mla"], tn=tn, out_dtype=BF16, a_cols=(MLA_Q_RANK, 0), gain=mla_g_q[0],
                    rope=(cos_b, sin_b, shift_b), head_major=(b, s), name="mla_q",
                    out_scale=((MLA_NOPE + MLA_ROPE) ** -0.5 * log2e, MLA_HEADS * LANES // tn))
    km = projection(misc, w_uk, tm=T["tm_mla"], tn=tn, out_dtype=BF16,
                    a_cols=(MLA_KV_RANK, MLA_Q_RANK // MLA_KV_RANK), gain=mla_g_kv[0],
                    add=(misc, krp_blk), rope=(cos_b, sin_b, shift_b), rope_add=True,
                    head_major=(b, s), name="mla_k")
    vm = projection(misc, w_uv, tm=T["tm_mla"], tn=tn, out_dtype=BF16,
                    a_cols=(MLA_KV_RANK, MLA_Q_RANK // MLA_KV_RANK), gain=mla_g_kv[0],
                    head_major=(b, s), name="mla_v")
    o_b = flash_heads(qm, km, vm, mode="causal", batch=b, seq=s, tq=tq, tk=tk, rc=rc,
                      q_head0=0, k_head0=0, v_head0=0,
                      kv_shared=False, out_dtype=BF16, name="mla_attention")

    mixed = gated_mix(o_c, o_s, o_w, o_b, xn, w_ga.astype(BF16), w_gb.astype(BF16),
                      w_o_nsa[0].astype(BF16), w_o_mla[0].astype(BF16), tm=tm, tn=tn)

    kv_mem = projection(mem.reshape(b * MEM_LEN, d), xa_wkv[0].astype(BF16), tm=MEM_LEN, tn=tn,
                        out_dtype=BF16, gain=g_mem[0], name="xattn_kv")
    h2 = out_proj_xattn(x2, mixed, w_out[0].astype(BF16), g_xattn[0], xa_wq[0].astype(BF16), kv_mem,
                        xa_wo[0].astype(BF16), seq=s, tm=T["tm_x"])

    out = mlp_final(h2, g_mlp[0], w_ff1[0].astype(BF16), w_ff2[0].astype(BF16), g_final,
                    tm=T["tm_mlp"], tf=T["tf"])
    return out.reshape(b, s, d)
```
